```python
import math
import jax
import jax.numpy as jnp
from jax import lax
import numpy as np

D_MODEL = 1024
BATCH = 16
SEQ = 4096
DEPTH = 4

CTX_LEN = 256
GRID_W = 64
D_MIX = D_MODEL
GROUP_W = D_MIX // 4
H_A = 4
HKV_A = 2
G_A = H_A // HKV_A
DH_A = GROUP_W // H_A
WINDOW = 128
BLK = 128
HY_W = GROUP_W
HY_ORDER = 2
HY_SHORT = 3
HY_BANDS = 16
HY_EMB = 1 + 2 * HY_BANDS
HY_HID = 64
HY_TARGET = 1e-2
HY_FAST = 0.3
HY_SLOW = 1.5
H_C = 4
DH_C = GROUP_W // H_C
CHUNK_C = 64
H_D = 4
DH_D = GROUP_W // (2 * H_D)
N_EXPERTS = 16
EC_CAPACITY = 2
D_EXPERT = D_MODEL
ROPE_BASE = 10000.0
EPS = 1e-6
NEG = -1e30
IN_SPLITS = (H_A * DH_A, HKV_A * DH_A, HKV_A * DH_A,
             3 * HY_W,
             H_C * DH_C, H_C * DH_C, H_C * DH_C, H_C * DH_C,
             4 * H_C,
             2 * H_D * DH_D, 2 * H_D * DH_D, 2 * H_D * DH_D)
IN_WIDTH = sum(IN_SPLITS)

kernel_name = 'hybrid_diffusion_trunk'


def _rms(x, g):
    xf = x.astype(jnp.float32)
    y = xf * lax.rsqrt(jnp.mean(xf * xf, axis=-1, keepdims=True) + EPS)
    return y.astype(x.dtype) * g.astype(x.dtype)


def _split_cols(p):
    return jnp.split(p, np.cumsum(IN_SPLITS)[:-1].tolist(), axis=-1)


def _axial_tables(T, dh):
    rows = T // GRID_W
    r = jnp.broadcast_to(jnp.arange(rows, dtype=jnp.float32)[:, None], (rows, GRID_W)).reshape(T)
    col = jnp.broadcast_to(jnp.arange(GRID_W, dtype=jnp.float32)[None, :], (rows, GRID_W)).reshape(T)
    nf = dh // 4
    inv = ROPE_BASE ** (-jnp.arange(nf, dtype=jnp.float32) / nf)
    ang = jnp.stack([r[:, None] * inv, col[:, None] * inv], axis=1)
    return jnp.cos(ang), jnp.sin(ang)


def _rope2d(x, cos, sin):
    nf = x.shape[-1] // 4
    xr = x.reshape(*x.shape[:-1], 2, 2, nf)
    bshape = (x.shape[1],) + (1,) * (x.ndim - 3) + (2, nf)
    c = cos.reshape(bshape).astype(x.dtype)
    s = sin.reshape(bshape).astype(x.dtype)
    a, b = xr[..., 0, :], xr[..., 1, :]
    return jnp.stack([a * c - b * s, b * c + a * s], axis=-2).reshape(x.shape)


def _sink_softmax(s, sink):
    m = jnp.maximum(jnp.max(s, axis=-1, keepdims=True), sink)
    p = jnp.exp(s - m)
    return p / (jnp.sum(p, axis=-1, keepdims=True) + jnp.exp(sink - m))


def _window_gqa(qc, kc, vc, ql, kl, vl, q_gain, k_gain, sink, cos, sin, ctx_out):
    B, T = ql.shape[:2]
    S = kc.shape[1]
    nb = T // BLK
    scale = DH_A ** -0.5
    sink_hg = sink.astype(jnp.float32).reshape(HKV_A, G_A, 1, 1)
    kc = _rms(kc.reshape(B, S, HKV_A, DH_A), k_gain)
    vc = vc.reshape(B, S, HKV_A, DH_A)
    ql = _rope2d(_rms(ql.reshape(B, T, H_A, DH_A), q_gain), cos, sin)
    kl = _rope2d(_rms(kl.reshape(B, T, HKV_A, DH_A), k_gain), cos, sin)
    qb = ql.reshape(B, nb, BLK, HKV_A, G_A, DH_A)

    def windows(t):
        tp = jnp.pad(t, ((0, 0), (BLK, BLK), (0, 0), (0, 0))).reshape(B, nb + 2, BLK, HKV_A, DH_A)
        return jnp.concatenate([tp[:, :-2], tp[:, 1:-1], tp[:, 2:]], axis=2)

    kw = windows(kl)
    vw = windows(vl.reshape(B, T, HKV_A, DH_A))
    kpos = jnp.arange(-BLK, T + BLK).reshape(nb + 2, BLK)
    kpos = jnp.concatenate([kpos[:-2], kpos[1:-1], kpos[2:]], axis=1)
    qpos = jnp.arange(T).reshape(nb, BLK)
    valid = ((jnp.abs(qpos[:, :, None] - kpos[:, None, :]) <= WINDOW)
             & (kpos >= 0)[:, None, :] & (kpos < T)[:, None, :])
    s_loc = jnp.einsum('bnqhgd,bnkhd->bnhgqk', qb, kw).astype(jnp.float32) * scale
    s_loc = jnp.where(valid[None, :, None, None], s_loc, NEG)
    s_ctx = jnp.einsum('bnqhgd,bkhd->bnhgqk', qb, kc).astype(jnp.float32) * scale
    p = _sink_softmax(jnp.concatenate([s_loc, s_ctx], axis=-1), sink_hg).astype(vl.dtype)
    nk = 3 * BLK
    ol = (jnp.einsum('bnhgqk,bnkhd->bnqhgd', p[..., :nk], vw)
          + jnp.einsum('bnhgqk,bkhd->bnqhgd', p[..., nk:], vc)).reshape(B, T, H_A * DH_A)
    oc = None
    if ctx_out:
        qcg = _rms(qc.reshape(B, S, H_A, DH_A), q_gain).reshape(B, S, HKV_A, G_A, DH_A)
        sc = jnp.einsum('bqhgd,bkhd->bhgqk', qcg, kc).astype(jnp.float32) * scale
        pc = _sink_softmax(sc, sink_hg).astype(vc.dtype)
        oc = jnp.einsum('bhgqk,bkhd->bqhgd', pc, vc).reshape(B, S, H_A * DH_A)
    return oc, ol


def _hyena_spectra(L, fw1, fb1, freq, fw2, fb2, fw3):
    f32 = jnp.float32
    t = jnp.linspace(0.0, 1.0, L, dtype=f32)[:, None]
    w = 2.0 * math.pi * jnp.arange(L, dtype=f32)[:, None] / L
    bands = jnp.linspace(1e-4, HY_BANDS - 1, HY_BANDS, dtype=f32)
    z = jnp.concatenate([t, jnp.cos(bands * w), -jnp.sin(bands * w)], axis=-1)
    fr = freq.astype(f32)
    h = jnp.sin(fr * (z @ fw1.astype(f32) + fb1.astype(f32)))
    h = jnp.sin(fr * (h @ fw2.astype(f32) + fb2.astype(f32)))
    h = (h @ fw3.astype(f32)).reshape(L, 2, HY_ORDER, HY_W)
    deltas = jnp.abs(jnp.linspace(math.log(HY_TARGET) / HY_SLOW, math.log(HY_TARGET) / HY_FAST, HY_W, dtype=f32))
    h = h * jnp.exp(-t * deltas)[:, None, None, :]
    two_sided = jnp.concatenate([h[:, 0], jnp.zeros((1, HY_ORDER, HY_W), f32), jnp.flip(h[1:, 1], axis=0)], axis=0)
    two_sided = two_sided * lax.rsqrt(jnp.sum(two_sided * two_sided, axis=0, keepdims=True) + EPS)
    return jnp.fft.rfft(two_sided, axis=0)


def _fftconv(z, spec, bias):
    L = z.shape[1]
    zf = z.astype(jnp.float32)
    y = jnp.fft.irfft(jnp.fft.rfft(zf, n=2 * L, axis=1) * spec, n=2 * L, axis=1)[:, :L]
    return (y + zf * bias.astype(jnp.float32)).astype(z.dtype)


def _hyena_mixer(uc, ul, conv_w, fw1, fb1, freq, fw2, fb2, fw3, bias, ctx_out):
    def run(u):
        L = u.shape[1]
        up = jnp.pad(u, ((0, 0), (1, 1), (0, 0)))
        u = conv_w[0] * up[:, 0:L] + conv_w[1] * up[:, 1:L + 1] + conv_w[2] * up[:, 2:L + 2]
        v, x1, x2 = jnp.split(u, 3, axis=-1)
        spec = _hyena_spectra(L, fw1, fb1, freq, fw2, fb2, fw3)
        z = v
        for n, gate in enumerate((x1, x2)):
            z = gate * _fftconv(z, spec[:, n], bias[n])
        return z
    return (run(uc) if ctx_out else None), run(ul)


def _mlstm_scan(q, k, v, ig, lf, state, emit):
    B, H, T, d = q.shape
    nc = T // CHUNK_C

    def chunks(t):
        return jnp.moveaxis(t.reshape(B, H, nc, CHUNK_C, *t.shape[3:]), 2, 0)

    tril = jnp.tril(jnp.ones((CHUNK_C, CHUNK_C), dtype=bool))

    def step(carry, xs):
        C, n, m = carry
        qc, kc, vc, ic, fc = xs
        b = jnp.cumsum(fc, axis=-1)
        g_end = b[..., -1:] - b + ic
        m_new = jnp.maximum(b[..., -1] + m, jnp.max(g_end, axis=-1))
        w_prev = jnp.exp(b[..., -1] + m - m_new)
        w_tok = jnp.exp(g_end - m_new[..., None])
        C_new = w_prev[..., None, None] * C + jnp.einsum('bhs,bhse,bhsk->bhek', w_tok, vc, kc)
        n_new = w_prev[..., None] * n + jnp.einsum('bhs,bhsk->bhk', w_tok, kc)
        if not emit:
            return (C_new, n_new, m_new), None
        logw = jnp.where(tril, b[..., :, None] - b[..., None, :] + ic[..., None, :], NEG)
        inter = b + m[..., None]
        m_t = jnp.maximum(inter, jnp.max(logw, axis=-1))
        w_in = jnp.exp(inter - m_t)
        s = jnp.einsum('bhtk,bhsk->bhts', qc, kc) * jnp.exp(logw - m_t[..., None])
        num = w_in[..., None] * jnp.einsum('bhek,bhtk->bhte', C, qc) + jnp.einsum('bhts,bhse->bhte', s, vc)
        den = w_in * jnp.einsum('bhk,bhtk->bht', n, qc) + jnp.sum(s, axis=-1)
        h = num / jnp.maximum(jnp.abs(den), jnp.exp(-m_t))[..., None]
        return (C_new, n_new, m_new), h

    carry, hs = lax.scan(step, state, tuple(chunks(t) for t in (q, k, v, ig, lf)))
    if not emit:
        return None, carry
    return jnp.moveaxis(hs, 0, 2).reshape(B, H, T, d), carry


def _mlstm_mixer(pc, pl, b_gate, norm_gain, ctx_out):
    f32 = jnp.float32

    def prep(q, k, v, g):
        B, L = q.shape[:2]

        def heads(t):
            return jnp.swapaxes(t.reshape(B, L, H_C, DH_C), 1, 2).astype(f32)

        gates = jnp.moveaxis(g.reshape(B, L, 4, H_C).astype(f32) + b_gate.astype(f32), 1, -1)
        return heads(q), heads(k) * DH_C ** -0.5, heads(v), gates

    qc, kc, vc, oc_pre, gc = pc
    ql, kl, vl, ol_pre, gl = pl
    Qc, Kc, Vc, Gc = prep(qc, kc, vc, gc)
    Ql, Kl, Vl, Gl = prep(ql, kl, vl, gl)
    B = ql.shape[0]
    s0 = (jnp.zeros((B, H_C, DH_C, DH_C), f32), jnp.zeros((B, H_C, DH_C), f32), jnp.zeros((B, H_C), f32))
    lsig = jax.nn.log_sigmoid

    def flip(t):
        return jnp.flip(t, axis=2)

    hcf, sf = _mlstm_scan(Qc, Kc, Vc, Gc[:, 0], lsig(Gc[:, 2]), s0, ctx_out)
    hlf, _ = _mlstm_scan(Ql, Kl, Vl, Gl[:, 0], lsig(Gl[:, 2]), sf, True)
    hcb, sb = _mlstm_scan(flip(Qc), flip(Kc), flip(Vc), flip(Gc[:, 1]), flip(lsig(Gc[:, 3])), s0, ctx_out)
    hlb, _ = _mlstm_scan(flip(Ql), flip(Kl), flip(Vl), flip(Gl[:, 1]), flip(lsig(Gl[:, 3])), sb, True)

    def out(h, o_pre):
        B_, L = o_pre.shape[:2]
        h = jnp.swapaxes(h, 1, 2)
        o = jax.nn.sigmoid(o_pre.reshape(B_, L, H_C, DH_C).astype(f32))
        return (o * _rms(h, norm_gain)).astype(o_pre.dtype).reshape(B_, L, H_C * DH_C)

    yl = out(hlf + flip(hlb), ol_pre)
    yc = out(hcf + flip(hcb), oc_pre) if ctx_out else None
    return yc, yl


def _diff_attn(qc, kc, vc, ql, kl, vl, q_gain, k_gain, lq1, lk1, lq2, lk2, sub_gain, lam_init, cos, sin, ctx_out):
    B, T = ql.shape[:2]
    S = kc.shape[1]
    nb = T // BLK
    scale = DH_D ** -0.5
    f32 = jnp.float32
    lam = (jnp.exp(jnp.sum(lq1.astype(f32) * lk1.astype(f32)))
           - jnp.exp(jnp.sum(lq2.astype(f32) * lk2.astype(f32))) + lam_init)
    kc = _rms(kc.reshape(B, S, H_D, 2, DH_D), k_gain)
    vc = vc.reshape(B, S, H_D, 2 * DH_D)
    ql = _rope2d(_rms(ql.reshape(B, T, H_D, 2, DH_D), q_gain), cos, sin)
    kl = _rope2d(_rms(kl.reshape(B, T, H_D, 2, DH_D), k_gain), cos, sin)
    k_all = jnp.concatenate([kl, kc], axis=1)
    v_all = jnp.concatenate([vl.reshape(B, T, H_D, 2 * DH_D), vc], axis=1)

    def attend(q, k, v):
        s = jnp.einsum('bqhcd,bkhcd->bhcqk', q, k).astype(f32) * scale
        p = jax.nn.softmax(s, axis=-1)
        a = (p[:, :, 0] - lam * p[:, :, 1]).astype(v.dtype)
        return jnp.einsum('bhqk,bkhe->bqhe', a, v)

    def post(o):
        return (_rms(o, sub_gain) * (1.0 - lam_init)).reshape(*o.shape[:2], H_D * 2 * DH_D)

    qb = jnp.moveaxis(ql.reshape(B, nb, BLK, H_D, 2, DH_D), 1, 0)
    ol = lax.map(lambda q: attend(q, k_all, v_all), qb)
    ol = post(jnp.moveaxis(ol, 0, 1).reshape(B, T, H_D, 2 * DH_D))
    oc = post(attend(_rms(qc.reshape(B, S, H_D, 2, DH_D), q_gain), kc, vc)) if ctx_out else None
    return oc, ol


def _expert_choice_ffn(h, w_router, w_g, w_u, w_d):
    B, T, D = h.shape
    cap = (EC_CAPACITY * T) // N_EXPERTS
    aff = jax.nn.softmax((h @ w_router).astype(jnp.float32), axis=-1)
    gate, idx = lax.top_k(jnp.swapaxes(aff, 1, 2), cap)
    bidx = jnp.arange(B)[:, None, None]
    xe = h[bidx, idx]
    a = jnp.einsum('becd,edf->becf', xe, w_g)
    u = jnp.einsum('becd,edf->becf', xe, w_u)
    y = jnp.einsum('becf,efd->becd', jax.nn.silu(a) * u, w_d) * gate[..., None].astype(h.dtype)
    return jnp.zeros_like(h).at[bidx, idx].add(y)


def setup_inputs(seed: int = 0) -> dict:
    key = jax.random.key(seed)
    ks = iter(list(jax.random.split(key, 48)))
    f32 = jnp.float32

    def nrm(shape, s):
        return jax.random.normal(next(ks), shape, f32) * s

    def gain(shape):
        return 1.0 + nrm(shape, 0.05)

    L, D = DEPTH, D_MODEL
    b_gate = jnp.concatenate([nrm((L, 2, H_C), 0.1),
                              jnp.linspace(3.0, 6.0, H_C, dtype=f32) + nrm((L, 2, H_C), 0.1)], axis=1)
    return {
        'x': nrm((BATCH, SEQ, D), 1.0),
        'c': nrm((BATCH, D), 1.0),
        'ctx': nrm((BATCH, CTX_LEN, D), 1.0),
        'c_ctx': nrm((D,), 1.0),
        'w_ada': nrm((L, D, 6 * D), 0.5 * D ** -0.5),
        'b_ada': nrm((L, 6 * D), 0.02),
        'norm1_g': gain((L, D)),
        'norm2_g': gain((L, D)),
        'w_in': nrm((L, D, IN_WIDTH), D ** -0.5),
        'b_gate': b_gate,
        'a_qnorm': gain((L, DH_A)),
        'a_knorm': gain((L, DH_A)),
        'a_sink': nrm((L, H_A), 0.5),
        'hy_conv': nrm((L, HY_SHORT, 3 * HY_W), HY_SHORT ** -0.5),
        'hy_fw1': nrm((L, HY_EMB, HY_HID), HY_EMB ** -0.5),
        'hy_fb1': nrm((L, HY_HID), 0.02),
        'hy_freq': gain((L, HY_HID)),
        'hy_fw2': nrm((L, HY_HID, HY_HID), HY_HID ** -0.5),
        'hy_fb2': nrm((L, HY_HID), 0.02),
        'hy_fw3': nrm((L, HY_HID, 2 * HY_ORDER * HY_W), HY_HID ** -0.5),
        'hy_bias': nrm((L, HY_ORDER, HY_W), 0.5),
        'ml_norm': gain((L, DH_C)),
        'd_qnorm': gain((L, DH_D)),
        'd_knorm': gain((L, DH_D)),
        'd_lq1': nrm((L, DH_D), 0.1),
        'd_lk1': nrm((L, DH_D), 0.1),
        'd_lq2': nrm((L, DH_D), 0.1),
        'd_lk2': nrm((L, DH_D), 0.1),
        'd_subnorm': gain((L, 2 * DH_D)),
        'w_out': nrm((L, D_MIX, D), D_MIX ** -0.5),
        'w_router': nrm((L, D, N_EXPERTS), D ** -0.5),
        'w_e_gate': nrm((L, N_EXPERTS, D, D_EXPERT), D ** -0.5),
        'w_e_up': nrm((L, N_EXPERTS, D, D_EXPERT), D ** -0.5),
        'w_e_down': nrm((L, N_EXPERTS, D_EXPERT, D), D_EXPERT ** -0.5),
    }


def reference(x, c, ctx, c_ctx, w_ada, b_ada, norm1_g, norm2_g, w_in, b_gate,
              a_qnorm, a_knorm, a_sink,
              hy_conv, hy_fw1, hy_fb1, hy_freq, hy_fw2, hy_fb2, hy_fw3, hy_bias,
              ml_norm, d_qnorm, d_knorm, d_lq1, d_lk1, d_lq2, d_lk2, d_subnorm,
              w_out, w_router, w_e_gate, w_e_up, w_e_down):
    T = x.shape[1]
    cos_a, sin_a = _axial_tables(T, DH_A)
    cos_d, sin_d = _axial_tables(T, DH_D)
    silu_c = jax.nn.silu(c)
    silu_cc = jax.nn.silu(c_ctx)
    for l in range(DEPTH):
        last = l == DEPTH - 1
        ctx_out = not last
        lam_init = 0.8 - 0.6 * math.exp(-0.3 * l)
        mod_l = (silu_c @ w_ada[l] + b_ada[l])[:, None, :]
        mod_c = (silu_cc @ w_ada[l] + b_ada[l])[None, None, :]
        sh1l, sc1l, gt1l, sh2l, sc2l, gt2l = jnp.split(mod_l, 6, axis=-1)
        sh1c, sc1c, gt1c, sh2c, sc2c, gt2c = jnp.split(mod_c, 6, axis=-1)

        hl = _rms(x, norm1_g[l]) * (1.0 + sc1l) + sh1l
        hc = _rms(ctx, norm1_g[l]) * (1.0 + sc1c) + sh1c
        pl = _split_cols(hl @ w_in[l])
        pc = _split_cols(hc @ w_in[l])
        ya_c, ya_l = _window_gqa(pc[0], pc[1], pc[2], pl[0], pl[1], pl[2],
                                 a_qnorm[l], a_knorm[l], a_sink[l], cos_a, sin_a, ctx_out)
        yb_c, yb_l = _hyena_mixer(pc[3], pl[3], hy_conv[l], hy_fw1[l], hy_fb1[l], hy_freq[l],
                                  hy_fw2[l], hy_fb2[l], hy_fw3[l], hy_bias[l], ctx_out)
        yc_c, yc_l = _mlstm_mixer(pc[4:9], pl[4:9], b_gate[l], ml_norm[l], ctx_out)
        yd_c, yd_l = _diff_attn(pc[9], pc[10], pc[11], pl[9], pl[10], pl[11],
                                d_qnorm[l], d_knorm[l], d_lq1[l], d_lk1[l], d_lq2[l], d_lk2[l],
                                d_subnorm[l], lam_init, cos_d, sin_d, ctx_out)
        x = x + gt1l * (jnp.concatenate([ya_l, yb_l, yc_l, yd_l], axis=-1) @ w_out[l])

        x = x + gt2l * _expert_choice_ffn(_rms(x, norm2_g[l]) * (1.0 + sc2l) + sh2l,
                                          w_router[l], w_e_gate[l], w_e_up[l], w_e_down[l])
        if ctx_out:
            ctx = ctx + gt1c * (jnp.concatenate([ya_c, yb_c, yc_c, yd_c], axis=-1) @ w_out[l])
            ctx = ctx + gt2c * _expert_choice_ffn(_rms(ctx, norm2_g[l]) * (1.0 + sc2c) + sh2c,
                                                  w_router[l], w_e_gate[l], w_e_up[l], w_e_down[l])
    return x
```

```python
import functools
import math

import jax
import jax.numpy as jnp
import numpy as np
from jax import lax
from jax.experimental import pallas as pl
from jax.experimental.pallas import tpu as pltpu

F32 = jnp.float32
BF16 = jnp.bfloat16
I32 = jnp.int32
HI = lax.Precision.HIGHEST

D_MODEL = 1024
GRID_W = 64
GROUP_W = D_MODEL // 4
H_A, HKV_A = 4, 2
G_A = H_A // HKV_A
DH_A = GROUP_W // H_A
WINDOW = 128
BLK = 128
HY_W = GROUP_W
HY_ORDER = 2
HY_BANDS = 16
HY_EMB = 1 + 2 * HY_BANDS
HY_HID = 64
HY_TARGET, HY_FAST, HY_SLOW = 1e-2, 0.3, 1.5
H_C = 4
DH_C = GROUP_W // H_C
H_D = 4
DH_D = GROUP_W // (2 * H_D)
N_EXPERTS = 16
EC_CAPACITY = 2
ROPE_BASE = 10000.0
EPS = 1e-6
NEG = -1e30

ROW_BLOCK = 256
MLSTM_CHUNK = 256
IDX_BATCH_GROUP = 4
VMEM_LIMIT = 56 * 1024 * 1024

_C_QA, _C_KA, _C_VA, _C_UB = 0, 256, 384, 512
_C_QC, _C_KC, _C_VC, _C_OC, _C_GC = 1280, 1536, 1792, 2048, 2304
_C_QD, _C_KD, _C_VD, _C_END = 2432, 2688, 2944, 3200


def _cparams(sem, vmem=VMEM_LIMIT):
    return pltpu.CompilerParams(dimension_semantics=sem, vmem_limit_bytes=vmem)


def _dot(a, b, **kw):
    return jnp.dot(a, b, preferred_element_type=F32, **kw)


def _dot_nt(a, b, **kw):
    return lax.dot_general(a, b, (((1,), (1,)), ((), ())), preferred_element_type=F32, **kw)


def _dot_tn(a, b, **kw):
    return lax.dot_general(a, b, (((0,), (0,)), ((), ())), preferred_element_type=F32, **kw)


def _ada_kernel(c_ref, w_ref, b_ref, o_ref):
    c = c_ref[...]
    s = c * jax.nn.sigmoid(c)
    o_ref[0] = _dot(s, w_ref[0], precision=HI) + b_ref[0]


def _ada_mods(cc, w_ada, b_ada):
    L, D, W6 = w_ada.shape
    R = cc.shape[0]
    cb = 1536
    return pl.pallas_call(
        _ada_kernel,
        grid=(L, W6 // cb),
        in_specs=[pl.BlockSpec((R, D), lambda l, j: (0, 0)),
                  pl.BlockSpec((1, D, cb), lambda l, j: (l, 0, j)),
                  pl.BlockSpec((1, 1, cb), lambda l, j: (l, 0, j))],
        out_specs=pl.BlockSpec((1, R, cb), lambda l, j: (l, 0, j)),
        out_shape=jax.ShapeDtypeStruct((L, R, W6), F32),
        compiler_params=_cparams(("arbitrary", "arbitrary")),
        name="ada_mods",
    )(cc, w_ada, b_ada.reshape(L, 1, W6))


def _inproj_kernel(x_ref, mod_ref, g1_ref, w_ref, ca_ref, sa_ref, cd_ref, sd_ref,
                   gains_ref, g64_ref, g32_ref, p64_ref, p32_ref,
                   qa, ka, va, ub, qc, kc, vc, oc, gc, qd, kd, vd):
    x = x_ref[0]
    ms = jnp.mean(x * x, axis=-1, keepdims=True)
    xn = x * lax.rsqrt(ms + EPS) * g1_ref[...]
    mod = mod_ref[0, 0]
    h = xn * (1.0 + mod[1:2, :]) + mod[0:1, :]
    p = _dot(h.astype(BF16), w_ref[...])

    def headnorm_rope(t, gmat, gain, pmat, cos, sin, scale):
        w = t.shape[1]
        msq = _dot(t * t, gmat[0:w, 0:w], precision=HI)
        tn = t * lax.rsqrt(msq + EPS) * gain
        tn = tn * cos[:, 0:w] + _dot(tn, pmat[0:w, 0:w], precision=HI) * sin[:, 0:w]
        return tn * scale

    ca, sa, cd, sd = ca_ref[...], sa_ref[...], cd_ref[...], sd_ref[...]
    gains = gains_ref[...]
    qa[0] = headnorm_rope(p[:, _C_QA:_C_KA], g64_ref, gains[0:1, :], p64_ref, ca, sa, DH_A ** -0.5).astype(BF16)
    ka[0] = headnorm_rope(p[:, _C_KA:_C_VA], g64_ref, gains[1:2, 0:128], p64_ref, ca, sa, 1.0).astype(BF16)
    va[0] = p[:, _C_VA:_C_UB].astype(BF16)
    ub[0] = p[:, _C_UB:_C_QC].astype(BF16)
    qc[0] = p[:, _C_QC:_C_KC].astype(BF16)
    kc[0] = (p[:, _C_KC:_C_VC] * DH_C ** -0.5).astype(BF16)
    vc[0] = p[:, _C_VC:_C_OC].astype(BF16)
    oc[0] = p[:, _C_OC:_C_GC].astype(BF16)
    gc[0] = p[:, _C_GC:_C_QD]
    qd[0] = headnorm_rope(p[:, _C_QD:_C_KD], g32_ref, gains[2:3, :], p32_ref, cd, sd, DH_D ** -0.5).astype(BF16)
    kd[0] = headnorm_rope(p[:, _C_KD:_C_VD], g32_ref, gains[3:4, :], p32_ref, cd, sd, 1.0).astype(BF16)
    vd[0] = p[:, _C_VD:_C_END].astype(BF16)


def _inproj(x, mod, g1, w_packed, tabs, gains, mats, T):
    B, N, D = x.shape
    rb = ROW_BLOCK
    nlat = T // rb
    row = lambda w: pl.BlockSpec((1, rb, w), lambda b, r: (b, r, 0))
    tab = pl.BlockSpec((rb, 256), lambda b, r: (r, 0))
    full = lambda a: pl.BlockSpec(a.shape, lambda b, r: (0,) * a.ndim)
    widths = [256, 128, 128, 768, 256, 256, 256, 256, 128, 256, 256, 256]
    dtypes = [BF16] * 8 + [F32] + [BF16] * 3
    return pl.pallas_call(
        _inproj_kernel,
        grid=(B, N // rb),
        in_specs=[row(D),
                  pl.BlockSpec((1, 1, 6, D), lambda b, r: (b, jnp.where(r >= nlat, 1, 0), 0, 0)),
                  full(g1), full(w_packed), tab, tab, tab, tab, full(gains)] + [full(m) for m in mats],
        out_specs=[row(w) for w in widths],
        out_shape=[jax.ShapeDtypeStruct((B, N, w), dt) for w, dt in zip(widths, dtypes)],
        compiler_params=_cparams(("arbitrary", "arbitrary")),
        name="inproj",
    )(x, mod, g1, w_packed, *tabs, gains, *mats)


def _wattn_kernel(q_ref, k_ref, v_ref, sink_ref, o_ref, *, T, S):
    j = pl.program_id(1)
    nlat = T // BLK
    is_lat = j < nlat
    start = pl.multiple_of(jnp.clip((j - 1) * BLK, 0, T - 3 * BLK), BLK)
    kw = k_ref[0, pl.ds(start, 3 * BLK), :]
    vw = v_ref[0, pl.ds(start, 3 * BLK), :]
    kc = k_ref[0, T:T + S, :]
    vc = v_ref[0, T:T + S, :]
    qpos = j * BLK + lax.broadcasted_iota(I32, (BLK, 3 * BLK), 0)
    kpos = start + lax.broadcasted_iota(I32, (BLK, 3 * BLK), 1)
    valid = (jnp.abs(qpos - kpos) <= WINDOW) & is_lat
    sinks = sink_ref[...]
    for hk in range(HKV_A):
        cs = slice(hk * DH_A, (hk + 1) * DH_A)
        kwh, vwh, kch, vch = kw[:, cs], vw[:, cs], kc[:, cs], vc[:, cs]
        for g in range(G_A):
            hd = hk * G_A + g
            q = q_ref[0, :, hd * DH_A:(hd + 1) * DH_A]
            s_loc = jnp.where(valid, _dot_nt(q, kwh), NEG)
            s_ctx = _dot_nt(q, kch)
            sink = sinks[0:1, hd:hd + 1]
            m = jnp.maximum(jnp.maximum(jnp.max(s_loc, axis=-1, keepdims=True),
                                        jnp.max(s_ctx, axis=-1, keepdims=True)), sink)
            p_loc = jnp.exp(s_loc - m)
            p_ctx = jnp.exp(s_ctx - m)
            den = (jnp.sum(p_loc, axis=-1, keepdims=True) + jnp.sum(p_ctx, axis=-1, keepdims=True)
                   + jnp.exp(sink - m))
            o = (_dot(p_loc.astype(BF16), vwh) + _dot(p_ctx.astype(BF16), vch)) / den
            o_ref[0, :, hd * DH_A:(hd + 1) * DH_A] = o.astype(BF16)


def _wattn(qa, ka, va, sink, T, S):
    B, N, _ = qa.shape
    return pl.pallas_call(
        functools.partial(_wattn_kernel, T=T, S=S),
        grid=(B, N // BLK),
        in_specs=[pl.BlockSpec((1, BLK, 256), lambda b, j: (b, j, 0)),
                  pl.BlockSpec((1, N, 128), lambda b, j: (b, 0, 0)),
                  pl.BlockSpec((1, N, 128), lambda b, j: (b, 0, 0)),
                  pl.BlockSpec((1, 128), lambda b, j: (0, 0))],
        out_specs=pl.BlockSpec((1, BLK, 256), lambda b, j: (b, j, 0)),
        out_shape=jax.ShapeDtypeStruct((B, N, 256), BF16),
        compiler_params=_cparams(("arbitrary", "arbitrary")),
        name="window_attn",
    )(qa, ka, va, sink)


def _dattn_kernel(q_ref, k_ref, v_ref, lam_ref, sub_ref, o_ref, *, T, S, lam_init):
    j = pl.program_id(1)
    lv = lam_ref[...]
    lam = (jnp.exp(jnp.sum(lv[0:1, :] * lv[1:2, :], axis=-1, keepdims=True))
           - jnp.exp(jnp.sum(lv[2:3, :] * lv[3:4, :], axis=-1, keepdims=True)) + lam_init)
    subg = sub_ref[...] * (1.0 - lam_init)

    def body(k0, nk):
        for hd in range(H_D):
            a = None
            for c in range(2):
                cs = slice(hd * 2 * DH_D + c * DH_D, hd * 2 * DH_D + (c + 1) * DH_D)
                s = _dot_nt(q_ref[0, :, cs], k_ref[0, k0:k0 + nk, cs])
                e = jnp.exp(s - jnp.max(s, axis=-1, keepdims=True))
                r = 1.0 / jnp.sum(e, axis=-1, keepdims=True)
                a = e * r if c == 0 else a - e * (lam * r)
            o = _dot(a.astype(BF16), v_ref[0, k0:k0 + nk, hd * 2 * DH_D:(hd + 1) * 2 * DH_D])
            ms = jnp.mean(o * o, axis=-1, keepdims=True)
            o_ref[0, :, hd * 2 * DH_D:(hd + 1) * 2 * DH_D] = (o * lax.rsqrt(ms + EPS) * subg).astype(BF16)

    @pl.when(j < T // ROW_BLOCK)
    def _():
        body(0, T + S)

    @pl.when(j >= T // ROW_BLOCK)
    def _():
        body(T, S)


def _dattn(qd, kd, vd, lam_vecs, subg, T, S, lam_init):
    B, N, _ = qd.shape
    rb = ROW_BLOCK
    return pl.pallas_call(
        functools.partial(_dattn_kernel, T=T, S=S, lam_init=lam_init),
        grid=(B, N // rb),
        in_specs=[pl.BlockSpec((1, rb, 256), lambda b, j: (b, j, 0)),
                  pl.BlockSpec((1, N, 256), lambda b, j: (b, 0, 0)),
                  pl.BlockSpec((1, N, 256), lambda b, j: (b, 0, 0)),
                  pl.BlockSpec((4, DH_D), lambda b, j: (0, 0)),
                  pl.BlockSpec((1, 2 * DH_D), lambda b, j: (0, 0))],
        out_specs=pl.BlockSpec((1, rb, 256), lambda b, j: (b, j, 0)),
        out_shape=jax.ShapeDtypeStruct((B, N, 256), BF16),
        compiler_params=_cparams(("arbitrary", "arbitrary")),
        name="diff_attn",
    )(qd, kd, vd, lam_vecs, subg)


def _log_sigmoid(x):
    return jnp.minimum(x, 0.0) - jnp.log1p(jnp.exp(-jnp.abs(x)))


def _mlstm_kernel(qf_ref, kf_ref, vf_ref, gf_ref, gtf_ref, qb_ref, kb_ref, vb_ref, gb_ref, gtb_ref,
                  bias_ref, biast_ref, hf_ref, hb_ref, ct_ref, m_ref):
    i = pl.program_id(1)
    lc = MLSTM_CHUNK

    @pl.when(i == 0)
    def _():
        ct_ref[...] = jnp.zeros_like(ct_ref)
        m_ref[...] = jnp.zeros_like(m_ref)

    r_io = lax.broadcasted_iota(I32, (lc, lc), 0)
    c_io = lax.broadcasted_iota(I32, (lc, lc), 1)
    lower = r_io >= c_io
    upper = r_io <= c_io
    tri_lo = jnp.where(lower, 1.0, 0.0).astype(F32)
    tri_up = jnp.where(upper, 1.0, 0.0).astype(F32)
    ones_col = jnp.where(lax.broadcasted_iota(I32, (lc, DH_C), 1) == 0, 1.0, 0.0).astype(BF16)

    def direction(d, q_ref, k_ref, v_ref, g_ref, gt_ref, out_ref):
        pre = g_ref[0] + bias_ref[...]
        pre_t = gt_ref[0] + biast_ref[...]
        lf = _log_sigmoid(pre)
        lf_t = _log_sigmoid(pre_t)
        if d == 0:
            bcol = _dot(tri_lo, lf, precision=HI)
            brow = _dot(lf_t, tri_up, precision=HI)
            valid, end_row = lower, lc - 1
        else:
            bcol = _dot(tri_up, lf, precision=HI)
            brow = _dot(lf_t, tri_lo, precision=HI)
            valid, end_row = upper, 0
        for hd in range(H_C):
            ci, cf = d * H_C + hd, (2 + d) * H_C + hd
            cs = slice(hd * DH_C, (hd + 1) * DH_C)
            q, k, v = q_ref[0, :, cs], k_ref[0, :, cs], v_ref[0, :, cs]
            v_aug = jnp.concatenate([v, ones_col], axis=1)
            b_t = bcol[:, cf:cf + 1]
            ig = pre[:, ci:ci + 1]
            b_end = bcol[end_row:end_row + 1, cf:cf + 1]
            slot = d * H_C + hd
            ct = ct_ref[slot]
            m = m_ref[slot][:, 0:1]
            logw = jnp.where(valid, b_t - (brow[cf:cf + 1, :] - pre_t[ci:ci + 1, :]), NEG)
            inter = b_t + m
            m_t = jnp.maximum(inter, jnp.max(logw, axis=-1, keepdims=True))
            w_in = jnp.exp(inter - m_t)
            smat = _dot_nt(q, k) * jnp.exp(logw - m_t)
            num = w_in * _dot(q, ct.astype(BF16)) + _dot(smat.astype(BF16), v_aug)
            den = jnp.maximum(jnp.abs(num[:, DH_C:DH_C + 1]), jnp.exp(-m_t))
            out_ref[0, :, cs] = num[:, 0:DH_C] / den
            g_end = b_end - b_t + ig
            m_new = jnp.maximum(b_end + m, jnp.max(g_end, axis=0, keepdims=True))
            w_tok = jnp.exp(g_end - m_new)
            wv = (w_tok * v_aug.astype(F32)).astype(BF16)
            ct_ref[slot] = jnp.exp(b_end + m - m_new) * ct + _dot_tn(k, wv)
            m_ref[slot] = jnp.broadcast_to(m_new, (1, 128))

    direction(0, qf_ref, kf_ref, vf_ref, gf_ref, gtf_ref, hf_ref)
    direction(1, qb_ref, kb_ref, vb_ref, gb_ref, gtb_ref, hb_ref)


def _mlstm(qc, kc, vc, gc, gct, bias, biast, T, S):
    B, N, _ = qc.shape
    lc = MLSTM_CHUNK
    nch, nlat = N // lc, T // lc
    fwd = lambda i: jnp.where(i < nch - nlat, nlat + i, i - (nch - nlat))
    bwd = lambda i: nch - 1 - i
    tok = lambda f: pl.BlockSpec((1, lc, 256), lambda b, i: (b, f(i), 0))
    gate = lambda f: pl.BlockSpec((1, lc, 128), lambda b, i: (b, f(i), 0))
    gate_t = lambda f: pl.BlockSpec((1, 16, lc), lambda b, i: (b, 0, f(i)))
    return pl.pallas_call(
        _mlstm_kernel,
        grid=(B, nch),
        in_specs=[tok(fwd), tok(fwd), tok(fwd), gate(fwd), gate_t(fwd),
                  tok(bwd), tok(bwd), tok(bwd), gate(bwd), gate_t(bwd),
                  pl.BlockSpec((1, 128), lambda b, i: (0, 0)),
                  pl.BlockSpec((16, 1), lambda b, i: (0, 0))],
        out_specs=[tok(fwd), tok(bwd)],
        out_shape=[jax.ShapeDtypeStruct((B, N, 256), F32)] * 2,
        scratch_shapes=[pltpu.VMEM((2 * H_C, DH_C, 128), F32), pltpu.VMEM((2 * H_C, 1, 128), F32)],
        compiler_params=_cparams(("arbitrary", "arbitrary")),
        name="mlstm",
    )(qc, kc, vc, gc, gct, qc, kc, vc, gc, gct, bias, biast)


def _shortconv_kernel(u_ref, w_ref, o_ref, *, T, S):
    u = u_ref[0].astype(F32)
    n = u.shape[0]
    w = w_ref[...]
    row = lax.broadcasted_iota(I32, u.shape, 0)
    zero = jnp.zeros((1, u.shape[1]), F32)
    prev = jnp.concatenate([zero, u[0:n - 1]], axis=0)
    nxt = jnp.concatenate([u[1:n], zero], axis=0)
    prev = jnp.where(row == T, 0.0, prev)
    nxt = jnp.where(row == T - 1, 0.0, nxt)
    o_ref[0] = (w[0:1, :] * prev + w[1:2, :] * u + w[2:3, :] * nxt).astype(BF16)


def _shortconv(ub, conv_w, T, S):
    B, N, W = ub.shape
    return pl.pallas_call(
        functools.partial(_shortconv_kernel, T=T, S=S),
        grid=(B, W // 256),
        in_specs=[pl.BlockSpec((1, N, 256), lambda b, j: (b, 0, j)),
                  pl.BlockSpec((3, 256), lambda b, j: (0, j))],
        out_specs=pl.BlockSpec((1, N, 256), lambda b, j: (b, 0, j)),
        out_shape=jax.ShapeDtypeStruct((B, N, W), BF16),
        compiler_params=_cparams(("arbitrary", "arbitrary")),
        name="hyena_shortconv",
    )(ub, conv_w)


def _hyfilter_kernel(z_ref, w1_ref, b1_ref, fr_ref, w2_ref, b2_ref, w3_ref, dec_ref,
                     hs_ref, hd_ref, ss_ref, ny_ref):
    i = pl.program_id(0)
    fr = fr_ref[...]
    h = jnp.sin(fr * (_dot(z_ref[...], w1_ref[...], precision=HI) + b1_ref[...]))
    h = jnp.sin(fr * (_dot(h, w2_ref[...], precision=HI) + b2_ref[...]))
    h = _dot(h, w3_ref[...], precision=HI)
    dec = dec_ref[...]
    dec2 = jnp.concatenate([dec, dec], axis=1)
    hw = HY_ORDER * HY_W
    h0 = h[:, 0:hw] * dec2
    h1 = h[:, hw:2 * hw] * dec2
    rows = h0.shape[0]
    t = i * rows + lax.broadcasted_iota(I32, h0.shape, 0)
    h1 = jnp.where(t == 0, 0.0, h1)
    hsum = h0 + h1
    hs_ref[...] = hsum.astype(BF16)
    hd_ref[...] = (h1 - h0).astype(BF16)
    alt = jnp.where((t & 1) == 0, 1.0, -1.0)

    @pl.when(i == 0)
    def _():
        ss_ref[...] = jnp.zeros_like(ss_ref)
        ny_ref[...] = jnp.zeros_like(ny_ref)

    ss_ref[...] += jnp.sum(h0 * h0 + h1 * h1, axis=0, keepdims=True)
    ny_ref[...] += jnp.sum(hsum * alt, axis=0, keepdims=True)


def _hyfilter(zfeat, w1, b1, fr, w2, b2, w3, decay):
    ls = zfeat.shape[0]
    rb = min(ls, 512)
    hw = HY_ORDER * HY_W
    full = lambda a: pl.BlockSpec(a.shape, lambda i: (0,) * a.ndim)
    return pl.pallas_call(
        _hyfilter_kernel,
        grid=(ls // rb,),
        in_specs=[pl.BlockSpec((rb, 128), lambda i: (i, 0)), full(w1), full(b1), full(fr), full(w2),
                  full(b2), full(w3), pl.BlockSpec((rb, HY_W), lambda i: (i, 0))],
        out_specs=[pl.BlockSpec((rb, hw), lambda i: (i, 0)), pl.BlockSpec((rb, hw), lambda i: (i, 0)),
                   pl.BlockSpec((1, hw), lambda i: (0, 0)), pl.BlockSpec((1, hw), lambda i: (0, 0))],
        out_shape=[jax.ShapeDtypeStruct((ls, hw), BF16), jax.ShapeDtypeStruct((ls, hw), BF16),
                   jax.ShapeDtypeStruct((1, hw), F32), jax.ShapeDtypeStruct((1, hw), F32)],
        compiler_params=_cparams(("arbitrary",)),
        name="hyena_filter",
    )(zfeat, w1, b1, fr, w2, b2, w3, decay)


def _hyspec_kernel(cm_ref, sm_ref, hs_ref, hd_ref, ss_ref, hr_ref, hi_ref):
    scale = lax.rsqrt(ss_ref[...] + EPS)
    hr_ref[...] = _dot(cm_ref[...], hs_ref[...]) * scale
    hi_ref[...] = _dot(sm_ref[...], hd_ref[...]) * scale


def _hyspec(cm, sm, hs, hd, ss):
    ls = cm.shape[0]
    tk = min(ls, 512)
    hw = HY_ORDER * HY_W
    full = lambda a: pl.BlockSpec(a.shape, lambda i: (0,) * a.ndim)
    return pl.pallas_call(
        _hyspec_kernel,
        grid=(ls // tk,),
        in_specs=[pl.BlockSpec((tk, ls), lambda i: (i, 0)), pl.BlockSpec((tk, ls), lambda i: (i, 0)),
                  full(hs), full(hd), full(ss)],
        out_specs=[pl.BlockSpec((tk, hw), lambda i: (i, 0))] * 2,
        out_shape=[jax.ShapeDtypeStruct((ls, hw), F32)] * 2,
        compiler_params=_cparams(("arbitrary",)),
        name="hyena_spectrum",
    )(cm, sm, hs, hd, ss)


def _hyconv_kernel(cm_ref, sm_ref, z_ref, gate_ref, hr_ref, hi_ref, hny_ref, bias_ref, o_ref,
                   yr_ref, yi_ref, zny_ref, *, ls):
    ph = pl.program_id(1)
    j = pl.program_id(2)
    tk = cm_ref.shape[0]
    inv_n = 1.0 / (2 * ls)
    rows = pl.ds(pl.multiple_of(j * tk, tk), tk)

    @pl.when(ph == 0)
    def _():
        z = z_ref[0]
        zr = _dot(cm_ref[...], z)
        zi = -_dot(sm_ref[...], z)
        hr, hi = hr_ref[...], hi_ref[...]
        k = j * tk + lax.broadcasted_iota(I32, (tk, 1), 0)
        wk = jnp.where(k == 0, inv_n, 2.0 * inv_n)
        yr_ref[rows, :] = ((zr * hr - zi * hi) * wk).astype(BF16)
        yi_ref[rows, :] = ((zr * hi + zi * hr) * wk).astype(BF16)

        @pl.when(j == 0)
        def _():
            t = lax.broadcasted_iota(I32, (ls, 1), 0)
            alt = jnp.where((t & 1) == 0, 1.0, -1.0)
            zny_ref[...] = jnp.sum(z.astype(F32) * alt, axis=0, keepdims=True)

    @pl.when(ph == 1)
    def _():
        y = _dot(cm_ref[...], yr_ref[...]) - _dot(sm_ref[...], yi_ref[...])
        t = j * tk + lax.broadcasted_iota(I32, (tk, 1), 0)
        alt = jnp.where((t & 1) == 0, 1.0, -1.0)
        y = y + alt * (zny_ref[...] * hny_ref[...] * inv_n)
        zt = z_ref[0, rows, :].astype(F32)
        o_ref[0] = (gate_ref[0].astype(F32) * (y + zt * bias_ref[...])).astype(BF16)


def _hyconv(cm, sm, zsrc, zcol, zrow, gsrc, gcol, grow, hr, hi, hny, bias, ls):
    B = zsrc.shape[0]
    tk = min(ls, 512)
    nt = ls // tk
    return pl.pallas_call(
        functools.partial(_hyconv_kernel, ls=ls),
        grid=(B, 2, nt),
        in_specs=[pl.BlockSpec((tk, ls), lambda b, p, j: (j, 0)),
                  pl.BlockSpec((tk, ls), lambda b, p, j: (j, 0)),
                  pl.BlockSpec((1, ls, HY_W), lambda b, p, j: (b, zrow, zcol)),
                  pl.BlockSpec((1, tk, HY_W), lambda b, p, j: (b, grow * nt + j * p, gcol)),
                  pl.BlockSpec((tk, HY_W), lambda b, p, j: (j * (1 - p), 0)),
                  pl.BlockSpec((tk, HY_W), lambda b, p, j: (j * (1 - p), 0)),
                  pl.BlockSpec((1, HY_W), lambda b, p, j: (0, 0)),
                  pl.BlockSpec((1, HY_W), lambda b, p, j: (0, 0))],
        out_specs=pl.BlockSpec((1, tk, HY_W), lambda b, p, j: (b, j * p, 0)),
        out_shape=jax.ShapeDtypeStruct((B, ls, HY_W), BF16),
        scratch_shapes=[pltpu.VMEM((ls, HY_W), BF16), pltpu.VMEM((ls, HY_W), BF16),
                        pltpu.VMEM((1, HY_W), F32)],
        compiler_params=_cparams(("arbitrary", "arbitrary", "arbitrary")),
        name="hyena_longconv",
    )(cm, sm, zsrc, gsrc, hr, hi, hny, bias)


def _outproj_kernel(x_ref, mod_ref, ya_ref, yb_ref, hf_ref, hb_ref, oc_ref, yd_ref, mlg_ref, g64_ref,
                    w_ref, g2_ref, wr_ref, xo_ref, h2_ref, aff_ref):
    hsum = hf_ref[0] + hb_ref[0]
    msq = _dot(hsum * hsum, g64_ref[...], precision=HI)
    yc = jax.nn.sigmoid(oc_ref[0].astype(F32)) * (hsum * lax.rsqrt(msq + EPS) * mlg_ref[...])
    y = jnp.concatenate([ya_ref[0], yb_ref[0], yc.astype(BF16), yd_ref[0]], axis=1)
    mod = mod_ref[0, 0]
    x = x_ref[0] + mod[2:3, :] * _dot(y, w_ref[...])
    xo_ref[0] = x
    ms = jnp.mean(x * x, axis=-1, keepdims=True)
    h2 = x * lax.rsqrt(ms + EPS) * g2_ref[...] * (1.0 + mod[4:5, :]) + mod[3:4, :]
    h2_ref[0] = h2
    logits = _dot(h2, wr_ref[...], precision=HI)
    lane = lax.broadcasted_iota(I32, logits.shape, 1)
    logits = jnp.where(lane < N_EXPERTS, logits, NEG)
    e = jnp.exp(logits - jnp.max(logits, axis=-1, keepdims=True))
    aff_ref[0] = e / jnp.sum(e, axis=-1, keepdims=True)


def _outproj(x, mod, ya, yb, hf, hb, oc, yd, mlg, g64, w_out, g2, w_router, T):
    B, N, D = x.shape
    rb = ROW_BLOCK
    nlat = T // rb
    row = lambda w: pl.BlockSpec((1, rb, w), lambda b, r: (b, r, 0))
    full = lambda a: pl.BlockSpec(a.shape, lambda b, r: (0,) * a.ndim)
    return pl.pallas_call(
        _outproj_kernel,
        grid=(B, N // rb),
        in_specs=[row(D), pl.BlockSpec((1, 1, 6, D), lambda b, r: (b, jnp.where(r >= nlat, 1, 0), 0, 0)),
                  row(256), row(256), row(256), row(256), row(256), row(256),
                  full(mlg), full(g64), full(w_out), full(g2), full(w_router)],
        out_specs=[row(D), row(D), row(128)],
        out_shape=[jax.ShapeDtypeStruct((B, N, D), F32), jax.ShapeDtypeStruct((B, N, D), F32),
                   jax.ShapeDtypeStruct((B, N, 128), F32)],
        compiler_params=_cparams(("arbitrary", "arbitrary")),
        name="outproj",
    )(x, mod, ya, yb, hf, hb, oc, yd, mlg, g64, w_out, g2, w_router)


def _prefix_exclusive(x):
    n = x.shape[1]
    lane = lax.broadcasted_iota(I32, x.shape, 1)
    inc = x
    d = 1
    while d < n:
        inc = inc + jnp.where(lane >= d, pltpu.roll(inc, d, axis=1), 0)
        d *= 2
    return inc - x


def _topk_kernel(aff_ref, o_ref, pos_ref, val_ref, *, T, S):
    def segment(t0, ts, cap, slot0):
        a = aff_ref[0, :, t0:t0 + ts]
        bits = lax.bitcast_convert_type(a, I32)

        def search(i, v):
            cand = v | jnp.left_shift(jnp.int32(1), 30 - i)
            cnt = jnp.sum((bits >= cand).astype(I32), axis=1, keepdims=True)
            return jnp.where(cnt >= cap, cand, v)

        thr = lax.fori_loop(0, 31, search, jnp.zeros((N_EXPERTS, 1), I32))
        gt = bits > thr
        eq = bits == thr
        need = cap - jnp.sum(gt.astype(I32), axis=1, keepdims=True)
        sel = gt | (eq & (_prefix_exclusive(eq.astype(I32)) < need))
        seli = sel.astype(I32)
        pos_ref[:, 0:ts] = jnp.where(sel, _prefix_exclusive(seli), -1)
        a_hi = a.astype(BF16)
        r1 = a - a_hi.astype(F32)
        a_mid = r1.astype(BF16)
        a_lo = (r1 - a_mid.astype(F32)).astype(BF16)
        val_ref[0, :, 0:ts] = a_hi.astype(F32)
        val_ref[1, :, 0:ts] = a_mid.astype(F32)
        val_ref[2, :, 0:ts] = a_lo.astype(F32)
        tabs = lax.broadcasted_iota(I32, (1, ts), 1) + t0
        t_hi = jnp.right_shift(tabs, 6).astype(F32)
        t_lo = (tabs & 63).astype(F32)
        slot = lax.broadcasted_iota(I32, (cap, ts), 0)
        zeros = jnp.zeros((11, ts), F32)

        def per_expert(e, _):
            onehot = jnp.where(pos_ref[pl.ds(e, 1), 0:ts] == slot, 1.0, 0.0).astype(BF16)
            vals = jnp.concatenate([t_hi, t_lo, val_ref[0, pl.ds(e, 1), 0:ts], val_ref[1, pl.ds(e, 1), 0:ts],
                                    val_ref[2, pl.ds(e, 1), 0:ts], zeros], axis=0).astype(BF16)
            o_ref[0, e, slot0:slot0 + cap, :] = _dot_nt(onehot, vals)
            return 0

        lax.fori_loop(0, N_EXPERTS, per_expert, 0)

    segment(0, T, (EC_CAPACITY * T) // N_EXPERTS, 0)
    segment(T, S, (EC_CAPACITY * S) // N_EXPERTS, (EC_CAPACITY * T) // N_EXPERTS)


def _topk(aff_t, T, S):
    B, E, N = aff_t.shape
    cap_t = (EC_CAPACITY * T) // E + (EC_CAPACITY * S) // E
    return pl.pallas_call(
        functools.partial(_topk_kernel, T=T, S=S),
        grid=(B,),
        in_specs=[pl.BlockSpec((1, E, N), lambda b: (b, 0, 0))],
        out_specs=pl.BlockSpec((1, E, cap_t, 16), lambda b: (b, 0, 0, 0)),
        out_shape=jax.ShapeDtypeStruct((B, E, cap_t, 16), F32),
        scratch_shapes=[pltpu.VMEM((E, T), I32), pltpu.VMEM((3, E, T), F32)],
        compiler_params=_cparams(("arbitrary",)),
        name="expert_topk",
    )(aff_t)


def _gather_kernel(idx_ref, h_ref, o_ref, *, cap_t):
    b, e = pl.program_id(0), pl.program_id(2)
    base = (b * N_EXPERTS + e) * cap_t

    def body(g, _):
        rows = [h_ref[0, pl.ds(idx_ref[base + g * 16 + k], 1), :] for k in range(16)]
        o_ref[0, 0, pl.ds(pl.multiple_of(g * 16, 16), 16), :] = jnp.concatenate(rows, axis=0).astype(BF16)
        return 0

    lax.fori_loop(0, cap_t // 16, body, 0)


def _gather(idx_flat, h2, cap_t):
    B, N, D = h2.shape
    hw = D // 2
    return pl.pallas_call(
        functools.partial(_gather_kernel, cap_t=cap_t),
        grid_spec=pltpu.PrefetchScalarGridSpec(
            num_scalar_prefetch=1,
            grid=(B, 2, N_EXPERTS),
            in_specs=[pl.BlockSpec((1, N, hw), lambda b, c, e, idx: (b, 0, c))],
            out_specs=pl.BlockSpec((1, 1, cap_t, hw), lambda b, c, e, idx: (b, e, 0, c)),
        ),
        out_shape=jax.ShapeDtypeStruct((B, N_EXPERTS, cap_t, D), BF16),
        compiler_params=_cparams(("arbitrary", "arbitrary", "arbitrary")),
        name="expert_gather",
    )(idx_flat, h2)


def _ffn_kernel(x_ref, wg_ref, wu_ref, wd_ref, gate_ref, gt_ref, o_ref, *, cap_l):
    x = x_ref[0, 0]
    a = _dot(x, wg_ref[0])
    u = _dot(x, wu_ref[0])
    hmid = (a * jax.nn.sigmoid(a) * u).astype(BF16)
    y = _dot(hmid, wd_ref[0])
    gt = gt_ref[0]
    row = lax.broadcasted_iota(I32, y.shape, 0)
    gt2 = jnp.where(row < cap_l, gt[0:1, :], gt[1:2, :])
    o_ref[0, 0] = y * gate_ref[0, 0] * gt2


def _ffn(xe, wg, wu, wd, gate, gt2, cap_l):
    B, E, cap_t, D = xe.shape
    wspec = pl.BlockSpec((1, D, D), lambda e, b: (e, 0, 0))
    return pl.pallas_call(
        functools.partial(_ffn_kernel, cap_l=cap_l),
        grid=(E, B),
        in_specs=[pl.BlockSpec((1, 1, cap_t, D), lambda e, b: (b, e, 0, 0)), wspec, wspec, wspec,
                  pl.BlockSpec((1, 1, cap_t, 1), lambda e, b: (b, e, 0, 0)),
                  pl.BlockSpec((1, 2, D), lambda e, b: (b, 0, 0))],
        out_specs=pl.BlockSpec((1, 1, cap_t, D), lambda e, b: (b, e, 0, 0)),
        out_shape=jax.ShapeDtypeStruct((B, E, cap_t, D), F32),
        compiler_params=_cparams(("arbitrary", "arbitrary")),
        name="expert_ffn",
    )(xe, wg, wu, wd, gate, gt2)


def _scatter_kernel(idx_ref, x_ref, y_ref, o_ref, *, cap_t):
    b, e = pl.program_id(0), pl.program_id(2)
    base = (b * N_EXPERTS + e) * cap_t

    @pl.when(e == 0)
    def _():
        o_ref[...] = x_ref[...]

    def body(g, _):
        tile = y_ref[0, 0, pl.ds(pl.multiple_of(g * 8, 8), 8), :]
        for k in range(8):
            r = idx_ref[base + g * 8 + k]
            o_ref[0, pl.ds(r, 1), :] = o_ref[0, pl.ds(r, 1), :] + tile[k:k + 1, :]
        return 0

    lax.fori_loop(0, cap_t // 8, body, 0)


def _scatter(idx_flat, x, ye, cap_t):
    B, N, D = x.shape
    hw = D // 2
    return pl.pallas_call(
        functools.partial(_scatter_kernel, cap_t=cap_t),
        grid_spec=pltpu.PrefetchScalarGridSpec(
            num_scalar_prefetch=1,
            grid=(B, 2, N_EXPERTS),
            in_specs=[pl.BlockSpec((1, N, hw), lambda b, c, e, idx: (b, 0, c)),
                      pl.BlockSpec((1, 1, cap_t, hw), lambda b, c, e, idx: (b, e, 0, c))],
            out_specs=pl.BlockSpec((1, N, hw), lambda b, c, e, idx: (b, 0, c)),
        ),
        out_shape=jax.ShapeDtypeStruct((B, N, D), F32),
        compiler_params=_cparams(("arbitrary", "arbitrary", "arbitrary")),
        name="expert_scatter",
    )(idx_flat, x, ye)


def _rope_tables(T, S, dh):
    rows = T // GRID_W
    r = jnp.broadcast_to(jnp.arange(rows, dtype=F32)[:, None], (rows, GRID_W)).reshape(T)
    col = jnp.broadcast_to(jnp.arange(GRID_W, dtype=F32)[None, :], (rows, GRID_W)).reshape(T)
    nf = dh // 4
    inv = ROPE_BASE ** (-jnp.arange(nf, dtype=F32) / nf)
    ar, ac = r[:, None] * inv, col[:, None] * inv
    ang = jnp.concatenate([ar, ar, ac, ac], axis=1)
    ang = jnp.concatenate([ang, jnp.zeros((S, dh), F32)], axis=0)
    reps = 256 // dh
    return jnp.tile(jnp.cos(ang), (1, reps)), jnp.tile(jnp.sin(ang), (1, reps))


def _group_mats(dh):
    i = np.arange(256)
    gmat = (i[:, None] // dh == i[None, :] // dh).astype(np.float32) / dh
    nf = dh // 4
    half = (i % (2 * nf)) // nf
    pmat = np.zeros((256, 256), np.float32)
    a_idx = i[half == 0]
    pmat[a_idx + nf, a_idx] = -1.0
    pmat[a_idx, a_idx + nf] = 1.0
    return jnp.asarray(gmat), jnp.asarray(pmat)


def _hyena_tables(ls):
    t = jnp.linspace(0.0, 1.0, ls, dtype=F32)[:, None]
    w = 2.0 * math.pi * jnp.arange(ls, dtype=F32)[:, None] / ls
    bands = jnp.linspace(1e-4, HY_BANDS - 1, HY_BANDS, dtype=F32)
    z = jnp.concatenate([t, jnp.cos(bands * w), -jnp.sin(bands * w)], axis=-1)
    z = jnp.pad(z, ((0, 0), (0, 128 - HY_EMB)))
    deltas = jnp.abs(jnp.linspace(math.log(HY_TARGET) / HY_SLOW, math.log(HY_TARGET) / HY_FAST, HY_W, dtype=F32))
    decay = jnp.exp(-t * deltas)
    k = jnp.arange(ls, dtype=I32)
    ang = ((k[:, None] * k[None, :]) % (2 * ls)).astype(F32) * (math.pi / ls)
    return z, decay, jnp.cos(ang).astype(BF16), jnp.sin(ang).astype(BF16)


def kernel(x, c, ctx, c_ctx, w_ada, b_ada, norm1_g, norm2_g, w_in, b_gate, a_qnorm, a_knorm, a_sink, hy_conv, hy_fw1, hy_fb1, hy_freq, hy_fw2, hy_fb2, hy_fw3, hy_bias, ml_norm, d_qnorm, d_knorm, d_lq1, d_lk1, d_lq2, d_lk2, d_subnorm, w_out, w_router, w_e_gate, w_e_up, w_e_down):
    B, T, D = x.shape
    S = ctx.shape[1]
    N = T + S
    depth = w_ada.shape[0]
    assert D == D_MODEL and T % S == 0 and S % ROW_BLOCK == 0 and S % MLSTM_CHUNK == 0 and T >= 3 * BLK
    cap_l, cap_c = (EC_CAPACITY * T) // N_EXPERTS, (EC_CAPACITY * S) // N_EXPERTS
    cap_t = cap_l + cap_c
    assert cap_t % 16 == 0

    xs = jnp.concatenate([x, ctx], axis=1)

    rpad = -(B + 1) % 8
    cc = jnp.concatenate([c, c_ctx[None, :], jnp.zeros((rpad, D), F32)], axis=0)
    mods = _ada_mods(cc, w_ada, b_ada)

    cos_a, sin_a = _rope_tables(T, S, DH_A)
    cos_d, sin_d = _rope_tables(T, S, DH_D)
    g64, p64 = _group_mats(DH_A)
    g32, p32 = _group_mats(DH_D)
    tabs_l = _hyena_tables(T)
    tabs_c = _hyena_tables(S)

    offs = np.cumsum((0, 256, 128, 128, 768, 256, 256, 256, 256, 16, 256, 256, 256))
    for l in range(depth):
        lam_init = 0.8 - 0.6 * math.exp(-0.3 * l)
        ml = mods[l]
        mod = jnp.stack([ml[:B].reshape(B, 6, D), jnp.broadcast_to(ml[B].reshape(1, 6, D), (B, 6, D))],
                        axis=1)
        w = w_in[l]
        wp = jnp.concatenate([w[:, offs[0]:offs[8]], w[:, offs[8]:offs[9]], jnp.zeros((D, 112), F32),
                              w[:, offs[9]:offs[12]]], axis=1).astype(BF16)
        gains = jnp.stack([jnp.tile(a_qnorm[l], 4), jnp.tile(a_knorm[l], 4),
                           jnp.tile(d_qnorm[l], 8), jnp.tile(d_knorm[l], 8)], axis=0)
        qa, ka, va, ub, qc, kc, vc, oc, gc, qd, kd, vd = _inproj(
            xs, mod, norm1_g[l].reshape(1, D), wp, (cos_a, sin_a, cos_d, sin_d), gains, (g64, g32, p64, p32), T)

        ya = _wattn(qa, ka, va, jnp.pad(a_sink[l], (0, 128 - H_A)).reshape(1, 128), T, S)
        yd = _dattn(qd, kd, vd, jnp.stack([d_lq1[l], d_lk1[l], d_lq2[l], d_lk2[l]], axis=0),
                    d_subnorm[l].reshape(1, 2 * DH_D), T, S, lam_init)

        bias = jnp.pad(b_gate[l].reshape(-1), (0, 112))
        hf, hb = _mlstm(qc, kc, vc, gc, jnp.swapaxes(gc[:, :, 0:16], 1, 2), bias.reshape(1, 128),
                        bias[0:16].reshape(16, 1), T, S)

        uc = _shortconv(ub, hy_conv[l], T, S)
        w1 = jnp.pad(hy_fw1[l], ((0, 128 - HY_EMB), (0, 0)))
        ybs = []
        for (zf, decay, cm, sm), ls, rowblk in ((tabs_l, T, 0), (tabs_c, S, T // S)):
            hs, hd, ss, ny = _hyfilter(zf, w1, hy_fb1[l].reshape(1, -1), hy_freq[l].reshape(1, -1), hy_fw2[l],
                                       hy_fb2[l].reshape(1, -1), hy_fw3[l], decay)
            hr, hi = _hyspec(cm, sm, hs, hd, ss)
            hny = ny * lax.rsqrt(ss + EPS)
            z1 = _hyconv(cm, sm, uc, 0, rowblk, uc, 1, rowblk, hr[:, 0:HY_W], hi[:, 0:HY_W], hny[:, 0:HY_W],
                         hy_bias[l, 0].reshape(1, HY_W), ls)
            z2 = _hyconv(cm, sm, z1, 0, 0, uc, 2, rowblk, hr[:, HY_W:], hi[:, HY_W:], hny[:, HY_W:],
                         hy_bias[l, 1].reshape(1, HY_W), ls)
            ybs.append(z2)
        yb = jnp.concatenate(ybs, axis=1)

        xs, h2, aff = _outproj(xs, mod, ya, yb, hf, hb, oc, yd, jnp.tile(ml_norm[l], 4).reshape(1, 256), g64,
                               w_out[l].astype(BF16), norm2_g[l].reshape(1, D),
                               jnp.pad(w_router[l], ((0, 0), (0, 128 - N_EXPERTS))), T)

        sel = _topk(jnp.swapaxes(aff[:, :, 0:N_EXPERTS], 1, 2), T, S)
        idx = (sel[..., 0] * 64.0 + sel[..., 1]).astype(I32)
        gate = (sel[..., 2] + sel[..., 3] + sel[..., 4])[..., None]
        gt2 = mod[:, :, 5, :]
        wg, wu, wd = w_e_gate[l].astype(BF16), w_e_up[l].astype(BF16), w_e_down[l].astype(BF16)
        bg = min(IDX_BATCH_GROUP, B)
        outs = []
        for b0 in range(0, B, bg):
            sl = slice(b0, b0 + bg)
            idx_flat = idx[sl].reshape(-1)
            xe = _gather(idx_flat, h2[sl], cap_t)
            ye = _ffn(xe, wg, wu, wd, gate[sl], gt2[sl], cap_l)
            outs.append(_scatter(idx_flat, xs[sl], ye, cap_t))
        xs = jnp.concatenate(outs, axis=0)
    return xs[:, 0:T]
```

```python
import functools
import math

import jax
import jax.numpy as jnp
import numpy as np
from jax import lax
from jax.experimental import pallas as pl
from jax.experimental.pallas import tpu as pltpu

F32 = jnp.float32
BF16 = jnp.bfloat16
I32 = jnp.int32
HI = lax.Precision.HIGHEST

D_MODEL = 1024
GRID_W = 64
GROUP_W = D_MODEL // 4
H_A, HKV_A = 4, 2
G_A = H_A // HKV_A
DH_A = GROUP_W // H_A
WINDOW = 128
BLK = 128
HY_W = GROUP_W
HY_ORDER = 2
HY_BANDS = 16
HY_EMB = 1 + 2 * HY_BANDS
HY_HID = 64
HY_TARGET, HY_FAST, HY_SLOW = 1e-2, 0.3, 1.5
H_C = 4
DH_C = GROUP_W // H_C
H_D = 4
DH_D = GROUP_W // (2 * H_D)
N_EXPERTS = 16
EC_CAPACITY = 2
ROPE_BASE = 10000.0
EPS = 1e-6
NEG = -1e30

ROW_BLOCK = 256
WATTN_QROWS = 256
MLSTM_CHUNK = 256
IDX_BATCH_GROUP = 4
VMEM_LIMIT = 56 * 1024 * 1024

_C_QA, _C_KA, _C_VA, _C_UB = 0, 256, 384, 512
_C_QC, _C_KC, _C_VC, _C_OC, _C_GC = 1280, 1536, 1792, 2048, 2304
_C_QD, _C_KD, _C_VD, _C_END = 2432, 2688, 2944, 3200


def _cparams(sem, vmem=VMEM_LIMIT):
    return pltpu.CompilerParams(dimension_semantics=sem, vmem_limit_bytes=vmem)


def _dot(a, b, **kw):
    return jnp.dot(a, b, preferred_element_type=F32, **kw)


def _dot_nt(a, b, **kw):
    return lax.dot_general(a, b, (((1,), (1,)), ((), ())), preferred_element_type=F32, **kw)


def _dot_tn(a, b, **kw):
    return lax.dot_general(a, b, (((0,), (0,)), ((), ())), preferred_element_type=F32, **kw)


def _pack_bf16_pair(a, b):
    ua = lax.bitcast_convert_type(a.astype(F32), I32)
    ub = lax.bitcast_convert_type(b.astype(F32), I32)
    return ua | lax.shift_right_logical(ub, 16)


def _unpack_bf16_pair(w):
    a = lax.bitcast_convert_type(w & jnp.int32(-65536), F32)
    b = lax.bitcast_convert_type(lax.shift_left(w, 16), F32)
    return a.astype(BF16), b.astype(BF16)


def _ada_kernel(c_ref, w_ref, b_ref, o_ref):
    c = c_ref[...]
    s = c * jax.nn.sigmoid(c)
    o_ref[0] = _dot(s, w_ref[0], precision=HI) + b_ref[0]


def _ada_mods(cc, w_ada, b_ada):
    L, D, W6 = w_ada.shape
    R = cc.shape[0]
    cb = 1536
    return pl.pallas_call(
        _ada_kernel,
        grid=(L, W6 // cb),
        in_specs=[pl.BlockSpec((R, D), lambda l, j: (0, 0)),
                  pl.BlockSpec((1, D, cb), lambda l, j: (l, 0, j)),
                  pl.BlockSpec((1, 1, cb), lambda l, j: (l, 0, j))],
        out_specs=pl.BlockSpec((1, R, cb), lambda l, j: (l, 0, j)),
        out_shape=jax.ShapeDtypeStruct((L, R, W6), F32),
        compiler_params=_cparams(("arbitrary", "arbitrary")),
        name="ada_mods",
    )(cc, w_ada, b_ada.reshape(L, 1, W6))


def _inproj_kernel(x_ref, mod_ref, g1_ref, w_ref, ca_ref, sa_ref, cd_ref, sd_ref,
                   gains_ref, g64_ref, g32_ref, p64_ref, p32_ref,
                   qa, ka, va, ub, qc, kc, vc, oc, gc, qd, kd, vd, gct):
    x = x_ref[0]
    ms = jnp.mean(x * x, axis=-1, keepdims=True)
    xn = x * lax.rsqrt(ms + EPS) * g1_ref[...]
    mod = mod_ref[0, 0]
    h = xn * (1.0 + mod[1:2, :]) + mod[0:1, :]
    p = _dot(h.astype(BF16), w_ref[...])

    def headnorm_rope(t, gmat, gain, pmat, cos, sin, scale):
        w = t.shape[1]
        msq = _dot((t * t).astype(BF16), gmat[0:w, 0:w])
        tn = t * lax.rsqrt(msq + EPS) * gain
        tn = tn * cos[:, 0:w] + _dot(tn.astype(BF16), pmat[0:w, 0:w]) * sin[:, 0:w]
        return tn * scale

    ca, sa, cd, sd = ca_ref[...], sa_ref[...], cd_ref[...], sd_ref[...]
    gains = gains_ref[...]
    qa[0] = headnorm_rope(p[:, _C_QA:_C_KA], g64_ref, gains[0:1, :], p64_ref, ca, sa, DH_A ** -0.5).astype(BF16)
    ka[0] = headnorm_rope(p[:, _C_KA:_C_VA], g64_ref, gains[1:2, 0:128], p64_ref, ca, sa, 1.0).astype(BF16)
    va[0] = p[:, _C_VA:_C_UB].astype(BF16)
    ub[0] = p[:, _C_UB:_C_QC].astype(BF16)
    qc[0] = p[:, _C_QC:_C_KC].astype(BF16)
    kc[0] = (p[:, _C_KC:_C_VC] * DH_C ** -0.5).astype(BF16)
    vc[0] = p[:, _C_VC:_C_OC].astype(BF16)
    oc[0] = p[:, _C_OC:_C_GC].astype(BF16)
    gc[0] = p[:, _C_GC:_C_QD]
    gct[0] = p[:, _C_GC:_C_QD].T
    qd[0] = headnorm_rope(p[:, _C_QD:_C_KD], g32_ref, gains[2:3, :], p32_ref, cd, sd, DH_D ** -0.5).astype(BF16)
    kd[0] = headnorm_rope(p[:, _C_KD:_C_VD], g32_ref, gains[3:4, :], p32_ref, cd, sd, 1.0).astype(BF16)
    vd[0] = p[:, _C_VD:_C_END].astype(BF16)


def _inproj(x, mod, g1, w_packed, tabs, gains, mats, T):
    B, N, D = x.shape
    rb = ROW_BLOCK
    nlat = T // rb
    row = lambda w: pl.BlockSpec((1, rb, w), lambda b, r: (b, r, 0))
    tab = pl.BlockSpec((rb, 256), lambda b, r: (r, 0))
    full = lambda a: pl.BlockSpec(a.shape, lambda b, r: (0,) * a.ndim)
    widths = [256, 128, 128, 768, 256, 256, 256, 256, 128, 256, 256, 256]
    dtypes = [BF16] * 8 + [F32] + [BF16] * 3
    return pl.pallas_call(
        _inproj_kernel,
        grid=(B, N // rb),
        in_specs=[row(D),
                  pl.BlockSpec((1, 1, 6, D), lambda b, r: (b, jnp.where(r >= nlat, 1, 0), 0, 0)),
                  full(g1), full(w_packed), tab, tab, tab, tab, full(gains)] + [full(m) for m in mats],
        out_specs=[row(w) for w in widths] + [pl.BlockSpec((1, 128, rb), lambda b, r: (b, 0, r))],
        out_shape=[jax.ShapeDtypeStruct((B, N, w), dt) for w, dt in zip(widths, dtypes)]
        + [jax.ShapeDtypeStruct((B, 128, N), F32)],
        compiler_params=_cparams(("arbitrary", "arbitrary")),
        name="inproj",
    )(x, mod, g1, w_packed, *tabs, gains, *mats)


def _wattn_kernel(q_ref, k_ref, v_ref, sink_ref, o_ref, *, T, S):
    j = pl.program_id(1)
    qb, kwin = WATTN_QROWS, WATTN_QROWS + 2 * BLK
    is_lat = j < T // qb
    start = pl.multiple_of(jnp.clip(j * qb - BLK, 0, T - kwin), BLK)
    kw = k_ref[0, pl.ds(start, kwin), :]
    vw = v_ref[0, pl.ds(start, kwin), :]
    kc = k_ref[0, T:T + S, :]
    vc = v_ref[0, T:T + S, :]
    row = lax.broadcasted_iota(I32, (G_A * qb, kwin), 0)
    qpos = j * qb + (row & (qb - 1))
    kpos = start + lax.broadcasted_iota(I32, (G_A * qb, kwin), 1)
    valid = (jnp.abs(qpos - kpos) <= WINDOW) & is_lat
    grp = jnp.right_shift(lax.broadcasted_iota(I32, (G_A * qb, 1), 0), qb.bit_length() - 1)
    sinks = sink_ref[...]
    for hk in range(HKV_A):
        cs = slice(hk * DH_A, (hk + 1) * DH_A)
        kwh, vwh, kch, vch = kw[:, cs], vw[:, cs], kc[:, cs], vc[:, cs]
        heads = [hk * G_A + g for g in range(G_A)]
        q = jnp.concatenate([q_ref[0, :, hd * DH_A:(hd + 1) * DH_A] for hd in heads], axis=0)
        sink = sinks[0:1, heads[0]:heads[0] + 1]
        for g in range(1, G_A):
            sink = jnp.where(grp == g, sinks[0:1, heads[g]:heads[g] + 1], sink)
        s_loc = jnp.where(valid, _dot_nt(q, kwh), NEG)
        s_ctx = _dot_nt(q, kch)
        m = jnp.maximum(jnp.maximum(jnp.max(s_loc, axis=-1, keepdims=True),
                                    jnp.max(s_ctx, axis=-1, keepdims=True)), sink)
        p_loc = jnp.exp(s_loc - m)
        p_ctx = jnp.exp(s_ctx - m)
        den = (jnp.sum(p_loc, axis=-1, keepdims=True) + jnp.sum(p_ctx, axis=-1, keepdims=True)
               + jnp.exp(sink - m))
        o = (_dot(p_loc.astype(BF16), vwh) + _dot(p_ctx.astype(BF16), vch)) / den
        for g, hd in enumerate(heads):
            o_ref[0, :, hd * DH_A:(hd + 1) * DH_A] = o[g * qb:(g + 1) * qb].astype(BF16)


def _wattn(qa, ka, va, sink, T, S):
    B, N, _ = qa.shape
    qb = WATTN_QROWS
    return pl.pallas_call(
        functools.partial(_wattn_kernel, T=T, S=S),
        grid=(B, N // qb),
        in_specs=[pl.BlockSpec((1, qb, 256), lambda b, j: (b, j, 0)),
                  pl.BlockSpec((1, N, 128), lambda b, j: (b, 0, 0)),
                  pl.BlockSpec((1, N, 128), lambda b, j: (b, 0, 0)),
                  pl.BlockSpec((1, 128), lambda b, j: (0, 0))],
        out_specs=pl.BlockSpec((1, qb, 256), lambda b, j: (b, j, 0)),
        out_shape=jax.ShapeDtypeStruct((B, N, 256), BF16),
        compiler_params=_cparams(("arbitrary", "arbitrary")),
        name="window_attn",
    )(qa, ka, va, sink)


def _dattn_kernel(q_ref, k_ref, v_ref, lam_ref, sub_ref, o_ref, vaug_ref, *, T, S, lam_init):
    j = pl.program_id(1)
    lv = lam_ref[...]
    lam = (jnp.exp(jnp.sum(lv[0:1, :] * lv[1:2, :], axis=-1, keepdims=True))
           - jnp.exp(jnp.sum(lv[2:3, :] * lv[3:4, :], axis=-1, keepdims=True)) + lam_init)
    subg = sub_ref[...] * (1.0 - lam_init)
    dv = 2 * DH_D

    @pl.when(j == 0)
    def _():
        ones = jnp.where(lax.broadcasted_iota(I32, (T + S, dv), 1) == 0, 1.0, 0.0).astype(BF16)
        for hd in range(H_D):
            vaug_ref[:, hd * 2 * dv:(hd + 1) * 2 * dv] = jnp.concatenate(
                [v_ref[0, :, hd * dv:(hd + 1) * dv], ones], axis=1)

    def body(k0, nk):
        for hd in range(H_D):
            terms = []
            for c in range(2):
                cs = slice(hd * dv + c * DH_D, hd * dv + (c + 1) * DH_D)
                s = _dot_nt(q_ref[0, :, cs], k_ref[0, k0:k0 + nk, cs])
                e = jnp.exp((s - jnp.max(s, axis=-1, keepdims=True)).astype(BF16))
                pv = _dot(e, vaug_ref[k0:k0 + nk, hd * 2 * dv:(hd + 1) * 2 * dv])
                terms.append(pv[:, 0:dv] / pv[:, dv:dv + 1])
            o = terms[0] - lam * terms[1]
            ms = jnp.mean(o * o, axis=-1, keepdims=True)
            o_ref[0, :, hd * dv:(hd + 1) * dv] = (o * lax.rsqrt(ms + EPS) * subg).astype(BF16)

    @pl.when(j < T // ROW_BLOCK)
    def _():
        body(0, T + S)

    @pl.when(j >= T // ROW_BLOCK)
    def _():
        body(T, S)


def _dattn(qd, kd, vd, lam_vecs, subg, T, S, lam_init):
    B, N, _ = qd.shape
    rb = ROW_BLOCK
    return pl.pallas_call(
        functools.partial(_dattn_kernel, T=T, S=S, lam_init=lam_init),
        grid=(B, N // rb),
        in_specs=[pl.BlockSpec((1, rb, 256), lambda b, j: (b, j, 0)),
                  pl.BlockSpec((1, N, 256), lambda b, j: (b, 0, 0)),
                  pl.BlockSpec((1, N, 256), lambda b, j: (b, 0, 0)),
                  pl.BlockSpec((4, DH_D), lambda b, j: (0, 0)),
                  pl.BlockSpec((1, 2 * DH_D), lambda b, j: (0, 0))],
        out_specs=pl.BlockSpec((1, rb, 256), lambda b, j: (b, j, 0)),
        out_shape=jax.ShapeDtypeStruct((B, N, 256), BF16),
        scratch_shapes=[pltpu.VMEM((N, 2 * 256), BF16)],
        compiler_params=_cparams(("arbitrary", "arbitrary")),
        name="diff_attn",
    )(qd, kd, vd, lam_vecs, subg)


def _log_sigmoid(x):
    return jnp.minimum(x, 0.0) - jnp.log1p(jnp.exp(-jnp.abs(x)))


def _mlstm_kernel(qf_ref, kf_ref, vf_ref, gf_ref, gtf_ref, qb_ref, kb_ref, vb_ref, gb_ref, gtb_ref,
                  bias_ref, biast_ref, hf_ref, hb_ref, ct_ref, m_ref):
    i = pl.program_id(1)
    lc = MLSTM_CHUNK

    @pl.when(i == 0)
    def _():
        ct_ref[...] = jnp.zeros_like(ct_ref)
        m_ref[...] = jnp.zeros_like(m_ref)

    r_io = lax.broadcasted_iota(I32, (lc, lc), 0)
    c_io = lax.broadcasted_iota(I32, (lc, lc), 1)
    lower = r_io >= c_io
    upper = r_io <= c_io
    tri_lo = jnp.where(lower, 1.0, 0.0).astype(F32)
    tri_up = jnp.where(upper, 1.0, 0.0).astype(F32)
    ones_col = jnp.where(lax.broadcasted_iota(I32, (lc, DH_C), 1) == 0, 1.0, 0.0).astype(BF16)

    def direction(d, q_ref, k_ref, v_ref, g_ref, gt_ref, out_ref):
        pre = g_ref[0] + bias_ref[...]
        pre_t = gt_ref[0] + biast_ref[...]
        lf = _log_sigmoid(pre)
        lf_t = _log_sigmoid(pre_t)
        if d == 0:
            bcol = _dot(tri_lo, lf, precision=HI)
            brow = _dot(lf_t, tri_up, precision=HI)
            valid, end_row = lower, lc - 1
        else:
            bcol = _dot(tri_up, lf, precision=HI)
            brow = _dot(lf_t, tri_lo, precision=HI)
            valid, end_row = upper, 0
        for hd in range(H_C):
            ci, cf = d * H_C + hd, (2 + d) * H_C + hd
            cs = slice(hd * DH_C, (hd + 1) * DH_C)
            q, k, v = q_ref[0, :, cs], k_ref[0, :, cs], v_ref[0, :, cs]
            v_aug = jnp.concatenate([v, ones_col], axis=1)
            b_t = bcol[:, cf:cf + 1]
            ig = pre[:, ci:ci + 1]
            b_end = bcol[end_row:end_row + 1, cf:cf + 1]
            slot = d * H_C + hd
            ct = ct_ref[slot]
            m = m_ref[slot][:, 0:1]
            logw = jnp.where(valid, b_t - (brow[cf:cf + 1, :] - pre_t[ci:ci + 1, :]), NEG)
            inter = b_t + m
            m_t = jnp.maximum(inter, jnp.max(logw, axis=-1, keepdims=True))
            w_in = jnp.exp(inter - m_t)
            smat = _dot_nt(q, k) * jnp.exp(logw - m_t)
            num = w_in * _dot(q, ct.astype(BF16)) + _dot(smat.astype(BF16), v_aug)
            den = jnp.maximum(jnp.abs(num[:, DH_C:DH_C + 1]), jnp.exp(-m_t))
            out_ref[0, :, cs] = num[:, 0:DH_C] / den
            g_end = b_end - b_t + ig
            m_new = jnp.maximum(b_end + m, jnp.max(g_end, axis=0, keepdims=True))
            w_tok = jnp.exp(g_end - m_new)
            wv = (w_tok * v_aug.astype(F32)).astype(BF16)
            ct_ref[slot] = jnp.exp(b_end + m - m_new) * ct + _dot_tn(k, wv)
            m_ref[slot] = jnp.broadcast_to(m_new, (1, 128))

    direction(0, qf_ref, kf_ref, vf_ref, gf_ref, gtf_ref, hf_ref)
    direction(1, qb_ref, kb_ref, vb_ref, gb_ref, gtb_ref, hb_ref)


def _mlstm(qc, kc, vc, gc, gct, bias, biast, T, S):
    B, N, _ = qc.shape
    lc = MLSTM_CHUNK
    nch, nlat = N // lc, T // lc
    fwd = lambda i: jnp.where(i < nch - nlat, nlat + i, i - (nch - nlat))
    bwd = lambda i: nch - 1 - i
    tok = lambda f: pl.BlockSpec((1, lc, 256), lambda b, i: (b, f(i), 0))
    gate = lambda f: pl.BlockSpec((1, lc, 128), lambda b, i: (b, f(i), 0))
    gate_t = lambda f: pl.BlockSpec((1, 16, lc), lambda b, i: (b, 0, f(i)))
    return pl.pallas_call(
        _mlstm_kernel,
        grid=(B, nch),
        in_specs=[tok(fwd), tok(fwd), tok(fwd), gate(fwd), gate_t(fwd),
                  tok(bwd), tok(bwd), tok(bwd), gate(bwd), gate_t(bwd),
                  pl.BlockSpec((1, 128), lambda b, i: (0, 0)),
                  pl.BlockSpec((16, 1), lambda b, i: (0, 0))],
        out_specs=[tok(fwd), tok(bwd)],
        out_shape=[jax.ShapeDtypeStruct((B, N, 256), F32)] * 2,
        scratch_shapes=[pltpu.VMEM((2 * H_C, DH_C, 128), F32), pltpu.VMEM((2 * H_C, 1, 128), F32)],
        compiler_params=_cparams(("arbitrary", "arbitrary")),
        name="mlstm",
    )(qc, kc, vc, gc, gct, qc, kc, vc, gc, gct, bias, biast)


def _shortconv_kernel(u_ref, w_ref, o_ref, *, T, S):
    u = u_ref[0].astype(F32)
    n = u.shape[0]
    w = w_ref[...]
    row = lax.broadcasted_iota(I32, u.shape, 0)
    zero = jnp.zeros((1, u.shape[1]), F32)
    prev = jnp.concatenate([zero, u[0:n - 1]], axis=0)
    nxt = jnp.concatenate([u[1:n], zero], axis=0)
    prev = jnp.where(row == T, 0.0, prev)
    nxt = jnp.where(row == T - 1, 0.0, nxt)
    o_ref[0] = (w[0:1, :] * prev + w[1:2, :] * u + w[2:3, :] * nxt).astype(BF16)


def _shortconv(ub, conv_w, T, S):
    B, N, W = ub.shape
    return pl.pallas_call(
        functools.partial(_shortconv_kernel, T=T, S=S),
        grid=(B, W // 256),
        in_specs=[pl.BlockSpec((1, N, 256), lambda b, j: (b, 0, j)),
                  pl.BlockSpec((3, 256), lambda b, j: (0, j))],
        out_specs=pl.BlockSpec((1, N, 256), lambda b, j: (b, 0, j)),
        out_shape=jax.ShapeDtypeStruct((B, N, W), BF16),
        compiler_params=_cparams(("arbitrary", "arbitrary")),
        name="hyena_shortconv",
    )(ub, conv_w)


def _hyfilter_kernel(z_ref, w1_ref, b1_ref, fr_ref, w2_ref, b2_ref, w3_ref, dec_ref,
                     hs_ref, hd_ref, ss_ref, ny_ref):
    i = pl.program_id(0)
    fr = fr_ref[...]
    h = jnp.sin(fr * (_dot(z_ref[...], w1_ref[...], precision=HI) + b1_ref[...]))
    h = jnp.sin(fr * (_dot(h, w2_ref[...], precision=HI) + b2_ref[...]))
    h = _dot(h, w3_ref[...], precision=HI)
    dec = dec_ref[...]
    dec2 = jnp.concatenate([dec, dec], axis=1)
    hw = HY_ORDER * HY_W
    h0 = h[:, 0:hw] * dec2
    h1 = h[:, hw:2 * hw] * dec2
    rows = h0.shape[0]
    t = i * rows + lax.broadcasted_iota(I32, h0.shape, 0)
    h1 = jnp.where(t == 0, 0.0, h1)
    hsum = h0 + h1
    hs_ref[...] = hsum.astype(BF16)
    hd_ref[...] = (h1 - h0).astype(BF16)
    alt = jnp.where((t & 1) == 0, 1.0, -1.0)

    @pl.when(i == 0)
    def _():
        ss_ref[...] = jnp.zeros_like(ss_ref)
        ny_ref[...] = jnp.zeros_like(ny_ref)

    ss_ref[...] += jnp.sum(h0 * h0 + h1 * h1, axis=0, keepdims=True)
    ny_ref[...] += jnp.sum(hsum * alt, axis=0, keepdims=True)


def _hyfilter(zfeat, w1, b1, fr, w2, b2, w3, decay):
    ls = zfeat.shape[0]
    rb = min(ls, 512)
    hw = HY_ORDER * HY_W
    full = lambda a: pl.BlockSpec(a.shape, lambda i: (0,) * a.ndim)
    return pl.pallas_call(
        _hyfilter_kernel,
        grid=(ls // rb,),
        in_specs=[pl.BlockSpec((rb, 128), lambda i: (i, 0)), full(w1), full(b1), full(fr), full(w2),
                  full(b2), full(w3), pl.BlockSpec((rb, HY_W), lambda i: (i, 0))],
        out_specs=[pl.BlockSpec((rb, hw), lambda i: (i, 0)), pl.BlockSpec((rb, hw), lambda i: (i, 0)),
                   pl.BlockSpec((1, hw), lambda i: (0, 0)), pl.BlockSpec((1, hw), lambda i: (0, 0))],
        out_shape=[jax.ShapeDtypeStruct((ls, hw), BF16), jax.ShapeDtypeStruct((ls, hw), BF16),
                   jax.ShapeDtypeStruct((1, hw), F32), jax.ShapeDtypeStruct((1, hw), F32)],
        compiler_params=_cparams(("arbitrary",)),
        name="hyena_filter",
    )(zfeat, w1, b1, fr, w2, b2, w3, decay)


def _hyspec_kernel(cm_ref, sm_ref, hs_ref, hd_ref, ss_ref, hr_ref, hi_ref):
    scale = lax.rsqrt(ss_ref[...] + EPS)
    hr_ref[...] = _dot(cm_ref[...], hs_ref[...]) * scale
    hi_ref[...] = _dot(sm_ref[...], hd_ref[...]) * scale


def _hyspec(cm, sm, hs, hd, ss):
    ls = cm.shape[0]
    tk = min(ls, 512)
    hw = HY_ORDER * HY_W
    full = lambda a: pl.BlockSpec(a.shape, lambda i: (0,) * a.ndim)
    return pl.pallas_call(
        _hyspec_kernel,
        grid=(ls // tk,),
        in_specs=[pl.BlockSpec((tk, ls), lambda i: (i, 0)), pl.BlockSpec((tk, ls), lambda i: (i, 0)),
                  full(hs), full(hd), full(ss)],
        out_specs=[pl.BlockSpec((tk, hw), lambda i: (i, 0))] * 2,
        out_shape=[jax.ShapeDtypeStruct((ls, hw), F32)] * 2,
        compiler_params=_cparams(("arbitrary",)),
        name="hyena_spectrum",
    )(cm, sm, hs, hd, ss)


def _hyconv_kernel(cm_ref, sm_ref, z_ref, gate_ref, hr_ref, hi_ref, hny_ref, bias_ref, o_ref,
                   yr_ref, yi_ref, zny_ref, *, ls):
    ph = pl.program_id(1)
    j = pl.program_id(2)
    tk = cm_ref.shape[0]
    inv_n = 1.0 / (2 * ls)
    rows = pl.ds(pl.multiple_of(j * tk, tk), tk)

    @pl.when(ph == 0)
    def _():
        z = z_ref[0]
        zr = _dot(cm_ref[...], z)
        zi = -_dot(sm_ref[...], z)
        hr, hi = hr_ref[...], hi_ref[...]
        k = j * tk + lax.broadcasted_iota(I32, (tk, 1), 0)
        wk = jnp.where(k == 0, inv_n, 2.0 * inv_n)
        yr_ref[rows, :] = ((zr * hr - zi * hi) * wk).astype(BF16)
        yi_ref[rows, :] = ((zr * hi + zi * hr) * wk).astype(BF16)

        @pl.when(j == 0)
        def _():
            t = lax.broadcasted_iota(I32, (ls, 1), 0)
            alt = jnp.where((t & 1) == 0, 1.0, -1.0)
            zny_ref[...] = jnp.sum(z.astype(F32) * alt, axis=0, keepdims=True)

    @pl.when(ph == 1)
    def _():
        y = _dot(cm_ref[...], yr_ref[...]) - _dot(sm_ref[...], yi_ref[...])
        t = j * tk + lax.broadcasted_iota(I32, (tk, 1), 0)
        alt = jnp.where((t & 1) == 0, 1.0, -1.0)
        y = y + alt * (zny_ref[...] * hny_ref[...] * inv_n)
        zt = z_ref[0, rows, :].astype(F32)
        o_ref[0] = (gate_ref[0].astype(F32) * (y + zt * bias_ref[...])).astype(BF16)


def _hyconv(cm, sm, zsrc, zcol, zrow, gsrc, gcol, grow, hr, hi, hny, bias, ls):
    B = zsrc.shape[0]
    tk = min(ls, 512)
    nt = ls // tk
    return pl.pallas_call(
        functools.partial(_hyconv_kernel, ls=ls),
        grid=(B, 2, nt),
        in_specs=[pl.BlockSpec((tk, ls), lambda b, p, j: (j, 0)),
                  pl.BlockSpec((tk, ls), lambda b, p, j: (j, 0)),
                  pl.BlockSpec((1, ls, HY_W), lambda b, p, j: (b, zrow, zcol)),
                  pl.BlockSpec((1, tk, HY_W), lambda b, p, j: (b, grow * nt + j * p, gcol)),
                  pl.BlockSpec((tk, HY_W), lambda b, p, j: (j * (1 - p), 0)),
                  pl.BlockSpec((tk, HY_W), lambda b, p, j: (j * (1 - p), 0)),
                  pl.BlockSpec((1, HY_W), lambda b, p, j: (0, 0)),
                  pl.BlockSpec((1, HY_W), lambda b, p, j: (0, 0))],
        out_specs=pl.BlockSpec((1, tk, HY_W), lambda b, p, j: (b, j * p, 0)),
        out_shape=jax.ShapeDtypeStruct((B, ls, HY_W), BF16),
        scratch_shapes=[pltpu.VMEM((ls, HY_W), BF16), pltpu.VMEM((ls, HY_W), BF16),
                        pltpu.VMEM((1, HY_W), F32)],
        compiler_params=_cparams(("arbitrary", "arbitrary", "arbitrary")),
        name="hyena_longconv",
    )(cm, sm, zsrc, gsrc, hr, hi, hny, bias)


def _outproj_kernel(x_ref, mod_ref, ya_ref, ybl_ref, ybc_ref, hf_ref, hb_ref, oc_ref, yd_ref, mlg_ref, g64_ref,
                    w_ref, g2_ref, wr_ref, xo_ref, h2_ref, aff_ref, *, nlat):
    hsum = hf_ref[0] + hb_ref[0]
    msq = _dot((hsum * hsum).astype(BF16), g64_ref[...])
    yc = jax.nn.sigmoid(oc_ref[0].astype(F32)) * (hsum * lax.rsqrt(msq + EPS) * mlg_ref[...])
    yb = jnp.where(pl.program_id(1) < nlat, ybl_ref[0], ybc_ref[0])
    y = jnp.concatenate([ya_ref[0], yb, yc.astype(BF16), yd_ref[0]], axis=1)
    mod = mod_ref[0, 0]
    x = x_ref[0] + mod[2:3, :] * _dot(y, w_ref[...])
    xo_ref[0] = x
    ms = jnp.mean(x * x, axis=-1, keepdims=True)
    h2 = x * lax.rsqrt(ms + EPS) * g2_ref[...] * (1.0 + mod[4:5, :]) + mod[3:4, :]
    h2_hi = h2.astype(BF16)
    h2_ref[0] = _pack_bf16_pair(h2_hi[:, 0:D_MODEL // 2], h2_hi[:, D_MODEL // 2:])
    h2_lo = (h2 - h2_hi.astype(F32)).astype(BF16)
    lg = _dot(h2_hi, wr_ref[...])
    logits = lg[:, 0:128] + lg[:, 128:256] + _dot(h2_lo, wr_ref[:, 0:128])
    lane = lax.broadcasted_iota(I32, logits.shape, 1)
    logits = jnp.where(lane < N_EXPERTS, logits, NEG)
    e = jnp.exp(logits - jnp.max(logits, axis=-1, keepdims=True))
    aff_ref[0] = (e / jnp.sum(e, axis=-1, keepdims=True)).T


def _outproj(x, mod, ya, yb_l, yb_c, hf, hb, oc, yd, mlg, g64, w_out, g2, w_router, T):
    B, N, D = x.shape
    rb = ROW_BLOCK
    nlat = T // rb
    row = lambda w: pl.BlockSpec((1, rb, w), lambda b, r: (b, r, 0))
    full = lambda a: pl.BlockSpec(a.shape, lambda b, r: (0,) * a.ndim)
    return pl.pallas_call(
        functools.partial(_outproj_kernel, nlat=nlat),
        grid=(B, N // rb),
        in_specs=[row(D), pl.BlockSpec((1, 1, 6, D), lambda b, r: (b, jnp.where(r >= nlat, 1, 0), 0, 0)),
                  row(256),
                  pl.BlockSpec((1, rb, 256), lambda b, r: (b, jnp.minimum(r, nlat - 1), 0)),
                  pl.BlockSpec((1, rb, 256), lambda b, r: (b, jnp.maximum(r - nlat, 0), 0)),
                  row(256), row(256), row(256), row(256),
                  full(mlg), full(g64), full(w_out), full(g2), full(w_router)],
        out_specs=[row(D), row(D // 2), pl.BlockSpec((1, 128, rb), lambda b, r: (b, 0, r))],
        out_shape=[jax.ShapeDtypeStruct((B, N, D), F32), jax.ShapeDtypeStruct((B, N, D // 2), I32),
                   jax.ShapeDtypeStruct((B, 128, N), F32)],
        compiler_params=_cparams(("arbitrary", "arbitrary")),
        name="outproj",
    )(x, mod, ya, yb_l, yb_c, hf, hb, oc, yd, mlg, g64, w_out, g2, w_router)


def _prefix_exclusive(x):
    n = x.shape[1]
    lane = lax.broadcasted_iota(I32, x.shape, 1)
    inc = x
    d = 1
    while d < n:
        inc = inc + jnp.where(lane >= d, pltpu.roll(inc, d, axis=1), 0)
        d *= 2
    return inc - x


def _topk_kernel(aff_ref, o_ref, pos_ref, val_ref, *, T, S):
    def segment(t0, ts, cap, slot0):
        a = aff_ref[0, :, t0:t0 + ts]
        bits = lax.bitcast_convert_type(a, I32)

        def search(i, v):
            cand = v | jnp.left_shift(jnp.int32(1), 30 - i)
            cnt = jnp.sum((bits >= cand).astype(I32), axis=1, keepdims=True)
            return jnp.where(cnt >= cap, cand, v)

        thr = lax.fori_loop(0, 31, search, jnp.zeros((N_EXPERTS, 1), I32))
        gt = bits > thr
        eq = bits == thr
        need = cap - jnp.sum(gt.astype(I32), axis=1, keepdims=True)
        sel = gt | (eq & (_prefix_exclusive(eq.astype(I32)) < need))
        seli = sel.astype(I32)
        pos_ref[:, 0:ts] = jnp.where(sel, _prefix_exclusive(seli), -1)
        a_hi = a.astype(BF16)
        r1 = a - a_hi.astype(F32)
        a_mid = r1.astype(BF16)
        a_lo = (r1 - a_mid.astype(F32)).astype(BF16)
        val_ref[0, :, 0:ts] = a_hi.astype(F32)
        val_ref[1, :, 0:ts] = a_mid.astype(F32)
        val_ref[2, :, 0:ts] = a_lo.astype(F32)
        tabs = lax.broadcasted_iota(I32, (1, ts), 1) + t0
        t_hi = jnp.right_shift(tabs, 6).astype(F32)
        t_lo = (tabs & 63).astype(F32)
        slot = lax.broadcasted_iota(I32, (cap, ts), 0)
        zeros = jnp.zeros((11, ts), F32)

        def per_expert(e, _):
            onehot = jnp.where(pos_ref[pl.ds(e, 1), 0:ts] == slot, 1.0, 0.0).astype(BF16)
            vals = jnp.concatenate([t_hi, t_lo, val_ref[0, pl.ds(e, 1), 0:ts], val_ref[1, pl.ds(e, 1), 0:ts],
                                    val_ref[2, pl.ds(e, 1), 0:ts], zeros], axis=0).astype(BF16)
            o_ref[0, e, slot0:slot0 + cap, :] = _dot_nt(onehot, vals)
            return 0

        lax.fori_loop(0, N_EXPERTS, per_expert, 0)

    segment(0, T, (EC_CAPACITY * T) // N_EXPERTS, 0)
    segment(T, S, (EC_CAPACITY * S) // N_EXPERTS, (EC_CAPACITY * T) // N_EXPERTS)


def _topk(aff_t, T, S):
    B, _, N = aff_t.shape
    E = N_EXPERTS
    cap_t = (EC_CAPACITY * T) // E + (EC_CAPACITY * S) // E
    return pl.pallas_call(
        functools.partial(_topk_kernel, T=T, S=S),
        grid=(B,),
        in_specs=[pl.BlockSpec((1, E, N), lambda b: (b, 0, 0))],
        out_specs=pl.BlockSpec((1, E, cap_t, 16), lambda b: (b, 0, 0, 0)),
        out_shape=jax.ShapeDtypeStruct((B, E, cap_t, 16), F32),
        scratch_shapes=[pltpu.VMEM((E, T), I32), pltpu.VMEM((3, E, T), F32)],
        compiler_params=_cparams(("arbitrary",)),
        name="expert_topk",
    )(aff_t)


def _gather_kernel(idx_ref, h_ref, *rest, cap_t):
    o_ref = rest[-1]
    b, e = pl.program_id(0), pl.program_id(1)
    base = (b * N_EXPERTS + e) * cap_t

    def body(g, _):
        rows = [h_ref[0, pl.ds(idx_ref[base + g * 16 + k], 1), :] for k in range(16)]
        o_ref[0, 0, pl.ds(pl.multiple_of(g * 16, 16), 16), :] = jnp.concatenate(rows, axis=0)
        return 0

    lax.fori_loop(0, cap_t // 16, body, 0)


def _gather(idx_flat, h2p, xe_buf, b0, bg, cap_t):
    B, N, hw = h2p.shape
    in_specs = [pl.BlockSpec((1, N, hw), lambda b, e, idx: (b0 + b, 0, 0))]
    args = [idx_flat, h2p]
    aliases = {}
    if xe_buf is not None:
        in_specs.append(pl.BlockSpec(memory_space=pl.ANY))
        args.append(xe_buf)
        aliases = {2: 0}
    return pl.pallas_call(
        functools.partial(_gather_kernel, cap_t=cap_t),
        grid_spec=pltpu.PrefetchScalarGridSpec(
            num_scalar_prefetch=1,
            grid=(bg, N_EXPERTS),
            in_specs=in_specs,
            out_specs=pl.BlockSpec((1, 1, cap_t, hw), lambda b, e, idx: (b0 + b, e, 0, 0)),
        ),
        out_shape=jax.ShapeDtypeStruct((B, N_EXPERTS, cap_t, hw), I32),
        input_output_aliases=aliases,
        compiler_params=_cparams(("arbitrary", "arbitrary")),
        name="expert_gather",
    )(*args)


def _ffn_kernel(x_ref, wg_ref, wu_ref, wd_ref, gate_ref, gt_ref, o_ref, wg_s, wu_s, wd_s, *, cap_l):
    @pl.when(pl.program_id(1) == 0)
    def _():
        wg_s[...] = wg_ref[0, 0].astype(BF16)
        wu_s[...] = wu_ref[0, 0].astype(BF16)
        wd_s[...] = wd_ref[0, 0].astype(BF16)

    x = jnp.concatenate(_unpack_bf16_pair(x_ref[0, 0]), axis=1)
    a = _dot(x, wg_s[...])
    u = _dot(x, wu_s[...])
    hmid = (a * jax.nn.sigmoid(a) * u).astype(BF16)
    y = _dot(hmid, wd_s[...])
    gt = gt_ref[0]
    row = lax.broadcasted_iota(I32, y.shape, 0)
    gt2 = jnp.where(row < cap_l, gt[0:1, :], gt[1:2, :])
    o_ref[0, 0] = y * gate_ref[0, 0] * gt2


def _ffn(xe, layer, wg, wu, wd, gate, gt2, cap_l):
    B, E, cap_t, hw = xe.shape
    D = 2 * hw
    wspec = pl.BlockSpec((1, 1, D, D), lambda e, b: (layer, e, 0, 0))
    return pl.pallas_call(
        functools.partial(_ffn_kernel, cap_l=cap_l),
        grid=(E, B),
        in_specs=[pl.BlockSpec((1, 1, cap_t, hw), lambda e, b: (b, e, 0, 0)), wspec, wspec, wspec,
                  pl.BlockSpec((1, 1, cap_t, 1), lambda e, b: (b, e, 0, 0)),
                  pl.BlockSpec((1, 2, D), lambda e, b: (b, 0, 0))],
        out_specs=pl.BlockSpec((1, 1, cap_t, D), lambda e, b: (b, e, 0, 0)),
        out_shape=jax.ShapeDtypeStruct((B, E, cap_t, D), F32),
        scratch_shapes=[pltpu.VMEM((D, D), BF16)] * 3,
        compiler_params=_cparams(("arbitrary", "arbitrary")),
        name="expert_ffn",
    )(xe, wg, wu, wd, gate, gt2)


def _scatter_kernel(idx_ref, x_ref, y_ref, o_ref, *, cap_t):
    b, e = pl.program_id(0), pl.program_id(2)
    base = (b * N_EXPERTS + e) * cap_t

    @pl.when(e == 0)
    def _():
        o_ref[...] = x_ref[...]

    def body(g, _):
        tile = y_ref[0, 0, pl.ds(pl.multiple_of(g * 8, 8), 8), :]
        rows = [idx_ref[base + g * 8 + k] for k in range(8)]
        cur = [o_ref[0, pl.ds(r, 1), :] for r in rows]
        for k, r in enumerate(rows):
            o_ref[0, pl.ds(r, 1), :] = cur[k] + tile[k:k + 1, :]
        return 0

    lax.fori_loop(0, cap_t // 8, body, 0)


def _scatter(idx_flat, x, ye, b0, bg, cap_t):
    B, N, D = x.shape
    hw = D // 2
    return pl.pallas_call(
        functools.partial(_scatter_kernel, cap_t=cap_t),
        grid_spec=pltpu.PrefetchScalarGridSpec(
            num_scalar_prefetch=1,
            grid=(bg, 2, N_EXPERTS),
            in_specs=[pl.BlockSpec((1, N, hw), lambda b, c, e, idx: (b0 + b, 0, c)),
                      pl.BlockSpec((1, 1, cap_t, hw), lambda b, c, e, idx: (b0 + b, e, 0, c))],
            out_specs=pl.BlockSpec((1, N, hw), lambda b, c, e, idx: (b0 + b, 0, c)),
        ),
        out_shape=jax.ShapeDtypeStruct((B, N, D), F32),
        input_output_aliases={1: 0},
        compiler_params=_cparams(("arbitrary", "arbitrary", "arbitrary")),
        name="expert_scatter",
    )(idx_flat, x, ye)


def _rope_tables(T, S, dh):
    rows = T // GRID_W
    r = jnp.broadcast_to(jnp.arange(rows, dtype=F32)[:, None], (rows, GRID_W)).reshape(T)
    col = jnp.broadcast_to(jnp.arange(GRID_W, dtype=F32)[None, :], (rows, GRID_W)).reshape(T)
    nf = dh // 4
    inv = ROPE_BASE ** (-jnp.arange(nf, dtype=F32) / nf)
    ar, ac = r[:, None] * inv, col[:, None] * inv
    ang = jnp.concatenate([ar, ar, ac, ac], axis=1)
    ang = jnp.concatenate([ang, jnp.zeros((S, dh), F32)], axis=0)
    reps = 256 // dh
    return jnp.tile(jnp.cos(ang), (1, reps)), jnp.tile(jnp.sin(ang), (1, reps))


def _group_mats(dh):
    i = np.arange(256)
    gmat = (i[:, None] // dh == i[None, :] // dh).astype(np.float32) / dh
    nf = dh // 4
    half = (i % (2 * nf)) // nf
    pmat = np.zeros((256, 256), np.float32)
    a_idx = i[half == 0]
    pmat[a_idx + nf, a_idx] = -1.0
    pmat[a_idx, a_idx + nf] = 1.0
    return jnp.asarray(gmat, BF16), jnp.asarray(pmat, BF16)


def _hyena_tables(ls):
    t = jnp.linspace(0.0, 1.0, ls, dtype=F32)[:, None]
    w = 2.0 * math.pi * jnp.arange(ls, dtype=F32)[:, None] / ls
    bands = jnp.linspace(1e-4, HY_BANDS - 1, HY_BANDS, dtype=F32)
    z = jnp.concatenate([t, jnp.cos(bands * w), -jnp.sin(bands * w)], axis=-1)
    z = jnp.pad(z, ((0, 0), (0, 128 - HY_EMB)))
    deltas = jnp.abs(jnp.linspace(math.log(HY_TARGET) / HY_SLOW, math.log(HY_TARGET) / HY_FAST, HY_W, dtype=F32))
    decay = jnp.exp(-t * deltas)
    k = jnp.arange(ls, dtype=I32)
    ang = ((k[:, None] * k[None, :]) % (2 * ls)).astype(F32) * (math.pi / ls)
    return z, decay, jnp.cos(ang).astype(BF16), jnp.sin(ang).astype(BF16)


def kernel(x, c, ctx, c_ctx, w_ada, b_ada, norm1_g, norm2_g, w_in, b_gate, a_qnorm, a_knorm, a_sink, hy_conv, hy_fw1, hy_fb1, hy_freq, hy_fw2, hy_fb2, hy_fw3, hy_bias, ml_norm, d_qnorm, d_knorm, d_lq1, d_lk1, d_lq2, d_lk2, d_subnorm, w_out, w_router, w_e_gate, w_e_up, w_e_down):
    B, T, D = x.shape
    S = ctx.shape[1]
    N = T + S
    depth = w_ada.shape[0]
    assert D == D_MODEL and T % S == 0 and S % ROW_BLOCK == 0 and S % MLSTM_CHUNK == 0
    assert S % WATTN_QROWS == 0 and T >= WATTN_QROWS + 2 * BLK and WATTN_QROWS & (WATTN_QROWS - 1) == 0
    cap_l, cap_c = (EC_CAPACITY * T) // N_EXPERTS, (EC_CAPACITY * S) // N_EXPERTS
    cap_t = cap_l + cap_c
    assert cap_t % 16 == 0

    xs = jnp.concatenate([x, ctx], axis=1)

    rpad = -(B + 1) % 8
    cc = jnp.concatenate([c, c_ctx[None, :], jnp.zeros((rpad, D), F32)], axis=0)
    mods = _ada_mods(cc, w_ada, b_ada)

    cos_a, sin_a = _rope_tables(T, S, DH_A)
    cos_d, sin_d = _rope_tables(T, S, DH_D)
    g64, p64 = _group_mats(DH_A)
    g32, p32 = _group_mats(DH_D)
    tabs_l = _hyena_tables(T)
    tabs_c = _hyena_tables(S)

    offs = np.cumsum((0, 256, 128, 128, 768, 256, 256, 256, 256, 16, 256, 256, 256))
    for l in range(depth):
        lam_init = 0.8 - 0.6 * math.exp(-0.3 * l)
        ml = mods[l]
        mod = jnp.stack([ml[:B].reshape(B, 6, D), jnp.broadcast_to(ml[B].reshape(1, 6, D), (B, 6, D))],
                        axis=1)
        w = w_in[l]
        wp = jnp.concatenate([w[:, offs[0]:offs[8]], w[:, offs[8]:offs[9]], jnp.zeros((D, 112), F32),
                              w[:, offs[9]:offs[12]]], axis=1).astype(BF16)
        gains = jnp.stack([jnp.tile(a_qnorm[l], 4), jnp.tile(a_knorm[l], 4),
                           jnp.tile(d_qnorm[l], 8), jnp.tile(d_knorm[l], 8)], axis=0)
        qa, ka, va, ub, qc, kc, vc, oc, gc, qd, kd, vd, gct = _inproj(
            xs, mod, norm1_g[l].reshape(1, D), wp, (cos_a, sin_a, cos_d, sin_d), gains, (g64, g32, p64, p32), T)

        ya = _wattn(qa, ka, va, jnp.pad(a_sink[l], (0, 128 - H_A)).reshape(1, 128), T, S)
        yd = _dattn(qd, kd, vd, jnp.stack([d_lq1[l], d_lk1[l], d_lq2[l], d_lk2[l]], axis=0),
                    d_subnorm[l].reshape(1, 2 * DH_D), T, S, lam_init)

        bias = jnp.pad(b_gate[l].reshape(-1), (0, 112))
        hf, hb = _mlstm(qc, kc, vc, gc, gct, bias.reshape(1, 128),
                        bias[0:16].reshape(16, 1), T, S)

        uc = _shortconv(ub, hy_conv[l], T, S)
        w1 = jnp.pad(hy_fw1[l], ((0, 128 - HY_EMB), (0, 0)))
        ybs = []
        for (zf, decay, cm, sm), ls, rowblk in ((tabs_l, T, 0), (tabs_c, S, T // S)):
            hs, hd, ss, ny = _hyfilter(zf, w1, hy_fb1[l].reshape(1, -1), hy_freq[l].reshape(1, -1), hy_fw2[l],
                                       hy_fb2[l].reshape(1, -1), hy_fw3[l], decay)
            hr, hi = _hyspec(cm, sm, hs, hd, ss)
            hny = ny * lax.rsqrt(ss + EPS)
            z1 = _hyconv(cm, sm, uc, 0, rowblk, uc, 1, rowblk, hr[:, 0:HY_W], hi[:, 0:HY_W], hny[:, 0:HY_W],
                         hy_bias[l, 0].reshape(1, HY_W), ls)
            z2 = _hyconv(cm, sm, z1, 0, 0, uc, 2, rowblk, hr[:, HY_W:], hi[:, HY_W:], hny[:, HY_W:],
                         hy_bias[l, 1].reshape(1, HY_W), ls)
            ybs.append(z2)

        wr = jnp.pad(w_router[l], ((0, 0), (0, 128 - N_EXPERTS)))
        wr_hi = wr.astype(BF16)
        wr_cat = jnp.concatenate([wr_hi, (wr - wr_hi.astype(F32)).astype(BF16)], axis=1)
        xs, h2, aff = _outproj(xs, mod, ya, ybs[0], ybs[1], hf, hb, oc, yd, jnp.tile(ml_norm[l], 4).reshape(1, 256), g64,
                               w_out[l].astype(BF16), norm2_g[l].reshape(1, D), wr_cat, T)

        sel = _topk(aff, T, S)
        idx = (sel[..., 0] * 64.0 + sel[..., 1]).astype(I32)
        gate = (sel[..., 2] + sel[..., 3] + sel[..., 4])[..., None]
        gt2 = mod[:, :, 5, :]
        bg = min(IDX_BATCH_GROUP, B)
        groups = [(b0, idx[b0:b0 + bg].reshape(-1)) for b0 in range(0, B, bg)]
        xe = None
        for b0, idx_flat in groups:
            xe = _gather(idx_flat, h2, xe, b0, bg, cap_t)
        ye = _ffn(xe, l, w_e_gate, w_e_up, w_e_down, gate, gt2, cap_l)
        for b0, idx_flat in groups:
            xs = _scatter(idx_flat, xs, ye, b0, bg, cap_t)
    return xs[:, 0:T]
```

```python
import functools
import math

import jax
import jax.numpy as jnp
import numpy as np
from jax import lax
from jax.experimental import pallas as pl
from jax.experimental.pallas import tpu as pltpu

F32 = jnp.float32
BF16 = jnp.bfloat16
I32 = jnp.int32
HI = lax.Precision.HIGHEST

D_MODEL = 1024
GRID_W = 64
GROUP_W = D_MODEL // 4
H_A, HKV_A = 4, 2
G_A = H_A // HKV_A
DH_A = GROUP_W // H_A
WINDOW = 128
BLK = 128
HY_W = GROUP_W
HY_ORDER = 2
HY_BANDS = 16
HY_EMB = 1 + 2 * HY_BANDS
HY_HID = 64
HY_TARGET, HY_FAST, HY_SLOW = 1e-2, 0.3, 1.5
HY_N2 = 16
H_C = 4
DH_C = GROUP_W // H_C
H_D = 4
DH_D = GROUP_W // (2 * H_D)
N_EXPERTS = 16
EC_CAPACITY = 2
ROPE_BASE = 10000.0
EPS = 1e-6
NEG = -1e30

ROW_BLOCK = 256
WATTN_QROWS = 256
MLSTM_CHUNK = 256
IDX_BATCH_GROUP = 4
VMEM_LIMIT = 56 * 1024 * 1024

_C_QA, _C_KA, _C_VA, _C_UB = 0, 256, 384, 512
_C_QC, _C_KC, _C_VC, _C_OC, _C_GC = 1280, 1536, 1792, 2048, 2304
_C_QD, _C_KD, _C_VD, _C_END = 2432, 2688, 2944, 3200


def _cparams(sem, vmem=VMEM_LIMIT):
    return pltpu.CompilerParams(dimension_semantics=sem, vmem_limit_bytes=vmem)


def _dot(a, b, **kw):
    return jnp.dot(a, b, preferred_element_type=F32, **kw)


def _dot_nt(a, b, **kw):
    return lax.dot_general(a, b, (((1,), (1,)), ((), ())), preferred_element_type=F32, **kw)


def _dot_tn(a, b, **kw):
    return lax.dot_general(a, b, (((0,), (0,)), ((), ())), preferred_element_type=F32, **kw)


def _pack_bf16_pair(a, b):
    ua = lax.bitcast_convert_type(a.astype(F32), I32)
    ub = lax.bitcast_convert_type(b.astype(F32), I32)
    return ua | lax.shift_right_logical(ub, 16)


def _unpack_bf16_pair(w):
    a = lax.bitcast_convert_type(w & jnp.int32(-65536), F32)
    b = lax.bitcast_convert_type(lax.shift_left(w, 16), F32)
    return a.astype(BF16), b.astype(BF16)


def _ada_kernel(c_ref, w_ref, b_ref, o_ref):
    c = c_ref[...]
    s = c * jax.nn.sigmoid(c)
    o_ref[0] = _dot(s, w_ref[0], precision=HI) + b_ref[0]


def _ada_mods(cc, w_ada, b_ada):
    L, D, W6 = w_ada.shape
    R = cc.shape[0]
    cb = 1536
    return pl.pallas_call(
        _ada_kernel,
        grid=(L, W6 // cb),
        in_specs=[pl.BlockSpec((R, D), lambda l, j: (0, 0)),
                  pl.BlockSpec((1, D, cb), lambda l, j: (l, 0, j)),
                  pl.BlockSpec((1, 1, cb), lambda l, j: (l, 0, j))],
        out_specs=pl.BlockSpec((1, R, cb), lambda l, j: (l, 0, j)),
        out_shape=jax.ShapeDtypeStruct((L, R, W6), F32),
        compiler_params=_cparams(("arbitrary", "arbitrary")),
        name="ada_mods",
    )(cc, w_ada, b_ada.reshape(L, 1, W6))


def _inproj_kernel(x_ref, mod_ref, g1_ref, w_ref, ca_ref, sa_ref, cd_ref, sd_ref,
                   gains_ref, g64_ref, g32_ref, p64_ref, p32_ref,
                   qa, ka, va, ub, qc, kc, vc, oc, gc, qd, kd, vd, gct):
    x = x_ref[0]
    ms = jnp.mean(x * x, axis=-1, keepdims=True)
    xn = x * lax.rsqrt(ms + EPS) * g1_ref[...]
    mod = mod_ref[0, 0]
    h = xn * (1.0 + mod[1:2, :]) + mod[0:1, :]
    p = _dot(h.astype(BF16), w_ref[...])

    def headnorm_rope(t, gmat, gain, pmat, cos, sin, scale):
        w = t.shape[1]
        msq = _dot((t * t).astype(BF16), gmat[0:w, 0:w])
        tn = t * lax.rsqrt(msq + EPS) * gain
        tn = tn * cos[:, 0:w] + _dot(tn.astype(BF16), pmat[0:w, 0:w]) * sin[:, 0:w]
        return tn * scale

    ca, sa, cd, sd = ca_ref[...], sa_ref[...], cd_ref[...], sd_ref[...]
    gains = gains_ref[...]
    qa[0] = headnorm_rope(p[:, _C_QA:_C_KA], g64_ref, gains[0:1, :], p64_ref, ca, sa, DH_A ** -0.5).astype(BF16)
    ka[0] = headnorm_rope(p[:, _C_KA:_C_VA], g64_ref, gains[1:2, 0:128], p64_ref, ca, sa, 1.0).astype(BF16)
    va[0] = p[:, _C_VA:_C_UB].astype(BF16)
    ub[0] = p[:, _C_UB:_C_QC].astype(BF16)
    qc[0] = p[:, _C_QC:_C_KC].astype(BF16)
    kc[0] = (p[:, _C_KC:_C_VC] * DH_C ** -0.5).astype(BF16)
    vc[0] = p[:, _C_VC:_C_OC].astype(BF16)
    oc[0] = p[:, _C_OC:_C_GC].astype(BF16)
    gc[0] = p[:, _C_GC:_C_QD]
    gct[0] = p[:, _C_GC:_C_QD].T
    qd[0] = headnorm_rope(p[:, _C_QD:_C_KD], g32_ref, gains[2:3, :], p32_ref, cd, sd, DH_D ** -0.5).astype(BF16)
    kd[0] = headnorm_rope(p[:, _C_KD:_C_VD], g32_ref, gains[3:4, :], p32_ref, cd, sd, 1.0).astype(BF16)
    vd[0] = p[:, _C_VD:_C_END].astype(BF16)


def _inproj(x, mod, g1, w_packed, tabs, gains, mats, T):
    B, N, D = x.shape
    rb = ROW_BLOCK
    nlat = T // rb
    row = lambda w: pl.BlockSpec((1, rb, w), lambda b, r: (b, r, 0))
    tab = pl.BlockSpec((rb, 256), lambda b, r: (r, 0))
    full = lambda a: pl.BlockSpec(a.shape, lambda b, r: (0,) * a.ndim)
    widths = [256, 128, 128, 768, 256, 256, 256, 256, 128, 256, 256, 256]
    dtypes = [BF16] * 8 + [F32] + [BF16] * 3
    return pl.pallas_call(
        _inproj_kernel,
        grid=(B, N // rb),
        in_specs=[row(D),
                  pl.BlockSpec((1, 1, 6, D), lambda b, r: (b, jnp.where(r >= nlat, 1, 0), 0, 0)),
                  full(g1), full(w_packed), tab, tab, tab, tab, full(gains)] + [full(m) for m in mats],
        out_specs=[row(w) for w in widths] + [pl.BlockSpec((1, 128, rb), lambda b, r: (b, 0, r))],
        out_shape=[jax.ShapeDtypeStruct((B, N, w), dt) for w, dt in zip(widths, dtypes)]
        + [jax.ShapeDtypeStruct((B, 128, N), F32)],
        compiler_params=_cparams(("arbitrary", "arbitrary")),
        name="inproj",
    )(x, mod, g1, w_packed, *tabs, gains, *mats)


def _wattn_kernel(q_ref, k_ref, v_ref, sink_ref, o_ref, *, T, S):
    j = pl.program_id(1)
    qb, kwin = WATTN_QROWS, WATTN_QROWS + 2 * BLK
    is_lat = j < T // qb
    start = pl.multiple_of(jnp.clip(j * qb - BLK, 0, T - kwin), BLK)
    kw = k_ref[0, pl.ds(start, kwin), :]
    vw = v_ref[0, pl.ds(start, kwin), :]
    kc = k_ref[0, T:T + S, :]
    vc = v_ref[0, T:T + S, :]
    row = lax.broadcasted_iota(I32, (G_A * qb, kwin), 0)
    qpos = j * qb + (row & (qb - 1))
    kpos = start + lax.broadcasted_iota(I32, (G_A * qb, kwin), 1)
    valid = (jnp.abs(qpos - kpos) <= WINDOW) & is_lat
    grp = jnp.right_shift(lax.broadcasted_iota(I32, (G_A * qb, 1), 0), qb.bit_length() - 1)
    sinks = sink_ref[...]
    for hk in range(HKV_A):
        cs = slice(hk * DH_A, (hk + 1) * DH_A)
        kwh, vwh, kch, vch = kw[:, cs], vw[:, cs], kc[:, cs], vc[:, cs]
        heads = [hk * G_A + g for g in range(G_A)]
        q = jnp.concatenate([q_ref[0, :, hd * DH_A:(hd + 1) * DH_A] for hd in heads], axis=0)
        sink = sinks[0:1, heads[0]:heads[0] + 1]
        for g in range(1, G_A):
            sink = jnp.where(grp == g, sinks[0:1, heads[g]:heads[g] + 1], sink)
        s_loc = jnp.where(valid, _dot_nt(q, kwh), NEG)
        s_ctx = _dot_nt(q, kch)
        m = jnp.maximum(jnp.maximum(jnp.max(s_loc, axis=-1, keepdims=True),
                                    jnp.max(s_ctx, axis=-1, keepdims=True)), sink)
        p_loc = jnp.exp(s_loc - m)
        p_ctx = jnp.exp(s_ctx - m)
        den = (jnp.sum(p_loc, axis=-1, keepdims=True) + jnp.sum(p_ctx, axis=-1, keepdims=True)
               + jnp.exp(sink - m))
        o = (_dot(p_loc.astype(BF16), vwh) + _dot(p_ctx.astype(BF16), vch)) / den
        for g, hd in enumerate(heads):
            o_ref[0, :, hd * DH_A:(hd + 1) * DH_A] = o[g * qb:(g + 1) * qb].astype(BF16)


def _wattn(qa, ka, va, sink, T, S):
    B, N, _ = qa.shape
    qb = WATTN_QROWS
    return pl.pallas_call(
        functools.partial(_wattn_kernel, T=T, S=S),
        grid=(B, N // qb),
        in_specs=[pl.BlockSpec((1, qb, 256), lambda b, j: (b, j, 0)),
                  pl.BlockSpec((1, N, 128), lambda b, j: (b, 0, 0)),
                  pl.BlockSpec((1, N, 128), lambda b, j: (b, 0, 0)),
                  pl.BlockSpec((1, 128), lambda b, j: (0, 0))],
        out_specs=pl.BlockSpec((1, qb, 256), lambda b, j: (b, j, 0)),
        out_shape=jax.ShapeDtypeStruct((B, N, 256), BF16),
        compiler_params=_cparams(("arbitrary", "arbitrary")),
        name="window_attn",
    )(qa, ka, va, sink)


def _dattn_kernel(q_ref, k_ref, v_ref, lam_ref, sub_ref, o_ref, vaug_ref, *, T, S, lam_init):
    j = pl.program_id(1)
    lv = lam_ref[...]
    lam = (jnp.exp(jnp.sum(lv[0:1, :] * lv[1:2, :], axis=-1, keepdims=True))
           - jnp.exp(jnp.sum(lv[2:3, :] * lv[3:4, :], axis=-1, keepdims=True)) + lam_init)
    subg = sub_ref[...] * (1.0 - lam_init)
    dv = 2 * DH_D

    @pl.when(j == 0)
    def _():
        ones = jnp.where(lax.broadcasted_iota(I32, (T + S, dv), 1) == 0, 1.0, 0.0).astype(BF16)
        for hd in range(H_D):
            vaug_ref[:, hd * 2 * dv:(hd + 1) * 2 * dv] = jnp.concatenate(
                [v_ref[0, :, hd * dv:(hd + 1) * dv], ones], axis=1)

    def body(k0, nk):
        for hd in range(H_D):
            terms = []
            for c in range(2):
                cs = slice(hd * dv + c * DH_D, hd * dv + (c + 1) * DH_D)
                s = _dot_nt(q_ref[0, :, cs], k_ref[0, k0:k0 + nk, cs])
                e = jnp.exp((s - jnp.max(s, axis=-1, keepdims=True)).astype(BF16))
                pv = _dot(e, vaug_ref[k0:k0 + nk, hd * 2 * dv:(hd + 1) * 2 * dv])
                terms.append(pv[:, 0:dv] / pv[:, dv:dv + 1])
            o = terms[0] - lam * terms[1]
            ms = jnp.mean(o * o, axis=-1, keepdims=True)
            o_ref[0, :, hd * dv:(hd + 1) * dv] = (o * lax.rsqrt(ms + EPS) * subg).astype(BF16)

    @pl.when(j < T // ROW_BLOCK)
    def _():
        body(0, T + S)

    @pl.when(j >= T // ROW_BLOCK)
    def _():
        body(T, S)


def _dattn(qd, kd, vd, lam_vecs, subg, T, S, lam_init):
    B, N, _ = qd.shape
    rb = ROW_BLOCK
    return pl.pallas_call(
        functools.partial(_dattn_kernel, T=T, S=S, lam_init=lam_init),
        grid=(B, N // rb),
        in_specs=[pl.BlockSpec((1, rb, 256), lambda b, j: (b, j, 0)),
                  pl.BlockSpec((1, N, 256), lambda b, j: (b, 0, 0)),
                  pl.BlockSpec((1, N, 256), lambda b, j: (b, 0, 0)),
                  pl.BlockSpec((4, DH_D), lambda b, j: (0, 0)),
                  pl.BlockSpec((1, 2 * DH_D), lambda b, j: (0, 0))],
        out_specs=pl.BlockSpec((1, rb, 256), lambda b, j: (b, j, 0)),
        out_shape=jax.ShapeDtypeStruct((B, N, 256), BF16),
        scratch_shapes=[pltpu.VMEM((N, 2 * 256), BF16)],
        compiler_params=_cparams(("arbitrary", "arbitrary")),
        name="diff_attn",
    )(qd, kd, vd, lam_vecs, subg)


def _log_sigmoid(x):
    return jnp.minimum(x, 0.0) - jnp.log1p(jnp.exp(-jnp.abs(x)))


def _mlstm_kernel(qf_ref, kf_ref, vf_ref, gf_ref, gtf_ref, qb_ref, kb_ref, vb_ref, gb_ref, gtb_ref,
                  bias_ref, biast_ref, hf_ref, hb_ref, ct_ref, m_ref):
    i = pl.program_id(1)
    lc = MLSTM_CHUNK

    @pl.when(i == 0)
    def _():
        ct_ref[...] = jnp.zeros_like(ct_ref)
        m_ref[...] = jnp.zeros_like(m_ref)

    r_io = lax.broadcasted_iota(I32, (lc, lc), 0)
    c_io = lax.broadcasted_iota(I32, (lc, lc), 1)
    lower = r_io >= c_io
    upper = r_io <= c_io
    tri_lo = jnp.where(lower, 1.0, 0.0).astype(F32)
    tri_up = jnp.where(upper, 1.0, 0.0).astype(F32)
    ones_col = jnp.where(lax.broadcasted_iota(I32, (lc, DH_C), 1) == 0, 1.0, 0.0).astype(BF16)

    def direction(d, q_ref, k_ref, v_ref, g_ref, gt_ref, out_ref):
        pre = g_ref[0] + bias_ref[...]
        pre_t = gt_ref[0] + biast_ref[...]
        lf = _log_sigmoid(pre)
        lf_t = _log_sigmoid(pre_t)
        if d == 0:
            bcol = _dot(tri_lo, lf, precision=HI)
            brow = _dot(lf_t, tri_up, precision=HI)
            valid, end_row = lower, lc - 1
        else:
            bcol = _dot(tri_up, lf, precision=HI)
            brow = _dot(lf_t, tri_lo, precision=HI)
            valid, end_row = upper, 0
        for hd in range(H_C):
            ci, cf = d * H_C + hd, (2 + d) * H_C + hd
            cs = slice(hd * DH_C, (hd + 1) * DH_C)
            q, k, v = q_ref[0, :, cs], k_ref[0, :, cs], v_ref[0, :, cs]
            v_aug = jnp.concatenate([v, ones_col], axis=1)
            b_t = bcol[:, cf:cf + 1]
            ig = pre[:, ci:ci + 1]
            b_end = bcol[end_row:end_row + 1, cf:cf + 1]
            slot = d * H_C + hd
            ct = ct_ref[slot]
            m = m_ref[slot][:, 0:1]
            logw = jnp.where(valid, b_t - (brow[cf:cf + 1, :] - pre_t[ci:ci + 1, :]), NEG)
            inter = b_t + m
            m_t = jnp.maximum(inter, jnp.max(logw, axis=-1, keepdims=True))
            w_in = jnp.exp(inter - m_t)
            smat = _dot_nt(q, k) * jnp.exp(logw - m_t)
            num = w_in * _dot(q, ct.astype(BF16)) + _dot(smat.astype(BF16), v_aug)
            den = jnp.maximum(jnp.abs(num[:, DH_C:DH_C + 1]), jnp.exp(-m_t))
            out_ref[0, :, cs] = num[:, 0:DH_C] / den
            g_end = b_end - b_t + ig
            m_new = jnp.maximum(b_end + m, jnp.max(g_end, axis=0, keepdims=True))
            w_tok = jnp.exp(g_end - m_new)
            wv = (w_tok * v_aug.astype(F32)).astype(BF16)
            ct_ref[slot] = jnp.exp(b_end + m - m_new) * ct + _dot_tn(k, wv)
            m_ref[slot] = jnp.broadcast_to(m_new, (1, 128))

    direction(0, qf_ref, kf_ref, vf_ref, gf_ref, gtf_ref, hf_ref)
    direction(1, qb_ref, kb_ref, vb_ref, gb_ref, gtb_ref, hb_ref)


def _mlstm(qc, kc, vc, gc, gct, bias, biast, T, S):
    B, N, _ = qc.shape
    lc = MLSTM_CHUNK
    nch, nlat = N // lc, T // lc
    fwd = lambda i: jnp.where(i < nch - nlat, nlat + i, i - (nch - nlat))
    bwd = lambda i: nch - 1 - i
    tok = lambda f: pl.BlockSpec((1, lc, 256), lambda b, i: (b, f(i), 0))
    gate = lambda f: pl.BlockSpec((1, lc, 128), lambda b, i: (b, f(i), 0))
    gate_t = lambda f: pl.BlockSpec((1, 16, lc), lambda b, i: (b, 0, f(i)))
    return pl.pallas_call(
        _mlstm_kernel,
        grid=(B, nch),
        in_specs=[tok(fwd), tok(fwd), tok(fwd), gate(fwd), gate_t(fwd),
                  tok(bwd), tok(bwd), tok(bwd), gate(bwd), gate_t(bwd),
                  pl.BlockSpec((1, 128), lambda b, i: (0, 0)),
                  pl.BlockSpec((16, 1), lambda b, i: (0, 0))],
        out_specs=[tok(fwd), tok(bwd)],
        out_shape=[jax.ShapeDtypeStruct((B, N, 256), F32)] * 2,
        scratch_shapes=[pltpu.VMEM((2 * H_C, DH_C, 128), F32), pltpu.VMEM((2 * H_C, 1, 128), F32)],
        compiler_params=_cparams(("arbitrary", "arbitrary")),
        name="mlstm",
    )(qc, kc, vc, gc, gct, qc, kc, vc, gc, gct, bias, biast)


def _shortconv_kernel(u_ref, w_ref, o_ref, *, T, S):
    u = u_ref[0].astype(F32)
    n = u.shape[0]
    w = w_ref[...]
    row = lax.broadcasted_iota(I32, u.shape, 0)
    zero = jnp.zeros((1, u.shape[1]), F32)
    prev = jnp.concatenate([zero, u[0:n - 1]], axis=0)
    nxt = jnp.concatenate([u[1:n], zero], axis=0)
    prev = jnp.where(row == T, 0.0, prev)
    nxt = jnp.where(row == T - 1, 0.0, nxt)
    o_ref[0] = (w[0:1, :] * prev + w[1:2, :] * u + w[2:3, :] * nxt).astype(BF16)


def _shortconv(ub, conv_w, T, S):
    B, N, W = ub.shape
    return pl.pallas_call(
        functools.partial(_shortconv_kernel, T=T, S=S),
        grid=(B, W // 256),
        in_specs=[pl.BlockSpec((1, N, 256), lambda b, j: (b, 0, j)),
                  pl.BlockSpec((3, 256), lambda b, j: (0, j))],
        out_specs=pl.BlockSpec((1, N, 256), lambda b, j: (b, 0, j)),
        out_shape=jax.ShapeDtypeStruct((B, N, W), BF16),
        compiler_params=_cparams(("arbitrary", "arbitrary")),
        name="hyena_shortconv",
    )(ub, conv_w)


def _hyfilter_kernel(z_ref, w1_ref, b1_ref, fr_ref, w2_ref, b2_ref, w3_ref, dec_ref,
                     hp_ref, hf_ref, ss_ref):
    i = pl.program_id(0)
    fr = fr_ref[...]
    h = jnp.sin(fr * (_dot(z_ref[...], w1_ref[...], precision=HI) + b1_ref[...]))
    h = jnp.sin(fr * (_dot(h, w2_ref[...], precision=HI) + b2_ref[...]))
    h = _dot(h, w3_ref[...], precision=HI)
    dec = dec_ref[...]
    dec2 = jnp.concatenate([dec, dec], axis=1)
    hw = HY_ORDER * HY_W
    h0 = h[:, 0:hw] * dec2
    h1 = h[:, hw:2 * hw] * dec2
    rows = h0.shape[0]
    t = i * rows + lax.broadcasted_iota(I32, h0.shape, 0)
    h1 = jnp.where(t == 0, 0.0, h1)
    hp_ref[...] = h0
    hf_ref[...] = h1

    @pl.when(i == 0)
    def _():
        ss_ref[...] = jnp.zeros_like(ss_ref)

    ss_ref[...] += jnp.sum(h0 * h0 + h1 * h1, axis=0, keepdims=True)


def _hyfilter(zfeat, w1, b1, fr, w2, b2, w3, decay):
    ls = zfeat.shape[0]
    rb = min(ls, 512)
    hw = HY_ORDER * HY_W
    full = lambda a: pl.BlockSpec(a.shape, lambda i: (0,) * a.ndim)
    return pl.pallas_call(
        _hyfilter_kernel,
        grid=(ls // rb,),
        in_specs=[pl.BlockSpec((rb, 128), lambda i: (i, 0)), full(w1), full(b1), full(fr), full(w2),
                  full(b2), full(w3), pl.BlockSpec((rb, HY_W), lambda i: (i, 0))],
        out_specs=[pl.BlockSpec((rb, hw), lambda i: (i, 0)), pl.BlockSpec((rb, hw), lambda i: (i, 0)),
                   pl.BlockSpec((1, hw), lambda i: (0, 0))],
        out_shape=[jax.ShapeDtypeStruct((ls, hw), F32), jax.ShapeDtypeStruct((ls, hw), F32),
                   jax.ShapeDtypeStruct((1, hw), F32)],
        compiler_params=_cparams(("arbitrary",)),
        name="hyena_filter",
    )(zfeat, w1, b1, fr, w2, b2, w3, decay)


def _cmul_const(v, c, s):
    vr, vi = v
    r = math.sqrt(0.5)
    if abs(s) < 1e-9:
        return (vr, vi) if c > 0 else (-vr, -vi)
    if abs(c) < 1e-9:
        return (-vi, vr) if s > 0 else (vi, -vr)
    if abs(abs(c) - r) < 1e-9 and abs(abs(s) - r) < 1e-9:
        a, b = (vr if c > 0 else -vr), (vi if s > 0 else -vi)
        p, q = (vi if c > 0 else -vi), (vr if s > 0 else -vr)
        return (a - b) * r, (p + q) * r
    return c * vr - s * vi, c * vi + s * vr


def _fft_pow2(xs, sign):
    n = len(xs)
    if n == 1:
        return xs
    ev, od = _fft_pow2(xs[0::2], sign), _fft_pow2(xs[1::2], sign)
    out = [None] * n
    for k in range(n // 2):
        ang = sign * 2.0 * math.pi * k / n
        tr, ti = _cmul_const(od[k], math.cos(ang), math.sin(ang))
        out[k] = (ev[k][0] + tr, ev[k][1] + ti)
        out[k + n // 2] = (ev[k][0] - tr, ev[k][1] - ti)
    return out


def _hy_stage1(src_ref, f_ref, a_ref, nseg):
    k1p = a_ref.shape[2]
    for n2 in range(HY_N2):
        if len(src_ref.shape) == 3:
            zs = jnp.concatenate([src_ref[t, pl.ds(n2, nseg, stride=HY_N2), :] for t in range(src_ref.shape[0])],
                                 axis=1).astype(BF16)
        else:
            zs = src_ref[pl.ds(n2, nseg, stride=HY_N2), :].astype(BF16)
        r = _dot(f_ref[n2], zs)
        a_ref[0, n2] = r[0:k1p]
        a_ref[1, n2] = r[k1p:2 * k1p]


def _hy_chunks(a_ref):
    k1p, lanes = a_ref.shape[2], a_ref.shape[3]
    return k1p // 8, [slice(t * 128, (t + 1) * 128) for t in range(lanes // 128)]


def _hyspec_kernel(hp_ref, hf_ref, ss_ref, f_ref, h_ref, ap_ref, af_ref, *, ls):
    nseg = ls // HY_N2
    _hy_stage1(hp_ref, f_ref, ap_ref, nseg)
    _hy_stage1(hf_ref, f_ref, af_ref, nseg)
    nchunk, lane_tiles = _hy_chunks(ap_ref)

    def body(i, _):
        rows = pl.ds(pl.multiple_of(i * 8, 8), 8)
        k1 = i * 8 + lax.broadcasted_iota(I32, (8, 1), 0)
        wk = jnp.where((k1 == 0) | (k1 == nseg), 1.0, 2.0) * (1.0 / (2 * ls))
        for ln in lane_tiles:
            scale = lax.rsqrt(ss_ref[:, ln] + EPS) * wk
            P = _fft_pow2([(ap_ref[0, n2, rows, ln], ap_ref[1, n2, rows, ln]) for n2 in range(HY_N2)], -1)
            Q = _fft_pow2([(af_ref[0, n2, rows, ln], af_ref[1, n2, rows, ln]) for n2 in range(HY_N2)], -1)
            for k2 in range(HY_N2):
                h_ref[0, 0, k2, rows, ln] = (P[k2][0] + Q[k2][0]) * scale
                h_ref[0, 1, k2, rows, ln] = (P[k2][1] - Q[k2][1]) * scale
        return 0

    lax.fori_loop(0, nchunk, body, 0)


def _hyspec(hp, hf, ss, fmat):
    ls = hp.shape[0]
    k1p = fmat.shape[1] // 2
    full = lambda a: pl.BlockSpec(a.shape, lambda o, c: (0,) * a.ndim)
    nc = HY_W // 128
    col = lambda o, c: (0, o * nc + c)
    return pl.pallas_call(
        functools.partial(_hyspec_kernel, ls=ls),
        grid=(HY_ORDER, nc),
        in_specs=[pl.BlockSpec((ls, 128), col), pl.BlockSpec((ls, 128), col), pl.BlockSpec((1, 128), col),
                  full(fmat)],
        out_specs=pl.BlockSpec((1, 2, HY_N2, k1p, 128), lambda o, c: (o, 0, 0, 0, c)),
        out_shape=jax.ShapeDtypeStruct((HY_ORDER, 2, HY_N2, k1p, HY_W), F32),
        scratch_shapes=[pltpu.VMEM((2, HY_N2, k1p, 128), F32)] * 2,
        compiler_params=_cparams(("arbitrary", "arbitrary")),
        name="hyena_spectrum",
    )(hp, hf, ss, fmat)


def _hyconv_kernel(z_ref, gate_ref, h_ref, f_ref, g_ref, bias_ref, o_ref, zf_ref, yf_ref, a_ref, *, ls):
    nseg = ls // HY_N2
    ntile = zf_ref.shape[0]
    for t in range(ntile):
        zf_ref[t] = z_ref[0, :, t * 128:(t + 1) * 128].astype(F32)
    _hy_stage1(zf_ref, f_ref, a_ref, nseg)
    nchunk, lane_tiles = _hy_chunks(a_ref)

    def body(i, _):
        rows = pl.ds(pl.multiple_of(i * 8, 8), 8)
        for ln in lane_tiles:
            X = _fft_pow2([(a_ref[0, n2, rows, ln], a_ref[1, n2, rows, ln]) for n2 in range(HY_N2)], -1)
            Y = []
            for k2 in range(HY_N2):
                hr, hi = h_ref[0, 0, k2, rows, ln], h_ref[0, 1, k2, rows, ln]
                xr, xi = X[k2]
                Y.append((xr * hr - xi * hi, xr * hi + xi * hr))
            Bv = _fft_pow2(Y, 1)
            for n2 in range(HY_N2):
                a_ref[0, n2, rows, ln] = Bv[n2][0]
                a_ref[1, n2, rows, ln] = Bv[n2][1]
        return 0

    lax.fori_loop(0, nchunk, body, 0)
    for n2 in range(HY_N2):
        bb = jnp.concatenate([a_ref[0, n2], a_ref[1, n2]], axis=0).astype(BF16)
        yv = _dot(g_ref[n2], bb)
        for t in range(ntile):
            yf_ref[t, pl.ds(n2, nseg, stride=HY_N2), :] = yv[:, t * 128:(t + 1) * 128]
    for t in range(ntile):
        ln = slice(t * 128, (t + 1) * 128)
        o_ref[0, :, ln] = (gate_ref[0, :, ln].astype(F32)
                           * (yf_ref[t] + zf_ref[t] * bias_ref[:, ln])).astype(BF16)


def _hyconv(zsrc, zcol, zrow, gsrc, gcol, grow, hspec, order, fmat, gmat, bias, ls):
    B = zsrc.shape[0]
    k1p = fmat.shape[1] // 2
    once = lambda a: pl.BlockSpec(a.shape, lambda b: (0,) * a.ndim, pipeline_mode=pl.Buffered(1))
    return pl.pallas_call(
        functools.partial(_hyconv_kernel, ls=ls),
        grid=(B,),
        in_specs=[pl.BlockSpec((1, ls, HY_W), lambda b: (b, zrow, zcol)),
                  pl.BlockSpec((1, ls, HY_W), lambda b: (b, grow, gcol)),
                  pl.BlockSpec((1, 2, HY_N2, k1p, HY_W), lambda b: (order, 0, 0, 0, 0),
                               pipeline_mode=pl.Buffered(1)),
                  once(fmat), once(gmat), pl.BlockSpec((1, HY_W), lambda b: (0, 0))],
        out_specs=pl.BlockSpec((1, ls, HY_W), lambda b: (b, 0, 0)),
        out_shape=jax.ShapeDtypeStruct((B, ls, HY_W), BF16),
        scratch_shapes=[pltpu.VMEM((HY_W // 128, ls, 128), F32), pltpu.VMEM((HY_W // 128, ls, 128), F32),
                        pltpu.VMEM((2, HY_N2, k1p, HY_W), F32)],
        compiler_params=_cparams(("arbitrary",)),
        name="hyena_longconv",
    )(zsrc, gsrc, hspec, fmat, gmat, bias)


def _outproj_kernel(x_ref, mod_ref, ya_ref, ybl_ref, ybc_ref, hf_ref, hb_ref, oc_ref, yd_ref, mlg_ref, g64_ref,
                    w_ref, g2_ref, wr_ref, xo_ref, h2_ref, aff_ref, *, nlat):
    hsum = hf_ref[0] + hb_ref[0]
    msq = _dot((hsum * hsum).astype(BF16), g64_ref[...])
    yc = jax.nn.sigmoid(oc_ref[0].astype(F32)) * (hsum * lax.rsqrt(msq + EPS) * mlg_ref[...])
    yb = jnp.where(pl.program_id(1) < nlat, ybl_ref[0], ybc_ref[0])
    y = jnp.concatenate([ya_ref[0], yb, yc.astype(BF16), yd_ref[0]], axis=1)
    mod = mod_ref[0, 0]
    x = x_ref[0] + mod[2:3, :] * _dot(y, w_ref[...])
    xo_ref[0] = x
    ms = jnp.mean(x * x, axis=-1, keepdims=True)
    h2 = x * lax.rsqrt(ms + EPS) * g2_ref[...] * (1.0 + mod[4:5, :]) + mod[3:4, :]
    h2_hi = h2.astype(BF16)
    h2_ref[0] = _pack_bf16_pair(h2_hi[:, 0:D_MODEL // 2], h2_hi[:, D_MODEL // 2:])
    h2_lo = (h2 - h2_hi.astype(F32)).astype(BF16)
    lg = _dot(h2_hi, wr_ref[...])
    logits = lg[:, 0:128] + lg[:, 128:256] + _dot(h2_lo, wr_ref[:, 0:128])
    lane = lax.broadcasted_iota(I32, logits.shape, 1)
    logits = jnp.where(lane < N_EXPERTS, logits, NEG)
    e = jnp.exp(logits - jnp.max(logits, axis=-1, keepdims=True))
    aff_ref[0] = (e / jnp.sum(e, axis=-1, keepdims=True)).T


def _outproj(x, mod, ya, yb_l, yb_c, hf, hb, oc, yd, mlg, g64, w_out, g2, w_router, T):
    B, N, D = x.shape
    rb = ROW_BLOCK
    nlat = T // rb
    row = lambda w: pl.BlockSpec((1, rb, w), lambda b, r: (b, r, 0))
    full = lambda a: pl.BlockSpec(a.shape, lambda b, r: (0,) * a.ndim)
    return pl.pallas_call(
        functools.partial(_outproj_kernel, nlat=nlat),
        grid=(B, N // rb),
        in_specs=[row(D), pl.BlockSpec((1, 1, 6, D), lambda b, r: (b, jnp.where(r >= nlat, 1, 0), 0, 0)),
                  row(256),
                  pl.BlockSpec((1, rb, 256), lambda b, r: (b, jnp.minimum(r, nlat - 1), 0)),
                  pl.BlockSpec((1, rb, 256), lambda b, r: (b, jnp.maximum(r - nlat, 0), 0)),
                  row(256), row(256), row(256), row(256),
                  full(mlg), full(g64), full(w_out), full(g2), full(w_router)],
        out_specs=[row(D), row(D // 2), pl.BlockSpec((1, 128, rb), lambda b, r: (b, 0, r))],
        out_shape=[jax.ShapeDtypeStruct((B, N, D), F32), jax.ShapeDtypeStruct((B, N, D // 2), I32),
                   jax.ShapeDtypeStruct((B, 128, N), F32)],
        compiler_params=_cparams(("arbitrary", "arbitrary")),
        name="outproj",
    )(x, mod, ya, yb_l, yb_c, hf, hb, oc, yd, mlg, g64, w_out, g2, w_router)


def _prefix_exclusive(x):
    n = x.shape[1]
    lane = lax.broadcasted_iota(I32, x.shape, 1)
    inc = x
    d = 1
    while d < n:
        inc = inc + jnp.where(lane >= d, pltpu.roll(inc, d, axis=1), 0)
        d *= 2
    return inc - x


def _topk_kernel(aff_ref, o_ref, pos_ref, val_ref, *, T, S):
    def segment(t0, ts, cap, slot0):
        a = aff_ref[0, :, t0:t0 + ts]
        bits = lax.bitcast_convert_type(a, I32)

        def search(i, v):
            cand = v | jnp.left_shift(jnp.int32(1), 30 - i)
            cnt = jnp.sum((bits >= cand).astype(I32), axis=1, keepdims=True)
            return jnp.where(cnt >= cap, cand, v)

        thr = lax.fori_loop(0, 31, search, jnp.zeros((N_EXPERTS, 1), I32))
        gt = bits > thr
        eq = bits == thr
        need = cap - jnp.sum(gt.astype(I32), axis=1, keepdims=True)
        sel = gt | (eq & (_prefix_exclusive(eq.astype(I32)) < need))
        seli = sel.astype(I32)
        pos_ref[:, 0:ts] = jnp.where(sel, _prefix_exclusive(seli), -1)
        a_hi = a.astype(BF16)
        r1 = a - a_hi.astype(F32)
        a_mid = r1.astype(BF16)
        a_lo = (r1 - a_mid.astype(F32)).astype(BF16)
        val_ref[0, :, 0:ts] = a_hi.astype(F32)
        val_ref[1, :, 0:ts] = a_mid.astype(F32)
        val_ref[2, :, 0:ts] = a_lo.astype(F32)
        tabs = lax.broadcasted_iota(I32, (1, ts), 1) + t0
        t_hi = jnp.right_shift(tabs, 6).astype(F32)
        t_lo = (tabs & 63).astype(F32)
        slot = lax.broadcasted_iota(I32, (cap, ts), 0)
        zeros = jnp.zeros((11, ts), F32)

        def per_expert(e, _):
            onehot = jnp.where(pos_ref[pl.ds(e, 1), 0:ts] == slot, 1.0, 0.0).astype(BF16)
            vals = jnp.concatenate([t_hi, t_lo, val_ref[0, pl.ds(e, 1), 0:ts], val_ref[1, pl.ds(e, 1), 0:ts],
                                    val_ref[2, pl.ds(e, 1), 0:ts], zeros], axis=0).astype(BF16)
            o_ref[0, e, slot0:slot0 + cap, :] = _dot_nt(onehot, vals)
            return 0

        lax.fori_loop(0, N_EXPERTS, per_expert, 0)

    segment(0, T, (EC_CAPACITY * T) // N_EXPERTS, 0)
    segment(T, S, (EC_CAPACITY * S) // N_EXPERTS, (EC_CAPACITY * T) // N_EXPERTS)


def _topk(aff_t, T, S):
    B, _, N = aff_t.shape
    E = N_EXPERTS
    cap_t = (EC_CAPACITY * T) // E + (EC_CAPACITY * S) // E
    return pl.pallas_call(
        functools.partial(_topk_kernel, T=T, S=S),
        grid=(B,),
        in_specs=[pl.BlockSpec((1, E, N), lambda b: (b, 0, 0))],
        out_specs=pl.BlockSpec((1, E, cap_t, 16), lambda b: (b, 0, 0, 0)),
        out_shape=jax.ShapeDtypeStruct((B, E, cap_t, 16), F32),
        scratch_shapes=[pltpu.VMEM((E, T), I32), pltpu.VMEM((3, E, T), F32)],
        compiler_params=_cparams(("arbitrary",)),
        name="expert_topk",
    )(aff_t)


def _gather_kernel(idx_ref, h_ref, *rest, cap_t):
    o_ref = rest[-1]
    b, e = pl.program_id(0), pl.program_id(1)
    base = (b * N_EXPERTS + e) * cap_t

    def body(g, _):
        rows = [h_ref[0, pl.ds(idx_ref[base + g * 16 + k], 1), :] for k in range(16)]
        o_ref[0, 0, pl.ds(pl.multiple_of(g * 16, 16), 16), :] = jnp.concatenate(rows, axis=0)
        return 0

    lax.fori_loop(0, cap_t // 16, body, 0)


def _gather(idx_flat, h2p, xe_buf, b0, bg, cap_t):
    B, N, hw = h2p.shape
    in_specs = [pl.BlockSpec((1, N, hw), lambda b, e, idx: (b0 + b, 0, 0))]
    args = [idx_flat, h2p]
    aliases = {}
    if xe_buf is not None:
        in_specs.append(pl.BlockSpec(memory_space=pl.ANY))
        args.append(xe_buf)
        aliases = {2: 0}
    return pl.pallas_call(
        functools.partial(_gather_kernel, cap_t=cap_t),
        grid_spec=pltpu.PrefetchScalarGridSpec(
            num_scalar_prefetch=1,
            grid=(bg, N_EXPERTS),
            in_specs=in_specs,
            out_specs=pl.BlockSpec((1, 1, cap_t, hw), lambda b, e, idx: (b0 + b, e, 0, 0)),
        ),
        out_shape=jax.ShapeDtypeStruct((B, N_EXPERTS, cap_t, hw), I32),
        input_output_aliases=aliases,
        compiler_params=_cparams(("arbitrary", "arbitrary")),
        name="expert_gather",
    )(*args)


def _ffn_kernel(x_ref, wg_ref, wu_ref, wd_ref, gate_ref, gt_ref, o_ref, wg_s, wu_s, wd_s, *, cap_l):
    @pl.when(pl.program_id(1) == 0)
    def _():
        wg_s[...] = wg_ref[0, 0].astype(BF16)
        wu_s[...] = wu_ref[0, 0].astype(BF16)
        wd_s[...] = wd_ref[0, 0].astype(BF16)

    x = jnp.concatenate(_unpack_bf16_pair(x_ref[0, 0]), axis=1)
    a = _dot(x, wg_s[...])
    u = _dot(x, wu_s[...])
    hmid = (a * jax.nn.sigmoid(a) * u).astype(BF16)
    y = _dot(hmid, wd_s[...])
    gt = gt_ref[0]
    row = lax.broadcasted_iota(I32, y.shape, 0)
    gt2 = jnp.where(row < cap_l, gt[0:1, :], gt[1:2, :])
    o_ref[0, 0] = y * gate_ref[0, 0] * gt2


def _ffn(xe, layer, wg, wu, wd, gate, gt2, cap_l):
    B, E, cap_t, hw = xe.shape
    D = 2 * hw
    wspec = pl.BlockSpec((1, 1, D, D), lambda e, b: (layer, e, 0, 0))
    return pl.pallas_call(
        functools.partial(_ffn_kernel, cap_l=cap_l),
        grid=(E, B),
        in_specs=[pl.BlockSpec((1, 1, cap_t, hw), lambda e, b: (b, e, 0, 0)), wspec, wspec, wspec,
                  pl.BlockSpec((1, 1, cap_t, 1), lambda e, b: (b, e, 0, 0)),
                  pl.BlockSpec((1, 2, D), lambda e, b: (b, 0, 0))],
        out_specs=pl.BlockSpec((1, 1, cap_t, D), lambda e, b: (b, e, 0, 0)),
        out_shape=jax.ShapeDtypeStruct((B, E, cap_t, D), F32),
        scratch_shapes=[pltpu.VMEM((D, D), BF16)] * 3,
        compiler_params=_cparams(("arbitrary", "arbitrary")),
        name="expert_ffn",
    )(xe, wg, wu, wd, gate, gt2)


def _scatter_kernel(idx_ref, x_ref, y_ref, o_ref, *, cap_t):
    b, e = pl.program_id(0), pl.program_id(2)
    base = (b * N_EXPERTS + e) * cap_t

    @pl.when(e == 0)
    def _():
        o_ref[...] = x_ref[...]

    def body(g, _):
        tile = y_ref[0, 0, pl.ds(pl.multiple_of(g * 8, 8), 8), :]
        rows = [idx_ref[base + g * 8 + k] for k in range(8)]
        cur = [o_ref[0, pl.ds(r, 1), :] for r in rows]
        for k, r in enumerate(rows):
            o_ref[0, pl.ds(r, 1), :] = cur[k] + tile[k:k + 1, :]
        return 0

    lax.fori_loop(0, cap_t // 8, body, 0)


def _scatter(idx_flat, x, ye, b0, bg, cap_t):
    B, N, D = x.shape
    hw = D // 2
    return pl.pallas_call(
        functools.partial(_scatter_kernel, cap_t=cap_t),
        grid_spec=pltpu.PrefetchScalarGridSpec(
            num_scalar_prefetch=1,
            grid=(bg, 2, N_EXPERTS),
            in_specs=[pl.BlockSpec((1, N, hw), lambda b, c, e, idx: (b0 + b, 0, c)),
                      pl.BlockSpec((1, 1, cap_t, hw), lambda b, c, e, idx: (b0 + b, e, 0, c))],
            out_specs=pl.BlockSpec((1, N, hw), lambda b, c, e, idx: (b0 + b, 0, c)),
        ),
        out_shape=jax.ShapeDtypeStruct((B, N, D), F32),
        input_output_aliases={1: 0},
        compiler_params=_cparams(("arbitrary", "arbitrary", "arbitrary")),
        name="expert_scatter",
    )(idx_flat, x, ye)


def _rope_tables(T, S, dh):
    rows = T // GRID_W
    r = jnp.broadcast_to(jnp.arange(rows, dtype=F32)[:, None], (rows, GRID_W)).reshape(T)
    col = jnp.broadcast_to(jnp.arange(GRID_W, dtype=F32)[None, :], (rows, GRID_W)).reshape(T)
    nf = dh // 4
    inv = ROPE_BASE ** (-jnp.arange(nf, dtype=F32) / nf)
    ar, ac = r[:, None] * inv, col[:, None] * inv
    ang = jnp.concatenate([ar, ar, ac, ac], axis=1)
    ang = jnp.concatenate([ang, jnp.zeros((S, dh), F32)], axis=0)
    reps = 256 // dh
    return jnp.tile(jnp.cos(ang), (1, reps)), jnp.tile(jnp.sin(ang), (1, reps))


def _group_mats(dh):
    i = np.arange(256)
    gmat = (i[:, None] // dh == i[None, :] // dh).astype(np.float32) / dh
    nf = dh // 4
    half = (i % (2 * nf)) // nf
    pmat = np.zeros((256, 256), np.float32)
    a_idx = i[half == 0]
    pmat[a_idx + nf, a_idx] = -1.0
    pmat[a_idx, a_idx + nf] = 1.0
    return jnp.asarray(gmat, BF16), jnp.asarray(pmat, BF16)


def _hyena_tables(ls):
    t = jnp.linspace(0.0, 1.0, ls, dtype=F32)[:, None]
    w = 2.0 * math.pi * jnp.arange(ls, dtype=F32)[:, None] / ls
    bands = jnp.linspace(1e-4, HY_BANDS - 1, HY_BANDS, dtype=F32)
    z = jnp.concatenate([t, jnp.cos(bands * w), -jnp.sin(bands * w)], axis=-1)
    z = jnp.pad(z, ((0, 0), (0, 128 - HY_EMB)))
    deltas = jnp.abs(jnp.linspace(math.log(HY_TARGET) / HY_SLOW, math.log(HY_TARGET) / HY_FAST, HY_W, dtype=F32))
    decay = jnp.exp(-t * deltas)
    n, nseg = 2 * ls, ls // HY_N2
    k1p = -(-(nseg + 1) // 8) * 8
    k1 = jnp.arange(k1p, dtype=I32)[None, :, None]
    tpos = HY_N2 * jnp.arange(nseg, dtype=I32)[None, None, :] + jnp.arange(HY_N2, dtype=I32)[:, None, None]
    ang = ((k1 * tpos) % n).astype(F32) * (2.0 * math.pi / n)
    keep = k1 <= nseg
    fmat = jnp.concatenate([jnp.where(keep, jnp.cos(ang), 0.0), jnp.where(keep, -jnp.sin(ang), 0.0)],
                           axis=1).astype(BF16)
    return z, decay, fmat, jnp.swapaxes(fmat, 1, 2)


def kernel(x, c, ctx, c_ctx, w_ada, b_ada, norm1_g, norm2_g, w_in, b_gate, a_qnorm, a_knorm, a_sink, hy_conv, hy_fw1, hy_fb1, hy_freq, hy_fw2, hy_fb2, hy_fw3, hy_bias, ml_norm, d_qnorm, d_knorm, d_lq1, d_lk1, d_lq2, d_lk2, d_subnorm, w_out, w_router, w_e_gate, w_e_up, w_e_down):
    B, T, D = x.shape
    S = ctx.shape[1]
    N = T + S
    depth = w_ada.shape[0]
    assert D == D_MODEL and T % S == 0 and S % ROW_BLOCK == 0 and S % MLSTM_CHUNK == 0
    assert S % WATTN_QROWS == 0 and T >= WATTN_QROWS + 2 * BLK and WATTN_QROWS & (WATTN_QROWS - 1) == 0
    cap_l, cap_c = (EC_CAPACITY * T) // N_EXPERTS, (EC_CAPACITY * S) // N_EXPERTS
    cap_t = cap_l + cap_c
    assert cap_t % 16 == 0

    xs = jnp.concatenate([x, ctx], axis=1)

    rpad = -(B + 1) % 8
    cc = jnp.concatenate([c, c_ctx[None, :], jnp.zeros((rpad, D), F32)], axis=0)
    mods = _ada_mods(cc, w_ada, b_ada)

    cos_a, sin_a = _rope_tables(T, S, DH_A)
    cos_d, sin_d = _rope_tables(T, S, DH_D)
    g64, p64 = _group_mats(DH_A)
    g32, p32 = _group_mats(DH_D)
    tabs_l = _hyena_tables(T)
    tabs_c = _hyena_tables(S)

    offs = np.cumsum((0, 256, 128, 128, 768, 256, 256, 256, 256, 16, 256, 256, 256))
    for l in range(depth):
        lam_init = 0.8 - 0.6 * math.exp(-0.3 * l)
        ml = mods[l]
        mod = jnp.stack([ml[:B].reshape(B, 6, D), jnp.broadcast_to(ml[B].reshape(1, 6, D), (B, 6, D))],
                        axis=1)
        w = w_in[l]
        wp = jnp.concatenate([w[:, offs[0]:offs[8]], w[:, offs[8]:offs[9]], jnp.zeros((D, 112), F32),
                              w[:, offs[9]:offs[12]]], axis=1).astype(BF16)
        gains = jnp.stack([jnp.tile(a_qnorm[l], 4), jnp.tile(a_knorm[l], 4),
                           jnp.tile(d_qnorm[l], 8), jnp.tile(d_knorm[l], 8)], axis=0)
        qa, ka, va, ub, qc, kc, vc, oc, gc, qd, kd, vd, gct = _inproj(
            xs, mod, norm1_g[l].reshape(1, D), wp, (cos_a, sin_a, cos_d, sin_d), gains, (g64, g32, p64, p32), T)

        ya = _wattn(qa, ka, va, jnp.pad(a_sink[l], (0, 128 - H_A)).reshape(1, 128), T, S)
        yd = _dattn(qd, kd, vd, jnp.stack([d_lq1[l], d_lk1[l], d_lq2[l], d_lk2[l]], axis=0),
                    d_subnorm[l].reshape(1, 2 * DH_D), T, S, lam_init)

        bias = jnp.pad(b_gate[l].reshape(-1), (0, 112))
        hf, hb = _mlstm(qc, kc, vc, gc, gct, bias.reshape(1, 128),
                        bias[0:16].reshape(16, 1), T, S)

        uc = _shortconv(ub, hy_conv[l], T, S)
        w1 = jnp.pad(hy_fw1[l], ((0, 128 - HY_EMB), (0, 0)))
        ybs = []
        for (zf, decay, fmat, gmat), ls, rowblk in ((tabs_l, T, 0), (tabs_c, S, T // S)):
            hpast, hfut, ss = _hyfilter(zf, w1, hy_fb1[l].reshape(1, -1), hy_freq[l].reshape(1, -1), hy_fw2[l],
                                        hy_fb2[l].reshape(1, -1), hy_fw3[l], decay)
            hspec = _hyspec(hpast, hfut, ss, fmat)
            z1 = _hyconv(uc, 0, rowblk, uc, 1, rowblk, hspec, 0, fmat, gmat, hy_bias[l, 0].reshape(1, HY_W), ls)
            z2 = _hyconv(z1, 0, 0, uc, 2, rowblk, hspec, 1, fmat, gmat, hy_bias[l, 1].reshape(1, HY_W), ls)
            ybs.append(z2)

        wr = jnp.pad(w_router[l], ((0, 0), (0, 128 - N_EXPERTS)))
        wr_hi = wr.astype(BF16)
        wr_cat = jnp.concatenate([wr_hi, (wr - wr_hi.astype(F32)).astype(BF16)], axis=1)
        xs, h2, aff = _outproj(xs, mod, ya, ybs[0], ybs[1], hf, hb, oc, yd, jnp.tile(ml_norm[l], 4).reshape(1, 256), g64,
                               w_out[l].astype(BF16), norm2_g[l].reshape(1, D), wr_cat, T)

        sel = _topk(aff, T, S)
        idx = (sel[..., 0] * 64.0 + sel[..., 1]).astype(I32)
        gate = (sel[..., 2] + sel[..., 3] + sel[..., 4])[..., None]
        gt2 = mod[:, :, 5, :]
        bg = min(IDX_BATCH_GROUP, B)
        groups = [(b0, idx[b0:b0 + bg].reshape(-1)) for b0 in range(0, B, bg)]
        xe = None
        for b0, idx_flat in groups:
            xe = _gather(idx_flat, h2, xe, b0, bg, cap_t)
        ye = _ffn(xe, l, w_e_gate, w_e_up, w_e_down, gate, gt2, cap_l)
        for b0, idx_flat in groups:
            xs = _scatter(idx_flat, xs, ye, b0, bg, cap_t)
    return xs[:, 0:T]
```

```python
import functools
import math

import jax
import jax.numpy as jnp
import numpy as np
from jax import lax
from jax.experimental import pallas as pl
from jax.experimental.pallas import tpu as pltpu

F32 = jnp.float32
BF16 = jnp.bfloat16
I32 = jnp.int32
HI = lax.Precision.HIGHEST

D_MODEL = 1024
GRID_W = 64
GROUP_W = D_MODEL // 4
H_A, HKV_A = 4, 2
G_A = H_A // HKV_A
DH_A = GROUP_W // H_A
WINDOW = 128
BLK = 128
HY_W = GROUP_W
HY_ORDER = 2
HY_BANDS = 16
HY_EMB = 1 + 2 * HY_BANDS
HY_HID = 64
HY_TARGET, HY_FAST, HY_SLOW = 1e-2, 0.3, 1.5
HY_N2 = 16
H_C = 4
DH_C = GROUP_W // H_C
H_D = 4
DH_D = GROUP_W // (2 * H_D)
N_EXPERTS = 16
EC_CAPACITY = 2
ROPE_BASE = 10000.0
EPS = 1e-6
NEG = -1e30

ROW_BLOCK = 256
WATTN_QROWS = 256
DATTN_MIN_ROWSUM = 1e-25
MLSTM_CHUNK = 256
IDX_BATCH_GROUP = 4
VMEM_LIMIT = 56 * 1024 * 1024

_C_QA, _C_KA, _C_VA, _C_UB = 0, 256, 384, 512
_C_QC, _C_KC, _C_VC, _C_OC, _C_GC = 1280, 1536, 1792, 2048, 2304
_C_QD, _C_KD, _C_VD, _C_END = 2432, 2688, 2944, 3200


def _cparams(sem, vmem=VMEM_LIMIT):
    return pltpu.CompilerParams(dimension_semantics=sem, vmem_limit_bytes=vmem)


def _dot(a, b, **kw):
    return jnp.dot(a, b, preferred_element_type=F32, **kw)


def _dot_nt(a, b, **kw):
    return lax.dot_general(a, b, (((1,), (1,)), ((), ())), preferred_element_type=F32, **kw)


def _dot_tn(a, b, **kw):
    return lax.dot_general(a, b, (((0,), (0,)), ((), ())), preferred_element_type=F32, **kw)


def _pack_bf16_pair(a, b):
    ua = lax.bitcast_convert_type(a.astype(F32), I32)
    ub = lax.bitcast_convert_type(b.astype(F32), I32)
    return ua | lax.shift_right_logical(ub, 16)


def _unpack_bf16_pair(w):
    a = lax.bitcast_convert_type(w & jnp.int32(-65536), F32)
    b = lax.bitcast_convert_type(lax.shift_left(w, 16), F32)
    return a.astype(BF16), b.astype(BF16)


def _ada_kernel(c_ref, w_ref, b_ref, o_ref):
    c = c_ref[...]
    s = c * jax.nn.sigmoid(c)
    o_ref[0] = _dot(s, w_ref[0], precision=HI) + b_ref[0]


def _ada_mods(cc, w_ada, b_ada):
    L, D, W6 = w_ada.shape
    R = cc.shape[0]
    cb = 1536
    return pl.pallas_call(
        _ada_kernel,
        grid=(L, W6 // cb),
        in_specs=[pl.BlockSpec((R, D), lambda l, j: (0, 0)),
                  pl.BlockSpec((1, D, cb), lambda l, j: (l, 0, j)),
                  pl.BlockSpec((1, 1, cb), lambda l, j: (l, 0, j))],
        out_specs=pl.BlockSpec((1, R, cb), lambda l, j: (l, 0, j)),
        out_shape=jax.ShapeDtypeStruct((L, R, W6), F32),
        compiler_params=_cparams(("arbitrary", "arbitrary")),
        name="ada_mods",
    )(cc, w_ada, b_ada.reshape(L, 1, W6))


def _inproj_kernel(x_ref, mod_ref, g1_ref, w_ref, ca_ref, sa_ref, cd_ref, sd_ref,
                   gains_ref, g64_ref, g32_ref, p64_ref, p32_ref,
                   qa, ka, va, ub, qc, kc, vc, oc, gc, qd, kd, vd, gct, kct):
    x = x_ref[0]
    ms = jnp.mean(x * x, axis=-1, keepdims=True)
    xn = x * lax.rsqrt(ms + EPS) * g1_ref[...]
    mod = mod_ref[0, 0]
    h = xn * (1.0 + mod[1:2, :]) + mod[0:1, :]
    p = _dot(h.astype(BF16), w_ref[...])

    def headnorm_rope(t, gmat, gain, pmat, cos, sin, scale):
        w = t.shape[1]
        msq = _dot((t * t).astype(BF16), gmat[0:w, 0:w])
        tn = t * lax.rsqrt(msq + EPS) * gain
        tn = tn * cos[:, 0:w] + _dot(tn.astype(BF16), pmat[0:w, 0:w]) * sin[:, 0:w]
        return tn * scale

    ca, sa, cd, sd = ca_ref[...], sa_ref[...], cd_ref[...], sd_ref[...]
    gains = gains_ref[...]
    qa[0] = headnorm_rope(p[:, _C_QA:_C_KA], g64_ref, gains[0:1, :], p64_ref, ca, sa, DH_A ** -0.5).astype(BF16)
    ka[0] = headnorm_rope(p[:, _C_KA:_C_VA], g64_ref, gains[1:2, 0:128], p64_ref, ca, sa, 1.0).astype(BF16)
    va[0] = p[:, _C_VA:_C_UB].astype(BF16)
    ub[0] = p[:, _C_UB:_C_QC].astype(BF16)
    qc[0] = p[:, _C_QC:_C_KC].astype(BF16)
    kscaled = p[:, _C_KC:_C_VC] * DH_C ** -0.5
    kc[0] = kscaled.astype(BF16)
    kct[0] = kscaled.T.astype(BF16)
    vc[0] = p[:, _C_VC:_C_OC].astype(BF16)
    oc[0] = p[:, _C_OC:_C_GC].astype(BF16)
    gc[0] = p[:, _C_GC:_C_QD]
    gct[0] = p[:, _C_GC:_C_QD].T
    qd[0] = headnorm_rope(p[:, _C_QD:_C_KD], g32_ref, gains[2:3, :], p32_ref, cd, sd, DH_D ** -0.5).astype(BF16)
    kd[0] = headnorm_rope(p[:, _C_KD:_C_VD], g32_ref, gains[3:4, :], p32_ref, cd, sd, 1.0).astype(BF16)
    vd[0] = p[:, _C_VD:_C_END].astype(BF16)


def _inproj(x, mod, g1, w_packed, tabs, gains, mats, T):
    B, N, D = x.shape
    rb = ROW_BLOCK
    nlat = T // rb
    row = lambda w: pl.BlockSpec((1, rb, w), lambda b, r: (b, r, 0))
    tab = pl.BlockSpec((rb, 256), lambda b, r: (r, 0))
    full = lambda a: pl.BlockSpec(a.shape, lambda b, r: (0,) * a.ndim)
    widths = [256, 128, 128, 768, 256, 256, 256, 256, 128, 256, 256, 256]
    dtypes = [BF16] * 8 + [F32] + [BF16] * 3
    return pl.pallas_call(
        _inproj_kernel,
        grid=(B, N // rb),
        in_specs=[row(D),
                  pl.BlockSpec((1, 1, 6, D), lambda b, r: (b, jnp.where(r >= nlat, 1, 0), 0, 0)),
                  full(g1), full(w_packed), tab, tab, tab, tab, full(gains)] + [full(m) for m in mats],
        out_specs=[row(w) for w in widths] + [pl.BlockSpec((1, 128, rb), lambda b, r: (b, 0, r)),
                                              pl.BlockSpec((1, 256, rb), lambda b, r: (b, 0, r))],
        out_shape=[jax.ShapeDtypeStruct((B, N, w), dt) for w, dt in zip(widths, dtypes)]
        + [jax.ShapeDtypeStruct((B, 128, N), F32), jax.ShapeDtypeStruct((B, 256, N), BF16)],
        compiler_params=_cparams(("arbitrary", "arbitrary")),
        name="inproj",
    )(x, mod, g1, w_packed, *tabs, gains, *mats)


def _wattn_kernel(q_ref, k_ref, v_ref, sink_ref, o_ref, *, T, S):
    j = pl.program_id(1)
    qb, kwin = WATTN_QROWS, WATTN_QROWS + 2 * BLK
    is_lat = j < T // qb
    start = pl.multiple_of(jnp.clip(j * qb - BLK, 0, T - kwin), BLK)
    kw = k_ref[0, pl.ds(start, kwin), :]
    vw = v_ref[0, pl.ds(start, kwin), :]
    kc = k_ref[0, T:T + S, :]
    vc = v_ref[0, T:T + S, :]
    row = lax.broadcasted_iota(I32, (G_A * qb, kwin), 0)
    qpos = j * qb + (row & (qb - 1))
    kpos = start + lax.broadcasted_iota(I32, (G_A * qb, kwin), 1)
    valid = (jnp.abs(qpos - kpos) <= WINDOW) & is_lat
    grp = jnp.right_shift(lax.broadcasted_iota(I32, (G_A * qb, 1), 0), qb.bit_length() - 1)
    sinks = sink_ref[...]
    for hk in range(HKV_A):
        cs = slice(hk * DH_A, (hk + 1) * DH_A)
        kwh, vwh, kch, vch = kw[:, cs], vw[:, cs], kc[:, cs], vc[:, cs]
        heads = [hk * G_A + g for g in range(G_A)]
        q = jnp.concatenate([q_ref[0, :, hd * DH_A:(hd + 1) * DH_A] for hd in heads], axis=0)
        sink = sinks[0:1, heads[0]:heads[0] + 1]
        for g in range(1, G_A):
            sink = jnp.where(grp == g, sinks[0:1, heads[g]:heads[g] + 1], sink)
        s_loc = jnp.where(valid, _dot_nt(q, kwh), NEG)
        s_ctx = _dot_nt(q, kch)
        m = jnp.maximum(jnp.maximum(jnp.max(s_loc, axis=-1, keepdims=True),
                                    jnp.max(s_ctx, axis=-1, keepdims=True)), sink)
        p_loc = jnp.exp(s_loc - m)
        p_ctx = jnp.exp(s_ctx - m)
        den = (jnp.sum(p_loc, axis=-1, keepdims=True) + jnp.sum(p_ctx, axis=-1, keepdims=True)
               + jnp.exp(sink - m))
        o = (_dot(p_loc.astype(BF16), vwh) + _dot(p_ctx.astype(BF16), vch)) / den
        for g, hd in enumerate(heads):
            o_ref[0, :, hd * DH_A:(hd + 1) * DH_A] = o[g * qb:(g + 1) * qb].astype(BF16)


def _wattn(qa, ka, va, sink, T, S):
    B, N, _ = qa.shape
    qb = WATTN_QROWS
    return pl.pallas_call(
        functools.partial(_wattn_kernel, T=T, S=S),
        grid=(B, N // qb),
        in_specs=[pl.BlockSpec((1, qb, 256), lambda b, j: (b, j, 0)),
                  pl.BlockSpec((1, N, 128), lambda b, j: (b, 0, 0)),
                  pl.BlockSpec((1, N, 128), lambda b, j: (b, 0, 0)),
                  pl.BlockSpec((1, 128), lambda b, j: (0, 0))],
        out_specs=pl.BlockSpec((1, qb, 256), lambda b, j: (b, j, 0)),
        out_shape=jax.ShapeDtypeStruct((B, N, 256), BF16),
        compiler_params=_cparams(("arbitrary", "arbitrary")),
        name="window_attn",
    )(qa, ka, va, sink)


def _dattn_kernel(q_ref, k_ref, v_ref, lam_ref, sub_ref, place_ref, gsum_ref, o_ref,
                  vaug_ref, kaug_ref, kmax_ref, *, T, S, lam_init):
    j = pl.program_id(1)
    lv = lam_ref[...]
    lam = (jnp.exp(jnp.sum(lv[0:1, :] * lv[1:2, :], axis=-1, keepdims=True))
           - jnp.exp(jnp.sum(lv[2:3, :] * lv[3:4, :], axis=-1, keepdims=True)) + lam_init)
    subg = sub_ref[...] * (1.0 - lam_init)
    dv = 2 * DH_D
    nchain, aw = 2 * H_D, 2 * DH_D

    @pl.when(j == 0)
    def _():
        ones = jnp.where(lax.broadcasted_iota(I32, (T + S, dv), 1) == 0, 1.0, 0.0).astype(BF16)
        for hd in range(H_D):
            vaug_ref[:, hd * 2 * dv:(hd + 1) * 2 * dv] = jnp.concatenate(
                [v_ref[0, :, hd * dv:(hd + 1) * dv], ones], axis=1)
        k_wide = _dot(k_ref[0], place_ref[...])
        lane = lax.broadcasted_iota(I32, (1, nchain * aw), 1)
        kaug_ref[...] = (k_wide + jnp.where((lane & (aw - 1)) == DH_D, 1.0, 0.0)).astype(BF16)
        ksq = _dot((k_wide * k_wide).astype(BF16), gsum_ref[...])
        kmax_ref[...] = jnp.sqrt(jnp.max(ksq, axis=0, keepdims=True))

    def finish(hd, terms):
        o = terms[0] - lam * terms[1]
        ms = jnp.mean(o * o, axis=-1, keepdims=True)
        o_ref[0, :, hd * dv:(hd + 1) * dv] = (o * lax.rsqrt(ms + EPS) * subg).astype(BF16)

    def body_fast(k0, nk):
        q_wide = _dot(q_ref[0], place_ref[...])
        qn = jnp.sqrt(_dot((q_wide * q_wide).astype(BF16), gsum_ref[...]))
        q_aug = (q_wide - qn * kmax_ref[...]).astype(BF16)
        terms, low = [], None
        for i in range(nchain):
            hd, cs = i // 2, slice(i * aw, (i + 1) * aw)
            s = _dot_nt(q_aug[:, cs], kaug_ref[k0:k0 + nk, cs])
            pv = _dot(jnp.exp(s.astype(BF16)), vaug_ref[k0:k0 + nk, hd * 2 * dv:(hd + 1) * 2 * dv])
            sigma = pv[:, dv:dv + 1]
            low = sigma if low is None else jnp.minimum(low, sigma)
            terms.append(pv[:, 0:dv] / sigma)
            if i % 2 == 1:
                finish(hd, terms)
                terms = []
        return jnp.min(low)

    def body_exact(k0, nk):
        terms = []
        for i in range(nchain):
            hd, cs = i // 2, slice(i * DH_D, (i + 1) * DH_D)
            s = _dot_nt(q_ref[0, :, cs], k_ref[0, k0:k0 + nk, cs])
            e = jnp.exp((s - jnp.max(s, axis=-1, keepdims=True)).astype(BF16))
            pv = _dot(e, vaug_ref[k0:k0 + nk, hd * 2 * dv:(hd + 1) * 2 * dv])
            terms.append(pv[:, 0:dv] / pv[:, dv:dv + 1])
            if i % 2 == 1:
                finish(hd, terms)
                terms = []

    def body(k0, nk):
        low = body_fast(k0, nk)

        @pl.when(jnp.logical_not(low > DATTN_MIN_ROWSUM))
        def _():
            body_exact(k0, nk)

    @pl.when(j < T // ROW_BLOCK)
    def _():
        body(0, T + S)

    @pl.when(j >= T // ROW_BLOCK)
    def _():
        body(T, S)


def _dattn_mats():
    nchain, aw = 2 * H_D, 2 * DH_D
    place = np.zeros((nchain * DH_D, nchain * aw), np.float32)
    gsum = np.zeros((nchain * aw, nchain * aw), np.float32)
    for i in range(nchain):
        for d in range(DH_D):
            place[i * DH_D + d, i * aw + d] = 1.0
            gsum[i * aw + d, i * aw + DH_D] = 1.0
    return jnp.asarray(place, BF16), jnp.asarray(gsum, BF16)


def _dattn(qd, kd, vd, lam_vecs, subg, T, S, lam_init):
    B, N, _ = qd.shape
    rb = ROW_BLOCK
    place, gsum = _dattn_mats()
    full = lambda a: pl.BlockSpec(a.shape, lambda b, j: (0,) * a.ndim)
    return pl.pallas_call(
        functools.partial(_dattn_kernel, T=T, S=S, lam_init=lam_init),
        grid=(B, N // rb),
        in_specs=[pl.BlockSpec((1, rb, 256), lambda b, j: (b, j, 0)),
                  pl.BlockSpec((1, N, 256), lambda b, j: (b, 0, 0)),
                  pl.BlockSpec((1, N, 256), lambda b, j: (b, 0, 0)),
                  pl.BlockSpec((4, DH_D), lambda b, j: (0, 0)),
                  pl.BlockSpec((1, 2 * DH_D), lambda b, j: (0, 0)), full(place), full(gsum)],
        out_specs=pl.BlockSpec((1, rb, 256), lambda b, j: (b, j, 0)),
        out_shape=jax.ShapeDtypeStruct((B, N, 256), BF16),
        scratch_shapes=[pltpu.VMEM((N, 2 * 256), BF16), pltpu.VMEM((N, 2 * 256), BF16),
                        pltpu.VMEM((1, 2 * 256), F32)],
        compiler_params=_cparams(("arbitrary", "arbitrary")),
        name="diff_attn",
    )(qd, kd, vd, lam_vecs, subg, place, gsum)


def _log_sigmoid(x):
    return jnp.minimum(x, 0.0) - jnp.log1p(jnp.exp(-jnp.abs(x)))


def _mlstm_kernel(qf_ref, kf_ref, ktf_ref, vf_ref, gf_ref, gtf_ref, qb_ref, kb_ref, ktb_ref, vb_ref, gb_ref,
                  gtb_ref, bias_ref, biast_ref, hf_ref, hb_ref, ct_ref, m_ref):
    i = pl.program_id(1)
    lc = MLSTM_CHUNK

    @pl.when(i == 0)
    def _():
        ct_ref[...] = jnp.zeros_like(ct_ref)
        m_ref[...] = jnp.zeros_like(m_ref)

    r_io = lax.broadcasted_iota(I32, (lc, lc), 0)
    c_io = lax.broadcasted_iota(I32, (lc, lc), 1)
    lower = r_io >= c_io
    upper = r_io <= c_io
    tri_lo = jnp.where(lower, 1.0, 0.0).astype(F32)
    tri_up = jnp.where(upper, 1.0, 0.0).astype(F32)
    ones_col = jnp.where(lax.broadcasted_iota(I32, (lc, DH_C), 1) == 0, 1.0, 0.0).astype(BF16)

    def direction(d, q_ref, k_ref, kt_ref, v_ref, g_ref, gt_ref, out_ref):
        pre = g_ref[0] + bias_ref[...]
        pre_t = gt_ref[0] + biast_ref[...]
        lf = _log_sigmoid(pre)
        lf_t = _log_sigmoid(pre_t)
        if d == 0:
            bcol = _dot(tri_lo, lf, precision=HI)
            brow = _dot(lf_t, tri_up, precision=HI)
            valid, end_row = lower, lc - 1
        else:
            bcol = _dot(tri_up, lf, precision=HI)
            brow = _dot(lf_t, tri_lo, precision=HI)
            valid, end_row = upper, 0
        for hd in range(H_C):
            ci, cf = d * H_C + hd, (2 + d) * H_C + hd
            cs = slice(hd * DH_C, (hd + 1) * DH_C)
            q, k, v = q_ref[0, :, cs], k_ref[0, :, cs], v_ref[0, :, cs]
            v_aug = jnp.concatenate([v, ones_col], axis=1)
            b_t = bcol[:, cf:cf + 1]
            ig = pre[:, ci:ci + 1]
            b_end = bcol[end_row:end_row + 1, cf:cf + 1]
            slot = d * H_C + hd
            ct = ct_ref[slot]
            m = m_ref[slot][:, 0:1]
            logw = jnp.where(valid, b_t - (brow[cf:cf + 1, :] - pre_t[ci:ci + 1, :]), NEG)
            inter = b_t + m
            m_t = jnp.maximum(inter, jnp.max(logw, axis=-1, keepdims=True))
            w_in = jnp.exp(inter - m_t)
            smat = _dot_nt(q, k) * jnp.exp(logw - m_t)
            num = w_in * _dot(q, ct.astype(BF16)) + _dot(smat.astype(BF16), v_aug)
            den = jnp.maximum(jnp.abs(num[:, DH_C:DH_C + 1]), jnp.exp(-m_t))
            out_ref[0, :, cs] = num[:, 0:DH_C] / den
            g_end = b_end - b_t + ig
            m_new = jnp.maximum(b_end + m, jnp.max(g_end, axis=0, keepdims=True))
            w_tok = jnp.exp(g_end - m_new)
            wv = (w_tok * v_aug.astype(F32)).astype(BF16)
            ct_ref[slot] = jnp.exp(b_end + m - m_new) * ct + _dot(kt_ref[0, cs, :], wv)
            m_ref[slot] = jnp.broadcast_to(m_new, (1, 128))

    direction(0, qf_ref, kf_ref, ktf_ref, vf_ref, gf_ref, gtf_ref, hf_ref)
    direction(1, qb_ref, kb_ref, ktb_ref, vb_ref, gb_ref, gtb_ref, hb_ref)


def _mlstm(qc, kc, kct, vc, gc, gct, bias, biast, T, S):
    B, N, _ = qc.shape
    lc = MLSTM_CHUNK
    nch, nlat = N // lc, T // lc
    fwd = lambda i: jnp.where(i < nch - nlat, nlat + i, i - (nch - nlat))
    bwd = lambda i: nch - 1 - i
    tok = lambda f: pl.BlockSpec((1, lc, 256), lambda b, i: (b, f(i), 0))
    tok_t = lambda f: pl.BlockSpec((1, 256, lc), lambda b, i: (b, 0, f(i)))
    gate = lambda f: pl.BlockSpec((1, lc, 128), lambda b, i: (b, f(i), 0))
    gate_t = lambda f: pl.BlockSpec((1, 16, lc), lambda b, i: (b, 0, f(i)))
    return pl.pallas_call(
        _mlstm_kernel,
        grid=(B, nch),
        in_specs=[tok(fwd), tok(fwd), tok_t(fwd), tok(fwd), gate(fwd), gate_t(fwd),
                  tok(bwd), tok(bwd), tok_t(bwd), tok(bwd), gate(bwd), gate_t(bwd),
                  pl.BlockSpec((1, 128), lambda b, i: (0, 0)),
                  pl.BlockSpec((16, 1), lambda b, i: (0, 0))],
        out_specs=[tok(fwd), tok(bwd)],
        out_shape=[jax.ShapeDtypeStruct((B, N, 256), F32)] * 2,
        scratch_shapes=[pltpu.VMEM((2 * H_C, DH_C, 128), F32), pltpu.VMEM((2 * H_C, 1, 128), F32)],
        compiler_params=_cparams(("arbitrary", "arbitrary")),
        name="mlstm",
    )(qc, kc, kct, vc, gc, gct, qc, kc, kct, vc, gc, gct, bias, biast)


def _shortconv_kernel(u_ref, w_ref, o_ref, *, T, S):
    u = u_ref[0].astype(F32)
    n = u.shape[0]
    w = w_ref[...]
    row = lax.broadcasted_iota(I32, u.shape, 0)
    zero = jnp.zeros((1, u.shape[1]), F32)
    prev = jnp.concatenate([zero, u[0:n - 1]], axis=0)
    nxt = jnp.concatenate([u[1:n], zero], axis=0)
    prev = jnp.where(row == T, 0.0, prev)
    nxt = jnp.where(row == T - 1, 0.0, nxt)
    o_ref[0] = (w[0:1, :] * prev + w[1:2, :] * u + w[2:3, :] * nxt).astype(BF16)


def _shortconv(ub, conv_w, T, S):
    B, N, W = ub.shape
    return pl.pallas_call(
        functools.partial(_shortconv_kernel, T=T, S=S),
        grid=(B, W // 256),
        in_specs=[pl.BlockSpec((1, N, 256), lambda b, j: (b, 0, j)),
                  pl.BlockSpec((3, 256), lambda b, j: (0, j))],
        out_specs=pl.BlockSpec((1, N, 256), lambda b, j: (b, 0, j)),
        out_shape=jax.ShapeDtypeStruct((B, N, W), BF16),
        compiler_params=_cparams(("arbitrary", "arbitrary")),
        name="hyena_shortconv",
    )(ub, conv_w)


def _hyfilter_kernel(z_ref, w1_ref, b1_ref, fr_ref, w2_ref, b2_ref, w3_ref, dec_ref,
                     hp_ref, hf_ref, ss_ref):
    i = pl.program_id(0)
    fr = fr_ref[...]
    h = jnp.sin(fr * (_dot(z_ref[...], w1_ref[...], precision=HI) + b1_ref[...]))
    h = jnp.sin(fr * (_dot(h, w2_ref[...], precision=HI) + b2_ref[...]))
    h = _dot(h, w3_ref[...], precision=HI)
    dec = dec_ref[...]
    dec2 = jnp.concatenate([dec, dec], axis=1)
    hw = HY_ORDER * HY_W
    h0 = h[:, 0:hw] * dec2
    h1 = h[:, hw:2 * hw] * dec2
    rows = h0.shape[0]
    t = i * rows + lax.broadcasted_iota(I32, h0.shape, 0)
    h1 = jnp.where(t == 0, 0.0, h1)
    hp_ref[...] = h0
    hf_ref[...] = h1

    @pl.when(i == 0)
    def _():
        ss_ref[...] = jnp.zeros_like(ss_ref)

    ss_ref[...] += jnp.sum(h0 * h0 + h1 * h1, axis=0, keepdims=True)


def _hyfilter(zfeat, w1, b1, fr, w2, b2, w3, decay):
    ls = zfeat.shape[0]
    rb = min(ls, 512)
    hw = HY_ORDER * HY_W
    full = lambda a: pl.BlockSpec(a.shape, lambda i: (0,) * a.ndim)
    return pl.pallas_call(
        _hyfilter_kernel,
        grid=(ls // rb,),
        in_specs=[pl.BlockSpec((rb, 128), lambda i: (i, 0)), full(w1), full(b1), full(fr), full(w2),
                  full(b2), full(w3), pl.BlockSpec((rb, HY_W), lambda i: (i, 0))],
        out_specs=[pl.BlockSpec((rb, hw), lambda i: (i, 0)), pl.BlockSpec((rb, hw), lambda i: (i, 0)),
                   pl.BlockSpec((1, hw), lambda i: (0, 0))],
        out_shape=[jax.ShapeDtypeStruct((ls, hw), F32), jax.ShapeDtypeStruct((ls, hw), F32),
                   jax.ShapeDtypeStruct((1, hw), F32)],
        compiler_params=_cparams(("arbitrary",)),
        name="hyena_filter",
    )(zfeat, w1, b1, fr, w2, b2, w3, decay)


def _cmul_const(v, c, s):
    vr, vi = v
    r = math.sqrt(0.5)
    if abs(s) < 1e-9:
        return (vr, vi) if c > 0 else (-vr, -vi)
    if abs(c) < 1e-9:
        return (-vi, vr) if s > 0 else (vi, -vr)
    if abs(abs(c) - r) < 1e-9 and abs(abs(s) - r) < 1e-9:
        a, b = (vr if c > 0 else -vr), (vi if s > 0 else -vi)
        p, q = (vi if c > 0 else -vi), (vr if s > 0 else -vr)
        return (a - b) * r, (p + q) * r
    return c * vr - s * vi, c * vi + s * vr


def _fft_pow2(xs, sign):
    n = len(xs)
    if n == 1:
        return xs
    ev, od = _fft_pow2(xs[0::2], sign), _fft_pow2(xs[1::2], sign)
    out = [None] * n
    for k in range(n // 2):
        ang = sign * 2.0 * math.pi * k / n
        tr, ti = _cmul_const(od[k], math.cos(ang), math.sin(ang))
        out[k] = (ev[k][0] + tr, ev[k][1] + ti)
        out[k + n // 2] = (ev[k][0] - tr, ev[k][1] - ti)
    return out


def _hy_stage1(src_ref, f_ref, a_ref, nseg):
    k1p = a_ref.shape[2]
    for n2 in range(HY_N2):
        if len(src_ref.shape) == 3:
            zs = jnp.concatenate([src_ref[t, pl.ds(n2, nseg, stride=HY_N2), :] for t in range(src_ref.shape[0])],
                                 axis=1).astype(BF16)
        else:
            zs = src_ref[pl.ds(n2, nseg, stride=HY_N2), :].astype(BF16)
        r = _dot(f_ref[n2], zs)
        a_ref[0, n2] = r[0:k1p]
        a_ref[1, n2] = r[k1p:2 * k1p]


def _hy_chunks(a_ref):
    k1p, lanes = a_ref.shape[2], a_ref.shape[3]
    return k1p // 8, [slice(t * 128, (t + 1) * 128) for t in range(lanes // 128)]


def _hyspec_kernel(hp_ref, hf_ref, ss_ref, f_ref, h_ref, ap_ref, af_ref, *, ls):
    nseg = ls // HY_N2
    _hy_stage1(hp_ref, f_ref, ap_ref, nseg)
    _hy_stage1(hf_ref, f_ref, af_ref, nseg)
    nchunk, lane_tiles = _hy_chunks(ap_ref)

    def body(i, _):
        rows = pl.ds(pl.multiple_of(i * 8, 8), 8)
        k1 = i * 8 + lax.broadcasted_iota(I32, (8, 1), 0)
        wk = jnp.where((k1 == 0) | (k1 == nseg), 1.0, 2.0) * (1.0 / (2 * ls))
        for ln in lane_tiles:
            scale = lax.rsqrt(ss_ref[:, ln] + EPS) * wk
            P = _fft_pow2([(ap_ref[0, n2, rows, ln], ap_ref[1, n2, rows, ln]) for n2 in range(HY_N2)], -1)
            Q = _fft_pow2([(af_ref[0, n2, rows, ln], af_ref[1, n2, rows, ln]) for n2 in range(HY_N2)], -1)
            for k2 in range(HY_N2):
                h_ref[0, 0, k2, rows, ln] = (P[k2][0] + Q[k2][0]) * scale
                h_ref[0, 1, k2, rows, ln] = (P[k2][1] - Q[k2][1]) * scale
        return 0

    lax.fori_loop(0, nchunk, body, 0)


def _hyspec(hp, hf, ss, fmat):
    ls = hp.shape[0]
    k1p = fmat.shape[1] // 2
    full = lambda a: pl.BlockSpec(a.shape, lambda o, c: (0,) * a.ndim)
    nc = HY_W // 128
    col = lambda o, c: (0, o * nc + c)
    return pl.pallas_call(
        functools.partial(_hyspec_kernel, ls=ls),
        grid=(HY_ORDER, nc),
        in_specs=[pl.BlockSpec((ls, 128), col), pl.BlockSpec((ls, 128), col), pl.BlockSpec((1, 128), col),
                  full(fmat)],
        out_specs=pl.BlockSpec((1, 2, HY_N2, k1p, 128), lambda o, c: (o, 0, 0, 0, c)),
        out_shape=jax.ShapeDtypeStruct((HY_ORDER, 2, HY_N2, k1p, HY_W), F32),
        scratch_shapes=[pltpu.VMEM((2, HY_N2, k1p, 128), F32)] * 2,
        compiler_params=_cparams(("arbitrary", "arbitrary")),
        name="hyena_spectrum",
    )(hp, hf, ss, fmat)


def _hyconv_kernel(z_ref, gate_ref, h_ref, f_ref, g_ref, bias_ref, o_ref, zf_ref, yf_ref, a_ref, *, ls):
    nseg = ls // HY_N2
    ntile = zf_ref.shape[0]
    for t in range(ntile):
        zf_ref[t] = z_ref[0, :, t * 128:(t + 1) * 128].astype(F32)
    _hy_stage1(zf_ref, f_ref, a_ref, nseg)
    nchunk, lane_tiles = _hy_chunks(a_ref)

    def body(i, _):
        rows = pl.ds(pl.multiple_of(i * 8, 8), 8)
        for ln in lane_tiles:
            X = _fft_pow2([(a_ref[0, n2, rows, ln], a_ref[1, n2, rows, ln]) for n2 in range(HY_N2)], -1)
            Y = []
            for k2 in range(HY_N2):
                hr, hi = h_ref[0, 0, k2, rows, ln], h_ref[0, 1, k2, rows, ln]
                xr, xi = X[k2]
                Y.append((xr * hr - xi * hi, xr * hi + xi * hr))
            Bv = _fft_pow2(Y, 1)
            for n2 in range(HY_N2):
                a_ref[0, n2, rows, ln] = Bv[n2][0]
                a_ref[1, n2, rows, ln] = Bv[n2][1]
        return 0

    lax.fori_loop(0, nchunk, body, 0)
    for n2 in range(HY_N2):
        bb = jnp.concatenate([a_ref[0, n2], a_ref[1, n2]], axis=0).astype(BF16)
        yv = _dot(g_ref[n2], bb)
        for t in range(ntile):
            yf_ref[t, pl.ds(n2, nseg, stride=HY_N2), :] = yv[:, t * 128:(t + 1) * 128]
    for t in range(ntile):
        ln = slice(t * 128, (t + 1) * 128)
        o_ref[0, :, ln] = (gate_ref[0, :, ln].astype(F32)
                           * (yf_ref[t] + zf_ref[t] * bias_ref[:, ln])).astype(BF16)


def _hyconv(zsrc, zcol, zrow, gsrc, gcol, grow, hspec, order, fmat, gmat, bias, ls):
    B = zsrc.shape[0]
    k1p = fmat.shape[1] // 2
    once = lambda a: pl.BlockSpec(a.shape, lambda b: (0,) * a.ndim, pipeline_mode=pl.Buffered(1))
    return pl.pallas_call(
        functools.partial(_hyconv_kernel, ls=ls),
        grid=(B,),
        in_specs=[pl.BlockSpec((1, ls, HY_W), lambda b: (b, zrow, zcol)),
                  pl.BlockSpec((1, ls, HY_W), lambda b: (b, grow, gcol)),
                  pl.BlockSpec((1, 2, HY_N2, k1p, HY_W), lambda b: (order, 0, 0, 0, 0),
                               pipeline_mode=pl.Buffered(1)),
                  once(fmat), once(gmat), pl.BlockSpec((1, HY_W), lambda b: (0, 0))],
        out_specs=pl.BlockSpec((1, ls, HY_W), lambda b: (b, 0, 0)),
        out_shape=jax.ShapeDtypeStruct((B, ls, HY_W), BF16),
        scratch_shapes=[pltpu.VMEM((HY_W // 128, ls, 128), F32), pltpu.VMEM((HY_W // 128, ls, 128), F32),
                        pltpu.VMEM((2, HY_N2, k1p, HY_W), F32)],
        compiler_params=_cparams(("arbitrary",)),
        name="hyena_longconv",
    )(zsrc, gsrc, hspec, fmat, gmat, bias)


def _outproj_kernel(x_ref, mod_ref, ya_ref, ybl_ref, ybc_ref, hf_ref, hb_ref, oc_ref, yd_ref, mlg_ref, g64_ref,
                    w_ref, g2_ref, wr_ref, xo_ref, h2_ref, aff_ref, *, nlat):
    hsum = hf_ref[0] + hb_ref[0]
    msq = _dot((hsum * hsum).astype(BF16), g64_ref[...])
    yc = jax.nn.sigmoid(oc_ref[0].astype(F32)) * (hsum * lax.rsqrt(msq + EPS) * mlg_ref[...])
    yb = jnp.where(pl.program_id(1) < nlat, ybl_ref[0], ybc_ref[0])
    y = jnp.concatenate([ya_ref[0], yb, yc.astype(BF16), yd_ref[0]], axis=1)
    mod = mod_ref[0, 0]
    x = x_ref[0] + mod[2:3, :] * _dot(y, w_ref[...])
    xo_ref[0] = x
    ms = jnp.mean(x * x, axis=-1, keepdims=True)
    h2 = x * lax.rsqrt(ms + EPS) * g2_ref[...] * (1.0 + mod[4:5, :]) + mod[3:4, :]
    h2_hi = h2.astype(BF16)
    h2_ref[0] = _pack_bf16_pair(h2_hi[:, 0:D_MODEL // 2], h2_hi[:, D_MODEL // 2:])
    h2_lo = (h2 - h2_hi.astype(F32)).astype(BF16)
    lg = _dot(h2_hi, wr_ref[...])
    logits = lg[:, 0:128] + lg[:, 128:256] + _dot(h2_lo, wr_ref[:, 0:128])
    lane = lax.broadcasted_iota(I32, logits.shape, 1)
    logits = jnp.where(lane < N_EXPERTS, logits, NEG)
    e = jnp.exp(logits - jnp.max(logits, axis=-1, keepdims=True))
    aff_ref[0] = (e / jnp.sum(e, axis=-1, keepdims=True)).T


def _outproj(x, mod, ya, yb_l, yb_c, hf, hb, oc, yd, mlg, g64, w_out, g2, w_router, T):
    B, N, D = x.shape
    rb = ROW_BLOCK
    nlat = T // rb
    row = lambda w: pl.BlockSpec((1, rb, w), lambda b, r: (b, r, 0))
    full = lambda a: pl.BlockSpec(a.shape, lambda b, r: (0,) * a.ndim)
    return pl.pallas_call(
        functools.partial(_outproj_kernel, nlat=nlat),
        grid=(B, N // rb),
        in_specs=[row(D), pl.BlockSpec((1, 1, 6, D), lambda b, r: (b, jnp.where(r >= nlat, 1, 0), 0, 0)),
                  row(256),
                  pl.BlockSpec((1, rb, 256), lambda b, r: (b, jnp.minimum(r, nlat - 1), 0)),
                  pl.BlockSpec((1, rb, 256), lambda b, r: (b, jnp.maximum(r - nlat, 0), 0)),
                  row(256), row(256), row(256), row(256),
                  full(mlg), full(g64), full(w_out), full(g2), full(w_router)],
        out_specs=[row(D), row(D // 2), pl.BlockSpec((1, 128, rb), lambda b, r: (b, 0, r))],
        out_shape=[jax.ShapeDtypeStruct((B, N, D), F32), jax.ShapeDtypeStruct((B, N, D // 2), I32),
                   jax.ShapeDtypeStruct((B, 128, N), F32)],
        compiler_params=_cparams(("arbitrary", "arbitrary")),
        name="outproj",
    )(x, mod, ya, yb_l, yb_c, hf, hb, oc, yd, mlg, g64, w_out, g2, w_router)


def _prefix_exclusive(x):
    n = x.shape[1]
    lane = lax.broadcasted_iota(I32, x.shape, 1)
    inc = x
    d = 1
    while d < n:
        inc = inc + jnp.where(lane >= d, pltpu.roll(inc, d, axis=1), 0)
        d *= 2
    return inc - x


def _topk_kernel(aff_ref, o_ref, pos_ref, val_ref, *, T, S):
    def segment(t0, ts, cap, slot0):
        a = aff_ref[0, :, t0:t0 + ts]
        bits = lax.bitcast_convert_type(a, I32)

        def search(i, v):
            cand = v | jnp.left_shift(jnp.int32(1), 30 - i)
            cnt = jnp.sum((bits >= cand).astype(I32), axis=1, keepdims=True)
            return jnp.where(cnt >= cap, cand, v)

        thr = lax.fori_loop(0, 31, search, jnp.zeros((N_EXPERTS, 1), I32))
        gt = bits > thr
        eq = bits == thr
        need = cap - jnp.sum(gt.astype(I32), axis=1, keepdims=True)
        sel = gt | (eq & (_prefix_exclusive(eq.astype(I32)) < need))
        seli = sel.astype(I32)
        pos_ref[:, 0:ts] = jnp.where(sel, _prefix_exclusive(seli), -1)
        a_hi = a.astype(BF16)
        r1 = a - a_hi.astype(F32)
        a_mid = r1.astype(BF16)
        a_lo = (r1 - a_mid.astype(F32)).astype(BF16)
        val_ref[0, :, 0:ts] = a_hi.astype(F32)
        val_ref[1, :, 0:ts] = a_mid.astype(F32)
        val_ref[2, :, 0:ts] = a_lo.astype(F32)
        tabs = lax.broadcasted_iota(I32, (1, ts), 1) + t0
        t_hi = jnp.right_shift(tabs, 6).astype(F32)
        t_lo = (tabs & 63).astype(F32)
        slot = lax.broadcasted_iota(I32, (cap, ts), 0)
        zeros = jnp.zeros((11, ts), F32)

        def per_expert(e, _):
            onehot = jnp.where(pos_ref[pl.ds(e, 1), 0:ts] == slot, 1.0, 0.0).astype(BF16)
            vals = jnp.concatenate([t_hi, t_lo, val_ref[0, pl.ds(e, 1), 0:ts], val_ref[1, pl.ds(e, 1), 0:ts],
                                    val_ref[2, pl.ds(e, 1), 0:ts], zeros], axis=0).astype(BF16)
            o_ref[0, e, slot0:slot0 + cap, :] = _dot_nt(onehot, vals)
            return 0

        lax.fori_loop(0, N_EXPERTS, per_expert, 0)

    segment(0, T, (EC_CAPACITY * T) // N_EXPERTS, 0)
    segment(T, S, (EC_CAPACITY * S) // N_EXPERTS, (EC_CAPACITY * T) // N_EXPERTS)


def _topk(aff_t, T, S):
    B, _, N = aff_t.shape
    E = N_EXPERTS
    cap_t = (EC_CAPACITY * T) // E + (EC_CAPACITY * S) // E
    return pl.pallas_call(
        functools.partial(_topk_kernel, T=T, S=S),
        grid=(B,),
        in_specs=[pl.BlockSpec((1, E, N), lambda b: (b, 0, 0))],
        out_specs=pl.BlockSpec((1, E, cap_t, 16), lambda b: (b, 0, 0, 0)),
        out_shape=jax.ShapeDtypeStruct((B, E, cap_t, 16), F32),
        scratch_shapes=[pltpu.VMEM((E, T), I32), pltpu.VMEM((3, E, T), F32)],
        compiler_params=_cparams(("arbitrary",)),
        name="expert_topk",
    )(aff_t)


def _gather_kernel(idx_ref, h_ref, *rest, cap_t):
    o_ref = rest[-1]
    b, e = pl.program_id(0), pl.program_id(1)
    base = (b * N_EXPERTS + e) * cap_t

    def body(g, _):
        rows = [h_ref[0, pl.ds(idx_ref[base + g * 16 + k], 1), :] for k in range(16)]
        o_ref[0, 0, pl.ds(pl.multiple_of(g * 16, 16), 16), :] = jnp.concatenate(rows, axis=0)
        return 0

    lax.fori_loop(0, cap_t // 16, body, 0)


def _gather(idx_flat, h2p, xe_buf, b0, bg, cap_t):
    B, N, hw = h2p.shape
    in_specs = [pl.BlockSpec((1, N, hw), lambda b, e, idx: (b0 + b, 0, 0))]
    args = [idx_flat, h2p]
    aliases = {}
    if xe_buf is not None:
        in_specs.append(pl.BlockSpec(memory_space=pl.ANY))
        args.append(xe_buf)
        aliases = {2: 0}
    return pl.pallas_call(
        functools.partial(_gather_kernel, cap_t=cap_t),
        grid_spec=pltpu.PrefetchScalarGridSpec(
            num_scalar_prefetch=1,
            grid=(bg, N_EXPERTS),
            in_specs=in_specs,
            out_specs=pl.BlockSpec((1, 1, cap_t, hw), lambda b, e, idx: (b0 + b, e, 0, 0)),
        ),
        out_shape=jax.ShapeDtypeStruct((B, N_EXPERTS, cap_t, hw), I32),
        input_output_aliases=aliases,
        compiler_params=_cparams(("arbitrary", "arbitrary")),
        name="expert_gather",
    )(*args)


def _ffn_kernel(x_ref, wg_ref, wu_ref, wd_ref, gate_ref, gt_ref, o_ref, wg_s, wu_s, wd_s, *, cap_l):
    @pl.when(pl.program_id(1) == 0)
    def _():
        wg_s[...] = wg_ref[0, 0].astype(BF16)
        wu_s[...] = wu_ref[0, 0].astype(BF16)
        wd_s[...] = wd_ref[0, 0].astype(BF16)

    x = jnp.concatenate(_unpack_bf16_pair(x_ref[0, 0]), axis=1)
    a = _dot(x, wg_s[...])
    u = _dot(x, wu_s[...])
    hmid = (a * jax.nn.sigmoid(a) * u).astype(BF16)
    y = _dot(hmid, wd_s[...])
    gt = gt_ref[0]
    row = lax.broadcasted_iota(I32, y.shape, 0)
    gt2 = jnp.where(row < cap_l, gt[0:1, :], gt[1:2, :])
    o_ref[0, 0] = y * gate_ref[0, 0] * gt2


def _ffn(xe, layer, wg, wu, wd, gate, gt2, cap_l):
    B, E, cap_t, hw = xe.shape
    D = 2 * hw
    wspec = pl.BlockSpec((1, 1, D, D), lambda e, b: (layer, e, 0, 0))
    return pl.pallas_call(
        functools.partial(_ffn_kernel, cap_l=cap_l),
        grid=(E, B),
        in_specs=[pl.BlockSpec((1, 1, cap_t, hw), lambda e, b: (b, e, 0, 0)), wspec, wspec, wspec,
                  pl.BlockSpec((1, 1, cap_t, 1), lambda e, b: (b, e, 0, 0)),
                  pl.BlockSpec((1, 2, D), lambda e, b: (b, 0, 0))],
        out_specs=pl.BlockSpec((1, 1, cap_t, D), lambda e, b: (b, e, 0, 0)),
        out_shape=jax.ShapeDtypeStruct((B, E, cap_t, D), F32),
        scratch_shapes=[pltpu.VMEM((D, D), BF16)] * 3,
        compiler_params=_cparams(("arbitrary", "arbitrary")),
        name="expert_ffn",
    )(xe, wg, wu, wd, gate, gt2)


def _scatter_kernel(idx_ref, x_ref, y_ref, o_ref, *, cap_t):
    b, e = pl.program_id(0), pl.program_id(2)
    base = (b * N_EXPERTS + e) * cap_t

    @pl.when(e == 0)
    def _():
        o_ref[...] = x_ref[...]

    def body(g, _):
        tile = y_ref[0, 0, pl.ds(pl.multiple_of(g * 8, 8), 8), :]
        rows = [idx_ref[base + g * 8 + k] for k in range(8)]
        cur = [o_ref[0, pl.ds(r, 1), :] for r in rows]
        for k, r in enumerate(rows):
            o_ref[0, pl.ds(r, 1), :] = cur[k] + tile[k:k + 1, :]
        return 0

    lax.fori_loop(0, cap_t // 8, body, 0)


def _scatter(idx_flat, x, ye, b0, bg, cap_t):
    B, N, D = x.shape
    hw = D // 2
    return pl.pallas_call(
        functools.partial(_scatter_kernel, cap_t=cap_t),
        grid_spec=pltpu.PrefetchScalarGridSpec(
            num_scalar_prefetch=1,
            grid=(bg, 2, N_EXPERTS),
            in_specs=[pl.BlockSpec((1, N, hw), lambda b, c, e, idx: (b0 + b, 0, c)),
                      pl.BlockSpec((1, 1, cap_t, hw), lambda b, c, e, idx: (b0 + b, e, 0, c))],
            out_specs=pl.BlockSpec((1, N, hw), lambda b, c, e, idx: (b0 + b, 0, c)),
        ),
        out_shape=jax.ShapeDtypeStruct((B, N, D), F32),
        input_output_aliases={1: 0},
        compiler_params=_cparams(("arbitrary", "arbitrary", "arbitrary")),
        name="expert_scatter",
    )(idx_flat, x, ye)


def _rope_tables(T, S, dh):
    rows = T // GRID_W
    r = jnp.broadcast_to(jnp.arange(rows, dtype=F32)[:, None], (rows, GRID_W)).reshape(T)
    col = jnp.broadcast_to(jnp.arange(GRID_W, dtype=F32)[None, :], (rows, GRID_W)).reshape(T)
    nf = dh // 4
    inv = ROPE_BASE ** (-jnp.arange(nf, dtype=F32) / nf)
    ar, ac = r[:, None] * inv, col[:, None] * inv
    ang = jnp.concatenate([ar, ar, ac, ac], axis=1)
    ang = jnp.concatenate([ang, jnp.zeros((S, dh), F32)], axis=0)
    reps = 256 // dh
    return jnp.tile(jnp.cos(ang), (1, reps)), jnp.tile(jnp.sin(ang), (1, reps))


def _group_mats(dh):
    i = np.arange(256)
    gmat = (i[:, None] // dh == i[None, :] // dh).astype(np.float32) / dh
    nf = dh // 4
    half = (i % (2 * nf)) // nf
    pmat = np.zeros((256, 256), np.float32)
    a_idx = i[half == 0]
    pmat[a_idx + nf, a_idx] = -1.0
    pmat[a_idx, a_idx + nf] = 1.0
    return jnp.asarray(gmat, BF16), jnp.asarray(pmat, BF16)


def _hyena_tables(ls):
    t = jnp.linspace(0.0, 1.0, ls, dtype=F32)[:, None]
    w = 2.0 * math.pi * jnp.arange(ls, dtype=F32)[:, None] / ls
    bands = jnp.linspace(1e-4, HY_BANDS - 1, HY_BANDS, dtype=F32)
    z = jnp.concatenate([t, jnp.cos(bands * w), -jnp.sin(bands * w)], axis=-1)
    z = jnp.pad(z, ((0, 0), (0, 128 - HY_EMB)))
    deltas = jnp.abs(jnp.linspace(math.log(HY_TARGET) / HY_SLOW, math.log(HY_TARGET) / HY_FAST, HY_W, dtype=F32))
    decay = jnp.exp(-t * deltas)
    n, nseg = 2 * ls, ls // HY_N2
    k1p = -(-(nseg + 1) // 8) * 8
    k1 = jnp.arange(k1p, dtype=I32)[None, :, None]
    tpos = HY_N2 * jnp.arange(nseg, dtype=I32)[None, None, :] + jnp.arange(HY_N2, dtype=I32)[:, None, None]
    ang = ((k1 * tpos) % n).astype(F32) * (2.0 * math.pi / n)
    keep = k1 <= nseg
    fmat = jnp.concatenate([jnp.where(keep, jnp.cos(ang), 0.0), jnp.where(keep, -jnp.sin(ang), 0.0)],
                           axis=1).astype(BF16)
    return z, decay, fmat, jnp.swapaxes(fmat, 1, 2)


def kernel(x, c, ctx, c_ctx, w_ada, b_ada, norm1_g, norm2_g, w_in, b_gate, a_qnorm, a_knorm, a_sink, hy_conv, hy_fw1, hy_fb1, hy_freq, hy_fw2, hy_fb2, hy_fw3, hy_bias, ml_norm, d_qnorm, d_knorm, d_lq1, d_lk1, d_lq2, d_lk2, d_subnorm, w_out, w_router, w_e_gate, w_e_up, w_e_down):
    B, T, D = x.shape
    S = ctx.shape[1]
    N = T + S
    depth = w_ada.shape[0]
    assert D == D_MODEL and T % S == 0 and S % ROW_BLOCK == 0 and S % MLSTM_CHUNK == 0
    assert S % WATTN_QROWS == 0 and T >= WATTN_QROWS + 2 * BLK and WATTN_QROWS & (WATTN_QROWS - 1) == 0
    cap_l, cap_c = (EC_CAPACITY * T) // N_EXPERTS, (EC_CAPACITY * S) // N_EXPERTS
    cap_t = cap_l + cap_c
    assert cap_t % 16 == 0

    xs = jnp.concatenate([x, ctx], axis=1)

    rpad = -(B + 1) % 8
    cc = jnp.concatenate([c, c_ctx[None, :], jnp.zeros((rpad, D), F32)], axis=0)
    mods = _ada_mods(cc, w_ada, b_ada)

    cos_a, sin_a = _rope_tables(T, S, DH_A)
    cos_d, sin_d = _rope_tables(T, S, DH_D)
    g64, p64 = _group_mats(DH_A)
    g32, p32 = _group_mats(DH_D)
    tabs_l = _hyena_tables(T)
    tabs_c = _hyena_tables(S)

    offs = np.cumsum((0, 256, 128, 128, 768, 256, 256, 256, 256, 16, 256, 256, 256))
    for l in range(depth):
        lam_init = 0.8 - 0.6 * math.exp(-0.3 * l)
        ml = mods[l]
        mod = jnp.stack([ml[:B].reshape(B, 6, D), jnp.broadcast_to(ml[B].reshape(1, 6, D), (B, 6, D))],
                        axis=1)
        w = w_in[l]
        wp = jnp.concatenate([w[:, offs[0]:offs[8]], w[:, offs[8]:offs[9]], jnp.zeros((D, 112), F32),
                              w[:, offs[9]:offs[12]]], axis=1).astype(BF16)
        gains = jnp.stack([jnp.tile(a_qnorm[l], 4), jnp.tile(a_knorm[l], 4),
                           jnp.tile(d_qnorm[l], 8), jnp.tile(d_knorm[l], 8)], axis=0)
        qa, ka, va, ub, qc, kc, vc, oc, gc, qd, kd, vd, gct, kct = _inproj(
            xs, mod, norm1_g[l].reshape(1, D), wp, (cos_a, sin_a, cos_d, sin_d), gains, (g64, g32, p64, p32), T)

        ya = _wattn(qa, ka, va, jnp.pad(a_sink[l], (0, 128 - H_A)).reshape(1, 128), T, S)
        yd = _dattn(qd, kd, vd, jnp.stack([d_lq1[l], d_lk1[l], d_lq2[l], d_lk2[l]], axis=0),
                    d_subnorm[l].reshape(1, 2 * DH_D), T, S, lam_init)

        bias = b_gate[l].reshape(-1)
        hf, hb = _mlstm(qc, kc, kct, vc, gc, gct, jnp.pad(bias, (0, 112)).reshape(1, 128), bias.reshape(16, 1), T, S)

        uc = _shortconv(ub, hy_conv[l], T, S)
        w1 = jnp.pad(hy_fw1[l], ((0, 128 - HY_EMB), (0, 0)))
        ybs = []
        for (zf, decay, fmat, gmat), ls, rowblk in ((tabs_l, T, 0), (tabs_c, S, T // S)):
            hpast, hfut, ss = _hyfilter(zf, w1, hy_fb1[l].reshape(1, -1), hy_freq[l].reshape(1, -1), hy_fw2[l],
                                        hy_fb2[l].reshape(1, -1), hy_fw3[l], decay)
            hspec = _hyspec(hpast, hfut, ss, fmat)
            z1 = _hyconv(uc, 0, rowblk, uc, 1, rowblk, hspec, 0, fmat, gmat, hy_bias[l, 0].reshape(1, HY_W), ls)
            z2 = _hyconv(z1, 0, 0, uc, 2, rowblk, hspec, 1, fmat, gmat, hy_bias[l, 1].reshape(1, HY_W), ls)
            ybs.append(z2)

        wr = jnp.pad(w_router[l], ((0, 0), (0, 128 - N_EXPERTS)))
        wr_hi = wr.astype(BF16)
        wr_cat = jnp.concatenate([wr_hi, (wr - wr_hi.astype(F32)).astype(BF16)], axis=1)
        xs, h2, aff = _outproj(xs, mod, ya, ybs[0], ybs[1], hf, hb, oc, yd, jnp.tile(ml_norm[l], 4).reshape(1, 256), g64,
                               w_out[l].astype(BF16), norm2_g[l].reshape(1, D), wr_cat, T)

        sel = _topk(aff, T, S)
        idx = (sel[..., 0] * 64.0 + sel[..., 1]).astype(I32)
        gate = (sel[..., 2] + sel[..., 3] + sel[..., 4])[..., None]
        gt2 = mod[:, :, 5, :]
        bg = min(IDX_BATCH_GROUP, B)
        groups = [(b0, idx[b0:b0 + bg].reshape(-1)) for b0 in range(0, B, bg)]
        xe = None
        for b0, idx_flat in groups:
            xe = _gather(idx_flat, h2, xe, b0, bg, cap_t)
        ye = _ffn(xe, l, w_e_gate, w_e_up, w_e_down, gate, gt2, cap_l)
        for b0, idx_flat in groups:
            xs = _scatter(idx_flat, xs, ye, b0, bg, cap_t)
    return xs[:, 0:T]
```

```python
import functools
import math

import jax
import jax.numpy as jnp
import numpy as np
from jax import lax
from jax.experimental import pallas as pl
from jax.experimental.pallas import tpu as pltpu

F32 = jnp.float32
BF16 = jnp.bfloat16
I32 = jnp.int32
HI = lax.Precision.HIGHEST

D_MODEL = 1024
GRID_W = 64
GROUP_W = D_MODEL // 4
H_A, HKV_A = 4, 2
G_A = H_A // HKV_A
DH_A = GROUP_W // H_A
WINDOW = 128
BLK = 128
HY_W = GROUP_W
HY_ORDER = 2
HY_BANDS = 16
HY_EMB = 1 + 2 * HY_BANDS
HY_HID = 64
HY_TARGET, HY_FAST, HY_SLOW = 1e-2, 0.3, 1.5
HY_N2 = 16
H_C = 4
DH_C = GROUP_W // H_C
H_D = 4
DH_D = GROUP_W // (2 * H_D)
N_EXPERTS = 16
EC_CAPACITY = 2
ROPE_BASE = 10000.0
EPS = 1e-6
NEG = -1e30

ROW_BLOCK = 256
WATTN_QROWS = 256
DATTN_MIN_ROWSUM = 1e-25
MLSTM_CHUNK = 256
IDX_BATCH_GROUP = 4
VMEM_LIMIT = 56 * 1024 * 1024

_C_QA, _C_KA, _C_VA, _C_UB = 0, 256, 384, 512
_C_QC, _C_KC, _C_VC, _C_OC, _C_GC = 1280, 1536, 1792, 2048, 2304
_C_QD, _C_KD, _C_VD, _C_END = 2432, 2688, 2944, 3200


def _cparams(sem, vmem=VMEM_LIMIT):
    return pltpu.CompilerParams(dimension_semantics=sem, vmem_limit_bytes=vmem)


def _dot(a, b, **kw):
    return jnp.dot(a, b, preferred_element_type=F32, **kw)


def _dot_nt(a, b, **kw):
    return lax.dot_general(a, b, (((1,), (1,)), ((), ())), preferred_element_type=F32, **kw)


def _dot_tn(a, b, **kw):
    return lax.dot_general(a, b, (((0,), (0,)), ((), ())), preferred_element_type=F32, **kw)


def _pack_bf16_pair(a, b):
    ua = lax.bitcast_convert_type(a.astype(F32), I32)
    ub = lax.bitcast_convert_type(b.astype(F32), I32)
    return ua | lax.shift_right_logical(ub, 16)


def _unpack_bf16_pair(w):
    a = lax.bitcast_convert_type(w & jnp.int32(-65536), F32)
    b = lax.bitcast_convert_type(lax.shift_left(w, 16), F32)
    return a.astype(BF16), b.astype(BF16)


def _ada_kernel(c_ref, w_ref, b_ref, o_ref):
    c = c_ref[...]
    s = c * jax.nn.sigmoid(c)
    o_ref[0] = _dot(s, w_ref[0], precision=HI) + b_ref[0]


def _ada_mods(cc, w_ada, b_ada):
    L, D, W6 = w_ada.shape
    R = cc.shape[0]
    cb = 1536
    return pl.pallas_call(
        _ada_kernel,
        grid=(L, W6 // cb),
        in_specs=[pl.BlockSpec((R, D), lambda l, j: (0, 0)),
                  pl.BlockSpec((1, D, cb), lambda l, j: (l, 0, j)),
                  pl.BlockSpec((1, 1, cb), lambda l, j: (l, 0, j))],
        out_specs=pl.BlockSpec((1, R, cb), lambda l, j: (l, 0, j)),
        out_shape=jax.ShapeDtypeStruct((L, R, W6), F32),
        compiler_params=_cparams(("arbitrary", "arbitrary")),
        name="ada_mods",
    )(cc, w_ada, b_ada.reshape(L, 1, W6))


def _inproj_kernel(x_ref, mod_ref, g1_ref, w_ref, ca_ref, sa_ref, cd_ref, sd_ref,
                   gains_ref, g64_ref, g32_ref, p64_ref, p32_ref,
                   qa, ka, va, ub, qc, kc, vc, oc, gc, qd, kd, vd, gct, kct):
    x = x_ref[0]
    ms = jnp.mean(x * x, axis=-1, keepdims=True)
    xn = x * lax.rsqrt(ms + EPS) * g1_ref[...]
    mod = mod_ref[0, 0]
    h = xn * (1.0 + mod[1:2, :]) + mod[0:1, :]
    p = _dot(h.astype(BF16), w_ref[...])

    def headnorm_rope(t, gmat, gain, pmat, cos, sin, scale):
        w = t.shape[1]
        msq = _dot((t * t).astype(BF16), gmat[0:w, 0:w])
        tn = t * lax.rsqrt(msq + EPS) * gain
        tn = tn * cos[:, 0:w] + _dot(tn.astype(BF16), pmat[0:w, 0:w]) * sin[:, 0:w]
        return tn * scale

    ca, sa, cd, sd = ca_ref[...], sa_ref[...], cd_ref[...], sd_ref[...]
    gains = gains_ref[...]
    qa[0] = headnorm_rope(p[:, _C_QA:_C_KA], g64_ref, gains[0:1, :], p64_ref, ca, sa, DH_A ** -0.5).astype(BF16)
    ka[0] = headnorm_rope(p[:, _C_KA:_C_VA], g64_ref, gains[1:2, 0:128], p64_ref, ca, sa, 1.0).astype(BF16)
    va[0] = p[:, _C_VA:_C_UB].astype(BF16)
    ub[0] = p[:, _C_UB:_C_QC].astype(BF16)
    qc[0] = p[:, _C_QC:_C_KC].astype(BF16)
    kscaled = p[:, _C_KC:_C_VC] * DH_C ** -0.5
    kc[0] = kscaled.astype(BF16)
    kct[0] = kscaled.T.astype(BF16)
    vc[0] = p[:, _C_VC:_C_OC].astype(BF16)
    oc[0] = p[:, _C_OC:_C_GC].astype(BF16)
    gc[0] = p[:, _C_GC:_C_QD]
    gct[0] = p[:, _C_GC:_C_QD].T
    qd[0] = headnorm_rope(p[:, _C_QD:_C_KD], g32_ref, gains[2:3, :], p32_ref, cd, sd, DH_D ** -0.5).astype(BF16)
    kd[0] = headnorm_rope(p[:, _C_KD:_C_VD], g32_ref, gains[3:4, :], p32_ref, cd, sd, 1.0).astype(BF16)
    vd[0] = p[:, _C_VD:_C_END].astype(BF16)


def _inproj(x, mod, g1, w_packed, tabs, gains, mats, T):
    B, N, D = x.shape
    rb = ROW_BLOCK
    nlat = T // rb
    row = lambda w: pl.BlockSpec((1, rb, w), lambda b, r: (b, r, 0))
    tab = pl.BlockSpec((rb, 256), lambda b, r: (r, 0))
    full = lambda a: pl.BlockSpec(a.shape, lambda b, r: (0,) * a.ndim)
    widths = [256, 128, 128, 768, 256, 256, 256, 256, 128, 256, 256, 256]
    dtypes = [BF16] * 8 + [F32] + [BF16] * 3
    return pl.pallas_call(
        _inproj_kernel,
        grid=(B, N // rb),
        in_specs=[row(D),
                  pl.BlockSpec((1, 1, 6, D), lambda b, r: (b, jnp.where(r >= nlat, 1, 0), 0, 0)),
                  full(g1), full(w_packed), tab, tab, tab, tab, full(gains)] + [full(m) for m in mats],
        out_specs=[row(w) for w in widths] + [pl.BlockSpec((1, 128, rb), lambda b, r: (b, 0, r)),
                                              pl.BlockSpec((1, 256, rb), lambda b, r: (b, 0, r))],
        out_shape=[jax.ShapeDtypeStruct((B, N, w), dt) for w, dt in zip(widths, dtypes)]
        + [jax.ShapeDtypeStruct((B, 128, N), F32), jax.ShapeDtypeStruct((B, 256, N), BF16)],
        compiler_params=_cparams(("arbitrary", "arbitrary")),
        name="inproj",
    )(x, mod, g1, w_packed, *tabs, gains, *mats)


def _wattn_kernel(q_ref, k_ref, v_ref, sink_ref, o_ref, *, T, S):
    j = pl.program_id(1)
    qb, kwin = WATTN_QROWS, WATTN_QROWS + 2 * BLK
    is_lat = j < T // qb
    start = pl.multiple_of(jnp.clip(j * qb - BLK, 0, T - kwin), BLK)
    kw = k_ref[0, pl.ds(start, kwin), :]
    vw = v_ref[0, pl.ds(start, kwin), :]
    kc = k_ref[0, T:T + S, :]
    vc = v_ref[0, T:T + S, :]
    row = lax.broadcasted_iota(I32, (G_A * qb, kwin), 0)
    qpos = j * qb + (row & (qb - 1))
    kpos = start + lax.broadcasted_iota(I32, (G_A * qb, kwin), 1)
    valid = (jnp.abs(qpos - kpos) <= WINDOW) & is_lat
    grp = jnp.right_shift(lax.broadcasted_iota(I32, (G_A * qb, 1), 0), qb.bit_length() - 1)
    sinks = sink_ref[...]
    for hk in range(HKV_A):
        cs = slice(hk * DH_A, (hk + 1) * DH_A)
        kwh, vwh, kch, vch = kw[:, cs], vw[:, cs], kc[:, cs], vc[:, cs]
        heads = [hk * G_A + g for g in range(G_A)]
        q = jnp.concatenate([q_ref[0, :, hd * DH_A:(hd + 1) * DH_A] for hd in heads], axis=0)
        sink = sinks[0:1, heads[0]:heads[0] + 1]
        for g in range(1, G_A):
            sink = jnp.where(grp == g, sinks[0:1, heads[g]:heads[g] + 1], sink)
        s_loc = jnp.where(valid, _dot_nt(q, kwh), NEG)
        s_ctx = _dot_nt(q, kch)
        m = jnp.maximum(jnp.maximum(jnp.max(s_loc, axis=-1, keepdims=True),
                                    jnp.max(s_ctx, axis=-1, keepdims=True)), sink)
        p_loc = jnp.exp(s_loc - m)
        p_ctx = jnp.exp(s_ctx - m)
        den = (jnp.sum(p_loc, axis=-1, keepdims=True) + jnp.sum(p_ctx, axis=-1, keepdims=True)
               + jnp.exp(sink - m))
        o = (_dot(p_loc.astype(BF16), vwh) + _dot(p_ctx.astype(BF16), vch)) / den
        for g, hd in enumerate(heads):
            o_ref[0, :, hd * DH_A:(hd + 1) * DH_A] = o[g * qb:(g + 1) * qb].astype(BF16)


def _wattn(qa, ka, va, sink, T, S):
    B, N, _ = qa.shape
    qb = WATTN_QROWS
    return pl.pallas_call(
        functools.partial(_wattn_kernel, T=T, S=S),
        grid=(B, N // qb),
        in_specs=[pl.BlockSpec((1, qb, 256), lambda b, j: (b, j, 0)),
                  pl.BlockSpec((1, N, 128), lambda b, j: (b, 0, 0)),
                  pl.BlockSpec((1, N, 128), lambda b, j: (b, 0, 0)),
                  pl.BlockSpec((1, 128), lambda b, j: (0, 0))],
        out_specs=pl.BlockSpec((1, qb, 256), lambda b, j: (b, j, 0)),
        out_shape=jax.ShapeDtypeStruct((B, N, 256), BF16),
        compiler_params=_cparams(("arbitrary", "arbitrary")),
        name="window_attn",
    )(qa, ka, va, sink)


def _dattn_kernel(q_ref, k_ref, v_ref, lam_ref, sub_ref, place_ref, gsum_ref, o_ref,
                  vaug_ref, kaug_ref, kmax_ref, kch_ref, qch_ref, tch_ref, *, T, S, lam_init):
    j = pl.program_id(1)
    lv = lam_ref[...]
    lam = (jnp.exp(jnp.sum(lv[0:1, :] * lv[1:2, :], axis=-1, keepdims=True))
           - jnp.exp(jnp.sum(lv[2:3, :] * lv[3:4, :], axis=-1, keepdims=True)) + lam_init)
    subg = sub_ref[...] * (1.0 - lam_init)
    dv = 2 * DH_D
    nchain, aw = 2 * H_D, 2 * DH_D

    @pl.when(j == 0)
    def _():
        ones = jnp.where(lax.broadcasted_iota(I32, (T + S, dv), 1) == 0, 1.0, 0.0).astype(BF16)
        for hd in range(H_D):
            vaug_ref[hd] = jnp.concatenate([v_ref[0, :, hd * dv:(hd + 1) * dv], ones], axis=1)
        k_wide = _dot(k_ref[0], place_ref[...])
        lane = lax.broadcasted_iota(I32, (1, nchain * aw), 1)
        k_aug = (k_wide + jnp.where((lane & (aw - 1)) == DH_D, 1.0, 0.0)).astype(BF16)
        kaug_ref[...] = k_aug
        for i in range(nchain):
            kch_ref[i] = k_aug[:, i * aw:(i + 1) * aw]
        ksq = _dot((k_wide * k_wide).astype(BF16), gsum_ref[...])
        kmax_ref[...] = jnp.sqrt(jnp.max(ksq, axis=0, keepdims=True))

    def finish(hd, terms):
        o = terms[0] - lam * terms[1]
        ms = jnp.mean(o * o, axis=-1, keepdims=True)
        o_ref[0, :, hd * dv:(hd + 1) * dv] = (o * lax.rsqrt(ms + EPS) * subg).astype(BF16)

    def body_fast(k0, nk):
        q_wide = _dot(q_ref[0], place_ref[...])
        qn = jnp.sqrt(_dot((q_wide * q_wide).astype(BF16), gsum_ref[...]))
        q_aug = (q_wide - qn * kmax_ref[...]).astype(BF16)
        terms, low = [], None
        for i in range(nchain):
            hd, cs = i // 2, slice(i * aw, (i + 1) * aw)
            s = _dot_nt(q_aug[:, cs], kaug_ref[k0:k0 + nk, cs])
            pv = _dot(jnp.exp(s.astype(BF16)), vaug_ref[hd, k0:k0 + nk, :])
            sigma = pv[:, dv:dv + 1]
            low = sigma if low is None else jnp.minimum(low, sigma)
            terms.append(pv[:, 0:dv] / sigma)
            if i % 2 == 1:
                finish(hd, terms)
                terms = []
        return jnp.min(low)

    def body_exact(k0, nk):
        q_wide = _dot(q_ref[0], place_ref[...])
        for i in range(nchain):
            qch_ref[i] = q_wide[:, i * aw:(i + 1) * aw].astype(BF16)

        def chain(i, carry):
            s = _dot_nt(qch_ref[i], kch_ref[i, k0:k0 + nk, :])
            e = jnp.exp((s - jnp.max(s, axis=-1, keepdims=True)).astype(BF16))
            pv = _dot(e, vaug_ref[jnp.right_shift(i, 1), k0:k0 + nk, :])
            tch_ref[i] = pv / pv[:, dv:dv + 1]
            return carry

        lax.fori_loop(0, nchain, chain, 0)
        for hd in range(H_D):
            finish(hd, [tch_ref[2 * hd][:, 0:dv], tch_ref[2 * hd + 1][:, 0:dv]])

    def body(k0, nk):
        low = body_fast(k0, nk)

        @pl.when(jnp.logical_not(low > DATTN_MIN_ROWSUM))
        def _():
            body_exact(k0, nk)

    @pl.when(j < T // ROW_BLOCK)
    def _():
        body(0, T + S)

    @pl.when(j >= T // ROW_BLOCK)
    def _():
        body(T, S)


def _dattn_mats():
    nchain, aw = 2 * H_D, 2 * DH_D
    place = np.zeros((nchain * DH_D, nchain * aw), np.float32)
    gsum = np.zeros((nchain * aw, nchain * aw), np.float32)
    for i in range(nchain):
        for d in range(DH_D):
            place[i * DH_D + d, i * aw + d] = 1.0
            gsum[i * aw + d, i * aw + DH_D] = 1.0
    return jnp.asarray(place, BF16), jnp.asarray(gsum, BF16)


def _dattn(qd, kd, vd, lam_vecs, subg, T, S, lam_init):
    B, N, _ = qd.shape
    rb = ROW_BLOCK
    place, gsum = _dattn_mats()
    full = lambda a: pl.BlockSpec(a.shape, lambda b, j: (0,) * a.ndim)
    return pl.pallas_call(
        functools.partial(_dattn_kernel, T=T, S=S, lam_init=lam_init),
        grid=(B, N // rb),
        in_specs=[pl.BlockSpec((1, rb, 256), lambda b, j: (b, j, 0)),
                  pl.BlockSpec((1, N, 256), lambda b, j: (b, 0, 0)),
                  pl.BlockSpec((1, N, 256), lambda b, j: (b, 0, 0)),
                  pl.BlockSpec((4, DH_D), lambda b, j: (0, 0)),
                  pl.BlockSpec((1, 2 * DH_D), lambda b, j: (0, 0)), full(place), full(gsum)],
        out_specs=pl.BlockSpec((1, rb, 256), lambda b, j: (b, j, 0)),
        out_shape=jax.ShapeDtypeStruct((B, N, 256), BF16),
        scratch_shapes=[pltpu.VMEM((H_D, N, 4 * DH_D), BF16), pltpu.VMEM((N, 2 * 256), BF16),
                        pltpu.VMEM((1, 2 * 256), F32), pltpu.VMEM((2 * H_D, N, 2 * DH_D), BF16),
                        pltpu.VMEM((2 * H_D, rb, 2 * DH_D), BF16), pltpu.VMEM((2 * H_D, rb, 4 * DH_D), F32)],
        compiler_params=_cparams(("arbitrary", "arbitrary")),
        name="diff_attn",
    )(qd, kd, vd, lam_vecs, subg, place, gsum)


def _log_sigmoid(x):
    return jnp.minimum(x, 0.0) - jnp.log1p(jnp.exp(-jnp.abs(x)))


def _mlstm_kernel(qf_ref, kf_ref, ktf_ref, vf_ref, gf_ref, gtf_ref, qb_ref, kb_ref, ktb_ref, vb_ref, gb_ref,
                  gtb_ref, bias_ref, biast_ref, hf_ref, hb_ref, ct_ref, m_ref):
    i = pl.program_id(1)
    lc = MLSTM_CHUNK

    @pl.when(i == 0)
    def _():
        ct_ref[...] = jnp.zeros_like(ct_ref)
        m_ref[...] = jnp.zeros_like(m_ref)

    r_io = lax.broadcasted_iota(I32, (lc, lc), 0)
    c_io = lax.broadcasted_iota(I32, (lc, lc), 1)
    lower = r_io >= c_io
    upper = r_io <= c_io
    tri_lo = jnp.where(lower, 1.0, 0.0).astype(F32)
    tri_up = jnp.where(upper, 1.0, 0.0).astype(F32)
    ones_col = jnp.where(lax.broadcasted_iota(I32, (lc, DH_C), 1) == 0, 1.0, 0.0).astype(BF16)

    def direction(d, q_ref, k_ref, kt_ref, v_ref, g_ref, gt_ref, out_ref):
        pre = g_ref[0] + bias_ref[...]
        pre_t = gt_ref[0] + biast_ref[...]
        lf = _log_sigmoid(pre)
        lf_t = _log_sigmoid(pre_t)
        if d == 0:
            bcol = _dot(tri_lo, lf, precision=HI)
            brow = _dot(lf_t, tri_up, precision=HI)
            valid, end_row = lower, lc - 1
        else:
            bcol = _dot(tri_up, lf, precision=HI)
            brow = _dot(lf_t, tri_lo, precision=HI)
            valid, end_row = upper, 0
        for hd in range(H_C):
            ci, cf = d * H_C + hd, (2 + d) * H_C + hd
            cs = slice(hd * DH_C, (hd + 1) * DH_C)
            q, k, v = q_ref[0, :, cs], k_ref[0, :, cs], v_ref[0, :, cs]
            v_aug = jnp.concatenate([v, ones_col], axis=1)
            b_t = bcol[:, cf:cf + 1]
            ig = pre[:, ci:ci + 1]
            b_end = bcol[end_row:end_row + 1, cf:cf + 1]
            slot = d * H_C + hd
            ct = ct_ref[slot]
            m = m_ref[slot][:, 0:1]
            logw = jnp.where(valid, b_t - (brow[cf:cf + 1, :] - pre_t[ci:ci + 1, :]), NEG)
            inter = b_t + m
            m_t = jnp.maximum(inter, jnp.max(logw, axis=-1, keepdims=True))
            w_in = jnp.exp(inter - m_t)
            smat = _dot_nt(q, k) * jnp.exp(logw - m_t)
            num = w_in * _dot(q, ct.astype(BF16)) + _dot(smat.astype(BF16), v_aug)
            den = jnp.maximum(jnp.abs(num[:, DH_C:DH_C + 1]), jnp.exp(-m_t))
            out_ref[0, :, cs] = num[:, 0:DH_C] / den
            g_end = b_end - b_t + ig
            m_new = jnp.maximum(b_end + m, jnp.max(g_end, axis=0, keepdims=True))
            w_tok = jnp.exp(g_end - m_new)
            wv = (w_tok * v_aug.astype(F32)).astype(BF16)
            ct_ref[slot] = jnp.exp(b_end + m - m_new) * ct + _dot(kt_ref[0, cs, :], wv)
            m_ref[slot] = jnp.broadcast_to(m_new, (1, 128))

    direction(0, qf_ref, kf_ref, ktf_ref, vf_ref, gf_ref, gtf_ref, hf_ref)
    direction(1, qb_ref, kb_ref, ktb_ref, vb_ref, gb_ref, gtb_ref, hb_ref)


def _mlstm(qc, kc, kct, vc, gc, gct, bias, biast, T, S):
    B, N, _ = qc.shape
    lc = MLSTM_CHUNK
    nch, nlat = N // lc, T // lc
    fwd = lambda i: jnp.where(i < nch - nlat, nlat + i, i - (nch - nlat))
    bwd = lambda i: nch - 1 - i
    tok = lambda f: pl.BlockSpec((1, lc, 256), lambda b, i: (b, f(i), 0))
    tok_t = lambda f: pl.BlockSpec((1, 256, lc), lambda b, i: (b, 0, f(i)))
    gate = lambda f: pl.BlockSpec((1, lc, 128), lambda b, i: (b, f(i), 0))
    gate_t = lambda f: pl.BlockSpec((1, 16, lc), lambda b, i: (b, 0, f(i)))
    return pl.pallas_call(
        _mlstm_kernel,
        grid=(B, nch),
        in_specs=[tok(fwd), tok(fwd), tok_t(fwd), tok(fwd), gate(fwd), gate_t(fwd),
                  tok(bwd), tok(bwd), tok_t(bwd), tok(bwd), gate(bwd), gate_t(bwd),
                  pl.BlockSpec((1, 128), lambda b, i: (0, 0)),
                  pl.BlockSpec((16, 1), lambda b, i: (0, 0))],
        out_specs=[tok(fwd), tok(bwd)],
        out_shape=[jax.ShapeDtypeStruct((B, N, 256), F32)] * 2,
        scratch_shapes=[pltpu.VMEM((2 * H_C, DH_C, 128), F32), pltpu.VMEM((2 * H_C, 1, 128), F32)],
        compiler_params=_cparams(("arbitrary", "arbitrary")),
        name="mlstm",
    )(qc, kc, kct, vc, gc, gct, qc, kc, kct, vc, gc, gct, bias, biast)


def _shortconv_kernel(u_ref, w_ref, o_ref, *, T, S):
    u = u_ref[0].astype(F32)
    n = u.shape[0]
    w = w_ref[...]
    row = lax.broadcasted_iota(I32, u.shape, 0)
    zero = jnp.zeros((1, u.shape[1]), F32)
    prev = jnp.concatenate([zero, u[0:n - 1]], axis=0)
    nxt = jnp.concatenate([u[1:n], zero], axis=0)
    prev = jnp.where(row == T, 0.0, prev)
    nxt = jnp.where(row == T - 1, 0.0, nxt)
    o_ref[0] = (w[0:1, :] * prev + w[1:2, :] * u + w[2:3, :] * nxt).astype(BF16)


def _shortconv(ub, conv_w, T, S):
    B, N, W = ub.shape
    return pl.pallas_call(
        functools.partial(_shortconv_kernel, T=T, S=S),
        grid=(B, W // 256),
        in_specs=[pl.BlockSpec((1, N, 256), lambda b, j: (b, 0, j)),
                  pl.BlockSpec((3, 256), lambda b, j: (0, j))],
        out_specs=pl.BlockSpec((1, N, 256), lambda b, j: (b, 0, j)),
        out_shape=jax.ShapeDtypeStruct((B, N, W), BF16),
        compiler_params=_cparams(("arbitrary", "arbitrary")),
        name="hyena_shortconv",
    )(ub, conv_w)


def _hyfilter_kernel(z_ref, w1_ref, b1_ref, fr_ref, w2_ref, b2_ref, w3_ref, dec_ref,
                     hp_ref, hf_ref, ss_ref):
    i = pl.program_id(0)
    fr = fr_ref[...]
    h = jnp.sin(fr * (_dot(z_ref[...], w1_ref[...], precision=HI) + b1_ref[...]))
    h = jnp.sin(fr * (_dot(h, w2_ref[...], precision=HI) + b2_ref[...]))
    h = _dot(h, w3_ref[...], precision=HI)
    dec = dec_ref[...]
    dec2 = jnp.concatenate([dec, dec], axis=1)
    hw = HY_ORDER * HY_W
    h0 = h[:, 0:hw] * dec2
    h1 = h[:, hw:2 * hw] * dec2
    rows = h0.shape[0]
    t = i * rows + lax.broadcasted_iota(I32, h0.shape, 0)
    h1 = jnp.where(t == 0, 0.0, h1)
    hp_ref[...] = h0
    hf_ref[...] = h1

    @pl.when(i == 0)
    def _():
        ss_ref[...] = jnp.zeros_like(ss_ref)

    ss_ref[...] += jnp.sum(h0 * h0 + h1 * h1, axis=0, keepdims=True)


def _hyfilter(zfeat, w1, b1, fr, w2, b2, w3, decay):
    ls = zfeat.shape[0]
    rb = min(ls, 512)
    hw = HY_ORDER * HY_W
    full = lambda a: pl.BlockSpec(a.shape, lambda i: (0,) * a.ndim)
    return pl.pallas_call(
        _hyfilter_kernel,
        grid=(ls // rb,),
        in_specs=[pl.BlockSpec((rb, 128), lambda i: (i, 0)), full(w1), full(b1), full(fr), full(w2),
                  full(b2), full(w3), pl.BlockSpec((rb, HY_W), lambda i: (i, 0))],
        out_specs=[pl.BlockSpec((rb, hw), lambda i: (i, 0)), pl.BlockSpec((rb, hw), lambda i: (i, 0)),
                   pl.BlockSpec((1, hw), lambda i: (0, 0))],
        out_shape=[jax.ShapeDtypeStruct((ls, hw), F32), jax.ShapeDtypeStruct((ls, hw), F32),
                   jax.ShapeDtypeStruct((1, hw), F32)],
        compiler_params=_cparams(("arbitrary",)),
        name="hyena_filter",
    )(zfeat, w1, b1, fr, w2, b2, w3, decay)


def _cmul_const(v, c, s):
    vr, vi = v
    r = math.sqrt(0.5)
    if abs(s) < 1e-9:
        return (vr, vi) if c > 0 else (-vr, -vi)
    if abs(c) < 1e-9:
        return (-vi, vr) if s > 0 else (vi, -vr)
    if abs(abs(c) - r) < 1e-9 and abs(abs(s) - r) < 1e-9:
        a, b = (vr if c > 0 else -vr), (vi if s > 0 else -vi)
        p, q = (vi if c > 0 else -vi), (vr if s > 0 else -vr)
        return (a - b) * r, (p + q) * r
    return c * vr - s * vi, c * vi + s * vr


def _fft_pow2(xs, sign):
    n = len(xs)
    if n == 1:
        return xs
    ev, od = _fft_pow2(xs[0::2], sign), _fft_pow2(xs[1::2], sign)
    out = [None] * n
    for k in range(n // 2):
        ang = sign * 2.0 * math.pi * k / n
        tr, ti = _cmul_const(od[k], math.cos(ang), math.sin(ang))
        out[k] = (ev[k][0] + tr, ev[k][1] + ti)
        out[k + n // 2] = (ev[k][0] - tr, ev[k][1] - ti)
    return out


def _hy_stage1(src_ref, f_ref, a_ref, nseg):
    k1p = a_ref.shape[2]
    for n2 in range(HY_N2):
        if len(src_ref.shape) == 3:
            zs = jnp.concatenate([src_ref[t, pl.ds(n2, nseg, stride=HY_N2), :] for t in range(src_ref.shape[0])],
                                 axis=1).astype(BF16)
        else:
            zs = src_ref[pl.ds(n2, nseg, stride=HY_N2), :].astype(BF16)
        r = _dot(f_ref[n2], zs)
        a_ref[0, n2] = r[0:k1p]
        a_ref[1, n2] = r[k1p:2 * k1p]


def _hy_chunks(a_ref):
    k1p, lanes = a_ref.shape[2], a_ref.shape[3]
    return k1p // 8, [slice(t * 128, (t + 1) * 128) for t in range(lanes // 128)]


def _hyspec_kernel(hp_ref, hf_ref, ss_ref, f_ref, h_ref, ap_ref, af_ref, *, ls):
    nseg = ls // HY_N2
    _hy_stage1(hp_ref, f_ref, ap_ref, nseg)
    _hy_stage1(hf_ref, f_ref, af_ref, nseg)
    nchunk, lane_tiles = _hy_chunks(ap_ref)

    def body(i, _):
        rows = pl.ds(pl.multiple_of(i * 8, 8), 8)
        k1 = i * 8 + lax.broadcasted_iota(I32, (8, 1), 0)
        wk = jnp.where((k1 == 0) | (k1 == nseg), 1.0, 2.0) * (1.0 / (2 * ls))
        for ln in lane_tiles:
            scale = lax.rsqrt(ss_ref[:, ln] + EPS) * wk
            P = _fft_pow2([(ap_ref[0, n2, rows, ln], ap_ref[1, n2, rows, ln]) for n2 in range(HY_N2)], -1)
            Q = _fft_pow2([(af_ref[0, n2, rows, ln], af_ref[1, n2, rows, ln]) for n2 in range(HY_N2)], -1)
            for k2 in range(HY_N2):
                h_ref[0, 0, k2, rows, ln] = (P[k2][0] + Q[k2][0]) * scale
                h_ref[0, 1, k2, rows, ln] = (P[k2][1] - Q[k2][1]) * scale
        return 0

    lax.fori_loop(0, nchunk, body, 0)


def _hyspec(hp, hf, ss, fmat):
    ls = hp.shape[0]
    k1p = fmat.shape[1] // 2
    full = lambda a: pl.BlockSpec(a.shape, lambda o, c: (0,) * a.ndim)
    nc = HY_W // 128
    col = lambda o, c: (0, o * nc + c)
    return pl.pallas_call(
        functools.partial(_hyspec_kernel, ls=ls),
        grid=(HY_ORDER, nc),
        in_specs=[pl.BlockSpec((ls, 128), col), pl.BlockSpec((ls, 128), col), pl.BlockSpec((1, 128), col),
                  full(fmat)],
        out_specs=pl.BlockSpec((1, 2, HY_N2, k1p, 128), lambda o, c: (o, 0, 0, 0, c)),
        out_shape=jax.ShapeDtypeStruct((HY_ORDER, 2, HY_N2, k1p, HY_W), F32),
        scratch_shapes=[pltpu.VMEM((2, HY_N2, k1p, 128), F32)] * 2,
        compiler_params=_cparams(("arbitrary", "arbitrary")),
        name="hyena_spectrum",
    )(hp, hf, ss, fmat)


def _hyconv_kernel(z_ref, gate_ref, h_ref, f_ref, g_ref, bias_ref, o_ref, zf_ref, yf_ref, a_ref, *, ls):
    nseg = ls // HY_N2
    ntile = zf_ref.shape[0]
    for t in range(ntile):
        zf_ref[t] = z_ref[0, :, t * 128:(t + 1) * 128].astype(F32)
    _hy_stage1(zf_ref, f_ref, a_ref, nseg)
    nchunk, lane_tiles = _hy_chunks(a_ref)

    def body(i, _):
        rows = pl.ds(pl.multiple_of(i * 8, 8), 8)
        for ln in lane_tiles:
            X = _fft_pow2([(a_ref[0, n2, rows, ln], a_ref[1, n2, rows, ln]) for n2 in range(HY_N2)], -1)
            Y = []
            for k2 in range(HY_N2):
                hr, hi = h_ref[0, 0, k2, rows, ln], h_ref[0, 1, k2, rows, ln]
                xr, xi = X[k2]
                Y.append((xr * hr - xi * hi, xr * hi + xi * hr))
            Bv = _fft_pow2(Y, 1)
            for n2 in range(HY_N2):
                a_ref[0, n2, rows, ln] = Bv[n2][0]
                a_ref[1, n2, rows, ln] = Bv[n2][1]
        return 0

    lax.fori_loop(0, nchunk, body, 0)
    for n2 in range(HY_N2):
        bb = jnp.concatenate([a_ref[0, n2], a_ref[1, n2]], axis=0).astype(BF16)
        yv = _dot(g_ref[n2], bb)
        for t in range(ntile):
            yf_ref[t, pl.ds(n2, nseg, stride=HY_N2), :] = yv[:, t * 128:(t + 1) * 128]
    for t in range(ntile):
        ln = slice(t * 128, (t + 1) * 128)
        o_ref[0, :, ln] = (gate_ref[0, :, ln].astype(F32)
                           * (yf_ref[t] + zf_ref[t] * bias_ref[:, ln])).astype(BF16)


def _hyconv(zsrc, zcol, zrow, gsrc, gcol, grow, hspec, order, fmat, gmat, bias, ls):
    B = zsrc.shape[0]
    k1p = fmat.shape[1] // 2
    once = lambda a: pl.BlockSpec(a.shape, lambda b: (0,) * a.ndim, pipeline_mode=pl.Buffered(1))
    return pl.pallas_call(
        functools.partial(_hyconv_kernel, ls=ls),
        grid=(B,),
        in_specs=[pl.BlockSpec((1, ls, HY_W), lambda b: (b, zrow, zcol)),
                  pl.BlockSpec((1, ls, HY_W), lambda b: (b, grow, gcol)),
                  pl.BlockSpec((1, 2, HY_N2, k1p, HY_W), lambda b: (order, 0, 0, 0, 0),
                               pipeline_mode=pl.Buffered(1)),
                  once(fmat), once(gmat), pl.BlockSpec((1, HY_W), lambda b: (0, 0))],
        out_specs=pl.BlockSpec((1, ls, HY_W), lambda b: (b, 0, 0)),
        out_shape=jax.ShapeDtypeStruct((B, ls, HY_W), BF16),
        scratch_shapes=[pltpu.VMEM((HY_W // 128, ls, 128), F32), pltpu.VMEM((HY_W // 128, ls, 128), F32),
                        pltpu.VMEM((2, HY_N2, k1p, HY_W), F32)],
        compiler_params=_cparams(("arbitrary",)),
        name="hyena_longconv",
    )(zsrc, gsrc, hspec, fmat, gmat, bias)


def _outproj_kernel(x_ref, mod_ref, ya_ref, ybl_ref, ybc_ref, hf_ref, hb_ref, oc_ref, yd_ref, mlg_ref, g64_ref,
                    w_ref, g2_ref, wr_ref, xo_ref, h2_ref, aff_ref, *, nlat):
    hsum = hf_ref[0] + hb_ref[0]
    msq = _dot((hsum * hsum).astype(BF16), g64_ref[...])
    yc = jax.nn.sigmoid(oc_ref[0].astype(F32)) * (hsum * lax.rsqrt(msq + EPS) * mlg_ref[...])
    yb = jnp.where(pl.program_id(1) < nlat, ybl_ref[0], ybc_ref[0])
    y = jnp.concatenate([ya_ref[0], yb, yc.astype(BF16), yd_ref[0]], axis=1)
    mod = mod_ref[0, 0]
    x = x_ref[0] + mod[2:3, :] * _dot(y, w_ref[...])
    xo_ref[0] = x
    ms = jnp.mean(x * x, axis=-1, keepdims=True)
    h2 = x * lax.rsqrt(ms + EPS) * g2_ref[...] * (1.0 + mod[4:5, :]) + mod[3:4, :]
    h2_hi = h2.astype(BF16)
    h2_ref[0] = _pack_bf16_pair(h2_hi[:, 0:D_MODEL // 2], h2_hi[:, D_MODEL // 2:])
    h2_lo = (h2 - h2_hi.astype(F32)).astype(BF16)
    lg = _dot(h2_hi, wr_ref[...])
    logits = lg[:, 0:128] + lg[:, 128:256] + _dot(h2_lo, wr_ref[:, 0:128])
    lane = lax.broadcasted_iota(I32, logits.shape, 1)
    logits = jnp.where(lane < N_EXPERTS, logits, NEG)
    e = jnp.exp(logits - jnp.max(logits, axis=-1, keepdims=True))
    aff_ref[0] = (e / jnp.sum(e, axis=-1, keepdims=True)).T


def _outproj(x, mod, ya, yb_l, yb_c, hf, hb, oc, yd, mlg, g64, w_out, g2, w_router, T):
    B, N, D = x.shape
    rb = ROW_BLOCK
    nlat = T // rb
    row = lambda w: pl.BlockSpec((1, rb, w), lambda b, r: (b, r, 0))
    full = lambda a: pl.BlockSpec(a.shape, lambda b, r: (0,) * a.ndim)
    return pl.pallas_call(
        functools.partial(_outproj_kernel, nlat=nlat),
        grid=(B, N // rb),
        in_specs=[row(D), pl.BlockSpec((1, 1, 6, D), lambda b, r: (b, jnp.where(r >= nlat, 1, 0), 0, 0)),
                  row(256),
                  pl.BlockSpec((1, rb, 256), lambda b, r: (b, jnp.minimum(r, nlat - 1), 0)),
                  pl.BlockSpec((1, rb, 256), lambda b, r: (b, jnp.maximum(r - nlat, 0), 0)),
                  row(256), row(256), row(256), row(256),
                  full(mlg), full(g64), full(w_out), full(g2), full(w_router)],
        out_specs=[row(D), row(D // 2), pl.BlockSpec((1, 128, rb), lambda b, r: (b, 0, r))],
        out_shape=[jax.ShapeDtypeStruct((B, N, D), F32), jax.ShapeDtypeStruct((B, N, D // 2), I32),
                   jax.ShapeDtypeStruct((B, 128, N), F32)],
        compiler_params=_cparams(("arbitrary", "arbitrary")),
        name="outproj",
    )(x, mod, ya, yb_l, yb_c, hf, hb, oc, yd, mlg, g64, w_out, g2, w_router)


def _prefix_exclusive(x):
    n = x.shape[1]
    lane = lax.broadcasted_iota(I32, x.shape, 1)
    inc = x
    d = 1
    while d < n:
        inc = inc + jnp.where(lane >= d, pltpu.roll(inc, d, axis=1), 0)
        d *= 2
    return inc - x


def _topk_kernel(aff_ref, o_ref, pos_ref, val_ref, *, T, S):
    def segment(t0, ts, cap, slot0):
        a = aff_ref[0, :, t0:t0 + ts]
        bits = lax.bitcast_convert_type(a, I32)

        def search(i, v):
            cand = v | jnp.left_shift(jnp.int32(1), 30 - i)
            cnt = jnp.sum((bits >= cand).astype(I32), axis=1, keepdims=True)
            return jnp.where(cnt >= cap, cand, v)

        thr = lax.fori_loop(0, 31, search, jnp.zeros((N_EXPERTS, 1), I32))
        gt = bits > thr
        eq = bits == thr
        need = cap - jnp.sum(gt.astype(I32), axis=1, keepdims=True)
        sel = gt | (eq & (_prefix_exclusive(eq.astype(I32)) < need))
        seli = sel.astype(I32)
        pos_ref[:, 0:ts] = jnp.where(sel, _prefix_exclusive(seli), -1)
        a_hi = a.astype(BF16)
        r1 = a - a_hi.astype(F32)
        a_mid = r1.astype(BF16)
        a_lo = (r1 - a_mid.astype(F32)).astype(BF16)
        val_ref[0, :, 0:ts] = a_hi.astype(F32)
        val_ref[1, :, 0:ts] = a_mid.astype(F32)
        val_ref[2, :, 0:ts] = a_lo.astype(F32)
        tabs = lax.broadcasted_iota(I32, (1, ts), 1) + t0
        t_hi = jnp.right_shift(tabs, 6).astype(F32)
        t_lo = (tabs & 63).astype(F32)
        slot = lax.broadcasted_iota(I32, (cap, ts), 0)
        zeros = jnp.zeros((11, ts), F32)

        def per_expert(e, _):
            onehot = jnp.where(pos_ref[pl.ds(e, 1), 0:ts] == slot, 1.0, 0.0).astype(BF16)
            vals = jnp.concatenate([t_hi, t_lo, val_ref[0, pl.ds(e, 1), 0:ts], val_ref[1, pl.ds(e, 1), 0:ts],
                                    val_ref[2, pl.ds(e, 1), 0:ts], zeros], axis=0).astype(BF16)
            o_ref[0, e, slot0:slot0 + cap, :] = _dot_nt(onehot, vals)
            return 0

        lax.fori_loop(0, N_EXPERTS, per_expert, 0)

    segment(0, T, (EC_CAPACITY * T) // N_EXPERTS, 0)
    segment(T, S, (EC_CAPACITY * S) // N_EXPERTS, (EC_CAPACITY * T) // N_EXPERTS)


def _topk(aff_t, T, S):
    B, _, N = aff_t.shape
    E = N_EXPERTS
    cap_t = (EC_CAPACITY * T) // E + (EC_CAPACITY * S) // E
    return pl.pallas_call(
        functools.partial(_topk_kernel, T=T, S=S),
        grid=(B,),
        in_specs=[pl.BlockSpec((1, E, N), lambda b: (b, 0, 0))],
        out_specs=pl.BlockSpec((1, E, cap_t, 16), lambda b: (b, 0, 0, 0)),
        out_shape=jax.ShapeDtypeStruct((B, E, cap_t, 16), F32),
        scratch_shapes=[pltpu.VMEM((E, T), I32), pltpu.VMEM((3, E, T), F32)],
        compiler_params=_cparams(("arbitrary",)),
        name="expert_topk",
    )(aff_t)


def _gather_kernel(idx_ref, h_ref, *rest, cap_t):
    o_ref = rest[-1]
    b, e = pl.program_id(0), pl.program_id(1)
    base = (b * N_EXPERTS + e) * cap_t

    def body(g, _):
        rows = [h_ref[0, pl.ds(idx_ref[base + g * 16 + k], 1), :] for k in range(16)]
        o_ref[0, 0, pl.ds(pl.multiple_of(g * 16, 16), 16), :] = jnp.concatenate(rows, axis=0)
        return 0

    lax.fori_loop(0, cap_t // 16, body, 0)


def _gather(idx_flat, h2p, xe_buf, b0, bg, cap_t):
    B, N, hw = h2p.shape
    in_specs = [pl.BlockSpec((1, N, hw), lambda b, e, idx: (b0 + b, 0, 0))]
    args = [idx_flat, h2p]
    aliases = {}
    if xe_buf is not None:
        in_specs.append(pl.BlockSpec(memory_space=pl.ANY))
        args.append(xe_buf)
        aliases = {2: 0}
    return pl.pallas_call(
        functools.partial(_gather_kernel, cap_t=cap_t),
        grid_spec=pltpu.PrefetchScalarGridSpec(
            num_scalar_prefetch=1,
            grid=(bg, N_EXPERTS),
            in_specs=in_specs,
            out_specs=pl.BlockSpec((1, 1, cap_t, hw), lambda b, e, idx: (b0 + b, e, 0, 0)),
        ),
        out_shape=jax.ShapeDtypeStruct((B, N_EXPERTS, cap_t, hw), I32),
        input_output_aliases=aliases,
        compiler_params=_cparams(("arbitrary", "arbitrary")),
        name="expert_gather",
    )(*args)


def _ffn_kernel(x_ref, wg_ref, wu_ref, wd_ref, gate_ref, gt_ref, o_ref, wg_s, wu_s, wd_s, *, cap_l):
    @pl.when(pl.program_id(1) == 0)
    def _():
        wg_s[...] = wg_ref[0, 0].astype(BF16)
        wu_s[...] = wu_ref[0, 0].astype(BF16)
        wd_s[...] = wd_ref[0, 0].astype(BF16)

    x = jnp.concatenate(_unpack_bf16_pair(x_ref[0, 0]), axis=1)
    a = _dot(x, wg_s[...])
    u = _dot(x, wu_s[...])
    hmid = (a * jax.nn.sigmoid(a) * u).astype(BF16)
    y = _dot(hmid, wd_s[...])
    gt = gt_ref[0]
    row = lax.broadcasted_iota(I32, y.shape, 0)
    gt2 = jnp.where(row < cap_l, gt[0:1, :], gt[1:2, :])
    o_ref[0, 0] = y * gate_ref[0, 0] * gt2


def _ffn(xe, layer, wg, wu, wd, gate, gt2, cap_l):
    B, E, cap_t, hw = xe.shape
    D = 2 * hw
    wspec = pl.BlockSpec((1, 1, D, D), lambda e, b: (layer, e, 0, 0))
    return pl.pallas_call(
        functools.partial(_ffn_kernel, cap_l=cap_l),
        grid=(E, B),
        in_specs=[pl.BlockSpec((1, 1, cap_t, hw), lambda e, b: (b, e, 0, 0)), wspec, wspec, wspec,
                  pl.BlockSpec((1, 1, cap_t, 1), lambda e, b: (b, e, 0, 0)),
                  pl.BlockSpec((1, 2, D), lambda e, b: (b, 0, 0))],
        out_specs=pl.BlockSpec((1, 1, cap_t, D), lambda e, b: (b, e, 0, 0)),
        out_shape=jax.ShapeDtypeStruct((B, E, cap_t, D), F32),
        scratch_shapes=[pltpu.VMEM((D, D), BF16)] * 3,
        compiler_params=_cparams(("arbitrary", "arbitrary")),
        name="expert_ffn",
    )(xe, wg, wu, wd, gate, gt2)


def _scatter_kernel(idx_ref, x_ref, y_ref, o_ref, *, cap_t):
    b, e = pl.program_id(0), pl.program_id(2)
    base = (b * N_EXPERTS + e) * cap_t

    @pl.when(e == 0)
    def _():
        o_ref[...] = x_ref[...]

    def body(g, _):
        tile = y_ref[0, 0, pl.ds(pl.multiple_of(g * 8, 8), 8), :]
        rows = [idx_ref[base + g * 8 + k] for k in range(8)]
        cur = [o_ref[0, pl.ds(r, 1), :] for r in rows]
        for k, r in enumerate(rows):
            o_ref[0, pl.ds(r, 1), :] = cur[k] + tile[k:k + 1, :]
        return 0

    lax.fori_loop(0, cap_t // 8, body, 0)


def _scatter(idx_flat, x, ye, b0, bg, cap_t):
    B, N, D = x.shape
    hw = D // 2
    return pl.pallas_call(
        functools.partial(_scatter_kernel, cap_t=cap_t),
        grid_spec=pltpu.PrefetchScalarGridSpec(
            num_scalar_prefetch=1,
            grid=(bg, 2, N_EXPERTS),
            in_specs=[pl.BlockSpec((1, N, hw), lambda b, c, e, idx: (b0 + b, 0, c)),
                      pl.BlockSpec((1, 1, cap_t, hw), lambda b, c, e, idx: (b0 + b, e, 0, c))],
            out_specs=pl.BlockSpec((1, N, hw), lambda b, c, e, idx: (b0 + b, 0, c)),
        ),
        out_shape=jax.ShapeDtypeStruct((B, N, D), F32),
        input_output_aliases={1: 0},
        compiler_params=_cparams(("arbitrary", "arbitrary", "arbitrary")),
        name="expert_scatter",
    )(idx_flat, x, ye)


def _rope_tables(T, S, dh):
    rows = T // GRID_W
    r = jnp.broadcast_to(jnp.arange(rows, dtype=F32)[:, None], (rows, GRID_W)).reshape(T)
    col = jnp.broadcast_to(jnp.arange(GRID_W, dtype=F32)[None, :], (rows, GRID_W)).reshape(T)
    nf = dh // 4
    inv = ROPE_BASE ** (-jnp.arange(nf, dtype=F32) / nf)
    ar, ac = r[:, None] * inv, col[:, None] * inv
    ang = jnp.concatenate([ar, ar, ac, ac], axis=1)
    ang = jnp.concatenate([ang, jnp.zeros((S, dh), F32)], axis=0)
    reps = 256 // dh
    return jnp.tile(jnp.cos(ang), (1, reps)), jnp.tile(jnp.sin(ang), (1, reps))


def _group_mats(dh):
    i = np.arange(256)
    gmat = (i[:, None] // dh == i[None, :] // dh).astype(np.float32) / dh
    nf = dh // 4
    half = (i % (2 * nf)) // nf
    pmat = np.zeros((256, 256), np.float32)
    a_idx = i[half == 0]
    pmat[a_idx + nf, a_idx] = -1.0
    pmat[a_idx, a_idx + nf] = 1.0
    return jnp.asarray(gmat, BF16), jnp.asarray(pmat, BF16)


def _hyena_tables(ls):
    t = jnp.linspace(0.0, 1.0, ls, dtype=F32)[:, None]
    w = 2.0 * math.pi * jnp.arange(ls, dtype=F32)[:, None] / ls
    bands = jnp.linspace(1e-4, HY_BANDS - 1, HY_BANDS, dtype=F32)
    z = jnp.concatenate([t, jnp.cos(bands * w), -jnp.sin(bands * w)], axis=-1)
    z = jnp.pad(z, ((0, 0), (0, 128 - HY_EMB)))
    deltas = jnp.abs(jnp.linspace(math.log(HY_TARGET) / HY_SLOW, math.log(HY_TARGET) / HY_FAST, HY_W, dtype=F32))
    decay = jnp.exp(-t * deltas)
    n, nseg = 2 * ls, ls // HY_N2
    k1p = -(-(nseg + 1) // 8) * 8
    k1 = jnp.arange(k1p, dtype=I32)[None, :, None]
    tpos = HY_N2 * jnp.arange(nseg, dtype=I32)[None, None, :] + jnp.arange(HY_N2, dtype=I32)[:, None, None]
    ang = ((k1 * tpos) % n).astype(F32) * (2.0 * math.pi / n)
    keep = k1 <= nseg
    fmat = jnp.concatenate([jnp.where(keep, jnp.cos(ang), 0.0), jnp.where(keep, -jnp.sin(ang), 0.0)],
                           axis=1).astype(BF16)
    return z, decay, fmat, jnp.swapaxes(fmat, 1, 2)


def kernel(x, c, ctx, c_ctx, w_ada, b_ada, norm1_g, norm2_g, w_in, b_gate, a_qnorm, a_knorm, a_sink, hy_conv, hy_fw1, hy_fb1, hy_freq, hy_fw2, hy_fb2, hy_fw3, hy_bias, ml_norm, d_qnorm, d_knorm, d_lq1, d_lk1, d_lq2, d_lk2, d_subnorm, w_out, w_router, w_e_gate, w_e_up, w_e_down):
    B, T, D = x.shape
    S = ctx.shape[1]
    N = T + S
    depth = w_ada.shape[0]
    assert D == D_MODEL and T % S == 0 and S % ROW_BLOCK == 0 and S % MLSTM_CHUNK == 0
    assert S % WATTN_QROWS == 0 and T >= WATTN_QROWS + 2 * BLK and WATTN_QROWS & (WATTN_QROWS - 1) == 0
    cap_l, cap_c = (EC_CAPACITY * T) // N_EXPERTS, (EC_CAPACITY * S) // N_EXPERTS
    cap_t = cap_l + cap_c
    assert cap_t % 16 == 0

    xs = jnp.concatenate([x, ctx], axis=1)

    rpad = -(B + 1) % 8
    cc = jnp.concatenate([c, c_ctx[None, :], jnp.zeros((rpad, D), F32)], axis=0)
    mods = _ada_mods(cc, w_ada, b_ada)

    cos_a, sin_a = _rope_tables(T, S, DH_A)
    cos_d, sin_d = _rope_tables(T, S, DH_D)
    g64, p64 = _group_mats(DH_A)
    g32, p32 = _group_mats(DH_D)
    tabs_l = _hyena_tables(T)
    tabs_c = _hyena_tables(S)

    offs = np.cumsum((0, 256, 128, 128, 768, 256, 256, 256, 256, 16, 256, 256, 256))
    for l in range(depth):
        lam_init = 0.8 - 0.6 * math.exp(-0.3 * l)
        ml = mods[l]
        mod = jnp.stack([ml[:B].reshape(B, 6, D), jnp.broadcast_to(ml[B].reshape(1, 6, D), (B, 6, D))],
                        axis=1)
        w = w_in[l]
        wp = jnp.concatenate([w[:, offs[0]:offs[8]], w[:, offs[8]:offs[9]], jnp.zeros((D, 112), F32),
                              w[:, offs[9]:offs[12]]], axis=1).astype(BF16)
        gains = jnp.stack([jnp.tile(a_qnorm[l], 4), jnp.tile(a_knorm[l], 4),
                           jnp.tile(d_qnorm[l], 8), jnp.tile(d_knorm[l], 8)], axis=0)
        qa, ka, va, ub, qc, kc, vc, oc, gc, qd, kd, vd, gct, kct = _inproj(
            xs, mod, norm1_g[l].reshape(1, D), wp, (cos_a, sin_a, cos_d, sin_d), gains, (g64, g32, p64, p32), T)

        ya = _wattn(qa, ka, va, jnp.pad(a_sink[l], (0, 128 - H_A)).reshape(1, 128), T, S)
        yd = _dattn(qd, kd, vd, jnp.stack([d_lq1[l], d_lk1[l], d_lq2[l], d_lk2[l]], axis=0),
                    d_subnorm[l].reshape(1, 2 * DH_D), T, S, lam_init)

        bias = b_gate[l].reshape(-1)
        hf, hb = _mlstm(qc, kc, kct, vc, gc, gct, jnp.pad(bias, (0, 112)).reshape(1, 128), bias.reshape(16, 1), T, S)

        uc = _shortconv(ub, hy_conv[l], T, S)
        w1 = jnp.pad(hy_fw1[l], ((0, 128 - HY_EMB), (0, 0)))
        ybs = []
        for (zf, decay, fmat, gmat), ls, rowblk in ((tabs_l, T, 0), (tabs_c, S, T // S)):
            hpast, hfut, ss = _hyfilter(zf, w1, hy_fb1[l].reshape(1, -1), hy_freq[l].reshape(1, -1), hy_fw2[l],
                                        hy_fb2[l].reshape(1, -1), hy_fw3[l], decay)
            hspec = _hyspec(hpast, hfut, ss, fmat)
            z1 = _hyconv(uc, 0, rowblk, uc, 1, rowblk, hspec, 0, fmat, gmat, hy_bias[l, 0].reshape(1, HY_W), ls)
            z2 = _hyconv(z1, 0, 0, uc, 2, rowblk, hspec, 1, fmat, gmat, hy_bias[l, 1].reshape(1, HY_W), ls)
            ybs.append(z2)

        wr = jnp.pad(w_router[l], ((0, 0), (0, 128 - N_EXPERTS)))
        wr_hi = wr.astype(BF16)
        wr_cat = jnp.concatenate([wr_hi, (wr - wr_hi.astype(F32)).astype(BF16)], axis=1)
        xs, h2, aff = _outproj(xs, mod, ya, ybs[0], ybs[1], hf, hb, oc, yd, jnp.tile(ml_norm[l], 4).reshape(1, 256), g64,
                               w_out[l].astype(BF16), norm2_g[l].reshape(1, D), wr_cat, T)

        sel = _topk(aff, T, S)
        idx = (sel[..., 0] * 64.0 + sel[..., 1]).astype(I32)
        gate = (sel[..., 2] + sel[..., 3] + sel[..., 4])[..., None]
        gt2 = mod[:, :, 5, :]
        bg = min(IDX_BATCH_GROUP, B)
        groups = [(b0, idx[b0:b0 + bg].reshape(-1)) for b0 in range(0, B, bg)]
        xe = None
        for b0, idx_flat in groups:
            xe = _gather(idx_flat, h2, xe, b0, bg, cap_t)
        ye = _ffn(xe, l, w_e_gate, w_e_up, w_e_down, gate, gt2, cap_l)
        for b0, idx_flat in groups:
            xs = _scatter(idx_flat, xs, ye, b0, bg, cap_t)
    return xs[:, 0:T]
```

```python
import functools
import math

import jax
import jax.numpy as jnp
import numpy as np
from jax import lax
from jax.experimental import pallas as pl
from jax.experimental.pallas import tpu as pltpu

F32 = jnp.float32
BF16 = jnp.bfloat16
I32 = jnp.int32
HI = lax.Precision.HIGHEST

D_MODEL = 1024
GRID_W = 64
GROUP_W = D_MODEL // 4
H_A, HKV_A = 4, 2
G_A = H_A // HKV_A
DH_A = GROUP_W // H_A
WINDOW = 128
BLK = 128
HY_W = GROUP_W
HY_ORDER = 2
HY_BANDS = 16
HY_EMB = 1 + 2 * HY_BANDS
HY_HID = 64
HY_TARGET, HY_FAST, HY_SLOW = 1e-2, 0.3, 1.5
HY_N2 = 16
H_C = 4
DH_C = GROUP_W // H_C
H_D = 4
DH_D = GROUP_W // (2 * H_D)
N_EXPERTS = 16
EC_CAPACITY = 2
ROPE_BASE = 10000.0
EPS = 1e-6
NEG = -1e30

ROW_BLOCK = 256
WATTN_QROWS = 256
DATTN_MIN_ROWSUM = 1e-25
MLSTM_CHUNK = 256
IDX_BATCH_GROUP = 4
VMEM_LIMIT = 56 * 1024 * 1024

_C_QA, _C_KA, _C_VA, _C_UB = 0, 256, 384, 512
_C_QC, _C_KC, _C_VC, _C_OC, _C_GC = 1280, 1536, 1792, 2048, 2304
_C_QD, _C_KD, _C_VD, _C_END = 2432, 2688, 2944, 3200


def _cparams(sem, vmem=VMEM_LIMIT):
    return pltpu.CompilerParams(dimension_semantics=sem, vmem_limit_bytes=vmem)


def _dot(a, b, **kw):
    return jnp.dot(a, b, preferred_element_type=F32, **kw)


def _dot_nt(a, b, **kw):
    return lax.dot_general(a, b, (((1,), (1,)), ((), ())), preferred_element_type=F32, **kw)


def _dot_tn(a, b, **kw):
    return lax.dot_general(a, b, (((0,), (0,)), ((), ())), preferred_element_type=F32, **kw)


def _pack_bf16_pair(a, b):
    ua = lax.bitcast_convert_type(a.astype(F32), I32)
    ub = lax.bitcast_convert_type(b.astype(F32), I32)
    return ua | lax.shift_right_logical(ub, 16)


def _unpack_bf16_pair(w):
    a = lax.bitcast_convert_type(w & jnp.int32(-65536), F32)
    b = lax.bitcast_convert_type(lax.shift_left(w, 16), F32)
    return a.astype(BF16), b.astype(BF16)


def _ada_kernel(c_ref, w_ref, b_ref, o_ref):
    c = c_ref[...]
    s = c * jax.nn.sigmoid(c)
    o_ref[0] = _dot(s, w_ref[0], precision=HI) + b_ref[0]


def _ada_mods(cc, w_ada, b_ada):
    L, D, W6 = w_ada.shape
    R = cc.shape[0]
    cb = 1536
    return pl.pallas_call(
        _ada_kernel,
        grid=(L, W6 // cb),
        in_specs=[pl.BlockSpec((R, D), lambda l, j: (0, 0)),
                  pl.BlockSpec((1, D, cb), lambda l, j: (l, 0, j)),
                  pl.BlockSpec((1, 1, cb), lambda l, j: (l, 0, j))],
        out_specs=pl.BlockSpec((1, R, cb), lambda l, j: (l, 0, j)),
        out_shape=jax.ShapeDtypeStruct((L, R, W6), F32),
        compiler_params=_cparams(("arbitrary", "arbitrary")),
        name="ada_mods",
    )(cc, w_ada, b_ada.reshape(L, 1, W6))


def _inproj_kernel(x_ref, mod_ref, g1_ref, w_ref, ca_ref, sa_ref, cd_ref, sd_ref,
                   gains_ref, g64_ref, g32_ref, p64_ref, p32_ref,
                   qa, ka, va, ub, kc, oc, gc, qd, kd, vd, gct, qct, vct):
    x = x_ref[0]
    ms = jnp.mean(x * x, axis=-1, keepdims=True)
    xn = x * lax.rsqrt(ms + EPS) * g1_ref[...]
    mod = mod_ref[0, 0]
    h = xn * (1.0 + mod[1:2, :]) + mod[0:1, :]
    p = _dot(h.astype(BF16), w_ref[...])

    def headnorm_rope(t, gmat, gain, pmat, cos, sin, scale):
        w = t.shape[1]
        msq = _dot((t * t).astype(BF16), gmat[0:w, 0:w])
        tn = t * lax.rsqrt(msq + EPS) * gain
        tn = tn * cos[:, 0:w] + _dot(tn.astype(BF16), pmat[0:w, 0:w]) * sin[:, 0:w]
        return tn * scale

    ca, sa, cd, sd = ca_ref[...], sa_ref[...], cd_ref[...], sd_ref[...]
    gains = gains_ref[...]
    qa[0] = headnorm_rope(p[:, _C_QA:_C_KA], g64_ref, gains[0:1, :], p64_ref, ca, sa, DH_A ** -0.5).astype(BF16)
    ka[0] = headnorm_rope(p[:, _C_KA:_C_VA], g64_ref, gains[1:2, 0:128], p64_ref, ca, sa, 1.0).astype(BF16)
    va[0] = p[:, _C_VA:_C_UB].astype(BF16)
    ub[0] = p[:, _C_UB:_C_QC].astype(BF16)
    qct[0] = p[:, _C_QC:_C_KC].T.astype(BF16)
    kc[0] = (p[:, _C_KC:_C_VC] * DH_C ** -0.5).astype(BF16)
    vct[0] = p[:, _C_VC:_C_OC].T.astype(BF16)
    oc[0] = p[:, _C_OC:_C_GC].astype(BF16)
    gc[0] = p[:, _C_GC:_C_QD]
    gct[0] = p[:, _C_GC:_C_QD].T
    qd[0] = headnorm_rope(p[:, _C_QD:_C_KD], g32_ref, gains[2:3, :], p32_ref, cd, sd, DH_D ** -0.5).astype(BF16)
    kd[0] = headnorm_rope(p[:, _C_KD:_C_VD], g32_ref, gains[3:4, :], p32_ref, cd, sd, 1.0).astype(BF16)
    vd[0] = p[:, _C_VD:_C_END].astype(BF16)


def _inproj(x, mod, g1, w_packed, tabs, gains, mats, T):
    B, N, D = x.shape
    rb = ROW_BLOCK
    nlat = T // rb
    row = lambda w: pl.BlockSpec((1, rb, w), lambda b, r: (b, r, 0))
    tab = pl.BlockSpec((rb, 256), lambda b, r: (r, 0))
    full = lambda a: pl.BlockSpec(a.shape, lambda b, r: (0,) * a.ndim)
    widths = [256, 128, 128, 768, 256, 256, 128, 256, 256, 256]
    dtypes = [BF16] * 6 + [F32] + [BF16] * 3
    tr = lambda w: pl.BlockSpec((1, w, rb), lambda b, r: (b, 0, r))
    return pl.pallas_call(
        _inproj_kernel,
        grid=(B, N // rb),
        in_specs=[row(D),
                  pl.BlockSpec((1, 1, 6, D), lambda b, r: (b, jnp.where(r >= nlat, 1, 0), 0, 0)),
                  full(g1), full(w_packed), tab, tab, tab, tab, full(gains)] + [full(m) for m in mats],
        out_specs=[row(w) for w in widths] + [tr(128), tr(256), tr(256)],
        out_shape=[jax.ShapeDtypeStruct((B, N, w), dt) for w, dt in zip(widths, dtypes)]
        + [jax.ShapeDtypeStruct((B, 128, N), F32), jax.ShapeDtypeStruct((B, 256, N), BF16),
           jax.ShapeDtypeStruct((B, 256, N), BF16)],
        compiler_params=_cparams(("arbitrary", "arbitrary")),
        name="inproj",
    )(x, mod, g1, w_packed, *tabs, gains, *mats)


def _wattn_kernel(q_ref, k_ref, v_ref, sink_ref, sinkw_ref, placeq_ref, gsumq_ref, placek_ref, gsumk_ref, o_ref,
                  kaug_ref, vaug_ref, kmax_ref, *, T, S):
    j = pl.program_id(1)
    qb, kwin = WATTN_QROWS, WATTN_QROWS + 2 * BLK
    is_lat = j < T // qb
    start = pl.multiple_of(jnp.clip(j * qb - BLK, 0, T - kwin), BLK)
    row = lax.broadcasted_iota(I32, (G_A * qb, kwin), 0)
    qpos = j * qb + (row & (qb - 1))
    kpos = start + lax.broadcasted_iota(I32, (G_A * qb, kwin), 1)
    valid = (jnp.abs(qpos - kpos) <= WINDOW) & is_lat
    grp = jnp.right_shift(lax.broadcasted_iota(I32, (G_A * qb, 1), 0), qb.bit_length() - 1)
    sinks = sink_ref[...]

    @pl.when(j == 0)
    def _():
        lane = lax.broadcasted_iota(I32, (1, HKV_A * 128), 1)
        one = jnp.where((lane & 127) == DH_A, 1.0, 0.0)
        k_wide = _dot(k_ref[0], placek_ref[...])
        kaug_ref[...] = (k_wide + one).astype(BF16)
        vaug_ref[...] = (_dot(v_ref[0], placek_ref[...]) + one).astype(BF16)
        ksq = _dot((k_wide * k_wide).astype(BF16), gsumk_ref[...])
        kmax_ref[...] = jnp.sqrt(jnp.max(ksq, axis=0, keepdims=True))

    def sink_rows(heads):
        sink = sinks[0:1, heads[0]:heads[0] + 1]
        for g in range(1, G_A):
            sink = jnp.where(grp == g, sinks[0:1, heads[g]:heads[g] + 1], sink)
        return sink

    def store(heads, o):
        for g, hd in enumerate(heads):
            o_ref[0, :, hd * DH_A:(hd + 1) * DH_A] = o[g * qb:(g + 1) * qb].astype(BF16)

    def body_fast():
        q_wide = _dot(q_ref[0], placeq_ref[...])
        qn = jnp.sqrt(_dot((q_wide * q_wide).astype(BF16), gsumq_ref[...]))
        km = kmax_ref[...]
        kmq = jnp.concatenate([km[:, (hd // G_A) * 128:(hd // G_A + 1) * 128] for hd in range(H_A)], axis=1)
        lane = lax.broadcasted_iota(I32, (1, H_A * 128), 1)
        shift = jnp.where((lane & 127) == DH_A, jnp.maximum(qn * kmq, sinkw_ref[...]), 0.0)
        q_aug = (q_wide - shift).astype(BF16)
        low = None
        for hk in range(HKV_A):
            heads = [hk * G_A + g for g in range(G_A)]
            ks = slice(hk * 128, (hk + 1) * 128)
            q2 = jnp.concatenate([q_aug[:, hd * 128:(hd + 1) * 128] for hd in heads], axis=0)
            m_used = -q2[:, DH_A:DH_A + 1].astype(F32)
            s_loc = jnp.where(valid, _dot_nt(q2, kaug_ref[pl.ds(start, kwin), ks]), NEG)
            s_ctx = _dot_nt(q2, kaug_ref[T:T + S, ks])
            pv = (_dot(jnp.exp(s_loc.astype(BF16)), vaug_ref[pl.ds(start, kwin), ks])
                  + _dot(jnp.exp(s_ctx.astype(BF16)), vaug_ref[T:T + S, ks]))
            sigma = pv[:, DH_A:DH_A + 1]
            low = sigma if low is None else jnp.minimum(low, sigma)
            store(heads, pv[:, 0:DH_A] / (sigma + jnp.exp(sink_rows(heads) - m_used)))
        return jnp.min(low)

    def body_exact():
        kw = k_ref[0, pl.ds(start, kwin), :]
        vw = v_ref[0, pl.ds(start, kwin), :]
        kc = k_ref[0, T:T + S, :]
        vc = v_ref[0, T:T + S, :]
        for hk in range(HKV_A):
            cs = slice(hk * DH_A, (hk + 1) * DH_A)
            kwh, vwh, kch, vch = kw[:, cs], vw[:, cs], kc[:, cs], vc[:, cs]
            heads = [hk * G_A + g for g in range(G_A)]
            q = jnp.concatenate([q_ref[0, :, hd * DH_A:(hd + 1) * DH_A] for hd in heads], axis=0)
            sink = sink_rows(heads)
            s_loc = jnp.where(valid, _dot_nt(q, kwh), NEG)
            s_ctx = _dot_nt(q, kch)
            m = jnp.maximum(jnp.maximum(jnp.max(s_loc, axis=-1, keepdims=True),
                                        jnp.max(s_ctx, axis=-1, keepdims=True)), sink)
            p_loc = jnp.exp(s_loc - m)
            p_ctx = jnp.exp(s_ctx - m)
            den = (jnp.sum(p_loc, axis=-1, keepdims=True) + jnp.sum(p_ctx, axis=-1, keepdims=True)
                   + jnp.exp(sink - m))
            store(heads, (_dot(p_loc.astype(BF16), vwh) + _dot(p_ctx.astype(BF16), vch)) / den)

    low = body_fast()

    @pl.when(jnp.logical_not(low > DATTN_MIN_ROWSUM))
    def _():
        body_exact()


def _wattn_mats(nheads):
    place = np.zeros((nheads * DH_A, nheads * 128), np.float32)
    gsum = np.zeros((nheads * 128, nheads * 128), np.float32)
    for h in range(nheads):
        for d in range(DH_A):
            place[h * DH_A + d, h * 128 + d] = 1.0
            gsum[h * 128 + d, h * 128 + DH_A] = 1.0
    return jnp.asarray(place, BF16), jnp.asarray(gsum, BF16)


def _wattn(qa, ka, va, sink, T, S):
    B, N, _ = qa.shape
    qb = WATTN_QROWS
    placeq, gsumq = _wattn_mats(H_A)
    placek, gsumk = _wattn_mats(HKV_A)
    sinkw = jnp.zeros((1, H_A * 128), F32).at[0, DH_A::128].set(sink[0, 0:H_A])
    full = lambda a: pl.BlockSpec(a.shape, lambda b, j: (0,) * a.ndim)
    return pl.pallas_call(
        functools.partial(_wattn_kernel, T=T, S=S),
        grid=(B, N // qb),
        in_specs=[pl.BlockSpec((1, qb, 256), lambda b, j: (b, j, 0)),
                  pl.BlockSpec((1, N, 128), lambda b, j: (b, 0, 0)),
                  pl.BlockSpec((1, N, 128), lambda b, j: (b, 0, 0)),
                  pl.BlockSpec((1, 128), lambda b, j: (0, 0)),
                  full(sinkw), full(placeq), full(gsumq), full(placek), full(gsumk)],
        out_specs=pl.BlockSpec((1, qb, 256), lambda b, j: (b, j, 0)),
        out_shape=jax.ShapeDtypeStruct((B, N, 256), BF16),
        scratch_shapes=[pltpu.VMEM((N, HKV_A * 128), BF16), pltpu.VMEM((N, HKV_A * 128), BF16),
                        pltpu.VMEM((1, HKV_A * 128), F32)],
        compiler_params=_cparams(("arbitrary", "arbitrary")),
        name="window_attn",
    )(qa, ka, va, sink, sinkw, placeq, gsumq, placek, gsumk)


def _dattn_kernel(q_ref, k_ref, v_ref, lam_ref, sub_ref, place_ref, gsum_ref, o_ref,
                  vaug_ref, kaug_ref, kmax_ref, kch_ref, qch_ref, tch_ref, *, T, S, lam_init):
    j = pl.program_id(1)
    lv = lam_ref[...]
    lam = (jnp.exp(jnp.sum(lv[0:1, :] * lv[1:2, :], axis=-1, keepdims=True))
           - jnp.exp(jnp.sum(lv[2:3, :] * lv[3:4, :], axis=-1, keepdims=True)) + lam_init)
    subg = sub_ref[...] * (1.0 - lam_init)
    dv = 2 * DH_D
    nchain, aw = 2 * H_D, 2 * DH_D

    @pl.when(j == 0)
    def _():
        ones = jnp.where(lax.broadcasted_iota(I32, (T + S, dv), 1) == 0, 1.0, 0.0).astype(BF16)
        for hd in range(H_D):
            vaug_ref[hd] = jnp.concatenate([v_ref[0, :, hd * dv:(hd + 1) * dv], ones], axis=1)
        k_wide = _dot(k_ref[0], place_ref[...])
        lane = lax.broadcasted_iota(I32, (1, nchain * aw), 1)
        k_aug = (k_wide + jnp.where((lane & (aw - 1)) == DH_D, 1.0, 0.0)).astype(BF16)
        kaug_ref[...] = k_aug
        for i in range(nchain):
            kch_ref[i] = k_aug[:, i * aw:(i + 1) * aw]
        ksq = _dot((k_wide * k_wide).astype(BF16), gsum_ref[...])
        kmax_ref[...] = jnp.sqrt(jnp.max(ksq, axis=0, keepdims=True))

    def finish(hd, terms):
        o = terms[0] - lam * terms[1]
        ms = jnp.mean(o * o, axis=-1, keepdims=True)
        o_ref[0, :, hd * dv:(hd + 1) * dv] = (o * lax.rsqrt(ms + EPS) * subg).astype(BF16)

    def body_fast(k0, nk):
        q_wide = _dot(q_ref[0], place_ref[...])
        qn = jnp.sqrt(_dot((q_wide * q_wide).astype(BF16), gsum_ref[...]))
        q_aug = (q_wide - qn * kmax_ref[...]).astype(BF16)
        terms, low = [], None
        for i in range(nchain):
            hd, cs = i // 2, slice(i * aw, (i + 1) * aw)
            s = _dot_nt(q_aug[:, cs], kaug_ref[k0:k0 + nk, cs])
            pv = _dot(jnp.exp(s.astype(BF16)), vaug_ref[hd, k0:k0 + nk, :])
            sigma = pv[:, dv:dv + 1]
            low = sigma if low is None else jnp.minimum(low, sigma)
            terms.append(pv[:, 0:dv] / sigma)
            if i % 2 == 1:
                finish(hd, terms)
                terms = []
        return jnp.min(low)

    def body_exact(k0, nk):
        q_wide = _dot(q_ref[0], place_ref[...])
        for i in range(nchain):
            qch_ref[i] = q_wide[:, i * aw:(i + 1) * aw].astype(BF16)

        def chain(i, carry):
            s = _dot_nt(qch_ref[i], kch_ref[i, k0:k0 + nk, :])
            e = jnp.exp((s - jnp.max(s, axis=-1, keepdims=True)).astype(BF16))
            pv = _dot(e, vaug_ref[jnp.right_shift(i, 1), k0:k0 + nk, :])
            tch_ref[i] = pv / pv[:, dv:dv + 1]
            return carry

        lax.fori_loop(0, nchain, chain, 0)
        for hd in range(H_D):
            finish(hd, [tch_ref[2 * hd][:, 0:dv], tch_ref[2 * hd + 1][:, 0:dv]])

    def body(k0, nk):
        low = body_fast(k0, nk)

        @pl.when(jnp.logical_not(low > DATTN_MIN_ROWSUM))
        def _():
            body_exact(k0, nk)

    @pl.when(j < T // ROW_BLOCK)
    def _():
        body(0, T + S)

    @pl.when(j >= T // ROW_BLOCK)
    def _():
        body(T, S)


def _dattn_mats():
    nchain, aw = 2 * H_D, 2 * DH_D
    place = np.zeros((nchain * DH_D, nchain * aw), np.float32)
    gsum = np.zeros((nchain * aw, nchain * aw), np.float32)
    for i in range(nchain):
        for d in range(DH_D):
            place[i * DH_D + d, i * aw + d] = 1.0
            gsum[i * aw + d, i * aw + DH_D] = 1.0
    return jnp.asarray(place, BF16), jnp.asarray(gsum, BF16)


def _dattn(qd, kd, vd, lam_vecs, subg, T, S, lam_init):
    B, N, _ = qd.shape
    rb = ROW_BLOCK
    place, gsum = _dattn_mats()
    full = lambda a: pl.BlockSpec(a.shape, lambda b, j: (0,) * a.ndim)
    return pl.pallas_call(
        functools.partial(_dattn_kernel, T=T, S=S, lam_init=lam_init),
        grid=(B, N // rb),
        in_specs=[pl.BlockSpec((1, rb, 256), lambda b, j: (b, j, 0)),
                  pl.BlockSpec((1, N, 256), lambda b, j: (b, 0, 0)),
                  pl.BlockSpec((1, N, 256), lambda b, j: (b, 0, 0)),
                  pl.BlockSpec((4, DH_D), lambda b, j: (0, 0)),
                  pl.BlockSpec((1, 2 * DH_D), lambda b, j: (0, 0)), full(place), full(gsum)],
        out_specs=pl.BlockSpec((1, rb, 256), lambda b, j: (b, j, 0)),
        out_shape=jax.ShapeDtypeStruct((B, N, 256), BF16),
        scratch_shapes=[pltpu.VMEM((H_D, N, 4 * DH_D), BF16), pltpu.VMEM((N, 2 * 256), BF16),
                        pltpu.VMEM((1, 2 * 256), F32), pltpu.VMEM((2 * H_D, N, 2 * DH_D), BF16),
                        pltpu.VMEM((2 * H_D, rb, 2 * DH_D), BF16), pltpu.VMEM((2 * H_D, rb, 4 * DH_D), F32)],
        compiler_params=_cparams(("arbitrary", "arbitrary")),
        name="diff_attn",
    )(qd, kd, vd, lam_vecs, subg, place, gsum)


def _log_sigmoid(x):
    return jnp.minimum(x, 0.0) - jnp.log1p(jnp.exp(-jnp.abs(x)))


def _dot3(a, b, split_b):
    x = b if split_b else a
    hi = x.astype(BF16)
    r1 = x - hi.astype(F32)
    mid = r1.astype(BF16)
    lo = (r1 - mid.astype(F32)).astype(BF16)
    if split_b:
        a = a.astype(BF16)
        return _dot(a, hi) + _dot(a, mid) + _dot(a, lo)
    b = b.astype(BF16)
    return _dot(hi, b) + _dot(mid, b) + _dot(lo, b)


def _mlstm_kernel(qtf_ref, kf_ref, vtf_ref, gf_ref, gtf_ref, qtb_ref, kb_ref, vtb_ref, gb_ref, gtb_ref,
                  bias_ref, biast_ref, hf_ref, hb_ref, c_ref, m_ref):
    i = pl.program_id(1)
    lc = MLSTM_CHUNK

    @pl.when(i == 0)
    def _():
        c_ref[...] = jnp.zeros_like(c_ref)
        m_ref[...] = jnp.zeros_like(m_ref)

    r_io = lax.broadcasted_iota(I32, (lc, lc), 0)
    c_io = lax.broadcasted_iota(I32, (lc, lc), 1)
    lower = r_io >= c_io
    upper = r_io <= c_io
    tri_lo = jnp.where(lower, 1.0, 0.0).astype(F32)
    tri_up = jnp.where(upper, 1.0, 0.0).astype(F32)
    ones_rows = jnp.where(lax.broadcasted_iota(I32, (DH_C, lc), 0) == 0, 1.0, 0.0).astype(BF16)

    def direction(d, qt_ref, k_ref, vt_ref, g_ref, gt_ref, out_ref):
        pre = g_ref[0] + bias_ref[...]
        pre_t = gt_ref[0] + biast_ref[...]
        lf = _log_sigmoid(pre)
        lf_t = _log_sigmoid(pre_t)
        if d == 0:
            bcol = _dot3(tri_lo, lf, True)
            brow = _dot3(lf_t, tri_up, False)
            valid, end = upper, lc - 1
        else:
            bcol = _dot3(tri_up, lf, True)
            brow = _dot3(lf_t, tri_lo, False)
            valid, end = lower, 0
        for hd in range(H_C):
            ci, cf = d * H_C + hd, (2 + d) * H_C + hd
            rs = slice(hd * DH_C, (hd + 1) * DH_C)
            q_t = qt_ref[0, rs, :]
            k = k_ref[0, :, rs]
            v_aug = jnp.concatenate([vt_ref[0, rs, :], ones_rows], axis=0)
            b_row = brow[cf:cf + 1, :]
            ig_row = pre_t[ci:ci + 1, :]
            bi_col = bcol[:, cf:cf + 1] - pre[:, ci:ci + 1]
            b_end = brow[cf:cf + 1, end:end + 1]
            slot = d * H_C + hd
            c = c_ref[slot]
            m = m_ref[slot][:, 0:1]
            logw = jnp.where(valid, b_row - bi_col, NEG)
            inter = b_row + m
            m_t = jnp.maximum(inter, jnp.max(logw, axis=0, keepdims=True))
            w_in = jnp.exp(inter - m_t)
            p_t = (_dot(k, q_t) * jnp.exp(logw - m_t)).astype(BF16)
            num = w_in * _dot(c.astype(BF16), q_t) + _dot(v_aug, p_t)
            den = jnp.maximum(jnp.abs(num[DH_C:DH_C + 1, :]), jnp.exp(-m_t))
            out_ref[0, rs, :] = num[0:DH_C, :] / den
            g_end = b_end - b_row + ig_row
            m_new = jnp.maximum(b_end + m, jnp.max(g_end, axis=1, keepdims=True))
            wv = (v_aug.astype(F32) * jnp.exp(g_end - m_new)).astype(BF16)
            c_ref[slot] = jnp.exp(b_end + m - m_new) * c + _dot(wv, k)
            m_ref[slot] = jnp.broadcast_to(m_new, (1, 128))

    direction(0, qtf_ref, kf_ref, vtf_ref, gf_ref, gtf_ref, hf_ref)
    direction(1, qtb_ref, kb_ref, vtb_ref, gb_ref, gtb_ref, hb_ref)


def _mlstm(qct, kc, vct, gc, gct, bias, biast, T, S):
    B, N, _ = kc.shape
    lc = MLSTM_CHUNK
    nch, nlat = N // lc, T // lc
    fwd = lambda i: jnp.where(i < nch - nlat, nlat + i, i - (nch - nlat))
    bwd = lambda i: nch - 1 - i
    tok = lambda f: pl.BlockSpec((1, lc, 256), lambda b, i: (b, f(i), 0))
    tok_t = lambda f: pl.BlockSpec((1, 256, lc), lambda b, i: (b, 0, f(i)))
    gate = lambda f: pl.BlockSpec((1, lc, 128), lambda b, i: (b, f(i), 0))
    gate_t = lambda f: pl.BlockSpec((1, 16, lc), lambda b, i: (b, 0, f(i)))
    return pl.pallas_call(
        _mlstm_kernel,
        grid=(B, nch),
        in_specs=[tok_t(fwd), tok(fwd), tok_t(fwd), gate(fwd), gate_t(fwd),
                  tok_t(bwd), tok(bwd), tok_t(bwd), gate(bwd), gate_t(bwd),
                  pl.BlockSpec((1, 128), lambda b, i: (0, 0)),
                  pl.BlockSpec((16, 1), lambda b, i: (0, 0))],
        out_specs=[tok_t(fwd), tok_t(bwd)],
        out_shape=[jax.ShapeDtypeStruct((B, 256, N), F32)] * 2,
        scratch_shapes=[pltpu.VMEM((2 * H_C, 2 * DH_C, DH_C), F32), pltpu.VMEM((2 * H_C, 1, 128), F32)],
        compiler_params=_cparams(("arbitrary", "arbitrary")),
        name="mlstm",
    )(qct, kc, vct, gc, gct, qct, kc, vct, gc, gct, bias, biast)


def _shortconv_kernel(u_ref, w_ref, o_ref, *, T, S):
    u = u_ref[0].astype(F32)
    n = u.shape[0]
    w = w_ref[...]
    row = lax.broadcasted_iota(I32, u.shape, 0)
    zero = jnp.zeros((1, u.shape[1]), F32)
    prev = jnp.concatenate([zero, u[0:n - 1]], axis=0)
    nxt = jnp.concatenate([u[1:n], zero], axis=0)
    prev = jnp.where(row == T, 0.0, prev)
    nxt = jnp.where(row == T - 1, 0.0, nxt)
    o_ref[0] = (w[0:1, :] * prev + w[1:2, :] * u + w[2:3, :] * nxt).astype(BF16)


def _shortconv(ub, conv_w, T, S):
    B, N, W = ub.shape
    return pl.pallas_call(
        functools.partial(_shortconv_kernel, T=T, S=S),
        grid=(B, W // 256),
        in_specs=[pl.BlockSpec((1, N, 256), lambda b, j: (b, 0, j)),
                  pl.BlockSpec((3, 256), lambda b, j: (0, j))],
        out_specs=pl.BlockSpec((1, N, 256), lambda b, j: (b, 0, j)),
        out_shape=jax.ShapeDtypeStruct((B, N, W), BF16),
        compiler_params=_cparams(("arbitrary", "arbitrary")),
        name="hyena_shortconv",
    )(ub, conv_w)


def _hyfilter_kernel(z_ref, w1_ref, b1_ref, fr_ref, w2_ref, b2_ref, w3_ref, dec_ref,
                     hp_ref, hf_ref, ss_ref):
    i = pl.program_id(0)
    fr = fr_ref[...]
    h = jnp.sin(fr * (_dot(z_ref[...], w1_ref[...], precision=HI) + b1_ref[...]))
    h = jnp.sin(fr * (_dot(h, w2_ref[...], precision=HI) + b2_ref[...]))
    h = _dot(h, w3_ref[...], precision=HI)
    dec = dec_ref[...]
    dec2 = jnp.concatenate([dec, dec], axis=1)
    hw = HY_ORDER * HY_W
    h0 = h[:, 0:hw] * dec2
    h1 = h[:, hw:2 * hw] * dec2
    rows = h0.shape[0]
    t = i * rows + lax.broadcasted_iota(I32, h0.shape, 0)
    h1 = jnp.where(t == 0, 0.0, h1)
    hp_ref[...] = h0
    hf_ref[...] = h1

    @pl.when(i == 0)
    def _():
        ss_ref[...] = jnp.zeros_like(ss_ref)

    ss_ref[...] += jnp.sum(h0 * h0 + h1 * h1, axis=0, keepdims=True)


def _hyfilter(zfeat, w1, b1, fr, w2, b2, w3, decay):
    ls = zfeat.shape[0]
    rb = min(ls, 512)
    hw = HY_ORDER * HY_W
    full = lambda a: pl.BlockSpec(a.shape, lambda i: (0,) * a.ndim)
    return pl.pallas_call(
        _hyfilter_kernel,
        grid=(ls // rb,),
        in_specs=[pl.BlockSpec((rb, 128), lambda i: (i, 0)), full(w1), full(b1), full(fr), full(w2),
                  full(b2), full(w3), pl.BlockSpec((rb, HY_W), lambda i: (i, 0))],
        out_specs=[pl.BlockSpec((rb, hw), lambda i: (i, 0)), pl.BlockSpec((rb, hw), lambda i: (i, 0)),
                   pl.BlockSpec((1, hw), lambda i: (0, 0))],
        out_shape=[jax.ShapeDtypeStruct((ls, hw), F32), jax.ShapeDtypeStruct((ls, hw), F32),
                   jax.ShapeDtypeStruct((1, hw), F32)],
        compiler_params=_cparams(("arbitrary",)),
        name="hyena_filter",
    )(zfeat, w1, b1, fr, w2, b2, w3, decay)


def _cmul_const(v, c, s):
    vr, vi = v
    r = math.sqrt(0.5)
    if abs(s) < 1e-9:
        return (vr, vi) if c > 0 else (-vr, -vi)
    if abs(c) < 1e-9:
        return (-vi, vr) if s > 0 else (vi, -vr)
    if abs(abs(c) - r) < 1e-9 and abs(abs(s) - r) < 1e-9:
        a, b = (vr if c > 0 else -vr), (vi if s > 0 else -vi)
        p, q = (vi if c > 0 else -vi), (vr if s > 0 else -vr)
        return (a - b) * r, (p + q) * r
    return c * vr - s * vi, c * vi + s * vr


def _fft_pow2(xs, sign):
    n = len(xs)
    if n == 1:
        return xs
    ev, od = _fft_pow2(xs[0::2], sign), _fft_pow2(xs[1::2], sign)
    out = [None] * n
    for k in range(n // 2):
        ang = sign * 2.0 * math.pi * k / n
        tr, ti = _cmul_const(od[k], math.cos(ang), math.sin(ang))
        out[k] = (ev[k][0] + tr, ev[k][1] + ti)
        out[k + n // 2] = (ev[k][0] - tr, ev[k][1] - ti)
    return out


def _hy_stage1(src_ref, f_ref, a_ref, nseg):
    k1p = a_ref.shape[2]
    for n2 in range(HY_N2):
        if len(src_ref.shape) == 3:
            zs = jnp.concatenate([src_ref[t, pl.ds(n2, nseg, stride=HY_N2), :] for t in range(src_ref.shape[0])],
                                 axis=1).astype(BF16)
        else:
            zs = src_ref[pl.ds(n2, nseg, stride=HY_N2), :].astype(BF16)
        r = _dot(f_ref[n2], zs)
        a_ref[0, n2] = r[0:k1p]
        a_ref[1, n2] = r[k1p:2 * k1p]


def _hy_chunks(a_ref):
    k1p, lanes = a_ref.shape[2], a_ref.shape[3]
    return k1p // 8, [slice(t * 128, (t + 1) * 128) for t in range(lanes // 128)]


def _hyspec_kernel(hp_ref, hf_ref, ss_ref, f_ref, h_ref, ap_ref, af_ref, *, ls):
    nseg = ls // HY_N2
    _hy_stage1(hp_ref, f_ref, ap_ref, nseg)
    _hy_stage1(hf_ref, f_ref, af_ref, nseg)
    nchunk, lane_tiles = _hy_chunks(ap_ref)

    def body(i, _):
        rows = pl.ds(pl.multiple_of(i * 8, 8), 8)
        k1 = i * 8 + lax.broadcasted_iota(I32, (8, 1), 0)
        wk = jnp.where((k1 == 0) | (k1 == nseg), 1.0, 2.0) * (1.0 / (2 * ls))
        for ln in lane_tiles:
            scale = lax.rsqrt(ss_ref[:, ln] + EPS) * wk
            P = _fft_pow2([(ap_ref[0, n2, rows, ln], ap_ref[1, n2, rows, ln]) for n2 in range(HY_N2)], -1)
            Q = _fft_pow2([(af_ref[0, n2, rows, ln], af_ref[1, n2, rows, ln]) for n2 in range(HY_N2)], -1)
            for k2 in range(HY_N2):
                h_ref[0, 0, k2, rows, ln] = (P[k2][0] + Q[k2][0]) * scale
                h_ref[0, 1, k2, rows, ln] = (P[k2][1] - Q[k2][1]) * scale
        return 0

    lax.fori_loop(0, nchunk, body, 0)


def _hyspec(hp, hf, ss, fmat):
    ls = hp.shape[0]
    k1p = fmat.shape[1] // 2
    full = lambda a: pl.BlockSpec(a.shape, lambda o, c: (0,) * a.ndim)
    nc = HY_W // 128
    col = lambda o, c: (0, o * nc + c)
    return pl.pallas_call(
        functools.partial(_hyspec_kernel, ls=ls),
        grid=(HY_ORDER, nc),
        in_specs=[pl.BlockSpec((ls, 128), col), pl.BlockSpec((ls, 128), col), pl.BlockSpec((1, 128), col),
                  full(fmat)],
        out_specs=pl.BlockSpec((1, 2, HY_N2, k1p, 128), lambda o, c: (o, 0, 0, 0, c)),
        out_shape=jax.ShapeDtypeStruct((HY_ORDER, 2, HY_N2, k1p, HY_W), F32),
        scratch_shapes=[pltpu.VMEM((2, HY_N2, k1p, 128), F32)] * 2,
        compiler_params=_cparams(("arbitrary", "arbitrary")),
        name="hyena_spectrum",
    )(hp, hf, ss, fmat)


def _hyconv_kernel(z_ref, gate_ref, h_ref, f_ref, g_ref, bias_ref, o_ref, zf_ref, yf_ref, a_ref, *, ls):
    nseg = ls // HY_N2
    ntile = zf_ref.shape[0]
    for t in range(ntile):
        zf_ref[t] = z_ref[0, :, t * 128:(t + 1) * 128].astype(F32)
    _hy_stage1(zf_ref, f_ref, a_ref, nseg)
    nchunk, lane_tiles = _hy_chunks(a_ref)

    def body(i, _):
        rows = pl.ds(pl.multiple_of(i * 8, 8), 8)
        for ln in lane_tiles:
            X = _fft_pow2([(a_ref[0, n2, rows, ln], a_ref[1, n2, rows, ln]) for n2 in range(HY_N2)], -1)
            Y = []
            for k2 in range(HY_N2):
                hr, hi = h_ref[0, 0, k2, rows, ln], h_ref[0, 1, k2, rows, ln]
                xr, xi = X[k2]
                Y.append((xr * hr - xi * hi, xr * hi + xi * hr))
            Bv = _fft_pow2(Y, 1)
            for n2 in range(HY_N2):
                a_ref[0, n2, rows, ln] = Bv[n2][0]
                a_ref[1, n2, rows, ln] = Bv[n2][1]
        return 0

    lax.fori_loop(0, nchunk, body, 0)
    for n2 in range(HY_N2):
        bb = jnp.concatenate([a_ref[0, n2], a_ref[1, n2]], axis=0).astype(BF16)
        yv = _dot(g_ref[n2], bb)
        for t in range(ntile):
            yf_ref[t, pl.ds(n2, nseg, stride=HY_N2), :] = yv[:, t * 128:(t + 1) * 128]
    for t in range(ntile):
        ln = slice(t * 128, (t + 1) * 128)
        o_ref[0, :, ln] = (gate_ref[0, :, ln].astype(F32)
                           * (yf_ref[t] + zf_ref[t] * bias_ref[:, ln])).astype(BF16)


def _hyconv(zsrc, zcol, zrow, gsrc, gcol, grow, hspec, order, fmat, gmat, bias, ls):
    B = zsrc.shape[0]
    k1p = fmat.shape[1] // 2
    once = lambda a: pl.BlockSpec(a.shape, lambda b: (0,) * a.ndim, pipeline_mode=pl.Buffered(1))
    return pl.pallas_call(
        functools.partial(_hyconv_kernel, ls=ls),
        grid=(B,),
        in_specs=[pl.BlockSpec((1, ls, HY_W), lambda b: (b, zrow, zcol)),
                  pl.BlockSpec((1, ls, HY_W), lambda b: (b, grow, gcol)),
                  pl.BlockSpec((1, 2, HY_N2, k1p, HY_W), lambda b: (order, 0, 0, 0, 0),
                               pipeline_mode=pl.Buffered(1)),
                  once(fmat), once(gmat), pl.BlockSpec((1, HY_W), lambda b: (0, 0))],
        out_specs=pl.BlockSpec((1, ls, HY_W), lambda b: (b, 0, 0)),
        out_shape=jax.ShapeDtypeStruct((B, ls, HY_W), BF16),
        scratch_shapes=[pltpu.VMEM((HY_W // 128, ls, 128), F32), pltpu.VMEM((HY_W // 128, ls, 128), F32),
                        pltpu.VMEM((2, HY_N2, k1p, HY_W), F32)],
        compiler_params=_cparams(("arbitrary",)),
        name="hyena_longconv",
    )(zsrc, gsrc, hspec, fmat, gmat, bias)


def _outproj_kernel(x_ref, mod_ref, ya_ref, ybl_ref, ybc_ref, hf_ref, hb_ref, oc_ref, yd_ref, mlg_ref, g64_ref,
                    w_ref, g2_ref, wr_ref, xo_ref, h2_ref, aff_ref, *, nlat):
    hsum = (hf_ref[0] + hb_ref[0]).T
    msq = _dot((hsum * hsum).astype(BF16), g64_ref[...])
    yc = jax.nn.sigmoid(oc_ref[0].astype(F32)) * (hsum * lax.rsqrt(msq + EPS) * mlg_ref[...])
    yb = jnp.where(pl.program_id(1) < nlat, ybl_ref[0], ybc_ref[0])
    y = jnp.concatenate([ya_ref[0], yb, yc.astype(BF16), yd_ref[0]], axis=1)
    mod = mod_ref[0, 0]
    x = x_ref[0] + mod[2:3, :] * _dot(y, w_ref[...])
    xo_ref[0] = x
    ms = jnp.mean(x * x, axis=-1, keepdims=True)
    h2 = x * lax.rsqrt(ms + EPS) * g2_ref[...] * (1.0 + mod[4:5, :]) + mod[3:4, :]
    h2_hi = h2.astype(BF16)
    h2_ref[0] = _pack_bf16_pair(h2_hi[:, 0:D_MODEL // 2], h2_hi[:, D_MODEL // 2:])
    h2_lo = (h2 - h2_hi.astype(F32)).astype(BF16)
    lg = _dot(h2_hi, wr_ref[...])
    logits = lg[:, 0:128] + lg[:, 128:256] + _dot(h2_lo, wr_ref[:, 0:128])
    lane = lax.broadcasted_iota(I32, logits.shape, 1)
    logits = jnp.where(lane < N_EXPERTS, logits, NEG)
    e = jnp.exp(logits - jnp.max(logits, axis=-1, keepdims=True))
    aff_ref[0] = (e / jnp.sum(e, axis=-1, keepdims=True)).T


def _outproj(x, mod, ya, yb_l, yb_c, hf, hb, oc, yd, mlg, g64, w_out, g2, w_router, T):
    B, N, D = x.shape
    rb = ROW_BLOCK
    nlat = T // rb
    row = lambda w: pl.BlockSpec((1, rb, w), lambda b, r: (b, r, 0))
    full = lambda a: pl.BlockSpec(a.shape, lambda b, r: (0,) * a.ndim)
    return pl.pallas_call(
        functools.partial(_outproj_kernel, nlat=nlat),
        grid=(B, N // rb),
        in_specs=[row(D), pl.BlockSpec((1, 1, 6, D), lambda b, r: (b, jnp.where(r >= nlat, 1, 0), 0, 0)),
                  row(256),
                  pl.BlockSpec((1, rb, 256), lambda b, r: (b, jnp.minimum(r, nlat - 1), 0)),
                  pl.BlockSpec((1, rb, 256), lambda b, r: (b, jnp.maximum(r - nlat, 0), 0)),
                  pl.BlockSpec((1, 256, rb), lambda b, r: (b, 0, r)),
                  pl.BlockSpec((1, 256, rb), lambda b, r: (b, 0, r)), row(256), row(256),
                  full(mlg), full(g64), full(w_out), full(g2), full(w_router)],
        out_specs=[row(D), row(D // 2), pl.BlockSpec((1, 128, rb), lambda b, r: (b, 0, r))],
        out_shape=[jax.ShapeDtypeStruct((B, N, D), F32), jax.ShapeDtypeStruct((B, N, D // 2), I32),
                   jax.ShapeDtypeStruct((B, 128, N), F32)],
        compiler_params=_cparams(("arbitrary", "arbitrary")),
        name="outproj",
    )(x, mod, ya, yb_l, yb_c, hf, hb, oc, yd, mlg, g64, w_out, g2, w_router)


def _prefix_exclusive(x):
    n = x.shape[1]
    lane = lax.broadcasted_iota(I32, x.shape, 1)
    inc = x
    d = 1
    while d < n:
        inc = inc + jnp.where(lane >= d, pltpu.roll(inc, d, axis=1), 0)
        d *= 2
    return inc - x


def _topk_kernel(aff_ref, o_ref, pos_ref, val_ref, *, T, S):
    def segment(t0, ts, cap, slot0):
        a = aff_ref[0, :, t0:t0 + ts]
        bits = lax.bitcast_convert_type(a, I32)

        def search(i, v):
            cand = v | jnp.left_shift(jnp.int32(1), 30 - i)
            cnt = jnp.sum((bits >= cand).astype(I32), axis=1, keepdims=True)
            return jnp.where(cnt >= cap, cand, v)

        thr = lax.fori_loop(0, 31, search, jnp.zeros((N_EXPERTS, 1), I32))
        gt = bits > thr
        eq = bits == thr
        need = cap - jnp.sum(gt.astype(I32), axis=1, keepdims=True)
        sel = gt | (eq & (_prefix_exclusive(eq.astype(I32)) < need))
        seli = sel.astype(I32)
        pos_ref[:, 0:ts] = jnp.where(sel, _prefix_exclusive(seli), -1)
        a_hi = a.astype(BF16)
        r1 = a - a_hi.astype(F32)
        a_mid = r1.astype(BF16)
        a_lo = (r1 - a_mid.astype(F32)).astype(BF16)
        val_ref[0, :, 0:ts] = a_hi.astype(F32)
        val_ref[1, :, 0:ts] = a_mid.astype(F32)
        val_ref[2, :, 0:ts] = a_lo.astype(F32)
        tabs = lax.broadcasted_iota(I32, (1, ts), 1) + t0
        t_hi = jnp.right_shift(tabs, 6).astype(F32)
        t_lo = (tabs & 63).astype(F32)
        slot = lax.broadcasted_iota(I32, (cap, ts), 0)
        zeros = jnp.zeros((11, ts), F32)

        def per_expert(e, _):
            onehot = jnp.where(pos_ref[pl.ds(e, 1), 0:ts] == slot, 1.0, 0.0).astype(BF16)
            vals = jnp.concatenate([t_hi, t_lo, val_ref[0, pl.ds(e, 1), 0:ts], val_ref[1, pl.ds(e, 1), 0:ts],
                                    val_ref[2, pl.ds(e, 1), 0:ts], zeros], axis=0).astype(BF16)
            o_ref[0, e, slot0:slot0 + cap, :] = _dot_nt(onehot, vals)
            return 0

        lax.fori_loop(0, N_EXPERTS, per_expert, 0)

    segment(0, T, (EC_CAPACITY * T) // N_EXPERTS, 0)
    segment(T, S, (EC_CAPACITY * S) // N_EXPERTS, (EC_CAPACITY * T) // N_EXPERTS)


def _topk(aff_t, T, S):
    B, _, N = aff_t.shape
    E = N_EXPERTS
    cap_t = (EC_CAPACITY * T) // E + (EC_CAPACITY * S) // E
    return pl.pallas_call(
        functools.partial(_topk_kernel, T=T, S=S),
        grid=(B,),
        in_specs=[pl.BlockSpec((1, E, N), lambda b: (b, 0, 0))],
        out_specs=pl.BlockSpec((1, E, cap_t, 16), lambda b: (b, 0, 0, 0)),
        out_shape=jax.ShapeDtypeStruct((B, E, cap_t, 16), F32),
        scratch_shapes=[pltpu.VMEM((E, T), I32), pltpu.VMEM((3, E, T), F32)],
        compiler_params=_cparams(("arbitrary",)),
        name="expert_topk",
    )(aff_t)


def _gather_kernel(idx_ref, h_ref, *rest, cap_t):
    o_ref = rest[-1]
    b, e = pl.program_id(0), pl.program_id(1)
    base = (b * N_EXPERTS + e) * cap_t

    def body(g, _):
        rows = [h_ref[0, pl.ds(idx_ref[base + g * 16 + k], 1), :] for k in range(16)]
        o_ref[0, 0, pl.ds(pl.multiple_of(g * 16, 16), 16), :] = jnp.concatenate(rows, axis=0)
        return 0

    lax.fori_loop(0, cap_t // 16, body, 0)


def _gather(idx_flat, h2p, xe_buf, b0, bg, cap_t):
    B, N, hw = h2p.shape
    in_specs = [pl.BlockSpec((1, N, hw), lambda b, e, idx: (b0 + b, 0, 0))]
    args = [idx_flat, h2p]
    aliases = {}
    if xe_buf is not None:
        in_specs.append(pl.BlockSpec(memory_space=pl.ANY))
        args.append(xe_buf)
        aliases = {2: 0}
    return pl.pallas_call(
        functools.partial(_gather_kernel, cap_t=cap_t),
        grid_spec=pltpu.PrefetchScalarGridSpec(
            num_scalar_prefetch=1,
            grid=(bg, N_EXPERTS),
            in_specs=in_specs,
            out_specs=pl.BlockSpec((1, 1, cap_t, hw), lambda b, e, idx: (b0 + b, e, 0, 0)),
        ),
        out_shape=jax.ShapeDtypeStruct((B, N_EXPERTS, cap_t, hw), I32),
        input_output_aliases=aliases,
        compiler_params=_cparams(("arbitrary", "arbitrary")),
        name="expert_gather",
    )(*args)


def _ffn_kernel(x_ref, wg_ref, wu_ref, wd_ref, gate_ref, gt_ref, o_ref, wg_s, wu_s, wd_s, *, cap_l):
    @pl.when(pl.program_id(1) == 0)
    def _():
        wg_s[...] = wg_ref[0, 0].astype(BF16)
        wu_s[...] = wu_ref[0, 0].astype(BF16)
        wd_s[...] = wd_ref[0, 0].astype(BF16)

    x = jnp.concatenate(_unpack_bf16_pair(x_ref[0, 0]), axis=1)
    a = _dot(x, wg_s[...])
    u = _dot(x, wu_s[...])
    hmid = (a * jax.nn.sigmoid(a) * u).astype(BF16)
    y = _dot(hmid, wd_s[...])
    gt = gt_ref[0]
    row = lax.broadcasted_iota(I32, y.shape, 0)
    gt2 = jnp.where(row < cap_l, gt[0:1, :], gt[1:2, :])
    o_ref[0, 0] = y * gate_ref[0, 0] * gt2


def _ffn(xe, layer, wg, wu, wd, gate, gt2, cap_l):
    B, E, cap_t, hw = xe.shape
    D = 2 * hw
    wspec = pl.BlockSpec((1, 1, D, D), lambda e, b: (layer, e, 0, 0))
    return pl.pallas_call(
        functools.partial(_ffn_kernel, cap_l=cap_l),
        grid=(E, B),
        in_specs=[pl.BlockSpec((1, 1, cap_t, hw), lambda e, b: (b, e, 0, 0)), wspec, wspec, wspec,
                  pl.BlockSpec((1, 1, cap_t, 1), lambda e, b: (b, e, 0, 0)),
                  pl.BlockSpec((1, 2, D), lambda e, b: (b, 0, 0))],
        out_specs=pl.BlockSpec((1, 1, cap_t, D), lambda e, b: (b, e, 0, 0)),
        out_shape=jax.ShapeDtypeStruct((B, E, cap_t, D), F32),
        scratch_shapes=[pltpu.VMEM((D, D), BF16)] * 3,
        compiler_params=_cparams(("arbitrary", "arbitrary")),
        name="expert_ffn",
    )(xe, wg, wu, wd, gate, gt2)


def _scatter_kernel(idx_ref, x_ref, y_ref, o_ref, *, cap_t):
    b, e = pl.program_id(0), pl.program_id(2)
    base = (b * N_EXPERTS + e) * cap_t

    @pl.when(e == 0)
    def _():
        o_ref[...] = x_ref[...]

    def body(g, _):
        tile = y_ref[0, 0, pl.ds(pl.multiple_of(g * 8, 8), 8), :]
        rows = [idx_ref[base + g * 8 + k] for k in range(8)]
        cur = [o_ref[0, pl.ds(r, 1), :] for r in rows]
        for k, r in enumerate(rows):
            o_ref[0, pl.ds(r, 1), :] = cur[k] + tile[k:k + 1, :]
        return 0

    lax.fori_loop(0, cap_t // 8, body, 0)


def _scatter(idx_flat, x, ye, b0, bg, cap_t):
    B, N, D = x.shape
    hw = D // 2
    return pl.pallas_call(
        functools.partial(_scatter_kernel, cap_t=cap_t),
        grid_spec=pltpu.PrefetchScalarGridSpec(
            num_scalar_prefetch=1,
            grid=(bg, 2, N_EXPERTS),
            in_specs=[pl.BlockSpec((1, N, hw), lambda b, c, e, idx: (b0 + b, 0, c)),
                      pl.BlockSpec((1, 1, cap_t, hw), lambda b, c, e, idx: (b0 + b, e, 0, c))],
            out_specs=pl.BlockSpec((1, N, hw), lambda b, c, e, idx: (b0 + b, 0, c)),
        ),
        out_shape=jax.ShapeDtypeStruct((B, N, D), F32),
        input_output_aliases={1: 0},
        compiler_params=_cparams(("arbitrary", "arbitrary", "arbitrary")),
        name="expert_scatter",
    )(idx_flat, x, ye)


def _rope_tables(T, S, dh):
    rows = T // GRID_W
    r = jnp.broadcast_to(jnp.arange(rows, dtype=F32)[:, None], (rows, GRID_W)).reshape(T)
    col = jnp.broadcast_to(jnp.arange(GRID_W, dtype=F32)[None, :], (rows, GRID_W)).reshape(T)
    nf = dh // 4
    inv = ROPE_BASE ** (-jnp.arange(nf, dtype=F32) / nf)
    ar, ac = r[:, None] * inv, col[:, None] * inv
    ang = jnp.concatenate([ar, ar, ac, ac], axis=1)
    ang = jnp.concatenate([ang, jnp.zeros((S, dh), F32)], axis=0)
    reps = 256 // dh
    return jnp.tile(jnp.cos(ang), (1, reps)), jnp.tile(jnp.sin(ang), (1, reps))


def _group_mats(dh):
    i = np.arange(256)
    gmat = (i[:, None] // dh == i[None, :] // dh).astype(np.float32) / dh
    nf = dh // 4
    half = (i % (2 * nf)) // nf
    pmat = np.zeros((256, 256), np.float32)
    a_idx = i[half == 0]
    pmat[a_idx + nf, a_idx] = -1.0
    pmat[a_idx, a_idx + nf] = 1.0
    return jnp.asarray(gmat, BF16), jnp.asarray(pmat, BF16)


def _hyena_tables(ls):
    t = jnp.linspace(0.0, 1.0, ls, dtype=F32)[:, None]
    w = 2.0 * math.pi * jnp.arange(ls, dtype=F32)[:, None] / ls
    bands = jnp.linspace(1e-4, HY_BANDS - 1, HY_BANDS, dtype=F32)
    z = jnp.concatenate([t, jnp.cos(bands * w), -jnp.sin(bands * w)], axis=-1)
    z = jnp.pad(z, ((0, 0), (0, 128 - HY_EMB)))
    deltas = jnp.abs(jnp.linspace(math.log(HY_TARGET) / HY_SLOW, math.log(HY_TARGET) / HY_FAST, HY_W, dtype=F32))
    decay = jnp.exp(-t * deltas)
    n, nseg = 2 * ls, ls // HY_N2
    k1p = -(-(nseg + 1) // 8) * 8
    k1 = jnp.arange(k1p, dtype=I32)[None, :, None]
    tpos = HY_N2 * jnp.arange(nseg, dtype=I32)[None, None, :] + jnp.arange(HY_N2, dtype=I32)[:, None, None]
    ang = ((k1 * tpos) % n).astype(F32) * (2.0 * math.pi / n)
    keep = k1 <= nseg
    fmat = jnp.concatenate([jnp.where(keep, jnp.cos(ang), 0.0), jnp.where(keep, -jnp.sin(ang), 0.0)],
                           axis=1).astype(BF16)
    return z, decay, fmat, jnp.swapaxes(fmat, 1, 2)


def kernel(x, c, ctx, c_ctx, w_ada, b_ada, norm1_g, norm2_g, w_in, b_gate, a_qnorm, a_knorm, a_sink, hy_conv, hy_fw1, hy_fb1, hy_freq, hy_fw2, hy_fb2, hy_fw3, hy_bias, ml_norm, d_qnorm, d_knorm, d_lq1, d_lk1, d_lq2, d_lk2, d_subnorm, w_out, w_router, w_e_gate, w_e_up, w_e_down):
    B, T, D = x.shape
    S = ctx.shape[1]
    N = T + S
    depth = w_ada.shape[0]
    assert D == D_MODEL and T % S == 0 and S % ROW_BLOCK == 0 and S % MLSTM_CHUNK == 0
    assert S % WATTN_QROWS == 0 and T >= WATTN_QROWS + 2 * BLK and WATTN_QROWS & (WATTN_QROWS - 1) == 0
    cap_l, cap_c = (EC_CAPACITY * T) // N_EXPERTS, (EC_CAPACITY * S) // N_EXPERTS
    cap_t = cap_l + cap_c
    assert cap_t % 16 == 0

    xs = jnp.concatenate([x, ctx], axis=1)

    rpad = -(B + 1) % 8
    cc = jnp.concatenate([c, c_ctx[None, :], jnp.zeros((rpad, D), F32)], axis=0)
    mods = _ada_mods(cc, w_ada, b_ada)

    cos_a, sin_a = _rope_tables(T, S, DH_A)
    cos_d, sin_d = _rope_tables(T, S, DH_D)
    g64, p64 = _group_mats(DH_A)
    g32, p32 = _group_mats(DH_D)
    tabs_l = _hyena_tables(T)
    tabs_c = _hyena_tables(S)

    offs = np.cumsum((0, 256, 128, 128, 768, 256, 256, 256, 256, 16, 256, 256, 256))
    for l in range(depth):
        lam_init = 0.8 - 0.6 * math.exp(-0.3 * l)
        ml = mods[l]
        mod = jnp.stack([ml[:B].reshape(B, 6, D), jnp.broadcast_to(ml[B].reshape(1, 6, D), (B, 6, D))],
                        axis=1)
        w = w_in[l]
        wp = jnp.concatenate([w[:, offs[0]:offs[8]], w[:, offs[8]:offs[9]], jnp.zeros((D, 112), F32),
                              w[:, offs[9]:offs[12]]], axis=1).astype(BF16)
        gains = jnp.stack([jnp.tile(a_qnorm[l], 4), jnp.tile(a_knorm[l], 4),
                           jnp.tile(d_qnorm[l], 8), jnp.tile(d_knorm[l], 8)], axis=0)
        qa, ka, va, ub, kc, oc, gc, qd, kd, vd, gct, qct, vct = _inproj(
            xs, mod, norm1_g[l].reshape(1, D), wp, (cos_a, sin_a, cos_d, sin_d), gains, (g64, g32, p64, p32), T)

        ya = _wattn(qa, ka, va, jnp.pad(a_sink[l], (0, 128 - H_A)).reshape(1, 128), T, S)
        yd = _dattn(qd, kd, vd, jnp.stack([d_lq1[l], d_lk1[l], d_lq2[l], d_lk2[l]], axis=0),
                    d_subnorm[l].reshape(1, 2 * DH_D), T, S, lam_init)

        bias = b_gate[l].reshape(-1)
        hf, hb = _mlstm(qct, kc, vct, gc, gct, jnp.pad(bias, (0, 112)).reshape(1, 128), bias.reshape(16, 1), T, S)

        uc = _shortconv(ub, hy_conv[l], T, S)
        w1 = jnp.pad(hy_fw1[l], ((0, 128 - HY_EMB), (0, 0)))
        ybs = []
        for (zf, decay, fmat, gmat), ls, rowblk in ((tabs_l, T, 0), (tabs_c, S, T // S)):
            hpast, hfut, ss = _hyfilter(zf, w1, hy_fb1[l].reshape(1, -1), hy_freq[l].reshape(1, -1), hy_fw2[l],
                                        hy_fb2[l].reshape(1, -1), hy_fw3[l], decay)
            hspec = _hyspec(hpast, hfut, ss, fmat)
            z1 = _hyconv(uc, 0, rowblk, uc, 1, rowblk, hspec, 0, fmat, gmat, hy_bias[l, 0].reshape(1, HY_W), ls)
            z2 = _hyconv(z1, 0, 0, uc, 2, rowblk, hspec, 1, fmat, gmat, hy_bias[l, 1].reshape(1, HY_W), ls)
            ybs.append(z2)

        wr = jnp.pad(w_router[l], ((0, 0), (0, 128 - N_EXPERTS)))
        wr_hi = wr.astype(BF16)
        wr_cat = jnp.concatenate([wr_hi, (wr - wr_hi.astype(F32)).astype(BF16)], axis=1)
        xs, h2, aff = _outproj(xs, mod, ya, ybs[0], ybs[1], hf, hb, oc, yd, jnp.tile(ml_norm[l], 4).reshape(1, 256), g64,
                               w_out[l].astype(BF16), norm2_g[l].reshape(1, D), wr_cat, T)

        sel = _topk(aff, T, S)
        idx = (sel[..., 0] * 64.0 + sel[..., 1]).astype(I32)
        gate = (sel[..., 2] + sel[..., 3] + sel[..., 4])[..., None]
        gt2 = mod[:, :, 5, :]
        bg = min(IDX_BATCH_GROUP, B)
        groups = [(b0, idx[b0:b0 + bg].reshape(-1)) for b0 in range(0, B, bg)]
        xe = None
        for b0, idx_flat in groups:
            xe = _gather(idx_flat, h2, xe, b0, bg, cap_t)
        ye = _ffn(xe, l, w_e_gate, w_e_up, w_e_down, gate, gt2, cap_l)
        for b0, idx_flat in groups:
            xs = _scatter(idx_flat, xs, ye, b0, bg, cap_t)
    return xs[:, 0:T]
```

```python
import functools
import math

import jax
import jax.numpy as jnp
import numpy as np
from jax import lax
from jax.experimental import pallas as pl
from jax.experimental.pallas import tpu as pltpu

F32 = jnp.float32
BF16 = jnp.bfloat16
I32 = jnp.int32
HI = lax.Precision.HIGHEST

D_MODEL = 1024
GRID_W = 64
GROUP_W = D_MODEL // 4
H_A, HKV_A = 4, 2
G_A = H_A // HKV_A
DH_A = GROUP_W // H_A
WINDOW = 128
BLK = 128
HY_W = GROUP_W
HY_ORDER = 2
HY_BANDS = 16
HY_EMB = 1 + 2 * HY_BANDS
HY_HID = 64
HY_TARGET, HY_FAST, HY_SLOW = 1e-2, 0.3, 1.5
HY_N2 = 16
H_C = 4
DH_C = GROUP_W // H_C
H_D = 4
DH_D = GROUP_W // (2 * H_D)
N_EXPERTS = 16
EC_CAPACITY = 2
ROPE_BASE = 10000.0
EPS = 1e-6
NEG = -1e30

ROW_BLOCK = 256
WATTN_QROWS = 256
DATTN_MIN_ROWSUM = 1e-25
MLSTM_CHUNK = 256
IDX_BATCH_GROUP = 4
VMEM_LIMIT = 56 * 1024 * 1024

_C_QA, _C_KA, _C_VA, _C_UB = 0, 256, 384, 512
_C_QC, _C_KC, _C_VC, _C_OC, _C_GC = 1280, 1536, 1792, 2048, 2304
_C_QD, _C_KD, _C_VD, _C_END = 2432, 2688, 2944, 3200


def _cparams(sem, vmem=VMEM_LIMIT):
    return pltpu.CompilerParams(dimension_semantics=sem, vmem_limit_bytes=vmem)


def _dot(a, b, **kw):
    return jnp.dot(a, b, preferred_element_type=F32, **kw)


def _dot_nt(a, b, **kw):
    return lax.dot_general(a, b, (((1,), (1,)), ((), ())), preferred_element_type=F32, **kw)


def _dot_tn(a, b, **kw):
    return lax.dot_general(a, b, (((0,), (0,)), ((), ())), preferred_element_type=F32, **kw)


def _pack_bf16_pair(a, b):
    ua = lax.bitcast_convert_type(a.astype(F32), I32)
    ub = lax.bitcast_convert_type(b.astype(F32), I32)
    return ua | lax.shift_right_logical(ub, 16)


def _unpack_bf16_pair(w):
    a = lax.bitcast_convert_type(w & jnp.int32(-65536), F32)
    b = lax.bitcast_convert_type(lax.shift_left(w, 16), F32)
    return a.astype(BF16), b.astype(BF16)


def _ada_kernel(c_ref, w_ref, b_ref, o_ref):
    c = c_ref[...]
    s = c * jax.nn.sigmoid(c)
    o_ref[0] = _dot(s, w_ref[0], precision=HI) + b_ref[0]


def _ada_mods(cc, w_ada, b_ada):
    L, D, W6 = w_ada.shape
    R = cc.shape[0]
    cb = 1536
    return pl.pallas_call(
        _ada_kernel,
        grid=(L, W6 // cb),
        in_specs=[pl.BlockSpec((R, D), lambda l, j: (0, 0)),
                  pl.BlockSpec((1, D, cb), lambda l, j: (l, 0, j)),
                  pl.BlockSpec((1, 1, cb), lambda l, j: (l, 0, j))],
        out_specs=pl.BlockSpec((1, R, cb), lambda l, j: (l, 0, j)),
        out_shape=jax.ShapeDtypeStruct((L, R, W6), F32),
        compiler_params=_cparams(("arbitrary", "arbitrary")),
        name="ada_mods",
    )(cc, w_ada, b_ada.reshape(L, 1, W6))


def _inproj_kernel(*refs, has_f):
    if has_f:
        x_ref, f_ref, *refs = refs
        *refs, xsum = refs
        x = x_ref[0] + f_ref[0]
        xsum[0] = x
    else:
        x_ref, *refs = refs
        x = x_ref[0]
    (mod_ref, g1_ref, w_ref, ca_ref, sa_ref, cd_ref, sd_ref, gains_ref, g64_ref, g32_ref, p64_ref, p32_ref,
     qa, ka, va, ub, kc, oc, gc, qd, kd, vd, gct, qct, vct) = refs
    ms = jnp.mean(x * x, axis=-1, keepdims=True)
    xn = x * lax.rsqrt(ms + EPS) * g1_ref[...]
    mod = mod_ref[0, 0]
    h = xn * (1.0 + mod[1:2, :]) + mod[0:1, :]
    p = _dot(h.astype(BF16), w_ref[...])

    def headnorm_rope(t, gmat, gain, pmat, cos, sin, scale):
        w = t.shape[1]
        msq = _dot((t * t).astype(BF16), gmat[0:w, 0:w])
        tn = t * lax.rsqrt(msq + EPS) * gain
        tn = tn * cos[:, 0:w] + _dot(tn.astype(BF16), pmat[0:w, 0:w]) * sin[:, 0:w]
        return tn * scale

    ca, sa, cd, sd = ca_ref[...], sa_ref[...], cd_ref[...], sd_ref[...]
    gains = gains_ref[...]
    qa[0] = headnorm_rope(p[:, _C_QA:_C_KA], g64_ref, gains[0:1, :], p64_ref, ca, sa, DH_A ** -0.5).astype(BF16)
    ka[0] = headnorm_rope(p[:, _C_KA:_C_VA], g64_ref, gains[1:2, 0:128], p64_ref, ca, sa, 1.0).astype(BF16)
    va[0] = p[:, _C_VA:_C_UB].astype(BF16)
    ub[0] = p[:, _C_UB:_C_QC].astype(BF16)
    qct[0] = p[:, _C_QC:_C_KC].T.astype(BF16)
    kc[0] = (p[:, _C_KC:_C_VC] * DH_C ** -0.5).astype(BF16)
    vct[0] = p[:, _C_VC:_C_OC].T.astype(BF16)
    oc[0] = p[:, _C_OC:_C_GC].astype(BF16)
    gc[0] = p[:, _C_GC:_C_QD]
    gct[0] = p[:, _C_GC:_C_QD].T
    qd[0] = headnorm_rope(p[:, _C_QD:_C_KD], g32_ref, gains[2:3, :], p32_ref, cd, sd, DH_D ** -0.5).astype(BF16)
    kd[0] = headnorm_rope(p[:, _C_KD:_C_VD], g32_ref, gains[3:4, :], p32_ref, cd, sd, 1.0).astype(BF16)
    vd[0] = p[:, _C_VD:_C_END].astype(BF16)


def _inproj(x, f, mod, g1, w_packed, tabs, gains, mats, T):
    B, N, D = x.shape
    rb = ROW_BLOCK
    nlat = T // rb
    row = lambda w: pl.BlockSpec((1, rb, w), lambda b, r: (b, r, 0))
    tab = pl.BlockSpec((rb, 256), lambda b, r: (r, 0))
    full = lambda a: pl.BlockSpec(a.shape, lambda b, r: (0,) * a.ndim)
    widths = [256, 128, 128, 768, 256, 256, 128, 256, 256, 256]
    dtypes = [BF16] * 6 + [F32] + [BF16] * 3
    tr = lambda w: pl.BlockSpec((1, w, rb), lambda b, r: (b, 0, r))
    has_f = f is not None
    return pl.pallas_call(
        functools.partial(_inproj_kernel, has_f=has_f),
        grid=(B, N // rb),
        in_specs=[row(D)] * (2 if has_f else 1)
        + [pl.BlockSpec((1, 1, 6, D), lambda b, r: (b, jnp.where(r >= nlat, 1, 0), 0, 0)),
           full(g1), full(w_packed), tab, tab, tab, tab, full(gains)] + [full(m) for m in mats],
        out_specs=[row(w) for w in widths] + [tr(128), tr(256), tr(256)] + ([row(D)] if has_f else []),
        out_shape=[jax.ShapeDtypeStruct((B, N, w), dt) for w, dt in zip(widths, dtypes)]
        + [jax.ShapeDtypeStruct((B, 128, N), F32), jax.ShapeDtypeStruct((B, 256, N), BF16),
           jax.ShapeDtypeStruct((B, 256, N), BF16)] + ([jax.ShapeDtypeStruct((B, N, D), F32)] if has_f else []),
        compiler_params=_cparams(("arbitrary", "arbitrary")),
        name="inproj",
    )(x, *([f] if has_f else []), mod, g1, w_packed, *tabs, gains, *mats)


def _wattn_kernel(q_ref, k_ref, v_ref, sink_ref, sinkw_ref, placeq_ref, gsumq_ref, placek_ref, gsumk_ref, o_ref,
                  kaug_ref, vaug_ref, kmax_ref, *, T, S):
    j = pl.program_id(1)
    qb, kwin = WATTN_QROWS, WATTN_QROWS + 2 * BLK
    is_lat = j < T // qb
    start = pl.multiple_of(jnp.clip(j * qb - BLK, 0, T - kwin), BLK)
    row = lax.broadcasted_iota(I32, (G_A * qb, kwin), 0)
    qpos = j * qb + (row & (qb - 1))
    kpos = start + lax.broadcasted_iota(I32, (G_A * qb, kwin), 1)
    valid = (jnp.abs(qpos - kpos) <= WINDOW) & is_lat
    grp = jnp.right_shift(lax.broadcasted_iota(I32, (G_A * qb, 1), 0), qb.bit_length() - 1)
    sinks = sink_ref[...]

    @pl.when(j == 0)
    def _():
        lane = lax.broadcasted_iota(I32, (1, HKV_A * 128), 1)
        one = jnp.where((lane & 127) == DH_A, 1.0, 0.0)
        k_wide = _dot(k_ref[0], placek_ref[...])
        kaug_ref[...] = (k_wide + one).astype(BF16)
        vaug_ref[...] = (_dot(v_ref[0], placek_ref[...]) + one).astype(BF16)
        ksq = _dot((k_wide * k_wide).astype(BF16), gsumk_ref[...])
        kmax_ref[...] = jnp.sqrt(jnp.max(ksq, axis=0, keepdims=True))

    def sink_rows(heads):
        sink = sinks[0:1, heads[0]:heads[0] + 1]
        for g in range(1, G_A):
            sink = jnp.where(grp == g, sinks[0:1, heads[g]:heads[g] + 1], sink)
        return sink

    def store(heads, o):
        for g, hd in enumerate(heads):
            o_ref[0, :, hd * DH_A:(hd + 1) * DH_A] = o[g * qb:(g + 1) * qb].astype(BF16)

    def body_fast():
        q_wide = _dot(q_ref[0], placeq_ref[...])
        qn = jnp.sqrt(_dot((q_wide * q_wide).astype(BF16), gsumq_ref[...]))
        km = kmax_ref[...]
        kmq = jnp.concatenate([km[:, (hd // G_A) * 128:(hd // G_A + 1) * 128] for hd in range(H_A)], axis=1)
        lane = lax.broadcasted_iota(I32, (1, H_A * 128), 1)
        shift = jnp.where((lane & 127) == DH_A, jnp.maximum(qn * kmq, sinkw_ref[...]), 0.0)
        q_aug = (q_wide - shift).astype(BF16)
        low = None
        for hk in range(HKV_A):
            heads = [hk * G_A + g for g in range(G_A)]
            ks = slice(hk * 128, (hk + 1) * 128)
            q2 = jnp.concatenate([q_aug[:, hd * 128:(hd + 1) * 128] for hd in heads], axis=0)
            m_used = -q2[:, DH_A:DH_A + 1].astype(F32)
            s_loc = jnp.where(valid, _dot_nt(q2, kaug_ref[pl.ds(start, kwin), ks]), NEG)
            s_ctx = _dot_nt(q2, kaug_ref[T:T + S, ks])
            pv = (_dot(jnp.exp(s_loc.astype(BF16)), vaug_ref[pl.ds(start, kwin), ks])
                  + _dot(jnp.exp(s_ctx.astype(BF16)), vaug_ref[T:T + S, ks]))
            sigma = pv[:, DH_A:DH_A + 1]
            low = sigma if low is None else jnp.minimum(low, sigma)
            store(heads, pv[:, 0:DH_A] / (sigma + jnp.exp(sink_rows(heads) - m_used)))
        return jnp.min(low)

    def body_exact():
        kw = k_ref[0, pl.ds(start, kwin), :]
        vw = v_ref[0, pl.ds(start, kwin), :]
        kc = k_ref[0, T:T + S, :]
        vc = v_ref[0, T:T + S, :]
        for hk in range(HKV_A):
            cs = slice(hk * DH_A, (hk + 1) * DH_A)
            kwh, vwh, kch, vch = kw[:, cs], vw[:, cs], kc[:, cs], vc[:, cs]
            heads = [hk * G_A + g for g in range(G_A)]
            q = jnp.concatenate([q_ref[0, :, hd * DH_A:(hd + 1) * DH_A] for hd in heads], axis=0)
            sink = sink_rows(heads)
            s_loc = jnp.where(valid, _dot_nt(q, kwh), NEG)
            s_ctx = _dot_nt(q, kch)
            m = jnp.maximum(jnp.maximum(jnp.max(s_loc, axis=-1, keepdims=True),
                                        jnp.max(s_ctx, axis=-1, keepdims=True)), sink)
            p_loc = jnp.exp(s_loc - m)
            p_ctx = jnp.exp(s_ctx - m)
            den = (jnp.sum(p_loc, axis=-1, keepdims=True) + jnp.sum(p_ctx, axis=-1, keepdims=True)
                   + jnp.exp(sink - m))
            store(heads, (_dot(p_loc.astype(BF16), vwh) + _dot(p_ctx.astype(BF16), vch)) / den)

    low = body_fast()

    @pl.when(jnp.logical_not(low > DATTN_MIN_ROWSUM))
    def _():
        body_exact()


def _wattn_mats(nheads):
    place = np.zeros((nheads * DH_A, nheads * 128), np.float32)
    gsum = np.zeros((nheads * 128, nheads * 128), np.float32)
    for h in range(nheads):
        for d in range(DH_A):
            place[h * DH_A + d, h * 128 + d] = 1.0
            gsum[h * 128 + d, h * 128 + DH_A] = 1.0
    return jnp.asarray(place, BF16), jnp.asarray(gsum, BF16)


def _wattn(qa, ka, va, sink, T, S):
    B, N, _ = qa.shape
    qb = WATTN_QROWS
    placeq, gsumq = _wattn_mats(H_A)
    placek, gsumk = _wattn_mats(HKV_A)
    sinkw = jnp.zeros((1, H_A * 128), F32).at[0, DH_A::128].set(sink[0, 0:H_A])
    full = lambda a: pl.BlockSpec(a.shape, lambda b, j: (0,) * a.ndim)
    return pl.pallas_call(
        functools.partial(_wattn_kernel, T=T, S=S),
        grid=(B, N // qb),
        in_specs=[pl.BlockSpec((1, qb, 256), lambda b, j: (b, j, 0)),
                  pl.BlockSpec((1, N, 128), lambda b, j: (b, 0, 0)),
                  pl.BlockSpec((1, N, 128), lambda b, j: (b, 0, 0)),
                  pl.BlockSpec((1, 128), lambda b, j: (0, 0)),
                  full(sinkw), full(placeq), full(gsumq), full(placek), full(gsumk)],
        out_specs=pl.BlockSpec((1, qb, 256), lambda b, j: (b, j, 0)),
        out_shape=jax.ShapeDtypeStruct((B, N, 256), BF16),
        scratch_shapes=[pltpu.VMEM((N, HKV_A * 128), BF16), pltpu.VMEM((N, HKV_A * 128), BF16),
                        pltpu.VMEM((1, HKV_A * 128), F32)],
        compiler_params=_cparams(("arbitrary", "arbitrary")),
        name="window_attn",
    )(qa, ka, va, sink, sinkw, placeq, gsumq, placek, gsumk)


def _dattn_kernel(q_ref, k_ref, v_ref, lam_ref, sub_ref, place_ref, gsum_ref, o_ref,
                  vaug_ref, kaug_ref, kmax_ref, kch_ref, qch_ref, tch_ref, *, T, S, lam_init):
    j = pl.program_id(1)
    lv = lam_ref[...]
    lam = (jnp.exp(jnp.sum(lv[0:1, :] * lv[1:2, :], axis=-1, keepdims=True))
           - jnp.exp(jnp.sum(lv[2:3, :] * lv[3:4, :], axis=-1, keepdims=True)) + lam_init)
    subg = sub_ref[...] * (1.0 - lam_init)
    dv = 2 * DH_D
    nchain, aw = 2 * H_D, 2 * DH_D

    @pl.when(j == 0)
    def _():
        ones = jnp.where(lax.broadcasted_iota(I32, (T + S, dv), 1) == 0, 1.0, 0.0).astype(BF16)
        for hd in range(H_D):
            vaug_ref[hd] = jnp.concatenate([v_ref[0, :, hd * dv:(hd + 1) * dv], ones], axis=1)
        k_wide = _dot(k_ref[0], place_ref[...])
        lane = lax.broadcasted_iota(I32, (1, nchain * aw), 1)
        k_aug = (k_wide + jnp.where((lane & (aw - 1)) == DH_D, 1.0, 0.0)).astype(BF16)
        kaug_ref[...] = k_aug
        for i in range(nchain):
            kch_ref[i] = k_aug[:, i * aw:(i + 1) * aw]
        ksq = _dot((k_wide * k_wide).astype(BF16), gsum_ref[...])
        kmax_ref[...] = jnp.sqrt(jnp.max(ksq, axis=0, keepdims=True))

    def finish(hd, terms):
        o = terms[0] - lam * terms[1]
        ms = jnp.mean(o * o, axis=-1, keepdims=True)
        o_ref[0, :, hd * dv:(hd + 1) * dv] = (o * lax.rsqrt(ms + EPS) * subg).astype(BF16)

    def body_fast(k0, nk):
        q_wide = _dot(q_ref[0], place_ref[...])
        qn = jnp.sqrt(_dot((q_wide * q_wide).astype(BF16), gsum_ref[...]))
        q_aug = (q_wide - qn * kmax_ref[...]).astype(BF16)
        terms, low = [], None
        for i in range(nchain):
            hd, cs = i // 2, slice(i * aw, (i + 1) * aw)
            s = _dot_nt(q_aug[:, cs], kaug_ref[k0:k0 + nk, cs])
            pv = _dot(jnp.exp(s.astype(BF16)), vaug_ref[hd, k0:k0 + nk, :])
            sigma = pv[:, dv:dv + 1]
            low = sigma if low is None else jnp.minimum(low, sigma)
            terms.append(pv[:, 0:dv] / sigma)
            if i % 2 == 1:
                finish(hd, terms)
                terms = []
        return jnp.min(low)

    def body_exact(k0, nk):
        q_wide = _dot(q_ref[0], place_ref[...])
        for i in range(nchain):
            qch_ref[i] = q_wide[:, i * aw:(i + 1) * aw].astype(BF16)

        def chain(i, carry):
            s = _dot_nt(qch_ref[i], kch_ref[i, k0:k0 + nk, :])
            e = jnp.exp((s - jnp.max(s, axis=-1, keepdims=True)).astype(BF16))
            pv = _dot(e, vaug_ref[jnp.right_shift(i, 1), k0:k0 + nk, :])
            tch_ref[i] = pv / pv[:, dv:dv + 1]
            return carry

        lax.fori_loop(0, nchain, chain, 0)
        for hd in range(H_D):
            finish(hd, [tch_ref[2 * hd][:, 0:dv], tch_ref[2 * hd + 1][:, 0:dv]])

    def body(k0, nk):
        low = body_fast(k0, nk)

        @pl.when(jnp.logical_not(low > DATTN_MIN_ROWSUM))
        def _():
            body_exact(k0, nk)

    @pl.when(j < T // ROW_BLOCK)
    def _():
        body(0, T + S)

    @pl.when(j >= T // ROW_BLOCK)
    def _():
        body(T, S)


def _dattn_mats():
    nchain, aw = 2 * H_D, 2 * DH_D
    place = np.zeros((nchain * DH_D, nchain * aw), np.float32)
    gsum = np.zeros((nchain * aw, nchain * aw), np.float32)
    for i in range(nchain):
        for d in range(DH_D):
            place[i * DH_D + d, i * aw + d] = 1.0
            gsum[i * aw + d, i * aw + DH_D] = 1.0
    return jnp.asarray(place, BF16), jnp.asarray(gsum, BF16)


def _dattn(qd, kd, vd, lam_vecs, subg, T, S, lam_init):
    B, N, _ = qd.shape
    rb = ROW_BLOCK
    place, gsum = _dattn_mats()
    full = lambda a: pl.BlockSpec(a.shape, lambda b, j: (0,) * a.ndim)
    return pl.pallas_call(
        functools.partial(_dattn_kernel, T=T, S=S, lam_init=lam_init),
        grid=(B, N // rb),
        in_specs=[pl.BlockSpec((1, rb, 256), lambda b, j: (b, j, 0)),
                  pl.BlockSpec((1, N, 256), lambda b, j: (b, 0, 0)),
                  pl.BlockSpec((1, N, 256), lambda b, j: (b, 0, 0)),
                  pl.BlockSpec((4, DH_D), lambda b, j: (0, 0)),
                  pl.BlockSpec((1, 2 * DH_D), lambda b, j: (0, 0)), full(place), full(gsum)],
        out_specs=pl.BlockSpec((1, rb, 256), lambda b, j: (b, j, 0)),
        out_shape=jax.ShapeDtypeStruct((B, N, 256), BF16),
        scratch_shapes=[pltpu.VMEM((H_D, N, 4 * DH_D), BF16), pltpu.VMEM((N, 2 * 256), BF16),
                        pltpu.VMEM((1, 2 * 256), F32), pltpu.VMEM((2 * H_D, N, 2 * DH_D), BF16),
                        pltpu.VMEM((2 * H_D, rb, 2 * DH_D), BF16), pltpu.VMEM((2 * H_D, rb, 4 * DH_D), F32)],
        compiler_params=_cparams(("arbitrary", "arbitrary")),
        name="diff_attn",
    )(qd, kd, vd, lam_vecs, subg, place, gsum)


def _log_sigmoid(x):
    return jnp.minimum(x, 0.0) - jnp.log1p(jnp.exp(-jnp.abs(x)))


def _dot3(a, b, split_b):
    x = b if split_b else a
    hi = x.astype(BF16)
    r1 = x - hi.astype(F32)
    mid = r1.astype(BF16)
    lo = (r1 - mid.astype(F32)).astype(BF16)
    if split_b:
        a = a.astype(BF16)
        return _dot(a, hi) + _dot(a, mid) + _dot(a, lo)
    b = b.astype(BF16)
    return _dot(hi, b) + _dot(mid, b) + _dot(lo, b)


def _mlstm_kernel(qtf_ref, kf_ref, vtf_ref, gf_ref, gtf_ref, qtb_ref, kb_ref, vtb_ref, gb_ref, gtb_ref,
                  bias_ref, biast_ref, hf_ref, hb_ref, c_ref, m_ref):
    i = pl.program_id(1)
    lc = MLSTM_CHUNK

    @pl.when(i == 0)
    def _():
        c_ref[...] = jnp.zeros_like(c_ref)
        m_ref[...] = jnp.zeros_like(m_ref)

    r_io = lax.broadcasted_iota(I32, (lc, lc), 0)
    c_io = lax.broadcasted_iota(I32, (lc, lc), 1)
    lower = r_io >= c_io
    upper = r_io <= c_io
    tri_lo = jnp.where(lower, 1.0, 0.0).astype(F32)
    tri_up = jnp.where(upper, 1.0, 0.0).astype(F32)
    ones_rows = jnp.where(lax.broadcasted_iota(I32, (DH_C, lc), 0) == 0, 1.0, 0.0).astype(BF16)

    def direction(d, qt_ref, k_ref, vt_ref, g_ref, gt_ref, out_ref):
        pre = g_ref[0] + bias_ref[...]
        pre_t = gt_ref[0] + biast_ref[...]
        lf = _log_sigmoid(pre)
        lf_t = _log_sigmoid(pre_t)
        if d == 0:
            bcol = _dot3(tri_lo, lf, True)
            brow = _dot3(lf_t, tri_up, False)
            valid, end = upper, lc - 1
        else:
            bcol = _dot3(tri_up, lf, True)
            brow = _dot3(lf_t, tri_lo, False)
            valid, end = lower, 0
        for hd in range(H_C):
            ci, cf = d * H_C + hd, (2 + d) * H_C + hd
            rs = slice(hd * DH_C, (hd + 1) * DH_C)
            q_t = qt_ref[0, rs, :]
            k = k_ref[0, :, rs]
            v_aug = jnp.concatenate([vt_ref[0, rs, :], ones_rows], axis=0)
            b_row = brow[cf:cf + 1, :]
            ig_row = pre_t[ci:ci + 1, :]
            bi_col = bcol[:, cf:cf + 1] - pre[:, ci:ci + 1]
            b_end = brow[cf:cf + 1, end:end + 1]
            slot = d * H_C + hd
            c = c_ref[slot]
            m = m_ref[slot][:, 0:1]
            logw = jnp.where(valid, b_row - bi_col, NEG)
            inter = b_row + m
            m_t = jnp.maximum(inter, jnp.max(logw, axis=0, keepdims=True))
            w_in = jnp.exp(inter - m_t)
            p_t = (_dot(k, q_t) * jnp.exp(logw - m_t)).astype(BF16)
            num = w_in * _dot(c.astype(BF16), q_t) + _dot(v_aug, p_t)
            den = jnp.maximum(jnp.abs(num[DH_C:DH_C + 1, :]), jnp.exp(-m_t))
            out_ref[0, rs, :] = num[0:DH_C, :] / den
            g_end = b_end - b_row + ig_row
            m_new = jnp.maximum(b_end + m, jnp.max(g_end, axis=1, keepdims=True))
            wv = (v_aug.astype(F32) * jnp.exp(g_end - m_new)).astype(BF16)
            c_ref[slot] = jnp.exp(b_end + m - m_new) * c + _dot(wv, k)
            m_ref[slot] = jnp.broadcast_to(m_new, (1, 128))

    direction(0, qtf_ref, kf_ref, vtf_ref, gf_ref, gtf_ref, hf_ref)
    direction(1, qtb_ref, kb_ref, vtb_ref, gb_ref, gtb_ref, hb_ref)


def _mlstm(qct, kc, vct, gc, gct, bias, biast, T, S):
    B, N, _ = kc.shape
    lc = MLSTM_CHUNK
    nch, nlat = N // lc, T // lc
    fwd = lambda i: jnp.where(i < nch - nlat, nlat + i, i - (nch - nlat))
    bwd = lambda i: nch - 1 - i
    tok = lambda f: pl.BlockSpec((1, lc, 256), lambda b, i: (b, f(i), 0))
    tok_t = lambda f: pl.BlockSpec((1, 256, lc), lambda b, i: (b, 0, f(i)))
    gate = lambda f: pl.BlockSpec((1, lc, 128), lambda b, i: (b, f(i), 0))
    gate_t = lambda f: pl.BlockSpec((1, 16, lc), lambda b, i: (b, 0, f(i)))
    return pl.pallas_call(
        _mlstm_kernel,
        grid=(B, nch),
        in_specs=[tok_t(fwd), tok(fwd), tok_t(fwd), gate(fwd), gate_t(fwd),
                  tok_t(bwd), tok(bwd), tok_t(bwd), gate(bwd), gate_t(bwd),
                  pl.BlockSpec((1, 128), lambda b, i: (0, 0)),
                  pl.BlockSpec((16, 1), lambda b, i: (0, 0))],
        out_specs=[tok_t(fwd), tok_t(bwd)],
        out_shape=[jax.ShapeDtypeStruct((B, 256, N), F32)] * 2,
        scratch_shapes=[pltpu.VMEM((2 * H_C, 2 * DH_C, DH_C), F32), pltpu.VMEM((2 * H_C, 1, 128), F32)],
        compiler_params=_cparams(("arbitrary", "arbitrary")),
        name="mlstm",
    )(qct, kc, vct, gc, gct, qct, kc, vct, gc, gct, bias, biast)


def _shortconv_kernel(u_ref, w_ref, o_ref, *, T, S):
    u = u_ref[0].astype(F32)
    n = u.shape[0]
    w = w_ref[...]
    row = lax.broadcasted_iota(I32, u.shape, 0)
    zero = jnp.zeros((1, u.shape[1]), F32)
    prev = jnp.concatenate([zero, u[0:n - 1]], axis=0)
    nxt = jnp.concatenate([u[1:n], zero], axis=0)
    prev = jnp.where(row == T, 0.0, prev)
    nxt = jnp.where(row == T - 1, 0.0, nxt)
    o_ref[0] = (w[0:1, :] * prev + w[1:2, :] * u + w[2:3, :] * nxt).astype(BF16)


def _shortconv(ub, conv_w, T, S):
    B, N, W = ub.shape
    return pl.pallas_call(
        functools.partial(_shortconv_kernel, T=T, S=S),
        grid=(B, W // 256),
        in_specs=[pl.BlockSpec((1, N, 256), lambda b, j: (b, 0, j)),
                  pl.BlockSpec((3, 256), lambda b, j: (0, j))],
        out_specs=pl.BlockSpec((1, N, 256), lambda b, j: (b, 0, j)),
        out_shape=jax.ShapeDtypeStruct((B, N, W), BF16),
        compiler_params=_cparams(("arbitrary", "arbitrary")),
        name="hyena_shortconv",
    )(ub, conv_w)


def _hyfilter_kernel(z_ref, w1_ref, b1_ref, fr_ref, w2_ref, b2_ref, w3_ref, dec_ref,
                     hp_ref, hf_ref, ss_ref):
    i = pl.program_id(0)
    fr = fr_ref[...]
    h = jnp.sin(fr * (_dot(z_ref[...], w1_ref[...], precision=HI) + b1_ref[...]))
    h = jnp.sin(fr * (_dot(h, w2_ref[...], precision=HI) + b2_ref[...]))
    h = _dot(h, w3_ref[...], precision=HI)
    dec = dec_ref[...]
    dec2 = jnp.concatenate([dec, dec], axis=1)
    hw = HY_ORDER * HY_W
    h0 = h[:, 0:hw] * dec2
    h1 = h[:, hw:2 * hw] * dec2
    rows = h0.shape[0]
    t = i * rows + lax.broadcasted_iota(I32, h0.shape, 0)
    h1 = jnp.where(t == 0, 0.0, h1)
    hp_ref[...] = h0
    hf_ref[...] = h1

    @pl.when(i == 0)
    def _():
        ss_ref[...] = jnp.zeros_like(ss_ref)

    ss_ref[...] += jnp.sum(h0 * h0 + h1 * h1, axis=0, keepdims=True)


def _hyfilter(zfeat, w1, b1, fr, w2, b2, w3, decay):
    ls = zfeat.shape[0]
    rb = min(ls, 512)
    hw = HY_ORDER * HY_W
    full = lambda a: pl.BlockSpec(a.shape, lambda i: (0,) * a.ndim)
    return pl.pallas_call(
        _hyfilter_kernel,
        grid=(ls // rb,),
        in_specs=[pl.BlockSpec((rb, 128), lambda i: (i, 0)), full(w1), full(b1), full(fr), full(w2),
                  full(b2), full(w3), pl.BlockSpec((rb, HY_W), lambda i: (i, 0))],
        out_specs=[pl.BlockSpec((rb, hw), lambda i: (i, 0)), pl.BlockSpec((rb, hw), lambda i: (i, 0)),
                   pl.BlockSpec((1, hw), lambda i: (0, 0))],
        out_shape=[jax.ShapeDtypeStruct((ls, hw), F32), jax.ShapeDtypeStruct((ls, hw), F32),
                   jax.ShapeDtypeStruct((1, hw), F32)],
        compiler_params=_cparams(("arbitrary",)),
        name="hyena_filter",
    )(zfeat, w1, b1, fr, w2, b2, w3, decay)


def _cmul_const(v, c, s):
    vr, vi = v
    r = math.sqrt(0.5)
    if abs(s) < 1e-9:
        return (vr, vi) if c > 0 else (-vr, -vi)
    if abs(c) < 1e-9:
        return (-vi, vr) if s > 0 else (vi, -vr)
    if abs(abs(c) - r) < 1e-9 and abs(abs(s) - r) < 1e-9:
        a, b = (vr if c > 0 else -vr), (vi if s > 0 else -vi)
        p, q = (vi if c > 0 else -vi), (vr if s > 0 else -vr)
        return (a - b) * r, (p + q) * r
    return c * vr - s * vi, c * vi + s * vr


def _fft_pow2(xs, sign):
    n = len(xs)
    if n == 1:
        return xs
    ev, od = _fft_pow2(xs[0::2], sign), _fft_pow2(xs[1::2], sign)
    out = [None] * n
    for k in range(n // 2):
        ang = sign * 2.0 * math.pi * k / n
        tr, ti = _cmul_const(od[k], math.cos(ang), math.sin(ang))
        out[k] = (ev[k][0] + tr, ev[k][1] + ti)
        out[k + n // 2] = (ev[k][0] - tr, ev[k][1] - ti)
    return out


def _hy_stage1(src_ref, f_ref, a_ref, nseg):
    k1p = a_ref.shape[2]
    for n2 in range(HY_N2):
        if len(src_ref.shape) == 3:
            zs = jnp.concatenate([src_ref[t, pl.ds(n2, nseg, stride=HY_N2), :] for t in range(src_ref.shape[0])],
                                 axis=1).astype(BF16)
        else:
            zs = src_ref[pl.ds(n2, nseg, stride=HY_N2), :].astype(BF16)
        r = _dot(f_ref[n2], zs)
        a_ref[0, n2] = r[0:k1p]
        a_ref[1, n2] = r[k1p:2 * k1p]


def _hy_chunks(a_ref):
    k1p, lanes = a_ref.shape[2], a_ref.shape[3]
    return k1p // 8, [slice(t * 128, (t + 1) * 128) for t in range(lanes // 128)]


def _hyspec_kernel(hp_ref, hf_ref, ss_ref, f_ref, h_ref, ap_ref, af_ref, *, ls):
    nseg = ls // HY_N2
    _hy_stage1(hp_ref, f_ref, ap_ref, nseg)
    _hy_stage1(hf_ref, f_ref, af_ref, nseg)
    nchunk, lane_tiles = _hy_chunks(ap_ref)

    def body(i, _):
        rows = pl.ds(pl.multiple_of(i * 8, 8), 8)
        k1 = i * 8 + lax.broadcasted_iota(I32, (8, 1), 0)
        wk = jnp.where((k1 == 0) | (k1 == nseg), 1.0, 2.0) * (1.0 / (2 * ls))
        for ln in lane_tiles:
            scale = lax.rsqrt(ss_ref[:, ln] + EPS) * wk
            P = _fft_pow2([(ap_ref[0, n2, rows, ln], ap_ref[1, n2, rows, ln]) for n2 in range(HY_N2)], -1)
            Q = _fft_pow2([(af_ref[0, n2, rows, ln], af_ref[1, n2, rows, ln]) for n2 in range(HY_N2)], -1)
            for k2 in range(HY_N2):
                h_ref[0, 0, k2, rows, ln] = (P[k2][0] + Q[k2][0]) * scale
                h_ref[0, 1, k2, rows, ln] = (P[k2][1] - Q[k2][1]) * scale
        return 0

    lax.fori_loop(0, nchunk, body, 0)


def _hyspec(hp, hf, ss, fmat):
    ls = hp.shape[0]
    k1p = fmat.shape[1] // 2
    full = lambda a: pl.BlockSpec(a.shape, lambda o, c: (0,) * a.ndim)
    nc = HY_W // 128
    col = lambda o, c: (0, o * nc + c)
    return pl.pallas_call(
        functools.partial(_hyspec_kernel, ls=ls),
        grid=(HY_ORDER, nc),
        in_specs=[pl.BlockSpec((ls, 128), col), pl.BlockSpec((ls, 128), col), pl.BlockSpec((1, 128), col),
                  full(fmat)],
        out_specs=pl.BlockSpec((1, 2, HY_N2, k1p, 128), lambda o, c: (o, 0, 0, 0, c)),
        out_shape=jax.ShapeDtypeStruct((HY_ORDER, 2, HY_N2, k1p, HY_W), F32),
        scratch_shapes=[pltpu.VMEM((2, HY_N2, k1p, 128), F32)] * 2,
        compiler_params=_cparams(("arbitrary", "arbitrary")),
        name="hyena_spectrum",
    )(hp, hf, ss, fmat)


def _hyconv_kernel(z_ref, gate_ref, h_ref, f_ref, g_ref, bias_ref, o_ref, zf_ref, yf_ref, a_ref, *, ls):
    nseg = ls // HY_N2
    ntile = zf_ref.shape[0]
    for t in range(ntile):
        zf_ref[t] = z_ref[0, :, t * 128:(t + 1) * 128].astype(F32)
    _hy_stage1(zf_ref, f_ref, a_ref, nseg)
    nchunk, lane_tiles = _hy_chunks(a_ref)

    def body(i, _):
        rows = pl.ds(pl.multiple_of(i * 8, 8), 8)
        for ln in lane_tiles:
            X = _fft_pow2([(a_ref[0, n2, rows, ln], a_ref[1, n2, rows, ln]) for n2 in range(HY_N2)], -1)
            Y = []
            for k2 in range(HY_N2):
                hr, hi = h_ref[0, 0, k2, rows, ln], h_ref[0, 1, k2, rows, ln]
                xr, xi = X[k2]
                Y.append((xr * hr - xi * hi, xr * hi + xi * hr))
            Bv = _fft_pow2(Y, 1)
            for n2 in range(HY_N2):
                a_ref[0, n2, rows, ln] = Bv[n2][0]
                a_ref[1, n2, rows, ln] = Bv[n2][1]
        return 0

    lax.fori_loop(0, nchunk, body, 0)
    for n2 in range(HY_N2):
        bb = jnp.concatenate([a_ref[0, n2], a_ref[1, n2]], axis=0).astype(BF16)
        yv = _dot(g_ref[n2], bb)
        for t in range(ntile):
            yf_ref[t, pl.ds(n2, nseg, stride=HY_N2), :] = yv[:, t * 128:(t + 1) * 128]
    for t in range(ntile):
        ln = slice(t * 128, (t + 1) * 128)
        o_ref[0, :, ln] = (gate_ref[0, :, ln].astype(F32)
                           * (yf_ref[t] + zf_ref[t] * bias_ref[:, ln])).astype(BF16)


def _hyconv(zsrc, zcol, zrow, gsrc, gcol, grow, hspec, order, fmat, gmat, bias, ls):
    B = zsrc.shape[0]
    k1p = fmat.shape[1] // 2
    once = lambda a: pl.BlockSpec(a.shape, lambda b: (0,) * a.ndim, pipeline_mode=pl.Buffered(1))
    return pl.pallas_call(
        functools.partial(_hyconv_kernel, ls=ls),
        grid=(B,),
        in_specs=[pl.BlockSpec((1, ls, HY_W), lambda b: (b, zrow, zcol)),
                  pl.BlockSpec((1, ls, HY_W), lambda b: (b, grow, gcol)),
                  pl.BlockSpec((1, 2, HY_N2, k1p, HY_W), lambda b: (order, 0, 0, 0, 0),
                               pipeline_mode=pl.Buffered(1)),
                  once(fmat), once(gmat), pl.BlockSpec((1, HY_W), lambda b: (0, 0))],
        out_specs=pl.BlockSpec((1, ls, HY_W), lambda b: (b, 0, 0)),
        out_shape=jax.ShapeDtypeStruct((B, ls, HY_W), BF16),
        scratch_shapes=[pltpu.VMEM((HY_W // 128, ls, 128), F32), pltpu.VMEM((HY_W // 128, ls, 128), F32),
                        pltpu.VMEM((2, HY_N2, k1p, HY_W), F32)],
        compiler_params=_cparams(("arbitrary",)),
        name="hyena_longconv",
    )(zsrc, gsrc, hspec, fmat, gmat, bias)


def _outproj_kernel(x_ref, mod_ref, ya_ref, ybl_ref, ybc_ref, hf_ref, hb_ref, oc_ref, yd_ref, mlg_ref, g64_ref,
                    w_ref, g2_ref, wr_ref, xo_ref, h2_ref, aff_ref, *, nlat):
    hsum = (hf_ref[0] + hb_ref[0]).T
    msq = _dot((hsum * hsum).astype(BF16), g64_ref[...])
    yc = jax.nn.sigmoid(oc_ref[0].astype(F32)) * (hsum * lax.rsqrt(msq + EPS) * mlg_ref[...])
    yb = jnp.where(pl.program_id(1) < nlat, ybl_ref[0], ybc_ref[0])
    y = jnp.concatenate([ya_ref[0], yb, yc.astype(BF16), yd_ref[0]], axis=1)
    mod = mod_ref[0, 0]
    x = x_ref[0] + mod[2:3, :] * _dot(y, w_ref[...])
    xo_ref[0] = x
    ms = jnp.mean(x * x, axis=-1, keepdims=True)
    h2 = x * lax.rsqrt(ms + EPS) * g2_ref[...] * (1.0 + mod[4:5, :]) + mod[3:4, :]
    h2_hi = h2.astype(BF16)
    h2_ref[0] = _pack_bf16_pair(h2_hi[:, 0:D_MODEL // 2], h2_hi[:, D_MODEL // 2:])
    h2_lo = (h2 - h2_hi.astype(F32)).astype(BF16)
    lg = _dot(h2_hi, wr_ref[...])
    logits = lg[:, 0:128] + lg[:, 128:256] + _dot(h2_lo, wr_ref[:, 0:128])
    lane = lax.broadcasted_iota(I32, logits.shape, 1)
    logits = jnp.where(lane < N_EXPERTS, logits, NEG)
    e = jnp.exp(logits - jnp.max(logits, axis=-1, keepdims=True))
    aff_ref[0] = (e / jnp.sum(e, axis=-1, keepdims=True)).T


def _outproj(x, mod, ya, yb_l, yb_c, hf, hb, oc, yd, mlg, g64, w_out, g2, w_router, T):
    B, N, D = x.shape
    rb = ROW_BLOCK
    nlat = T // rb
    row = lambda w: pl.BlockSpec((1, rb, w), lambda b, r: (b, r, 0))
    full = lambda a: pl.BlockSpec(a.shape, lambda b, r: (0,) * a.ndim)
    return pl.pallas_call(
        functools.partial(_outproj_kernel, nlat=nlat),
        grid=(B, N // rb),
        in_specs=[row(D), pl.BlockSpec((1, 1, 6, D), lambda b, r: (b, jnp.where(r >= nlat, 1, 0), 0, 0)),
                  row(256),
                  pl.BlockSpec((1, rb, 256), lambda b, r: (b, jnp.minimum(r, nlat - 1), 0)),
                  pl.BlockSpec((1, rb, 256), lambda b, r: (b, jnp.maximum(r - nlat, 0), 0)),
                  pl.BlockSpec((1, 256, rb), lambda b, r: (b, 0, r)),
                  pl.BlockSpec((1, 256, rb), lambda b, r: (b, 0, r)), row(256), row(256),
                  full(mlg), full(g64), full(w_out), full(g2), full(w_router)],
        out_specs=[row(D), row(D // 2), pl.BlockSpec((1, 128, rb), lambda b, r: (b, 0, r))],
        out_shape=[jax.ShapeDtypeStruct((B, N, D), F32), jax.ShapeDtypeStruct((B, N, D // 2), I32),
                   jax.ShapeDtypeStruct((B, 128, N), F32)],
        compiler_params=_cparams(("arbitrary", "arbitrary")),
        name="outproj",
    )(x, mod, ya, yb_l, yb_c, hf, hb, oc, yd, mlg, g64, w_out, g2, w_router)


def _prefix_exclusive(x):
    n = x.shape[1]
    lane = lax.broadcasted_iota(I32, x.shape, 1)
    inc = x
    d = 1
    while d < n:
        inc = inc + jnp.where(lane >= d, pltpu.roll(inc, d, axis=1), 0)
        d *= 2
    return inc - x


def _topk_kernel(aff_ref, o_ref, pos_ref, val_ref, *, T, S):
    def segment(t0, ts, cap, slot0):
        a = aff_ref[0, :, t0:t0 + ts]
        bits = lax.bitcast_convert_type(a, I32)

        def search(i, v):
            cand = v | jnp.left_shift(jnp.int32(1), 30 - i)
            cnt = jnp.sum((bits >= cand).astype(I32), axis=1, keepdims=True)
            return jnp.where(cnt >= cap, cand, v)

        thr = lax.fori_loop(0, 31, search, jnp.zeros((N_EXPERTS, 1), I32))
        gt = bits > thr
        eq = bits == thr
        need = cap - jnp.sum(gt.astype(I32), axis=1, keepdims=True)
        sel = gt | (eq & (_prefix_exclusive(eq.astype(I32)) < need))
        seli = sel.astype(I32)
        pos_ref[:, 0:ts] = jnp.where(sel, _prefix_exclusive(seli), -1)
        a_hi = a.astype(BF16)
        r1 = a - a_hi.astype(F32)
        a_mid = r1.astype(BF16)
        a_lo = (r1 - a_mid.astype(F32)).astype(BF16)
        val_ref[0, :, 0:ts] = a_hi.astype(F32)
        val_ref[1, :, 0:ts] = a_mid.astype(F32)
        val_ref[2, :, 0:ts] = a_lo.astype(F32)
        tabs = lax.broadcasted_iota(I32, (1, ts), 1) + t0
        t_hi = jnp.right_shift(tabs, 6).astype(F32)
        t_lo = (tabs & 63).astype(F32)
        slot = lax.broadcasted_iota(I32, (cap, ts), 0)
        zeros = jnp.zeros((11, ts), F32)

        def per_expert(e, _):
            onehot = jnp.where(pos_ref[pl.ds(e, 1), 0:ts] == slot, 1.0, 0.0).astype(BF16)
            vals = jnp.concatenate([t_hi, t_lo, val_ref[0, pl.ds(e, 1), 0:ts], val_ref[1, pl.ds(e, 1), 0:ts],
                                    val_ref[2, pl.ds(e, 1), 0:ts], zeros], axis=0).astype(BF16)
            o_ref[0, e, slot0:slot0 + cap, :] = _dot_nt(onehot, vals)
            return 0

        lax.fori_loop(0, N_EXPERTS, per_expert, 0)

    segment(0, T, (EC_CAPACITY * T) // N_EXPERTS, 0)
    segment(T, S, (EC_CAPACITY * S) // N_EXPERTS, (EC_CAPACITY * T) // N_EXPERTS)


def _topk(aff_t, T, S):
    B, _, N = aff_t.shape
    E = N_EXPERTS
    cap_t = (EC_CAPACITY * T) // E + (EC_CAPACITY * S) // E
    return pl.pallas_call(
        functools.partial(_topk_kernel, T=T, S=S),
        grid=(B,),
        in_specs=[pl.BlockSpec((1, E, N), lambda b: (b, 0, 0))],
        out_specs=pl.BlockSpec((1, E, cap_t, 16), lambda b: (b, 0, 0, 0)),
        out_shape=jax.ShapeDtypeStruct((B, E, cap_t, 16), F32),
        scratch_shapes=[pltpu.VMEM((E, T), I32), pltpu.VMEM((3, E, T), F32)],
        compiler_params=_cparams(("arbitrary",)),
        name="expert_topk",
    )(aff_t)


def _gather_kernel(idx_ref, h_ref, *rest, cap_t):
    o_ref = rest[-1]
    b, e = pl.program_id(0), pl.program_id(1)
    base = (b * N_EXPERTS + e) * cap_t

    def body(g, _):
        rows = [h_ref[0, pl.ds(idx_ref[base + g * 16 + k], 1), :] for k in range(16)]
        o_ref[0, 0, pl.ds(pl.multiple_of(g * 16, 16), 16), :] = jnp.concatenate(rows, axis=0)
        return 0

    lax.fori_loop(0, cap_t // 16, body, 0)


def _gather(idx_flat, h2p, xe_buf, b0, bg, cap_t):
    B, N, hw = h2p.shape
    in_specs = [pl.BlockSpec((1, N, hw), lambda b, e, idx: (b0 + b, 0, 0))]
    args = [idx_flat, h2p]
    aliases = {}
    if xe_buf is not None:
        in_specs.append(pl.BlockSpec(memory_space=pl.ANY))
        args.append(xe_buf)
        aliases = {2: 0}
    return pl.pallas_call(
        functools.partial(_gather_kernel, cap_t=cap_t),
        grid_spec=pltpu.PrefetchScalarGridSpec(
            num_scalar_prefetch=1,
            grid=(bg, N_EXPERTS),
            in_specs=in_specs,
            out_specs=pl.BlockSpec((1, 1, cap_t, hw), lambda b, e, idx: (b0 + b, e, 0, 0)),
        ),
        out_shape=jax.ShapeDtypeStruct((B, N_EXPERTS, cap_t, hw), I32),
        input_output_aliases=aliases,
        compiler_params=_cparams(("arbitrary", "arbitrary")),
        name="expert_gather",
    )(*args)


def _ffn_kernel(x_ref, wg_ref, wu_ref, wd_ref, gate_ref, gt_ref, o_ref, wg_s, wu_s, wd_s, *, cap_l):
    @pl.when(pl.program_id(1) == 0)
    def _():
        wg_s[...] = wg_ref[0, 0].astype(BF16)
        wu_s[...] = wu_ref[0, 0].astype(BF16)
        wd_s[...] = wd_ref[0, 0].astype(BF16)

    x = jnp.concatenate(_unpack_bf16_pair(x_ref[0, 0]), axis=1)
    a = _dot(x, wg_s[...])
    u = _dot(x, wu_s[...])
    hmid = (a * jax.nn.sigmoid(a) * u).astype(BF16)
    y = _dot(hmid, wd_s[...])
    gt = gt_ref[0]
    row = lax.broadcasted_iota(I32, y.shape, 0)
    gt2 = jnp.where(row < cap_l, gt[0:1, :], gt[1:2, :])
    o_ref[0, 0] = y * gate_ref[0, 0] * gt2


def _ffn(xe, layer, wg, wu, wd, gate, gt2, cap_l):
    B, E, cap_t, hw = xe.shape
    D = 2 * hw
    wspec = pl.BlockSpec((1, 1, D, D), lambda e, b: (layer, e, 0, 0))
    return pl.pallas_call(
        functools.partial(_ffn_kernel, cap_l=cap_l),
        grid=(E, B),
        in_specs=[pl.BlockSpec((1, 1, cap_t, hw), lambda e, b: (b, e, 0, 0)), wspec, wspec, wspec,
                  pl.BlockSpec((1, 1, cap_t, 1), lambda e, b: (b, e, 0, 0)),
                  pl.BlockSpec((1, 2, D), lambda e, b: (b, 0, 0))],
        out_specs=pl.BlockSpec((1, 1, cap_t, D), lambda e, b: (b, e, 0, 0)),
        out_shape=jax.ShapeDtypeStruct((B, E, cap_t, D), F32),
        scratch_shapes=[pltpu.VMEM((D, D), BF16)] * 3,
        compiler_params=_cparams(("arbitrary", "arbitrary")),
        name="expert_ffn",
    )(xe, wg, wu, wd, gate, gt2)


def _scatter_kernel(idx_ref, y_ref, *rest, cap_t):
    o_ref = rest[-1]
    b, e = pl.program_id(0), pl.program_id(1)
    base = (b * N_EXPERTS + e) * cap_t

    @pl.when(e == 0)
    def _():
        o_ref[...] = jnp.zeros_like(o_ref)

    def body(g, _):
        tile = y_ref[0, 0, pl.ds(pl.multiple_of(g * 8, 8), 8), :]
        rows = [idx_ref[base + g * 8 + k] for k in range(8)]
        cur = [o_ref[0, pl.ds(r, 1), :] for r in rows]
        for k, r in enumerate(rows):
            o_ref[0, pl.ds(r, 1), :] = cur[k] + tile[k:k + 1, :]
        return 0

    lax.fori_loop(0, cap_t // 8, body, 0)


def _scatter(idx_flat, ye, acc_buf, n_rows, b0, bg, cap_t):
    B, E, _, D = ye.shape
    in_specs = [pl.BlockSpec((1, 1, cap_t, D), lambda b, e, idx: (b0 + b, e, 0, 0))]
    args = [idx_flat, ye]
    aliases = {}
    if acc_buf is not None:
        in_specs.append(pl.BlockSpec(memory_space=pl.ANY))
        args.append(acc_buf)
        aliases = {2: 0}
    return pl.pallas_call(
        functools.partial(_scatter_kernel, cap_t=cap_t),
        grid_spec=pltpu.PrefetchScalarGridSpec(
            num_scalar_prefetch=1,
            grid=(bg, N_EXPERTS),
            in_specs=in_specs,
            out_specs=pl.BlockSpec((1, n_rows, D), lambda b, e, idx: (b0 + b, 0, 0)),
        ),
        out_shape=jax.ShapeDtypeStruct((B, n_rows, D), F32),
        input_output_aliases=aliases,
        compiler_params=_cparams(("arbitrary", "arbitrary")),
        name="expert_scatter",
    )(*args)


def _residual_out_kernel(x_ref, f_ref, o_ref):
    o_ref[...] = x_ref[...] + f_ref[...]


def _residual_out(x, f, T):
    B, N, D = x.shape
    rb = ROW_BLOCK
    spec = pl.BlockSpec((1, rb, D), lambda b, r: (b, r, 0))
    return pl.pallas_call(
        _residual_out_kernel,
        grid=(B, T // rb),
        in_specs=[spec, spec],
        out_specs=spec,
        out_shape=jax.ShapeDtypeStruct((B, T, D), F32),
        compiler_params=_cparams(("arbitrary", "arbitrary")),
        name="residual_out",
    )(x, f)


def _rope_tables(T, S, dh):
    rows = T // GRID_W
    r = jnp.broadcast_to(jnp.arange(rows, dtype=F32)[:, None], (rows, GRID_W)).reshape(T)
    col = jnp.broadcast_to(jnp.arange(GRID_W, dtype=F32)[None, :], (rows, GRID_W)).reshape(T)
    nf = dh // 4
    inv = ROPE_BASE ** (-jnp.arange(nf, dtype=F32) / nf)
    ar, ac = r[:, None] * inv, col[:, None] * inv
    ang = jnp.concatenate([ar, ar, ac, ac], axis=1)
    ang = jnp.concatenate([ang, jnp.zeros((S, dh), F32)], axis=0)
    reps = 256 // dh
    return jnp.tile(jnp.cos(ang), (1, reps)), jnp.tile(jnp.sin(ang), (1, reps))


def _group_mats(dh):
    i = np.arange(256)
    gmat = (i[:, None] // dh == i[None, :] // dh).astype(np.float32) / dh
    nf = dh // 4
    half = (i % (2 * nf)) // nf
    pmat = np.zeros((256, 256), np.float32)
    a_idx = i[half == 0]
    pmat[a_idx + nf, a_idx] = -1.0
    pmat[a_idx, a_idx + nf] = 1.0
    return jnp.asarray(gmat, BF16), jnp.asarray(pmat, BF16)


def _hyena_tables(ls):
    t = jnp.linspace(0.0, 1.0, ls, dtype=F32)[:, None]
    w = 2.0 * math.pi * jnp.arange(ls, dtype=F32)[:, None] / ls
    bands = jnp.linspace(1e-4, HY_BANDS - 1, HY_BANDS, dtype=F32)
    z = jnp.concatenate([t, jnp.cos(bands * w), -jnp.sin(bands * w)], axis=-1)
    z = jnp.pad(z, ((0, 0), (0, 128 - HY_EMB)))
    deltas = jnp.abs(jnp.linspace(math.log(HY_TARGET) / HY_SLOW, math.log(HY_TARGET) / HY_FAST, HY_W, dtype=F32))
    decay = jnp.exp(-t * deltas)
    n, nseg = 2 * ls, ls // HY_N2
    k1p = -(-(nseg + 1) // 8) * 8
    k1 = jnp.arange(k1p, dtype=I32)[None, :, None]
    tpos = HY_N2 * jnp.arange(nseg, dtype=I32)[None, None, :] + jnp.arange(HY_N2, dtype=I32)[:, None, None]
    ang = ((k1 * tpos) % n).astype(F32) * (2.0 * math.pi / n)
    keep = k1 <= nseg
    fmat = jnp.concatenate([jnp.where(keep, jnp.cos(ang), 0.0), jnp.where(keep, -jnp.sin(ang), 0.0)],
                           axis=1).astype(BF16)
    return z, decay, fmat, jnp.swapaxes(fmat, 1, 2)


def kernel(x, c, ctx, c_ctx, w_ada, b_ada, norm1_g, norm2_g, w_in, b_gate, a_qnorm, a_knorm, a_sink, hy_conv, hy_fw1, hy_fb1, hy_freq, hy_fw2, hy_fb2, hy_fw3, hy_bias, ml_norm, d_qnorm, d_knorm, d_lq1, d_lk1, d_lq2, d_lk2, d_subnorm, w_out, w_router, w_e_gate, w_e_up, w_e_down):
    B, T, D = x.shape
    S = ctx.shape[1]
    N = T + S
    depth = w_ada.shape[0]
    assert D == D_MODEL and T % S == 0 and S % ROW_BLOCK == 0 and S % MLSTM_CHUNK == 0
    assert S % WATTN_QROWS == 0 and T >= WATTN_QROWS + 2 * BLK and WATTN_QROWS & (WATTN_QROWS - 1) == 0
    cap_l, cap_c = (EC_CAPACITY * T) // N_EXPERTS, (EC_CAPACITY * S) // N_EXPERTS
    cap_t = cap_l + cap_c
    assert cap_t % 16 == 0

    xs = jnp.concatenate([x, ctx], axis=1)

    rpad = -(B + 1) % 8
    cc = jnp.concatenate([c, c_ctx[None, :], jnp.zeros((rpad, D), F32)], axis=0)
    mods = _ada_mods(cc, w_ada, b_ada)

    cos_a, sin_a = _rope_tables(T, S, DH_A)
    cos_d, sin_d = _rope_tables(T, S, DH_D)
    g64, p64 = _group_mats(DH_A)
    g32, p32 = _group_mats(DH_D)
    tabs_l = _hyena_tables(T)
    tabs_c = _hyena_tables(S)

    offs = np.cumsum((0, 256, 128, 128, 768, 256, 256, 256, 256, 16, 256, 256, 256))
    ffn_acc = None
    for l in range(depth):
        lam_init = 0.8 - 0.6 * math.exp(-0.3 * l)
        ml = mods[l]
        mod = jnp.stack([ml[:B].reshape(B, 6, D), jnp.broadcast_to(ml[B].reshape(1, 6, D), (B, 6, D))],
                        axis=1)
        w = w_in[l]
        wp = jnp.concatenate([w[:, offs[0]:offs[8]], w[:, offs[8]:offs[9]], jnp.zeros((D, 112), F32),
                              w[:, offs[9]:offs[12]]], axis=1).astype(BF16)
        gains = jnp.stack([jnp.tile(a_qnorm[l], 4), jnp.tile(a_knorm[l], 4),
                           jnp.tile(d_qnorm[l], 8), jnp.tile(d_knorm[l], 8)], axis=0)
        proj = _inproj(xs, ffn_acc, mod, norm1_g[l].reshape(1, D), wp, (cos_a, sin_a, cos_d, sin_d), gains,
                       (g64, g32, p64, p32), T)
        qa, ka, va, ub, kc, oc, gc, qd, kd, vd, gct, qct, vct = proj[0:13]
        if ffn_acc is not None:
            xs = proj[13]

        ya = _wattn(qa, ka, va, jnp.pad(a_sink[l], (0, 128 - H_A)).reshape(1, 128), T, S)
        yd = _dattn(qd, kd, vd, jnp.stack([d_lq1[l], d_lk1[l], d_lq2[l], d_lk2[l]], axis=0),
                    d_subnorm[l].reshape(1, 2 * DH_D), T, S, lam_init)

        bias = b_gate[l].reshape(-1)
        hf, hb = _mlstm(qct, kc, vct, gc, gct, jnp.pad(bias, (0, 112)).reshape(1, 128), bias.reshape(16, 1), T, S)

        uc = _shortconv(ub, hy_conv[l], T, S)
        w1 = jnp.pad(hy_fw1[l], ((0, 128 - HY_EMB), (0, 0)))
        ybs = []
        for (zf, decay, fmat, gmat), ls, rowblk in ((tabs_l, T, 0), (tabs_c, S, T // S)):
            hpast, hfut, ss = _hyfilter(zf, w1, hy_fb1[l].reshape(1, -1), hy_freq[l].reshape(1, -1), hy_fw2[l],
                                        hy_fb2[l].reshape(1, -1), hy_fw3[l], decay)
            hspec = _hyspec(hpast, hfut, ss, fmat)
            z1 = _hyconv(uc, 0, rowblk, uc, 1, rowblk, hspec, 0, fmat, gmat, hy_bias[l, 0].reshape(1, HY_W), ls)
            z2 = _hyconv(z1, 0, 0, uc, 2, rowblk, hspec, 1, fmat, gmat, hy_bias[l, 1].reshape(1, HY_W), ls)
            ybs.append(z2)

        wr = jnp.pad(w_router[l], ((0, 0), (0, 128 - N_EXPERTS)))
        wr_hi = wr.astype(BF16)
        wr_cat = jnp.concatenate([wr_hi, (wr - wr_hi.astype(F32)).astype(BF16)], axis=1)
        xs, h2, aff = _outproj(xs, mod, ya, ybs[0], ybs[1], hf, hb, oc, yd, jnp.tile(ml_norm[l], 4).reshape(1, 256), g64,
                               w_out[l].astype(BF16), norm2_g[l].reshape(1, D), wr_cat, T)

        sel = _topk(aff, T, S)
        idx = (sel[..., 0] * 64.0 + sel[..., 1]).astype(I32)
        gate = (sel[..., 2] + sel[..., 3] + sel[..., 4])[..., None]
        gt2 = mod[:, :, 5, :]
        bg = min(IDX_BATCH_GROUP, B)
        groups = [(b0, idx[b0:b0 + bg].reshape(-1)) for b0 in range(0, B, bg)]
        xe = None
        for b0, idx_flat in groups:
            xe = _gather(idx_flat, h2, xe, b0, bg, cap_t)
        ye = _ffn(xe, l, w_e_gate, w_e_up, w_e_down, gate, gt2, cap_l)
        ffn_acc = None
        for b0, idx_flat in groups:
            ffn_acc = _scatter(idx_flat, ye, ffn_acc, N, b0, bg, cap_t)
    return _residual_out(xs, ffn_acc, T)
```

```python
import functools
import math

import jax
import jax.numpy as jnp
import numpy as np
from jax import lax
from jax.experimental import pallas as pl
from jax.experimental.pallas import tpu as pltpu

F32 = jnp.float32
BF16 = jnp.bfloat16
I32 = jnp.int32
HI = lax.Precision.HIGHEST

D_MODEL = 1024
GRID_W = 64
GROUP_W = D_MODEL // 4
H_A, HKV_A = 4, 2
G_A = H_A // HKV_A
DH_A = GROUP_W // H_A
WINDOW = 128
BLK = 128
HY_W = GROUP_W
HY_ORDER = 2
HY_BANDS = 16
HY_EMB = 1 + 2 * HY_BANDS
HY_HID = 64
HY_TARGET, HY_FAST, HY_SLOW = 1e-2, 0.3, 1.5
HY_N2 = 16
H_C = 4
DH_C = GROUP_W // H_C
H_D = 4
DH_D = GROUP_W // (2 * H_D)
N_EXPERTS = 16
EC_CAPACITY = 2
ROPE_BASE = 10000.0
EPS = 1e-6
NEG = -1e30

ROW_BLOCK = 256
WATTN_QROWS = 256
DATTN_KEY_TILE = 256
DATTN_MIN_ROWSUM = 1e-25
MLSTM_CHUNK = 256
IDX_BATCH_GROUP = 4
VMEM_LIMIT = 56 * 1024 * 1024

_C_QA, _C_KA, _C_VA, _C_UB = 0, 256, 384, 512
_C_QC, _C_KC, _C_VC, _C_OC, _C_GC = 1280, 1536, 1792, 2048, 2304
_C_QD, _C_KD, _C_VD, _C_END = 2432, 2688, 2944, 3200


def _cparams(sem, vmem=VMEM_LIMIT):
    return pltpu.CompilerParams(dimension_semantics=sem, vmem_limit_bytes=vmem)


def _dot(a, b, **kw):
    return jnp.dot(a, b, preferred_element_type=F32, **kw)


def _dot_nt(a, b, **kw):
    return lax.dot_general(a, b, (((1,), (1,)), ((), ())), preferred_element_type=F32, **kw)


def _dot_tn(a, b, **kw):
    return lax.dot_general(a, b, (((0,), (0,)), ((), ())), preferred_element_type=F32, **kw)


def _pack_bf16_pair(a, b):
    ua = lax.bitcast_convert_type(a.astype(F32), I32)
    ub = lax.bitcast_convert_type(b.astype(F32), I32)
    return ua | lax.shift_right_logical(ub, 16)


def _unpack_bf16_pair(w):
    a = lax.bitcast_convert_type(w & jnp.int32(-65536), F32)
    b = lax.bitcast_convert_type(lax.shift_left(w, 16), F32)
    return a.astype(BF16), b.astype(BF16)


def _ada_kernel(c_ref, w_ref, b_ref, o_ref):
    c = c_ref[...]
    s = c * jax.nn.sigmoid(c)
    o_ref[0] = _dot(s, w_ref[0], precision=HI) + b_ref[0]


def _ada_mods(cc, w_ada, b_ada):
    L, D, W6 = w_ada.shape
    R = cc.shape[0]
    cb = 1536
    return pl.pallas_call(
        _ada_kernel,
        grid=(L, W6 // cb),
        in_specs=[pl.BlockSpec((R, D), lambda l, j: (0, 0)),
                  pl.BlockSpec((1, D, cb), lambda l, j: (l, 0, j)),
                  pl.BlockSpec((1, 1, cb), lambda l, j: (l, 0, j))],
        out_specs=pl.BlockSpec((1, R, cb), lambda l, j: (l, 0, j)),
        out_shape=jax.ShapeDtypeStruct((L, R, W6), F32),
        compiler_params=_cparams(("arbitrary", "arbitrary")),
        name="ada_mods",
    )(cc, w_ada, b_ada.reshape(L, 1, W6))


def _inproj_kernel(*refs, has_f):
    if has_f:
        x_ref, f_ref, *refs = refs
        *refs, xsum = refs
        x = x_ref[0] + f_ref[0]
        xsum[0] = x
    else:
        x_ref, *refs = refs
        x = x_ref[0]
    (mod_ref, g1_ref, w_ref, ca_ref, sa_ref, cd_ref, sd_ref, gains_ref, g64_ref, g32_ref, p64_ref, p32_ref,
     qa, ka, va, ub, kc, oc, gc, qd, kd, vd, gct, qct, vct) = refs
    ms = jnp.mean(x * x, axis=-1, keepdims=True)
    xn = x * lax.rsqrt(ms + EPS) * g1_ref[...]
    mod = mod_ref[0, 0]
    h = xn * (1.0 + mod[1:2, :]) + mod[0:1, :]
    p = _dot(h.astype(BF16), w_ref[...])

    def headnorm_rope(t, gmat, gain, pmat, cos, sin, scale):
        w = t.shape[1]
        msq = _dot((t * t).astype(BF16), gmat[0:w, 0:w])
        tn = t * lax.rsqrt(msq + EPS) * gain
        tn = tn * cos[:, 0:w] + _dot(tn.astype(BF16), pmat[0:w, 0:w]) * sin[:, 0:w]
        return tn * scale

    ca, sa, cd, sd = ca_ref[...], sa_ref[...], cd_ref[...], sd_ref[...]
    gains = gains_ref[...]
    qa[0] = headnorm_rope(p[:, _C_QA:_C_KA], g64_ref, gains[0:1, :], p64_ref, ca, sa, DH_A ** -0.5).astype(BF16)
    ka[0] = headnorm_rope(p[:, _C_KA:_C_VA], g64_ref, gains[1:2, 0:128], p64_ref, ca, sa, 1.0).astype(BF16)
    va[0] = p[:, _C_VA:_C_UB].astype(BF16)
    ub[0] = p[:, _C_UB:_C_QC].astype(BF16)
    qct[0] = p[:, _C_QC:_C_KC].T.astype(BF16)
    kc[0] = (p[:, _C_KC:_C_VC] * DH_C ** -0.5).astype(BF16)
    vct[0] = p[:, _C_VC:_C_OC].T.astype(BF16)
    oc[0] = p[:, _C_OC:_C_GC].astype(BF16)
    gc[0] = p[:, _C_GC:_C_QD]
    gct[0] = p[:, _C_GC:_C_QD].T
    qd[0] = headnorm_rope(p[:, _C_QD:_C_KD], g32_ref, gains[2:3, :], p32_ref, cd, sd, DH_D ** -0.5).astype(BF16)
    kd[0] = headnorm_rope(p[:, _C_KD:_C_VD], g32_ref, gains[3:4, :], p32_ref, cd, sd, 1.0).astype(BF16)
    vd[0] = p[:, _C_VD:_C_END].astype(BF16)


def _inproj(x, f, mod, g1, w_packed, tabs, gains, mats, T):
    B, N, D = x.shape
    rb = ROW_BLOCK
    nlat = T // rb
    row = lambda w: pl.BlockSpec((1, rb, w), lambda b, r: (b, r, 0))
    tab = pl.BlockSpec((rb, 256), lambda b, r: (r, 0))
    full = lambda a: pl.BlockSpec(a.shape, lambda b, r: (0,) * a.ndim)
    widths = [256, 128, 128, 768, 256, 256, 128, 256, 256, 256]
    dtypes = [BF16] * 6 + [F32] + [BF16] * 3
    tr = lambda w: pl.BlockSpec((1, w, rb), lambda b, r: (b, 0, r))
    has_f = f is not None
    return pl.pallas_call(
        functools.partial(_inproj_kernel, has_f=has_f),
        grid=(B, N // rb),
        in_specs=[row(D)] * (2 if has_f else 1)
        + [pl.BlockSpec((1, 1, 6, D), lambda b, r: (b, jnp.where(r >= nlat, 1, 0), 0, 0)),
           full(g1), full(w_packed), tab, tab, tab, tab, full(gains)] + [full(m) for m in mats],
        out_specs=[row(w) for w in widths] + [tr(128), tr(256), tr(256)] + ([row(D)] if has_f else []),
        out_shape=[jax.ShapeDtypeStruct((B, N, w), dt) for w, dt in zip(widths, dtypes)]
        + [jax.ShapeDtypeStruct((B, 128, N), F32), jax.ShapeDtypeStruct((B, 256, N), BF16),
           jax.ShapeDtypeStruct((B, 256, N), BF16)] + ([jax.ShapeDtypeStruct((B, N, D), F32)] if has_f else []),
        compiler_params=_cparams(("arbitrary", "arbitrary")),
        name="inproj",
    )(x, *([f] if has_f else []), mod, g1, w_packed, *tabs, gains, *mats)


def _wattn_kernel(q_ref, k_ref, v_ref, sink_ref, sinkw_ref, placeq_ref, gsumq_ref, placek_ref, gsumk_ref, o_ref,
                  kaug_ref, vaug_ref, kmax_ref, *, T, S):
    j = pl.program_id(1)
    qb, kwin = WATTN_QROWS, WATTN_QROWS + 2 * BLK
    is_lat = j < T // qb
    start = pl.multiple_of(jnp.clip(j * qb - BLK, 0, T - kwin), BLK)
    row = lax.broadcasted_iota(I32, (G_A * qb, kwin), 0)
    qpos = j * qb + (row & (qb - 1))
    kpos = start + lax.broadcasted_iota(I32, (G_A * qb, kwin), 1)
    valid = (jnp.abs(qpos - kpos) <= WINDOW) & is_lat
    grp = jnp.right_shift(lax.broadcasted_iota(I32, (G_A * qb, 1), 0), qb.bit_length() - 1)
    sinks = sink_ref[...]

    @pl.when(j == 0)
    def _():
        lane = lax.broadcasted_iota(I32, (1, HKV_A * 128), 1)
        one = jnp.where((lane & 127) == DH_A, 1.0, 0.0)
        k_wide = _dot(k_ref[0], placek_ref[...])
        kaug_ref[...] = (k_wide + one).astype(BF16)
        vaug_ref[...] = (_dot(v_ref[0], placek_ref[...]) + one).astype(BF16)
        ksq = _dot((k_wide * k_wide).astype(BF16), gsumk_ref[...])
        kmax_ref[...] = jnp.sqrt(jnp.max(ksq, axis=0, keepdims=True))

    def sink_rows(heads):
        sink = sinks[0:1, heads[0]:heads[0] + 1]
        for g in range(1, G_A):
            sink = jnp.where(grp == g, sinks[0:1, heads[g]:heads[g] + 1], sink)
        return sink

    def store(heads, o):
        for g, hd in enumerate(heads):
            o_ref[0, :, hd * DH_A:(hd + 1) * DH_A] = o[g * qb:(g + 1) * qb].astype(BF16)

    def body_fast():
        q_wide = _dot(q_ref[0], placeq_ref[...])
        qn = jnp.sqrt(_dot((q_wide * q_wide).astype(BF16), gsumq_ref[...]))
        km = kmax_ref[...]
        kmq = jnp.concatenate([km[:, (hd // G_A) * 128:(hd // G_A + 1) * 128] for hd in range(H_A)], axis=1)
        lane = lax.broadcasted_iota(I32, (1, H_A * 128), 1)
        shift = jnp.where((lane & 127) == DH_A, jnp.maximum(qn * kmq, sinkw_ref[...]), 0.0)
        q_aug = (q_wide - shift).astype(BF16)
        low = None
        for hk in range(HKV_A):
            heads = [hk * G_A + g for g in range(G_A)]
            ks = slice(hk * 128, (hk + 1) * 128)
            q2 = jnp.concatenate([q_aug[:, hd * 128:(hd + 1) * 128] for hd in heads], axis=0)
            m_used = -q2[:, DH_A:DH_A + 1].astype(F32)
            s_loc = jnp.where(valid, _dot_nt(q2, kaug_ref[pl.ds(start, kwin), ks]), NEG)
            s_ctx = _dot_nt(q2, kaug_ref[T:T + S, ks])
            pv = (_dot(jnp.exp(s_loc.astype(BF16)), vaug_ref[pl.ds(start, kwin), ks])
                  + _dot(jnp.exp(s_ctx.astype(BF16)), vaug_ref[T:T + S, ks]))
            sigma = pv[:, DH_A:DH_A + 1]
            low = sigma if low is None else jnp.minimum(low, sigma)
            store(heads, pv[:, 0:DH_A] / (sigma + jnp.exp(sink_rows(heads) - m_used)))
        return jnp.min(low)

    def body_exact():
        kw = k_ref[0, pl.ds(start, kwin), :]
        vw = v_ref[0, pl.ds(start, kwin), :]
        kc = k_ref[0, T:T + S, :]
        vc = v_ref[0, T:T + S, :]
        for hk in range(HKV_A):
            cs = slice(hk * DH_A, (hk + 1) * DH_A)
            kwh, vwh, kch, vch = kw[:, cs], vw[:, cs], kc[:, cs], vc[:, cs]
            heads = [hk * G_A + g for g in range(G_A)]
            q = jnp.concatenate([q_ref[0, :, hd * DH_A:(hd + 1) * DH_A] for hd in heads], axis=0)
            sink = sink_rows(heads)
            s_loc = jnp.where(valid, _dot_nt(q, kwh), NEG)
            s_ctx = _dot_nt(q, kch)
            m = jnp.maximum(jnp.maximum(jnp.max(s_loc, axis=-1, keepdims=True),
                                        jnp.max(s_ctx, axis=-1, keepdims=True)), sink)
            p_loc = jnp.exp(s_loc - m)
            p_ctx = jnp.exp(s_ctx - m)
            den = (jnp.sum(p_loc, axis=-1, keepdims=True) + jnp.sum(p_ctx, axis=-1, keepdims=True)
                   + jnp.exp(sink - m))
            store(heads, (_dot(p_loc.astype(BF16), vwh) + _dot(p_ctx.astype(BF16), vch)) / den)

    low = body_fast()

    @pl.when(jnp.logical_not(low > DATTN_MIN_ROWSUM))
    def _():
        body_exact()


def _wattn_mats(nheads):
    place = np.zeros((nheads * DH_A, nheads * 128), np.float32)
    gsum = np.zeros((nheads * 128, nheads * 128), np.float32)
    for h in range(nheads):
        for d in range(DH_A):
            place[h * DH_A + d, h * 128 + d] = 1.0
            gsum[h * 128 + d, h * 128 + DH_A] = 1.0
    return jnp.asarray(place, BF16), jnp.asarray(gsum, BF16)


def _wattn(qa, ka, va, sink, T, S):
    B, N, _ = qa.shape
    qb = WATTN_QROWS
    placeq, gsumq = _wattn_mats(H_A)
    placek, gsumk = _wattn_mats(HKV_A)
    sinkw = jnp.zeros((1, H_A * 128), F32).at[0, DH_A::128].set(sink[0, 0:H_A])
    full = lambda a: pl.BlockSpec(a.shape, lambda b, j: (0,) * a.ndim)
    return pl.pallas_call(
        functools.partial(_wattn_kernel, T=T, S=S),
        grid=(B, N // qb),
        in_specs=[pl.BlockSpec((1, qb, 256), lambda b, j: (b, j, 0)),
                  pl.BlockSpec((1, N, 128), lambda b, j: (b, 0, 0)),
                  pl.BlockSpec((1, N, 128), lambda b, j: (b, 0, 0)),
                  pl.BlockSpec((1, 128), lambda b, j: (0, 0)),
                  full(sinkw), full(placeq), full(gsumq), full(placek), full(gsumk)],
        out_specs=pl.BlockSpec((1, qb, 256), lambda b, j: (b, j, 0)),
        out_shape=jax.ShapeDtypeStruct((B, N, 256), BF16),
        scratch_shapes=[pltpu.VMEM((N, HKV_A * 128), BF16), pltpu.VMEM((N, HKV_A * 128), BF16),
                        pltpu.VMEM((1, HKV_A * 128), F32)],
        compiler_params=_cparams(("arbitrary", "arbitrary")),
        name="window_attn",
    )(qa, ka, va, sink, sinkw, placeq, gsumq, placek, gsumk)


def _dattn_kernel(q_ref, k_ref, v_ref, lam_ref, sub_ref, place_ref, gsum_ref, o_ref,
                  vaug_ref, kaug_ref, kmax_ref, kch_ref, qch_ref, tch_ref, *, T, S, lam_init):
    j = pl.program_id(1)
    lv = lam_ref[...]
    lam = (jnp.exp(jnp.sum(lv[0:1, :] * lv[1:2, :], axis=-1, keepdims=True))
           - jnp.exp(jnp.sum(lv[2:3, :] * lv[3:4, :], axis=-1, keepdims=True)) + lam_init)
    subg = sub_ref[...] * (1.0 - lam_init)
    dv = 2 * DH_D
    nchain, aw = 2 * H_D, 2 * DH_D

    @pl.when(j == 0)
    def _():
        ones = jnp.where(lax.broadcasted_iota(I32, (T + S, dv), 1) == 0, 1.0, 0.0).astype(BF16)
        for hd in range(H_D):
            vaug_ref[hd] = jnp.concatenate([v_ref[0, :, hd * dv:(hd + 1) * dv], ones], axis=1)
        k_wide = _dot(k_ref[0], place_ref[...])
        lane = lax.broadcasted_iota(I32, (1, nchain * aw), 1)
        k_aug = (k_wide + jnp.where((lane & (aw - 1)) == DH_D, 1.0, 0.0)).astype(BF16)
        kaug_ref[...] = k_aug
        for i in range(nchain):
            kch_ref[i] = k_aug[:, i * aw:(i + 1) * aw]
        ksq = _dot((k_wide * k_wide).astype(BF16), gsum_ref[...])
        kmax_ref[...] = jnp.sqrt(jnp.max(ksq, axis=0, keepdims=True))

    def finish(hd, terms):
        o = terms[0] - lam * terms[1]
        ms = jnp.mean(o * o, axis=-1, keepdims=True)
        o_ref[0, :, hd * dv:(hd + 1) * dv] = (o * lax.rsqrt(ms + EPS) * subg).astype(BF16)

    def body_fast(k0, nk):
        q_wide = _dot(q_ref[0], place_ref[...])
        qn = jnp.sqrt(_dot((q_wide * q_wide).astype(BF16), gsum_ref[...]))
        q_aug = (q_wide - qn * kmax_ref[...]).astype(BF16)
        terms, low = [], None
        for i in range(nchain):
            hd, cs = i // 2, slice(i * aw, (i + 1) * aw)
            pv = None
            for t0 in range(k0, k0 + nk, DATTN_KEY_TILE):
                s = _dot_nt(q_aug[:, cs], kaug_ref[t0:t0 + DATTN_KEY_TILE, cs])
                part = _dot(jnp.exp(s.astype(BF16)), vaug_ref[hd, t0:t0 + DATTN_KEY_TILE, :])
                pv = part if pv is None else pv + part
            sigma = pv[:, dv:dv + 1]
            low = sigma if low is None else jnp.minimum(low, sigma)
            terms.append(pv[:, 0:dv] / sigma)
            if i % 2 == 1:
                finish(hd, terms)
                terms = []
        return jnp.min(low)

    def body_exact(k0, nk):
        q_wide = _dot(q_ref[0], place_ref[...])
        for i in range(nchain):
            qch_ref[i] = q_wide[:, i * aw:(i + 1) * aw].astype(BF16)

        def chain(i, carry):
            s = _dot_nt(qch_ref[i], kch_ref[i, k0:k0 + nk, :])
            e = jnp.exp((s - jnp.max(s, axis=-1, keepdims=True)).astype(BF16))
            pv = _dot(e, vaug_ref[jnp.right_shift(i, 1), k0:k0 + nk, :])
            tch_ref[i] = pv / pv[:, dv:dv + 1]
            return carry

        lax.fori_loop(0, nchain, chain, 0)
        for hd in range(H_D):
            finish(hd, [tch_ref[2 * hd][:, 0:dv], tch_ref[2 * hd + 1][:, 0:dv]])

    def body(k0, nk):
        low = body_fast(k0, nk)

        @pl.when(jnp.logical_not(low > DATTN_MIN_ROWSUM))
        def _():
            body_exact(k0, nk)

    @pl.when(j < T // ROW_BLOCK)
    def _():
        body(0, T + S)

    @pl.when(j >= T // ROW_BLOCK)
    def _():
        body(T, S)


def _dattn_mats():
    nchain, aw = 2 * H_D, 2 * DH_D
    place = np.zeros((nchain * DH_D, nchain * aw), np.float32)
    gsum = np.zeros((nchain * aw, nchain * aw), np.float32)
    for i in range(nchain):
        for d in range(DH_D):
            place[i * DH_D + d, i * aw + d] = 1.0
            gsum[i * aw + d, i * aw + DH_D] = 1.0
    return jnp.asarray(place, BF16), jnp.asarray(gsum, BF16)


def _dattn(qd, kd, vd, lam_vecs, subg, T, S, lam_init):
    B, N, _ = qd.shape
    rb = ROW_BLOCK
    place, gsum = _dattn_mats()
    full = lambda a: pl.BlockSpec(a.shape, lambda b, j: (0,) * a.ndim)
    return pl.pallas_call(
        functools.partial(_dattn_kernel, T=T, S=S, lam_init=lam_init),
        grid=(B, N // rb),
        in_specs=[pl.BlockSpec((1, rb, 256), lambda b, j: (b, j, 0)),
                  pl.BlockSpec((1, N, 256), lambda b, j: (b, 0, 0)),
                  pl.BlockSpec((1, N, 256), lambda b, j: (b, 0, 0)),
                  pl.BlockSpec((4, DH_D), lambda b, j: (0, 0)),
                  pl.BlockSpec((1, 2 * DH_D), lambda b, j: (0, 0)), full(place), full(gsum)],
        out_specs=pl.BlockSpec((1, rb, 256), lambda b, j: (b, j, 0)),
        out_shape=jax.ShapeDtypeStruct((B, N, 256), BF16),
        scratch_shapes=[pltpu.VMEM((H_D, N, 4 * DH_D), BF16), pltpu.VMEM((N, 2 * 256), BF16),
                        pltpu.VMEM((1, 2 * 256), F32), pltpu.VMEM((2 * H_D, N, 2 * DH_D), BF16),
                        pltpu.VMEM((2 * H_D, rb, 2 * DH_D), BF16), pltpu.VMEM((2 * H_D, rb, 4 * DH_D), F32)],
        compiler_params=_cparams(("arbitrary", "arbitrary")),
        name="diff_attn",
    )(qd, kd, vd, lam_vecs, subg, place, gsum)


def _log_sigmoid(x):
    return jnp.minimum(x, 0.0) - jnp.log1p(jnp.exp(-jnp.abs(x)))


def _dot3(a, b, split_b):
    x = b if split_b else a
    hi = x.astype(BF16)
    r1 = x - hi.astype(F32)
    mid = r1.astype(BF16)
    lo = (r1 - mid.astype(F32)).astype(BF16)
    if split_b:
        a = a.astype(BF16)
        return _dot(a, hi) + _dot(a, mid) + _dot(a, lo)
    b = b.astype(BF16)
    return _dot(hi, b) + _dot(mid, b) + _dot(lo, b)


def _mlstm_kernel(qtf_ref, kf_ref, vtf_ref, gf_ref, gtf_ref, qtb_ref, kb_ref, vtb_ref, gb_ref, gtb_ref,
                  bias_ref, biast_ref, hf_ref, hb_ref, c_ref, m_ref):
    i = pl.program_id(1)
    lc = MLSTM_CHUNK

    @pl.when(i == 0)
    def _():
        c_ref[...] = jnp.zeros_like(c_ref)
        m_ref[...] = jnp.zeros_like(m_ref)

    r_io = lax.broadcasted_iota(I32, (lc, lc), 0)
    c_io = lax.broadcasted_iota(I32, (lc, lc), 1)
    lower = r_io >= c_io
    upper = r_io <= c_io
    tri_lo = jnp.where(lower, 1.0, 0.0).astype(F32)
    tri_up = jnp.where(upper, 1.0, 0.0).astype(F32)
    ones_rows = jnp.where(lax.broadcasted_iota(I32, (DH_C, lc), 0) == 0, 1.0, 0.0).astype(BF16)

    def direction(d, qt_ref, k_ref, vt_ref, g_ref, gt_ref, out_ref):
        pre = g_ref[0] + bias_ref[...]
        pre_t = gt_ref[0] + biast_ref[...]
        lf = _log_sigmoid(pre)
        lf_t = _log_sigmoid(pre_t)
        if d == 0:
            bcol = _dot3(tri_lo, lf, True)
            brow = _dot3(lf_t, tri_up, False)
            valid, end = upper, lc - 1
        else:
            bcol = _dot3(tri_up, lf, True)
            brow = _dot3(lf_t, tri_lo, False)
            valid, end = lower, 0
        for hd in range(H_C):
            ci, cf = d * H_C + hd, (2 + d) * H_C + hd
            rs = slice(hd * DH_C, (hd + 1) * DH_C)
            q_t = qt_ref[0, rs, :]
            k = k_ref[0, :, rs]
            v_aug = jnp.concatenate([vt_ref[0, rs, :], ones_rows], axis=0)
            b_row = brow[cf:cf + 1, :]
            ig_row = pre_t[ci:ci + 1, :]
            bi_col = bcol[:, cf:cf + 1] - pre[:, ci:ci + 1]
            b_end = brow[cf:cf + 1, end:end + 1]
            slot = d * H_C + hd
            c = c_ref[slot]
            m = m_ref[slot][:, 0:1]
            logw = jnp.where(valid, b_row - bi_col, NEG)
            inter = b_row + m
            m_t = jnp.maximum(inter, jnp.max(logw, axis=0, keepdims=True))
            w_in = jnp.exp(inter - m_t)
            p_t = (_dot(k, q_t) * jnp.exp(logw - m_t)).astype(BF16)
            num = w_in * _dot(c.astype(BF16), q_t) + _dot(v_aug, p_t)
            den = jnp.maximum(jnp.abs(num[DH_C:DH_C + 1, :]), jnp.exp(-m_t))
            out_ref[0, rs, :] = num[0:DH_C, :] / den
            g_end = b_end - b_row + ig_row
            m_new = jnp.maximum(b_end + m, jnp.max(g_end, axis=1, keepdims=True))
            wv = (v_aug.astype(F32) * jnp.exp(g_end - m_new)).astype(BF16)
            c_ref[slot] = jnp.exp(b_end + m - m_new) * c + _dot(wv, k)
            m_ref[slot] = jnp.broadcast_to(m_new, (1, 128))

    direction(0, qtf_ref, kf_ref, vtf_ref, gf_ref, gtf_ref, hf_ref)
    direction(1, qtb_ref, kb_ref, vtb_ref, gb_ref, gtb_ref, hb_ref)


def _mlstm(qct, kc, vct, gc, gct, bias, biast, T, S):
    B, N, _ = kc.shape
    lc = MLSTM_CHUNK
    nch, nlat = N // lc, T // lc
    fwd = lambda i: jnp.where(i < nch - nlat, nlat + i, i - (nch - nlat))
    bwd = lambda i: nch - 1 - i
    tok = lambda f: pl.BlockSpec((1, lc, 256), lambda b, i: (b, f(i), 0))
    tok_t = lambda f: pl.BlockSpec((1, 256, lc), lambda b, i: (b, 0, f(i)))
    gate = lambda f: pl.BlockSpec((1, lc, 128), lambda b, i: (b, f(i), 0))
    gate_t = lambda f: pl.BlockSpec((1, 16, lc), lambda b, i: (b, 0, f(i)))
    return pl.pallas_call(
        _mlstm_kernel,
        grid=(B, nch),
        in_specs=[tok_t(fwd), tok(fwd), tok_t(fwd), gate(fwd), gate_t(fwd),
                  tok_t(bwd), tok(bwd), tok_t(bwd), gate(bwd), gate_t(bwd),
                  pl.BlockSpec((1, 128), lambda b, i: (0, 0)),
                  pl.BlockSpec((16, 1), lambda b, i: (0, 0))],
        out_specs=[tok_t(fwd), tok_t(bwd)],
        out_shape=[jax.ShapeDtypeStruct((B, 256, N), F32)] * 2,
        scratch_shapes=[pltpu.VMEM((2 * H_C, 2 * DH_C, DH_C), F32), pltpu.VMEM((2 * H_C, 1, 128), F32)],
        compiler_params=_cparams(("arbitrary", "arbitrary")),
        name="mlstm",
    )(qct, kc, vct, gc, gct, qct, kc, vct, gc, gct, bias, biast)


def _shortconv_kernel(u_ref, w_ref, o_ref, *, T, S):
    u = u_ref[0].astype(F32)
    n = u.shape[0]
    w = w_ref[...]
    row = lax.broadcasted_iota(I32, u.shape, 0)
    zero = jnp.zeros((1, u.shape[1]), F32)
    prev = jnp.concatenate([zero, u[0:n - 1]], axis=0)
    nxt = jnp.concatenate([u[1:n], zero], axis=0)
    prev = jnp.where(row == T, 0.0, prev)
    nxt = jnp.where(row == T - 1, 0.0, nxt)
    o_ref[0] = (w[0:1, :] * prev + w[1:2, :] * u + w[2:3, :] * nxt).astype(BF16)


def _shortconv(ub, conv_w, T, S):
    B, N, W = ub.shape
    return pl.pallas_call(
        functools.partial(_shortconv_kernel, T=T, S=S),
        grid=(B, W // 256),
        in_specs=[pl.BlockSpec((1, N, 256), lambda b, j: (b, 0, j)),
                  pl.BlockSpec((3, 256), lambda b, j: (0, j))],
        out_specs=pl.BlockSpec((1, N, 256), lambda b, j: (b, 0, j)),
        out_shape=jax.ShapeDtypeStruct((B, N, W), BF16),
        compiler_params=_cparams(("arbitrary", "arbitrary")),
        name="hyena_shortconv",
    )(ub, conv_w)


def _hyfilter_kernel(z_ref, w1_ref, b1_ref, fr_ref, w2_ref, b2_ref, w3_ref, dec_ref,
                     hp_ref, hf_ref, ss_ref):
    i = pl.program_id(0)
    fr = fr_ref[...]
    h = jnp.sin(fr * (_dot(z_ref[...], w1_ref[...], precision=HI) + b1_ref[...]))
    h = jnp.sin(fr * (_dot(h, w2_ref[...], precision=HI) + b2_ref[...]))
    h = _dot(h, w3_ref[...], precision=HI)
    dec = dec_ref[...]
    dec2 = jnp.concatenate([dec, dec], axis=1)
    hw = HY_ORDER * HY_W
    h0 = h[:, 0:hw] * dec2
    h1 = h[:, hw:2 * hw] * dec2
    rows = h0.shape[0]
    t = i * rows + lax.broadcasted_iota(I32, h0.shape, 0)
    h1 = jnp.where(t == 0, 0.0, h1)
    hp_ref[...] = h0
    hf_ref[...] = h1

    @pl.when(i == 0)
    def _():
        ss_ref[...] = jnp.zeros_like(ss_ref)

    ss_ref[...] += jnp.sum(h0 * h0 + h1 * h1, axis=0, keepdims=True)


def _hyfilter(zfeat, w1, b1, fr, w2, b2, w3, decay):
    ls = zfeat.shape[0]
    rb = min(ls, 512)
    hw = HY_ORDER * HY_W
    full = lambda a: pl.BlockSpec(a.shape, lambda i: (0,) * a.ndim)
    return pl.pallas_call(
        _hyfilter_kernel,
        grid=(ls // rb,),
        in_specs=[pl.BlockSpec((rb, 128), lambda i: (i, 0)), full(w1), full(b1), full(fr), full(w2),
                  full(b2), full(w3), pl.BlockSpec((rb, HY_W), lambda i: (i, 0))],
        out_specs=[pl.BlockSpec((rb, hw), lambda i: (i, 0)), pl.BlockSpec((rb, hw), lambda i: (i, 0)),
                   pl.BlockSpec((1, hw), lambda i: (0, 0))],
        out_shape=[jax.ShapeDtypeStruct((ls, hw), F32), jax.ShapeDtypeStruct((ls, hw), F32),
                   jax.ShapeDtypeStruct((1, hw), F32)],
        compiler_params=_cparams(("arbitrary",)),
        name="hyena_filter",
    )(zfeat, w1, b1, fr, w2, b2, w3, decay)


def _cmul_const(v, c, s):
    vr, vi = v
    r = math.sqrt(0.5)
    if abs(s) < 1e-9:
        return (vr, vi) if c > 0 else (-vr, -vi)
    if abs(c) < 1e-9:
        return (-vi, vr) if s > 0 else (vi, -vr)
    if abs(abs(c) - r) < 1e-9 and abs(abs(s) - r) < 1e-9:
        a, b = (vr if c > 0 else -vr), (vi if s > 0 else -vi)
        p, q = (vi if c > 0 else -vi), (vr if s > 0 else -vr)
        return (a - b) * r, (p + q) * r
    return c * vr - s * vi, c * vi + s * vr


def _fft_pow2(xs, sign):
    n = len(xs)
    if n == 1:
        return xs
    ev, od = _fft_pow2(xs[0::2], sign), _fft_pow2(xs[1::2], sign)
    out = [None] * n
    for k in range(n // 2):
        ang = sign * 2.0 * math.pi * k / n
        tr, ti = _cmul_const(od[k], math.cos(ang), math.sin(ang))
        out[k] = (ev[k][0] + tr, ev[k][1] + ti)
        out[k + n // 2] = (ev[k][0] - tr, ev[k][1] - ti)
    return out


def _hy_stage1(src_ref, f_ref, a_ref, nseg):
    k1p = a_ref.shape[2]
    for n2 in range(HY_N2):
        if len(src_ref.shape) == 3:
            zs = jnp.concatenate([src_ref[t, pl.ds(n2, nseg, stride=HY_N2), :] for t in range(src_ref.shape[0])],
                                 axis=1).astype(BF16)
        else:
            zs = src_ref[pl.ds(n2, nseg, stride=HY_N2), :].astype(BF16)
        r = _dot(f_ref[n2], zs)
        a_ref[0, n2] = r[0:k1p]
        a_ref[1, n2] = r[k1p:2 * k1p]


def _hy_chunks(a_ref):
    k1p, lanes = a_ref.shape[2], a_ref.shape[3]
    return k1p // 8, [slice(t * 128, (t + 1) * 128) for t in range(lanes // 128)]


def _hyspec_kernel(hp_ref, hf_ref, ss_ref, f_ref, h_ref, ap_ref, af_ref, *, ls):
    nseg = ls // HY_N2
    _hy_stage1(hp_ref, f_ref, ap_ref, nseg)
    _hy_stage1(hf_ref, f_ref, af_ref, nseg)
    nchunk, lane_tiles = _hy_chunks(ap_ref)

    def body(i, _):
        rows = pl.ds(pl.multiple_of(i * 8, 8), 8)
        k1 = i * 8 + lax.broadcasted_iota(I32, (8, 1), 0)
        wk = jnp.where((k1 == 0) | (k1 == nseg), 1.0, 2.0) * (1.0 / (2 * ls))
        for ln in lane_tiles:
            scale = lax.rsqrt(ss_ref[:, ln] + EPS) * wk
            P = _fft_pow2([(ap_ref[0, n2, rows, ln], ap_ref[1, n2, rows, ln]) for n2 in range(HY_N2)], -1)
            Q = _fft_pow2([(af_ref[0, n2, rows, ln], af_ref[1, n2, rows, ln]) for n2 in range(HY_N2)], -1)
            for k2 in range(HY_N2):
                h_ref[0, 0, k2, rows, ln] = (P[k2][0] + Q[k2][0]) * scale
                h_ref[0, 1, k2, rows, ln] = (P[k2][1] - Q[k2][1]) * scale
        return 0

    lax.fori_loop(0, nchunk, body, 0)


def _hyspec(hp, hf, ss, fmat):
    ls = hp.shape[0]
    k1p = fmat.shape[1] // 2
    full = lambda a: pl.BlockSpec(a.shape, lambda o, c: (0,) * a.ndim)
    nc = HY_W // 128
    col = lambda o, c: (0, o * nc + c)
    return pl.pallas_call(
        functools.partial(_hyspec_kernel, ls=ls),
        grid=(HY_ORDER, nc),
        in_specs=[pl.BlockSpec((ls, 128), col), pl.BlockSpec((ls, 128), col), pl.BlockSpec((1, 128), col),
                  full(fmat)],
        out_specs=pl.BlockSpec((1, 2, HY_N2, k1p, 128), lambda o, c: (o, 0, 0, 0, c)),
        out_shape=jax.ShapeDtypeStruct((HY_ORDER, 2, HY_N2, k1p, HY_W), F32),
        scratch_shapes=[pltpu.VMEM((2, HY_N2, k1p, 128), F32)] * 2,
        compiler_params=_cparams(("arbitrary", "arbitrary")),
        name="hyena_spectrum",
    )(hp, hf, ss, fmat)


def _hyconv_kernel(z_ref, gate_ref, h_ref, f_ref, g_ref, bias_ref, o_ref, zf_ref, yf_ref, a_ref, *, ls):
    nseg = ls // HY_N2
    ntile = zf_ref.shape[0]
    for t in range(ntile):
        zf_ref[t] = z_ref[0, :, t * 128:(t + 1) * 128].astype(F32)
    _hy_stage1(zf_ref, f_ref, a_ref, nseg)
    nchunk, lane_tiles = _hy_chunks(a_ref)

    def body(i, _):
        rows = pl.ds(pl.multiple_of(i * 8, 8), 8)
        for ln in lane_tiles:
            X = _fft_pow2([(a_ref[0, n2, rows, ln], a_ref[1, n2, rows, ln]) for n2 in range(HY_N2)], -1)
            Y = []
            for k2 in range(HY_N2):
                hr, hi = h_ref[0, 0, k2, rows, ln], h_ref[0, 1, k2, rows, ln]
                xr, xi = X[k2]
                Y.append((xr * hr - xi * hi, xr * hi + xi * hr))
            Bv = _fft_pow2(Y, 1)
            for n2 in range(HY_N2):
                a_ref[0, n2, rows, ln] = Bv[n2][0]
                a_ref[1, n2, rows, ln] = Bv[n2][1]
        return 0

    lax.fori_loop(0, nchunk, body, 0)
    for n2 in range(HY_N2):
        bb = jnp.concatenate([a_ref[0, n2], a_ref[1, n2]], axis=0).astype(BF16)
        yv = _dot(g_ref[n2], bb)
        for t in range(ntile):
            yf_ref[t, pl.ds(n2, nseg, stride=HY_N2), :] = yv[:, t * 128:(t + 1) * 128]
    for t in range(ntile):
        ln = slice(t * 128, (t + 1) * 128)
        o_ref[0, :, ln] = (gate_ref[0, :, ln].astype(F32)
                           * (yf_ref[t] + zf_ref[t] * bias_ref[:, ln])).astype(BF16)


def _hyconv(zsrc, zcol, zrow, gsrc, gcol, grow, hspec, order, fmat, gmat, bias, ls):
    B = zsrc.shape[0]
    k1p = fmat.shape[1] // 2
    once = lambda a: pl.BlockSpec(a.shape, lambda b: (0,) * a.ndim, pipeline_mode=pl.Buffered(1))
    return pl.pallas_call(
        functools.partial(_hyconv_kernel, ls=ls),
        grid=(B,),
        in_specs=[pl.BlockSpec((1, ls, HY_W), lambda b: (b, zrow, zcol)),
                  pl.BlockSpec((1, ls, HY_W), lambda b: (b, grow, gcol)),
                  pl.BlockSpec((1, 2, HY_N2, k1p, HY_W), lambda b: (order, 0, 0, 0, 0),
                               pipeline_mode=pl.Buffered(1)),
                  once(fmat), once(gmat), pl.BlockSpec((1, HY_W), lambda b: (0, 0))],
        out_specs=pl.BlockSpec((1, ls, HY_W), lambda b: (b, 0, 0)),
        out_shape=jax.ShapeDtypeStruct((B, ls, HY_W), BF16),
        scratch_shapes=[pltpu.VMEM((HY_W // 128, ls, 128), F32), pltpu.VMEM((HY_W // 128, ls, 128), F32),
                        pltpu.VMEM((2, HY_N2, k1p, HY_W), F32)],
        compiler_params=_cparams(("arbitrary",)),
        name="hyena_longconv",
    )(zsrc, gsrc, hspec, fmat, gmat, bias)


def _outproj_kernel(x_ref, mod_ref, ya_ref, ybl_ref, ybc_ref, hf_ref, hb_ref, oc_ref, yd_ref, mlg_ref, g64_ref,
                    w_ref, g2_ref, wr_ref, xo_ref, h2_ref, aff_ref, *, nlat):
    hsum = (hf_ref[0] + hb_ref[0]).T
    msq = _dot((hsum * hsum).astype(BF16), g64_ref[...])
    yc = jax.nn.sigmoid(oc_ref[0].astype(F32)) * (hsum * lax.rsqrt(msq + EPS) * mlg_ref[...])
    yb = jnp.where(pl.program_id(1) < nlat, ybl_ref[0], ybc_ref[0])
    y = jnp.concatenate([ya_ref[0], yb, yc.astype(BF16), yd_ref[0]], axis=1)
    mod = mod_ref[0, 0]
    x = x_ref[0] + mod[2:3, :] * _dot(y, w_ref[...])
    xo_ref[0] = x
    ms = jnp.mean(x * x, axis=-1, keepdims=True)
    h2 = x * lax.rsqrt(ms + EPS) * g2_ref[...] * (1.0 + mod[4:5, :]) + mod[3:4, :]
    h2_hi = h2.astype(BF16)
    h2_ref[0] = _pack_bf16_pair(h2_hi[:, 0:D_MODEL // 2], h2_hi[:, D_MODEL // 2:])
    h2_lo = (h2 - h2_hi.astype(F32)).astype(BF16)
    lg = _dot(h2_hi, wr_ref[...])
    logits = lg[:, 0:128] + lg[:, 128:256] + _dot(h2_lo, wr_ref[:, 0:128])
    lane = lax.broadcasted_iota(I32, logits.shape, 1)
    logits = jnp.where(lane < N_EXPERTS, logits, NEG)
    e = jnp.exp(logits - jnp.max(logits, axis=-1, keepdims=True))
    aff_ref[0] = (e / jnp.sum(e, axis=-1, keepdims=True)).T


def _outproj(x, mod, ya, yb_l, yb_c, hf, hb, oc, yd, mlg, g64, w_out, g2, w_router, T):
    B, N, D = x.shape
    rb = ROW_BLOCK
    nlat = T // rb
    row = lambda w: pl.BlockSpec((1, rb, w), lambda b, r: (b, r, 0))
    full = lambda a: pl.BlockSpec(a.shape, lambda b, r: (0,) * a.ndim)
    return pl.pallas_call(
        functools.partial(_outproj_kernel, nlat=nlat),
        grid=(B, N // rb),
        in_specs=[row(D), pl.BlockSpec((1, 1, 6, D), lambda b, r: (b, jnp.where(r >= nlat, 1, 0), 0, 0)),
                  row(256),
                  pl.BlockSpec((1, rb, 256), lambda b, r: (b, jnp.minimum(r, nlat - 1), 0)),
                  pl.BlockSpec((1, rb, 256), lambda b, r: (b, jnp.maximum(r - nlat, 0), 0)),
                  pl.BlockSpec((1, 256, rb), lambda b, r: (b, 0, r)),
                  pl.BlockSpec((1, 256, rb), lambda b, r: (b, 0, r)), row(256), row(256),
                  full(mlg), full(g64), full(w_out), full(g2), full(w_router)],
        out_specs=[row(D), row(D // 2), pl.BlockSpec((1, 128, rb), lambda b, r: (b, 0, r))],
        out_shape=[jax.ShapeDtypeStruct((B, N, D), F32), jax.ShapeDtypeStruct((B, N, D // 2), I32),
                   jax.ShapeDtypeStruct((B, 128, N), F32)],
        compiler_params=_cparams(("arbitrary", "arbitrary")),
        name="outproj",
    )(x, mod, ya, yb_l, yb_c, hf, hb, oc, yd, mlg, g64, w_out, g2, w_router)


def _prefix_exclusive(x):
    n = x.shape[1]
    lane = lax.broadcasted_iota(I32, x.shape, 1)
    inc = x
    d = 1
    while d < n:
        inc = inc + jnp.where(lane >= d, pltpu.roll(inc, d, axis=1), 0)
        d *= 2
    return inc - x


def _topk_kernel(aff_ref, i_ref, g_ref, pos_ref, val_ref, *, T, S):
    def segment(t0, ts, cap, slot0):
        a = aff_ref[0, :, t0:t0 + ts]
        bits = lax.bitcast_convert_type(a, I32)

        def search(i, v):
            cand = v | jnp.left_shift(jnp.int32(1), 30 - i)
            cnt = jnp.sum((bits >= cand).astype(I32), axis=1, keepdims=True)
            return jnp.where(cnt >= cap, cand, v)

        thr = lax.fori_loop(0, 31, search, jnp.zeros((N_EXPERTS, 1), I32))
        gt = bits > thr
        eq = bits == thr
        need = cap - jnp.sum(gt.astype(I32), axis=1, keepdims=True)
        sel = gt | (eq & (_prefix_exclusive(eq.astype(I32)) < need))
        seli = sel.astype(I32)
        pos_ref[:, 0:ts] = jnp.where(sel, _prefix_exclusive(seli), -1)
        a_hi = a.astype(BF16)
        r1 = a - a_hi.astype(F32)
        a_mid = r1.astype(BF16)
        a_lo = (r1 - a_mid.astype(F32)).astype(BF16)
        val_ref[0, :, 0:ts] = a_hi.astype(F32)
        val_ref[1, :, 0:ts] = a_mid.astype(F32)
        val_ref[2, :, 0:ts] = a_lo.astype(F32)
        tabs = lax.broadcasted_iota(I32, (1, ts), 1) + t0
        t_hi = jnp.right_shift(tabs, 6).astype(F32)
        t_lo = (tabs & 63).astype(F32)
        slot = lax.broadcasted_iota(I32, (cap, ts), 0)
        zeros = jnp.zeros((11, ts), F32)

        def per_expert(e, _):
            onehot = jnp.where(pos_ref[pl.ds(e, 1), 0:ts] == slot, 1.0, 0.0).astype(BF16)
            vals = jnp.concatenate([t_hi, t_lo, val_ref[0, pl.ds(e, 1), 0:ts], val_ref[1, pl.ds(e, 1), 0:ts],
                                    val_ref[2, pl.ds(e, 1), 0:ts], zeros], axis=0).astype(BF16)
            r = _dot_nt(onehot, vals)
            i_ref[0, e, slot0:slot0 + cap, :] = (r[:, 0:1] * 64.0 + r[:, 1:2]).astype(I32)
            g_ref[0, e, slot0:slot0 + cap, :] = r[:, 2:3] + r[:, 3:4] + r[:, 4:5]
            return 0

        lax.fori_loop(0, N_EXPERTS, per_expert, 0)

    segment(0, T, (EC_CAPACITY * T) // N_EXPERTS, 0)
    segment(T, S, (EC_CAPACITY * S) // N_EXPERTS, (EC_CAPACITY * T) // N_EXPERTS)


def _topk(aff_t, T, S):
    B, _, N = aff_t.shape
    E = N_EXPERTS
    cap_t = (EC_CAPACITY * T) // E + (EC_CAPACITY * S) // E
    return pl.pallas_call(
        functools.partial(_topk_kernel, T=T, S=S),
        grid=(B,),
        in_specs=[pl.BlockSpec((1, E, N), lambda b: (b, 0, 0))],
        out_specs=[pl.BlockSpec((1, E, cap_t, 1), lambda b: (b, 0, 0, 0))] * 2,
        out_shape=[jax.ShapeDtypeStruct((B, E, cap_t, 1), I32), jax.ShapeDtypeStruct((B, E, cap_t, 1), F32)],
        scratch_shapes=[pltpu.VMEM((E, T), I32), pltpu.VMEM((3, E, T), F32)],
        compiler_params=_cparams(("arbitrary",)),
        name="expert_topk",
    )(aff_t)


def _gather_kernel(idx_ref, h_ref, *rest, cap_t):
    o_ref = rest[-1]
    b, e = pl.program_id(0), pl.program_id(1)
    base = (b * N_EXPERTS + e) * cap_t

    def body(g, _):
        rows = [h_ref[0, pl.ds(idx_ref[base + g * 16 + k], 1), :] for k in range(16)]
        o_ref[0, 0, pl.ds(pl.multiple_of(g * 16, 16), 16), :] = jnp.concatenate(rows, axis=0)
        return 0

    lax.fori_loop(0, cap_t // 16, body, 0)


def _gather(idx_flat, h2p, xe_buf, b0, bg, cap_t):
    B, N, hw = h2p.shape
    in_specs = [pl.BlockSpec((1, N, hw), lambda b, e, idx: (b0 + b, 0, 0))]
    args = [idx_flat, h2p]
    aliases = {}
    if xe_buf is not None:
        in_specs.append(pl.BlockSpec(memory_space=pl.ANY))
        args.append(xe_buf)
        aliases = {2: 0}
    return pl.pallas_call(
        functools.partial(_gather_kernel, cap_t=cap_t),
        grid_spec=pltpu.PrefetchScalarGridSpec(
            num_scalar_prefetch=1,
            grid=(bg, N_EXPERTS),
            in_specs=in_specs,
            out_specs=pl.BlockSpec((1, 1, cap_t, hw), lambda b, e, idx: (b0 + b, e, 0, 0)),
        ),
        out_shape=jax.ShapeDtypeStruct((B, N_EXPERTS, cap_t, hw), I32),
        input_output_aliases=aliases,
        compiler_params=_cparams(("arbitrary", "arbitrary")),
        name="expert_gather",
    )(*args)


def _ffn_kernel(x_ref, wg_ref, wu_ref, wd_ref, gate_ref, gt_ref, o_ref, wg_s, wu_s, wd_s, *, cap_l):
    @pl.when(pl.program_id(1) == 0)
    def _():
        wg_s[...] = wg_ref[0, 0].astype(BF16)
        wu_s[...] = wu_ref[0, 0].astype(BF16)
        wd_s[...] = wd_ref[0, 0].astype(BF16)

    x = jnp.concatenate(_unpack_bf16_pair(x_ref[0, 0]), axis=1)
    a = _dot(x, wg_s[...])
    u = _dot(x, wu_s[...])
    hmid = (a * jax.nn.sigmoid(a) * u).astype(BF16)
    y = _dot(hmid, wd_s[...])
    gt = gt_ref[0]
    row = lax.broadcasted_iota(I32, y.shape, 0)
    gt2 = jnp.where(row < cap_l, gt[0:1, :], gt[1:2, :])
    o_ref[0, 0] = y * gate_ref[0, 0] * gt2


def _ffn(xe, layer, wg, wu, wd, gate, gt2, cap_l):
    B, E, cap_t, hw = xe.shape
    D = 2 * hw
    wspec = pl.BlockSpec((1, 1, D, D), lambda e, b: (layer, e, 0, 0))
    return pl.pallas_call(
        functools.partial(_ffn_kernel, cap_l=cap_l),
        grid=(E, B),
        in_specs=[pl.BlockSpec((1, 1, cap_t, hw), lambda e, b: (b, e, 0, 0)), wspec, wspec, wspec,
                  pl.BlockSpec((1, 1, cap_t, 1), lambda e, b: (b, e, 0, 0)),
                  pl.BlockSpec((1, 2, D), lambda e, b: (b, 0, 0))],
        out_specs=pl.BlockSpec((1, 1, cap_t, D), lambda e, b: (b, e, 0, 0)),
        out_shape=jax.ShapeDtypeStruct((B, E, cap_t, D), F32),
        scratch_shapes=[pltpu.VMEM((D, D), BF16)] * 3,
        compiler_params=_cparams(("arbitrary", "arbitrary")),
        name="expert_ffn",
    )(xe, wg, wu, wd, gate, gt2)


def _scatter_kernel(idx_ref, y_ref, *rest, cap_t):
    o_ref = rest[-1]
    b, e = pl.program_id(0), pl.program_id(1)
    base = (b * N_EXPERTS + e) * cap_t

    @pl.when(e == 0)
    def _():
        o_ref[...] = jnp.zeros_like(o_ref)

    def body(g, _):
        tile = y_ref[0, 0, pl.ds(pl.multiple_of(g * 8, 8), 8), :]
        rows = [idx_ref[base + g * 8 + k] for k in range(8)]
        cur = [o_ref[0, pl.ds(r, 1), :] for r in rows]
        for k, r in enumerate(rows):
            o_ref[0, pl.ds(r, 1), :] = cur[k] + tile[k:k + 1, :]
        return 0

    lax.fori_loop(0, cap_t // 8, body, 0)


def _scatter(idx_flat, ye, acc_buf, n_rows, b0, bg, cap_t):
    B, E, _, D = ye.shape
    in_specs = [pl.BlockSpec((1, 1, cap_t, D), lambda b, e, idx: (b0 + b, e, 0, 0))]
    args = [idx_flat, ye]
    aliases = {}
    if acc_buf is not None:
        in_specs.append(pl.BlockSpec(memory_space=pl.ANY))
        args.append(acc_buf)
        aliases = {2: 0}
    return pl.pallas_call(
        functools.partial(_scatter_kernel, cap_t=cap_t),
        grid_spec=pltpu.PrefetchScalarGridSpec(
            num_scalar_prefetch=1,
            grid=(bg, N_EXPERTS),
            in_specs=in_specs,
            out_specs=pl.BlockSpec((1, n_rows, D), lambda b, e, idx: (b0 + b, 0, 0)),
        ),
        out_shape=jax.ShapeDtypeStruct((B, n_rows, D), F32),
        input_output_aliases=aliases,
        compiler_params=_cparams(("arbitrary", "arbitrary")),
        name="expert_scatter",
    )(*args)


def _residual_out_kernel(x_ref, f_ref, o_ref):
    o_ref[...] = x_ref[...] + f_ref[...]


def _residual_out(x, f, T):
    B, N, D = x.shape
    rb = ROW_BLOCK
    spec = pl.BlockSpec((1, rb, D), lambda b, r: (b, r, 0))
    return pl.pallas_call(
        _residual_out_kernel,
        grid=(B, T // rb),
        in_specs=[spec, spec],
        out_specs=spec,
        out_shape=jax.ShapeDtypeStruct((B, T, D), F32),
        compiler_params=_cparams(("arbitrary", "arbitrary")),
        name="residual_out",
    )(x, f)


def _rope_tables(T, S, dh):
    rows = T // GRID_W
    r = jnp.broadcast_to(jnp.arange(rows, dtype=F32)[:, None], (rows, GRID_W)).reshape(T)
    col = jnp.broadcast_to(jnp.arange(GRID_W, dtype=F32)[None, :], (rows, GRID_W)).reshape(T)
    nf = dh // 4
    inv = ROPE_BASE ** (-jnp.arange(nf, dtype=F32) / nf)
    ar, ac = r[:, None] * inv, col[:, None] * inv
    ang = jnp.concatenate([ar, ar, ac, ac], axis=1)
    ang = jnp.concatenate([ang, jnp.zeros((S, dh), F32)], axis=0)
    reps = 256 // dh
    return jnp.tile(jnp.cos(ang), (1, reps)), jnp.tile(jnp.sin(ang), (1, reps))


def _group_mats(dh):
    i = np.arange(256)
    gmat = (i[:, None] // dh == i[None, :] // dh).astype(np.float32) / dh
    nf = dh // 4
    half = (i % (2 * nf)) // nf
    pmat = np.zeros((256, 256), np.float32)
    a_idx = i[half == 0]
    pmat[a_idx + nf, a_idx] = -1.0
    pmat[a_idx, a_idx + nf] = 1.0
    return jnp.asarray(gmat, BF16), jnp.asarray(pmat, BF16)


def _hyena_tables(ls):
    t = jnp.linspace(0.0, 1.0, ls, dtype=F32)[:, None]
    w = 2.0 * math.pi * jnp.arange(ls, dtype=F32)[:, None] / ls
    bands = jnp.linspace(1e-4, HY_BANDS - 1, HY_BANDS, dtype=F32)
    z = jnp.concatenate([t, jnp.cos(bands * w), -jnp.sin(bands * w)], axis=-1)
    z = jnp.pad(z, ((0, 0), (0, 128 - HY_EMB)))
    deltas = jnp.abs(jnp.linspace(math.log(HY_TARGET) / HY_SLOW, math.log(HY_TARGET) / HY_FAST, HY_W, dtype=F32))
    decay = jnp.exp(-t * deltas)
    n, nseg = 2 * ls, ls // HY_N2
    k1p = -(-(nseg + 1) // 8) * 8
    k1 = jnp.arange(k1p, dtype=I32)[None, :, None]
    tpos = HY_N2 * jnp.arange(nseg, dtype=I32)[None, None, :] + jnp.arange(HY_N2, dtype=I32)[:, None, None]
    ang = ((k1 * tpos) % n).astype(F32) * (2.0 * math.pi / n)
    keep = k1 <= nseg
    fmat = jnp.concatenate([jnp.where(keep, jnp.cos(ang), 0.0), jnp.where(keep, -jnp.sin(ang), 0.0)],
                           axis=1).astype(BF16)
    return z, decay, fmat, jnp.swapaxes(fmat, 1, 2)


def kernel(x, c, ctx, c_ctx, w_ada, b_ada, norm1_g, norm2_g, w_in, b_gate, a_qnorm, a_knorm, a_sink, hy_conv, hy_fw1, hy_fb1, hy_freq, hy_fw2, hy_fb2, hy_fw3, hy_bias, ml_norm, d_qnorm, d_knorm, d_lq1, d_lk1, d_lq2, d_lk2, d_subnorm, w_out, w_router, w_e_gate, w_e_up, w_e_down):
    B, T, D = x.shape
    S = ctx.shape[1]
    N = T + S
    depth = w_ada.shape[0]
    assert D == D_MODEL and T % S == 0 and S % ROW_BLOCK == 0 and S % MLSTM_CHUNK == 0
    assert S % WATTN_QROWS == 0 and T >= WATTN_QROWS + 2 * BLK and WATTN_QROWS & (WATTN_QROWS - 1) == 0
    cap_l, cap_c = (EC_CAPACITY * T) // N_EXPERTS, (EC_CAPACITY * S) // N_EXPERTS
    cap_t = cap_l + cap_c
    assert cap_t % 16 == 0

    xs = jnp.concatenate([x, ctx], axis=1)

    rpad = -(B + 1) % 8
    cc = jnp.concatenate([c, c_ctx[None, :], jnp.zeros((rpad, D), F32)], axis=0)
    mods = _ada_mods(cc, w_ada, b_ada)

    cos_a, sin_a = _rope_tables(T, S, DH_A)
    cos_d, sin_d = _rope_tables(T, S, DH_D)
    g64, p64 = _group_mats(DH_A)
    g32, p32 = _group_mats(DH_D)
    tabs_l = _hyena_tables(T)
    tabs_c = _hyena_tables(S)

    offs = np.cumsum((0, 256, 128, 128, 768, 256, 256, 256, 256, 16, 256, 256, 256))
    ffn_acc = None
    for l in range(depth):
        lam_init = 0.8 - 0.6 * math.exp(-0.3 * l)
        ml = mods[l]
        mod = jnp.stack([ml[:B].reshape(B, 6, D), jnp.broadcast_to(ml[B].reshape(1, 6, D), (B, 6, D))],
                        axis=1)
        w = w_in[l]
        wp = jnp.concatenate([w[:, offs[0]:offs[8]], w[:, offs[8]:offs[9]], jnp.zeros((D, 112), F32),
                              w[:, offs[9]:offs[12]]], axis=1).astype(BF16)
        gains = jnp.stack([jnp.tile(a_qnorm[l], 4), jnp.tile(a_knorm[l], 4),
                           jnp.tile(d_qnorm[l], 8), jnp.tile(d_knorm[l], 8)], axis=0)
        proj = _inproj(xs, ffn_acc, mod, norm1_g[l].reshape(1, D), wp, (cos_a, sin_a, cos_d, sin_d), gains,
                       (g64, g32, p64, p32), T)
        qa, ka, va, ub, kc, oc, gc, qd, kd, vd, gct, qct, vct = proj[0:13]
        if ffn_acc is not None:
            xs = proj[13]

        ya = _wattn(qa, ka, va, jnp.pad(a_sink[l], (0, 128 - H_A)).reshape(1, 128), T, S)
        yd = _dattn(qd, kd, vd, jnp.stack([d_lq1[l], d_lk1[l], d_lq2[l], d_lk2[l]], axis=0),
                    d_subnorm[l].reshape(1, 2 * DH_D), T, S, lam_init)

        bias = b_gate[l].reshape(-1)
        hf, hb = _mlstm(qct, kc, vct, gc, gct, jnp.pad(bias, (0, 112)).reshape(1, 128), bias.reshape(16, 1), T, S)

        uc = _shortconv(ub, hy_conv[l], T, S)
        w1 = jnp.pad(hy_fw1[l], ((0, 128 - HY_EMB), (0, 0)))
        ybs = []
        for (zf, decay, fmat, gmat), ls, rowblk in ((tabs_l, T, 0), (tabs_c, S, T // S)):
            hpast, hfut, ss = _hyfilter(zf, w1, hy_fb1[l].reshape(1, -1), hy_freq[l].reshape(1, -1), hy_fw2[l],
                                        hy_fb2[l].reshape(1, -1), hy_fw3[l], decay)
            hspec = _hyspec(hpast, hfut, ss, fmat)
            z1 = _hyconv(uc, 0, rowblk, uc, 1, rowblk, hspec, 0, fmat, gmat, hy_bias[l, 0].reshape(1, HY_W), ls)
            z2 = _hyconv(z1, 0, 0, uc, 2, rowblk, hspec, 1, fmat, gmat, hy_bias[l, 1].reshape(1, HY_W), ls)
            ybs.append(z2)

        wr = jnp.pad(w_router[l], ((0, 0), (0, 128 - N_EXPERTS)))
        wr_hi = wr.astype(BF16)
        wr_cat = jnp.concatenate([wr_hi, (wr - wr_hi.astype(F32)).astype(BF16)], axis=1)
        xs, h2, aff = _outproj(xs, mod, ya, ybs[0], ybs[1], hf, hb, oc, yd, jnp.tile(ml_norm[l], 4).reshape(1, 256), g64,
                               w_out[l].astype(BF16), norm2_g[l].reshape(1, D), wr_cat, T)

        idx, gate = _topk(aff, T, S)
        gt2 = mod[:, :, 5, :]
        bg = min(IDX_BATCH_GROUP, B)
        groups = [(b0, idx[b0:b0 + bg].reshape(-1)) for b0 in range(0, B, bg)]
        xe = None
        for b0, idx_flat in groups:
            xe = _gather(idx_flat, h2, xe, b0, bg, cap_t)
        ye = _ffn(xe, l, w_e_gate, w_e_up, w_e_down, gate, gt2, cap_l)
        ffn_acc = None
        for b0, idx_flat in groups:
            ffn_acc = _scatter(idx_flat, ye, ffn_acc, N, b0, bg, cap_t)
    return _residual_out(xs, ffn_acc, T)
```

```python
import functools
import math

import jax
import jax.numpy as jnp
import numpy as np
from jax import lax
from jax.experimental import pallas as pl
from jax.experimental.pallas import tpu as pltpu

F32 = jnp.float32
BF16 = jnp.bfloat16
I32 = jnp.int32
HI = lax.Precision.HIGHEST

D_MODEL = 1024
GRID_W = 64
GROUP_W = D_MODEL // 4
H_A, HKV_A = 4, 2
G_A = H_A // HKV_A
DH_A = GROUP_W // H_A
WINDOW = 128
BLK = 128
HY_W = GROUP_W
HY_ORDER = 2
HY_BANDS = 16
HY_EMB = 1 + 2 * HY_BANDS
HY_HID = 64
HY_TARGET, HY_FAST, HY_SLOW = 1e-2, 0.3, 1.5
HY_N2 = 16
H_C = 4
DH_C = GROUP_W // H_C
H_D = 4
DH_D = GROUP_W // (2 * H_D)
N_EXPERTS = 16
EC_CAPACITY = 2
ROPE_BASE = 10000.0
EPS = 1e-6
NEG = -1e30

ROW_BLOCK = 256
WATTN_QROWS = 256
DATTN_MIN_ROWSUM = 1e-25
MLSTM_CHUNK = 256
IDX_BATCH_GROUP = 4
GATHER_EXPERTS = 4
SCATTER_EXPERTS = 2
VMEM_LIMIT = 56 * 1024 * 1024

_C_QA, _C_KA, _C_VA, _C_UB = 0, 256, 384, 512
_C_QC, _C_KC, _C_VC, _C_OC, _C_GC = 1280, 1536, 1792, 2048, 2304
_C_QD, _C_KD, _C_VD, _C_END = 2432, 2688, 2944, 3200


def _cparams(sem, vmem=VMEM_LIMIT):
    return pltpu.CompilerParams(dimension_semantics=sem, vmem_limit_bytes=vmem)


def _dot(a, b, **kw):
    return jnp.dot(a, b, preferred_element_type=F32, **kw)


def _dot_nt(a, b, **kw):
    return lax.dot_general(a, b, (((1,), (1,)), ((), ())), preferred_element_type=F32, **kw)


def _dot_tn(a, b, **kw):
    return lax.dot_general(a, b, (((0,), (0,)), ((), ())), preferred_element_type=F32, **kw)


def _pack_bf16_pair(a, b):
    ua = lax.bitcast_convert_type(a.astype(F32), I32)
    ub = lax.bitcast_convert_type(b.astype(F32), I32)
    return ua | lax.shift_right_logical(ub, 16)


def _unpack_bf16_pair(w):
    a = lax.bitcast_convert_type(w & jnp.int32(-65536), F32)
    b = lax.bitcast_convert_type(lax.shift_left(w, 16), F32)
    return a.astype(BF16), b.astype(BF16)


def _ada_kernel(c_ref, w_ref, b_ref, o_ref):
    c = c_ref[...]
    s = c * jax.nn.sigmoid(c)
    o_ref[0] = _dot(s, w_ref[0], precision=HI) + b_ref[0]


def _ada_mods(cc, w_ada, b_ada):
    L, D, W6 = w_ada.shape
    R = cc.shape[0]
    cb = 1536
    return pl.pallas_call(
        _ada_kernel,
        grid=(L, W6 // cb),
        in_specs=[pl.BlockSpec((R, D), lambda l, j: (0, 0)),
                  pl.BlockSpec((1, D, cb), lambda l, j: (l, 0, j)),
                  pl.BlockSpec((1, 1, cb), lambda l, j: (l, 0, j))],
        out_specs=pl.BlockSpec((1, R, cb), lambda l, j: (l, 0, j)),
        out_shape=jax.ShapeDtypeStruct((L, R, W6), F32),
        compiler_params=_cparams(("arbitrary", "arbitrary")),
        name="ada_mods",
    )(cc, w_ada, b_ada.reshape(L, 1, W6))


def _inproj_kernel(*refs, has_f):
    if has_f:
        x_ref, f_ref, *refs = refs
        *refs, xsum = refs
        x = x_ref[0] + f_ref[0]
        xsum[0] = x
    else:
        x_ref, *refs = refs
        x = x_ref[0]
    (mod_ref, g1_ref, w_ref, ca_ref, sa_ref, cd_ref, sd_ref, gains_ref, g64_ref, g32_ref, p64_ref, p32_ref,
     qa, ka, va, ub, kc, oc, gc, qd, kd, vd, gct, qct, vct) = refs
    ms = jnp.mean(x * x, axis=-1, keepdims=True)
    xn = x * lax.rsqrt(ms + EPS) * g1_ref[...]
    mod = mod_ref[0, 0]
    h = xn * (1.0 + mod[1:2, :]) + mod[0:1, :]
    p = _dot(h.astype(BF16), w_ref[...])

    def headnorm_rope(t, gmat, gain, pmat, cos, sin, scale):
        w = t.shape[1]
        msq = _dot((t * t).astype(BF16), gmat[0:w, 0:w])
        tn = t * lax.rsqrt(msq + EPS) * gain
        tn = tn * cos[:, 0:w] + _dot(tn.astype(BF16), pmat[0:w, 0:w]) * sin[:, 0:w]
        return tn * scale

    ca, sa, cd, sd = ca_ref[...], sa_ref[...], cd_ref[...], sd_ref[...]
    gains = gains_ref[...]
    qa[0] = headnorm_rope(p[:, _C_QA:_C_KA], g64_ref, gains[0:1, :], p64_ref, ca, sa, DH_A ** -0.5).astype(BF16)
    ka[0] = headnorm_rope(p[:, _C_KA:_C_VA], g64_ref, gains[1:2, 0:128], p64_ref, ca, sa, 1.0).astype(BF16)
    va[0] = p[:, _C_VA:_C_UB].astype(BF16)
    ub[0] = p[:, _C_UB:_C_QC].astype(BF16)
    qct[0] = p[:, _C_QC:_C_KC].T.astype(BF16)
    kc[0] = (p[:, _C_KC:_C_VC] * DH_C ** -0.5).astype(BF16)
    vct[0] = p[:, _C_VC:_C_OC].T.astype(BF16)
    oc[0] = p[:, _C_OC:_C_GC].astype(BF16)
    gc[0] = p[:, _C_GC:_C_QD]
    gct[0] = p[:, _C_GC:_C_QD].T
    qd[0] = headnorm_rope(p[:, _C_QD:_C_KD], g32_ref, gains[2:3, :], p32_ref, cd, sd, DH_D ** -0.5).astype(BF16)
    kd[0] = headnorm_rope(p[:, _C_KD:_C_VD], g32_ref, gains[3:4, :], p32_ref, cd, sd, 1.0).astype(BF16)
    vd[0] = p[:, _C_VD:_C_END].astype(BF16)


def _inproj(x, f, mod, g1, w_packed, tabs, gains, mats, T):
    B, N, D = x.shape
    rb = ROW_BLOCK
    nlat = T // rb
    row = lambda w: pl.BlockSpec((1, rb, w), lambda b, r: (b, r, 0))
    tab = pl.BlockSpec((rb, 256), lambda b, r: (r, 0))
    full = lambda a: pl.BlockSpec(a.shape, lambda b, r: (0,) * a.ndim)
    widths = [256, 128, 128, 768, 256, 256, 128, 256, 256, 256]
    dtypes = [BF16] * 6 + [F32] + [BF16] * 3
    tr = lambda w: pl.BlockSpec((1, w, rb), lambda b, r: (b, 0, r))
    has_f = f is not None
    return pl.pallas_call(
        functools.partial(_inproj_kernel, has_f=has_f),
        grid=(B, N // rb),
        in_specs=[row(D)] * (2 if has_f else 1)
        + [pl.BlockSpec((1, 1, 6, D), lambda b, r: (b, jnp.where(r >= nlat, 1, 0), 0, 0)),
           full(g1), full(w_packed), tab, tab, tab, tab, full(gains)] + [full(m) for m in mats],
        out_specs=[row(w) for w in widths] + [tr(128), tr(256), tr(256)] + ([row(D)] if has_f else []),
        out_shape=[jax.ShapeDtypeStruct((B, N, w), dt) for w, dt in zip(widths, dtypes)]
        + [jax.ShapeDtypeStruct((B, 128, N), F32), jax.ShapeDtypeStruct((B, 256, N), BF16),
           jax.ShapeDtypeStruct((B, 256, N), BF16)] + ([jax.ShapeDtypeStruct((B, N, D), F32)] if has_f else []),
        compiler_params=_cparams(("arbitrary", "arbitrary")),
        name="inproj",
    )(x, *([f] if has_f else []), mod, g1, w_packed, *tabs, gains, *mats)


def _wattn_kernel(q_ref, k_ref, v_ref, sink_ref, sinkw_ref, placeq_ref, gsumq_ref, placek_ref, gsumk_ref, o_ref,
                  kaug_ref, vaug_ref, kmax_ref, *, T, S):
    j = pl.program_id(1)
    qb, kwin = WATTN_QROWS, WATTN_QROWS + 2 * BLK
    is_lat = j < T // qb
    start = pl.multiple_of(jnp.clip(j * qb - BLK, 0, T - kwin), BLK)
    row = lax.broadcasted_iota(I32, (G_A * qb, kwin), 0)
    qpos = j * qb + (row & (qb - 1))
    kpos = start + lax.broadcasted_iota(I32, (G_A * qb, kwin), 1)
    valid = (jnp.abs(qpos - kpos) <= WINDOW) & is_lat
    grp = jnp.right_shift(lax.broadcasted_iota(I32, (G_A * qb, 1), 0), qb.bit_length() - 1)
    sinks = sink_ref[...]

    @pl.when(j == 0)
    def _():
        lane = lax.broadcasted_iota(I32, (1, HKV_A * 128), 1)
        one = jnp.where((lane & 127) == DH_A, 1.0, 0.0)
        k_wide = _dot(k_ref[0], placek_ref[...])
        kaug_ref[...] = (k_wide + one).astype(BF16)
        vaug_ref[...] = (_dot(v_ref[0], placek_ref[...]) + one).astype(BF16)
        ksq = _dot((k_wide * k_wide).astype(BF16), gsumk_ref[...])
        kmax_ref[...] = jnp.sqrt(jnp.max(ksq, axis=0, keepdims=True))

    def sink_rows(heads):
        sink = sinks[0:1, heads[0]:heads[0] + 1]
        for g in range(1, G_A):
            sink = jnp.where(grp == g, sinks[0:1, heads[g]:heads[g] + 1], sink)
        return sink

    def store(heads, o):
        for g, hd in enumerate(heads):
            o_ref[0, :, hd * DH_A:(hd + 1) * DH_A] = o[g * qb:(g + 1) * qb].astype(BF16)

    def body_fast():
        q_wide = _dot(q_ref[0], placeq_ref[...])
        qn = jnp.sqrt(_dot((q_wide * q_wide).astype(BF16), gsumq_ref[...]))
        km = kmax_ref[...]
        kmq = jnp.concatenate([km[:, (hd // G_A) * 128:(hd // G_A + 1) * 128] for hd in range(H_A)], axis=1)
        lane = lax.broadcasted_iota(I32, (1, H_A * 128), 1)
        shift = jnp.where((lane & 127) == DH_A, jnp.maximum(qn * kmq, sinkw_ref[...]), 0.0)
        q_aug = (q_wide - shift).astype(BF16)
        low = None
        for hk in range(HKV_A):
            heads = [hk * G_A + g for g in range(G_A)]
            ks = slice(hk * 128, (hk + 1) * 128)
            q2 = jnp.concatenate([q_aug[:, hd * 128:(hd + 1) * 128] for hd in heads], axis=0)
            m_used = -q2[:, DH_A:DH_A + 1].astype(F32)
            s_loc = jnp.where(valid, _dot_nt(q2, kaug_ref[pl.ds(start, kwin), ks]), NEG)
            s_ctx = _dot_nt(q2, kaug_ref[T:T + S, ks])
            pv = (_dot(jnp.exp(s_loc.astype(BF16)), vaug_ref[pl.ds(start, kwin), ks])
                  + _dot(jnp.exp(s_ctx.astype(BF16)), vaug_ref[T:T + S, ks]))
            sigma = pv[:, DH_A:DH_A + 1]
            low = sigma if low is None else jnp.minimum(low, sigma)
            store(heads, pv[:, 0:DH_A] / (sigma + jnp.exp(sink_rows(heads) - m_used)))
        return jnp.min(low)

    def body_exact():
        kw = k_ref[0, pl.ds(start, kwin), :]
        vw = v_ref[0, pl.ds(start, kwin), :]
        kc = k_ref[0, T:T + S, :]
        vc = v_ref[0, T:T + S, :]
        for hk in range(HKV_A):
            cs = slice(hk * DH_A, (hk + 1) * DH_A)
            kwh, vwh, kch, vch = kw[:, cs], vw[:, cs], kc[:, cs], vc[:, cs]
            heads = [hk * G_A + g for g in range(G_A)]
            q = jnp.concatenate([q_ref[0, :, hd * DH_A:(hd + 1) * DH_A] for hd in heads], axis=0)
            sink = sink_rows(heads)
            s_loc = jnp.where(valid, _dot_nt(q, kwh), NEG)
            s_ctx = _dot_nt(q, kch)
            m = jnp.maximum(jnp.maximum(jnp.max(s_loc, axis=-1, keepdims=True),
                                        jnp.max(s_ctx, axis=-1, keepdims=True)), sink)
            p_loc = jnp.exp(s_loc - m)
            p_ctx = jnp.exp(s_ctx - m)
            den = (jnp.sum(p_loc, axis=-1, keepdims=True) + jnp.sum(p_ctx, axis=-1, keepdims=True)
                   + jnp.exp(sink - m))
            store(heads, (_dot(p_loc.astype(BF16), vwh) + _dot(p_ctx.astype(BF16), vch)) / den)

    low = body_fast()

    @pl.when(jnp.logical_not(low > DATTN_MIN_ROWSUM))
    def _():
        body_exact()


def _wattn_mats(nheads):
    place = np.zeros((nheads * DH_A, nheads * 128), np.float32)
    gsum = np.zeros((nheads * 128, nheads * 128), np.float32)
    for h in range(nheads):
        for d in range(DH_A):
            place[h * DH_A + d, h * 128 + d] = 1.0
            gsum[h * 128 + d, h * 128 + DH_A] = 1.0
    return jnp.asarray(place, BF16), jnp.asarray(gsum, BF16)


def _wattn(qa, ka, va, sink, T, S):
    B, N, _ = qa.shape
    qb = WATTN_QROWS
    placeq, gsumq = _wattn_mats(H_A)
    placek, gsumk = _wattn_mats(HKV_A)
    sinkw = jnp.zeros((1, H_A * 128), F32).at[0, DH_A::128].set(sink[0, 0:H_A])
    full = lambda a: pl.BlockSpec(a.shape, lambda b, j: (0,) * a.ndim)
    return pl.pallas_call(
        functools.partial(_wattn_kernel, T=T, S=S),
        grid=(B, N // qb),
        in_specs=[pl.BlockSpec((1, qb, 256), lambda b, j: (b, j, 0)),
                  pl.BlockSpec((1, N, 128), lambda b, j: (b, 0, 0)),
                  pl.BlockSpec((1, N, 128), lambda b, j: (b, 0, 0)),
                  pl.BlockSpec((1, 128), lambda b, j: (0, 0)),
                  full(sinkw), full(placeq), full(gsumq), full(placek), full(gsumk)],
        out_specs=pl.BlockSpec((1, qb, 256), lambda b, j: (b, j, 0)),
        out_shape=jax.ShapeDtypeStruct((B, N, 256), BF16),
        scratch_shapes=[pltpu.VMEM((N, HKV_A * 128), BF16), pltpu.VMEM((N, HKV_A * 128), BF16),
                        pltpu.VMEM((1, HKV_A * 128), F32)],
        compiler_params=_cparams(("arbitrary", "arbitrary")),
        name="window_attn",
    )(qa, ka, va, sink, sinkw, placeq, gsumq, placek, gsumk)


def _dattn_kernel(q_ref, k_ref, v_ref, lam_ref, sub_ref, place_ref, gsum_ref, o_ref,
                  vaug_ref, kaug_ref, kmax_ref, kch_ref, qch_ref, tch_ref, *, T, S, lam_init):
    j = pl.program_id(1)
    lv = lam_ref[...]
    lam = (jnp.exp(jnp.sum(lv[0:1, :] * lv[1:2, :], axis=-1, keepdims=True))
           - jnp.exp(jnp.sum(lv[2:3, :] * lv[3:4, :], axis=-1, keepdims=True)) + lam_init)
    subg = sub_ref[...] * (1.0 - lam_init)
    dv = 2 * DH_D
    nchain, aw = 2 * H_D, 2 * DH_D

    @pl.when(j == 0)
    def _():
        ones = jnp.where(lax.broadcasted_iota(I32, (T + S, dv), 1) == 0, 1.0, 0.0).astype(BF16)
        for hd in range(H_D):
            vaug_ref[hd] = jnp.concatenate([v_ref[0, :, hd * dv:(hd + 1) * dv], ones], axis=1)
        k_wide = _dot(k_ref[0], place_ref[...])
        lane = lax.broadcasted_iota(I32, (1, nchain * aw), 1)
        k_aug = (k_wide + jnp.where((lane & (aw - 1)) == DH_D, 1.0, 0.0)).astype(BF16)
        kaug_ref[...] = k_aug
        for i in range(nchain):
            kch_ref[i] = k_aug[:, i * aw:(i + 1) * aw]
        ksq = _dot((k_wide * k_wide).astype(BF16), gsum_ref[...])
        kmax_ref[...] = jnp.sqrt(jnp.max(ksq, axis=0, keepdims=True))

    def finish(hd, terms):
        o = terms[0] - lam * terms[1]
        ms = jnp.mean(o * o, axis=-1, keepdims=True)
        o_ref[0, :, hd * dv:(hd + 1) * dv] = (o * lax.rsqrt(ms + EPS) * subg).astype(BF16)

    def body_fast(k0, nk):
        q_wide = _dot(q_ref[0], place_ref[...])
        qn = jnp.sqrt(_dot((q_wide * q_wide).astype(BF16), gsum_ref[...]))
        q_aug = (q_wide - qn * kmax_ref[...]).astype(BF16)
        terms, low = [], None
        for i in range(nchain):
            hd, cs = i // 2, slice(i * aw, (i + 1) * aw)
            s = _dot_nt(q_aug[:, cs], kaug_ref[k0:k0 + nk, cs])
            pv = _dot(jnp.exp(s.astype(BF16)), vaug_ref[hd, k0:k0 + nk, :])
            sigma = pv[:, dv:dv + 1]
            low = sigma if low is None else jnp.minimum(low, sigma)
            terms.append(pv[:, 0:dv] / sigma)
            if i % 2 == 1:
                finish(hd, terms)
                terms = []
        return jnp.min(low)

    def body_exact(k0, nk):
        q_wide = _dot(q_ref[0], place_ref[...])
        for i in range(nchain):
            qch_ref[i] = q_wide[:, i * aw:(i + 1) * aw].astype(BF16)

        def chain(i, carry):
            s = _dot_nt(qch_ref[i], kch_ref[i, k0:k0 + nk, :])
            e = jnp.exp((s - jnp.max(s, axis=-1, keepdims=True)).astype(BF16))
            pv = _dot(e, vaug_ref[jnp.right_shift(i, 1), k0:k0 + nk, :])
            tch_ref[i] = pv / pv[:, dv:dv + 1]
            return carry

        lax.fori_loop(0, nchain, chain, 0)
        for hd in range(H_D):
            finish(hd, [tch_ref[2 * hd][:, 0:dv], tch_ref[2 * hd + 1][:, 0:dv]])

    def body(k0, nk):
        low = body_fast(k0, nk)

        @pl.when(jnp.logical_not(low > DATTN_MIN_ROWSUM))
        def _():
            body_exact(k0, nk)

    @pl.when(j < T // ROW_BLOCK)
    def _():
        body(0, T + S)

    @pl.when(j >= T // ROW_BLOCK)
    def _():
        body(T, S)


def _dattn_mats():
    nchain, aw = 2 * H_D, 2 * DH_D
    place = np.zeros((nchain * DH_D, nchain * aw), np.float32)
    gsum = np.zeros((nchain * aw, nchain * aw), np.float32)
    for i in range(nchain):
        for d in range(DH_D):
            place[i * DH_D + d, i * aw + d] = 1.0
            gsum[i * aw + d, i * aw + DH_D] = 1.0
    return jnp.asarray(place, BF16), jnp.asarray(gsum, BF16)


def _dattn(qd, kd, vd, lam_vecs, subg, T, S, lam_init):
    B, N, _ = qd.shape
    rb = ROW_BLOCK
    place, gsum = _dattn_mats()
    full = lambda a: pl.BlockSpec(a.shape, lambda b, j: (0,) * a.ndim)
    return pl.pallas_call(
        functools.partial(_dattn_kernel, T=T, S=S, lam_init=lam_init),
        grid=(B, N // rb),
        in_specs=[pl.BlockSpec((1, rb, 256), lambda b, j: (b, j, 0)),
                  pl.BlockSpec((1, N, 256), lambda b, j: (b, 0, 0)),
                  pl.BlockSpec((1, N, 256), lambda b, j: (b, 0, 0)),
                  pl.BlockSpec((4, DH_D), lambda b, j: (0, 0)),
                  pl.BlockSpec((1, 2 * DH_D), lambda b, j: (0, 0)), full(place), full(gsum)],
        out_specs=pl.BlockSpec((1, rb, 256), lambda b, j: (b, j, 0)),
        out_shape=jax.ShapeDtypeStruct((B, N, 256), BF16),
        scratch_shapes=[pltpu.VMEM((H_D, N, 4 * DH_D), BF16), pltpu.VMEM((N, 2 * 256), BF16),
                        pltpu.VMEM((1, 2 * 256), F32), pltpu.VMEM((2 * H_D, N, 2 * DH_D), BF16),
                        pltpu.VMEM((2 * H_D, rb, 2 * DH_D), BF16), pltpu.VMEM((2 * H_D, rb, 4 * DH_D), F32)],
        compiler_params=_cparams(("arbitrary", "arbitrary")),
        name="diff_attn",
    )(qd, kd, vd, lam_vecs, subg, place, gsum)


def _log_sigmoid(x):
    return jnp.minimum(x, 0.0) - jnp.log1p(jnp.exp(-jnp.abs(x)))


def _dot3(a, b, split_b):
    x = b if split_b else a
    hi = x.astype(BF16)
    r1 = x - hi.astype(F32)
    mid = r1.astype(BF16)
    lo = (r1 - mid.astype(F32)).astype(BF16)
    if split_b:
        a = a.astype(BF16)
        return _dot(a, hi) + _dot(a, mid) + _dot(a, lo)
    b = b.astype(BF16)
    return _dot(hi, b) + _dot(mid, b) + _dot(lo, b)


def _mlstm_kernel(qtf_ref, kf_ref, vtf_ref, gf_ref, gtf_ref, qtb_ref, kb_ref, vtb_ref, gb_ref, gtb_ref,
                  bias_ref, biast_ref, hf_ref, hb_ref, c_ref, m_ref):
    i = pl.program_id(1)
    lc = MLSTM_CHUNK

    @pl.when(i == 0)
    def _():
        c_ref[...] = jnp.zeros_like(c_ref)
        m_ref[...] = jnp.zeros_like(m_ref)

    r_io = lax.broadcasted_iota(I32, (lc, lc), 0)
    c_io = lax.broadcasted_iota(I32, (lc, lc), 1)
    lower = r_io >= c_io
    upper = r_io <= c_io
    tri_lo = jnp.where(lower, 1.0, 0.0).astype(F32)
    tri_up = jnp.where(upper, 1.0, 0.0).astype(F32)
    ones_rows = jnp.where(lax.broadcasted_iota(I32, (DH_C, lc), 0) == 0, 1.0, 0.0).astype(BF16)

    def direction(d, qt_ref, k_ref, vt_ref, g_ref, gt_ref, out_ref):
        pre = g_ref[0] + bias_ref[...]
        pre_t = gt_ref[0] + biast_ref[...]
        lf = _log_sigmoid(pre)
        lf_t = _log_sigmoid(pre_t)
        if d == 0:
            bcol = _dot3(tri_lo, lf, True)
            brow = _dot3(lf_t, tri_up, False)
            valid, end = upper, lc - 1
        else:
            bcol = _dot3(tri_up, lf, True)
            brow = _dot3(lf_t, tri_lo, False)
            valid, end = lower, 0
        for hd in range(H_C):
            ci, cf = d * H_C + hd, (2 + d) * H_C + hd
            rs = slice(hd * DH_C, (hd + 1) * DH_C)
            q_t = qt_ref[0, rs, :]
            k = k_ref[0, :, rs]
            v_aug = jnp.concatenate([vt_ref[0, rs, :], ones_rows], axis=0)
            b_row = brow[cf:cf + 1, :]
            ig_row = pre_t[ci:ci + 1, :]
            bi_col = bcol[:, cf:cf + 1] - pre[:, ci:ci + 1]
            b_end = brow[cf:cf + 1, end:end + 1]
            slot = d * H_C + hd
            c = c_ref[slot]
            m = m_ref[slot][:, 0:1]
            logw = jnp.where(valid, b_row - bi_col, NEG)
            inter = b_row + m
            m_t = jnp.maximum(inter, jnp.max(logw, axis=0, keepdims=True))
            w_in = jnp.exp(inter - m_t)
            p_t = (_dot(k, q_t) * jnp.exp(logw - m_t)).astype(BF16)
            num = w_in * _dot(c.astype(BF16), q_t) + _dot(v_aug, p_t)
            den = jnp.maximum(jnp.abs(num[DH_C:DH_C + 1, :]), jnp.exp(-m_t))
            out_ref[0, rs, :] = num[0:DH_C, :] / den
            g_end = b_end - b_row + ig_row
            m_new = jnp.maximum(b_end + m, jnp.max(g_end, axis=1, keepdims=True))
            wv = (v_aug.astype(F32) * jnp.exp(g_end - m_new)).astype(BF16)
            c_ref[slot] = jnp.exp(b_end + m - m_new) * c + _dot(wv, k)
            m_ref[slot] = jnp.broadcast_to(m_new, (1, 128))

    direction(0, qtf_ref, kf_ref, vtf_ref, gf_ref, gtf_ref, hf_ref)
    direction(1, qtb_ref, kb_ref, vtb_ref, gb_ref, gtb_ref, hb_ref)


def _mlstm(qct, kc, vct, gc, gct, bias, biast, T, S):
    B, N, _ = kc.shape
    lc = MLSTM_CHUNK
    nch, nlat = N // lc, T // lc
    fwd = lambda i: jnp.where(i < nch - nlat, nlat + i, i - (nch - nlat))
    bwd = lambda i: nch - 1 - i
    tok = lambda f: pl.BlockSpec((1, lc, 256), lambda b, i: (b, f(i), 0))
    tok_t = lambda f: pl.BlockSpec((1, 256, lc), lambda b, i: (b, 0, f(i)))
    gate = lambda f: pl.BlockSpec((1, lc, 128), lambda b, i: (b, f(i), 0))
    gate_t = lambda f: pl.BlockSpec((1, 16, lc), lambda b, i: (b, 0, f(i)))
    return pl.pallas_call(
        _mlstm_kernel,
        grid=(B, nch),
        in_specs=[tok_t(fwd), tok(fwd), tok_t(fwd), gate(fwd), gate_t(fwd),
                  tok_t(bwd), tok(bwd), tok_t(bwd), gate(bwd), gate_t(bwd),
                  pl.BlockSpec((1, 128), lambda b, i: (0, 0)),
                  pl.BlockSpec((16, 1), lambda b, i: (0, 0))],
        out_specs=[tok_t(fwd), tok_t(bwd)],
        out_shape=[jax.ShapeDtypeStruct((B, 256, N), F32)] * 2,
        scratch_shapes=[pltpu.VMEM((2 * H_C, 2 * DH_C, DH_C), F32), pltpu.VMEM((2 * H_C, 1, 128), F32)],
        compiler_params=_cparams(("arbitrary", "arbitrary")),
        name="mlstm",
    )(qct, kc, vct, gc, gct, qct, kc, vct, gc, gct, bias, biast)


def _shortconv_kernel(u_ref, w_ref, o_ref, *, T, S):
    u = u_ref[0].astype(F32)
    n = u.shape[0]
    w = w_ref[...]
    row = lax.broadcasted_iota(I32, u.shape, 0)
    zero = jnp.zeros((1, u.shape[1]), F32)
    prev = jnp.concatenate([zero, u[0:n - 1]], axis=0)
    nxt = jnp.concatenate([u[1:n], zero], axis=0)
    prev = jnp.where(row == T, 0.0, prev)
    nxt = jnp.where(row == T - 1, 0.0, nxt)
    o_ref[0] = (w[0:1, :] * prev + w[1:2, :] * u + w[2:3, :] * nxt).astype(BF16)


def _shortconv(ub, conv_w, T, S):
    B, N, W = ub.shape
    return pl.pallas_call(
        functools.partial(_shortconv_kernel, T=T, S=S),
        grid=(B, W // 256),
        in_specs=[pl.BlockSpec((1, N, 256), lambda b, j: (b, 0, j)),
                  pl.BlockSpec((3, 256), lambda b, j: (0, j))],
        out_specs=pl.BlockSpec((1, N, 256), lambda b, j: (b, 0, j)),
        out_shape=jax.ShapeDtypeStruct((B, N, W), BF16),
        compiler_params=_cparams(("arbitrary", "arbitrary")),
        name="hyena_shortconv",
    )(ub, conv_w)


def _hyfilter_kernel(z_ref, w1_ref, b1_ref, fr_ref, w2_ref, b2_ref, w3_ref, dec_ref,
                     hp_ref, hf_ref, ss_ref):
    i = pl.program_id(0)
    fr = fr_ref[...]
    h = jnp.sin(fr * (_dot(z_ref[...], w1_ref[...], precision=HI) + b1_ref[...]))
    h = jnp.sin(fr * (_dot(h, w2_ref[...], precision=HI) + b2_ref[...]))
    h = _dot(h, w3_ref[...], precision=HI)
    dec = dec_ref[...]
    dec2 = jnp.concatenate([dec, dec], axis=1)
    hw = HY_ORDER * HY_W
    h0 = h[:, 0:hw] * dec2
    h1 = h[:, hw:2 * hw] * dec2
    rows = h0.shape[0]
    t = i * rows + lax.broadcasted_iota(I32, h0.shape, 0)
    h1 = jnp.where(t == 0, 0.0, h1)
    hp_ref[...] = h0
    hf_ref[...] = h1

    @pl.when(i == 0)
    def _():
        ss_ref[...] = jnp.zeros_like(ss_ref)

    ss_ref[...] += jnp.sum(h0 * h0 + h1 * h1, axis=0, keepdims=True)


def _hyfilter(zfeat, w1, b1, fr, w2, b2, w3, decay):
    ls = zfeat.shape[0]
    rb = min(ls, 512)
    hw = HY_ORDER * HY_W
    full = lambda a: pl.BlockSpec(a.shape, lambda i: (0,) * a.ndim)
    return pl.pallas_call(
        _hyfilter_kernel,
        grid=(ls // rb,),
        in_specs=[pl.BlockSpec((rb, 128), lambda i: (i, 0)), full(w1), full(b1), full(fr), full(w2),
                  full(b2), full(w3), pl.BlockSpec((rb, HY_W), lambda i: (i, 0))],
        out_specs=[pl.BlockSpec((rb, hw), lambda i: (i, 0)), pl.BlockSpec((rb, hw), lambda i: (i, 0)),
                   pl.BlockSpec((1, hw), lambda i: (0, 0))],
        out_shape=[jax.ShapeDtypeStruct((ls, hw), F32), jax.ShapeDtypeStruct((ls, hw), F32),
                   jax.ShapeDtypeStruct((1, hw), F32)],
        compiler_params=_cparams(("arbitrary",)),
        name="hyena_filter",
    )(zfeat, w1, b1, fr, w2, b2, w3, decay)


def _cmul_const(v, c, s):
    vr, vi = v
    r = math.sqrt(0.5)
    if abs(s) < 1e-9:
        return (vr, vi) if c > 0 else (-vr, -vi)
    if abs(c) < 1e-9:
        return (-vi, vr) if s > 0 else (vi, -vr)
    if abs(abs(c) - r) < 1e-9 and abs(abs(s) - r) < 1e-9:
        a, b = (vr if c > 0 else -vr), (vi if s > 0 else -vi)
        p, q = (vi if c > 0 else -vi), (vr if s > 0 else -vr)
        return (a - b) * r, (p + q) * r
    return c * vr - s * vi, c * vi + s * vr


def _fft_pow2(xs, sign):
    n = len(xs)
    if n == 1:
        return xs
    ev, od = _fft_pow2(xs[0::2], sign), _fft_pow2(xs[1::2], sign)
    out = [None] * n
    for k in range(n // 2):
        ang = sign * 2.0 * math.pi * k / n
        tr, ti = _cmul_const(od[k], math.cos(ang), math.sin(ang))
        out[k] = (ev[k][0] + tr, ev[k][1] + ti)
        out[k + n // 2] = (ev[k][0] - tr, ev[k][1] - ti)
    return out


def _hy_stage1(src_ref, f_ref, a_ref, nseg):
    k1p = a_ref.shape[2]
    for n2 in range(HY_N2):
        if len(src_ref.shape) == 3:
            zs = jnp.concatenate([src_ref[t, pl.ds(n2, nseg, stride=HY_N2), :] for t in range(src_ref.shape[0])],
                                 axis=1).astype(BF16)
        else:
            zs = src_ref[pl.ds(n2, nseg, stride=HY_N2), :].astype(BF16)
        r = _dot(f_ref[n2], zs)
        a_ref[0, n2] = r[0:k1p]
        a_ref[1, n2] = r[k1p:2 * k1p]


def _hy_chunks(a_ref):
    k1p, lanes = a_ref.shape[2], a_ref.shape[3]
    return k1p // 8, [slice(t * 128, (t + 1) * 128) for t in range(lanes // 128)]


def _hyspec_kernel(hp_ref, hf_ref, ss_ref, f_ref, h_ref, ap_ref, af_ref, *, ls):
    nseg = ls // HY_N2
    _hy_stage1(hp_ref, f_ref, ap_ref, nseg)
    _hy_stage1(hf_ref, f_ref, af_ref, nseg)
    nchunk, lane_tiles = _hy_chunks(ap_ref)

    def body(i, _):
        rows = pl.ds(pl.multiple_of(i * 8, 8), 8)
        k1 = i * 8 + lax.broadcasted_iota(I32, (8, 1), 0)
        wk = jnp.where((k1 == 0) | (k1 == nseg), 1.0, 2.0) * (1.0 / (2 * ls))
        for ln in lane_tiles:
            scale = lax.rsqrt(ss_ref[:, ln] + EPS) * wk
            P = _fft_pow2([(ap_ref[0, n2, rows, ln], ap_ref[1, n2, rows, ln]) for n2 in range(HY_N2)], -1)
            Q = _fft_pow2([(af_ref[0, n2, rows, ln], af_ref[1, n2, rows, ln]) for n2 in range(HY_N2)], -1)
            for k2 in range(HY_N2):
                h_ref[0, 0, k2, rows, ln] = (P[k2][0] + Q[k2][0]) * scale
                h_ref[0, 1, k2, rows, ln] = (P[k2][1] - Q[k2][1]) * scale
        return 0

    lax.fori_loop(0, nchunk, body, 0)


def _hyspec(hp, hf, ss, fmat):
    ls = hp.shape[0]
    k1p = fmat.shape[1] // 2
    full = lambda a: pl.BlockSpec(a.shape, lambda o, c: (0,) * a.ndim)
    nc = HY_W // 128
    col = lambda o, c: (0, o * nc + c)
    return pl.pallas_call(
        functools.partial(_hyspec_kernel, ls=ls),
        grid=(HY_ORDER, nc),
        in_specs=[pl.BlockSpec((ls, 128), col), pl.BlockSpec((ls, 128), col), pl.BlockSpec((1, 128), col),
                  full(fmat)],
        out_specs=pl.BlockSpec((1, 2, HY_N2, k1p, 128), lambda o, c: (o, 0, 0, 0, c)),
        out_shape=jax.ShapeDtypeStruct((HY_ORDER, 2, HY_N2, k1p, HY_W), F32),
        scratch_shapes=[pltpu.VMEM((2, HY_N2, k1p, 128), F32)] * 2,
        compiler_params=_cparams(("arbitrary", "arbitrary")),
        name="hyena_spectrum",
    )(hp, hf, ss, fmat)


def _hyconv_kernel(z_ref, gate_ref, h_ref, f_ref, g_ref, bias_ref, o_ref, zf_ref, yf_ref, a_ref, *, ls):
    nseg = ls // HY_N2
    ntile = zf_ref.shape[0]
    for t in range(ntile):
        zf_ref[t] = z_ref[0, :, t * 128:(t + 1) * 128].astype(F32)
    _hy_stage1(zf_ref, f_ref, a_ref, nseg)
    nchunk, lane_tiles = _hy_chunks(a_ref)

    def body(i, _):
        rows = pl.ds(pl.multiple_of(i * 8, 8), 8)
        for ln in lane_tiles:
            X = _fft_pow2([(a_ref[0, n2, rows, ln], a_ref[1, n2, rows, ln]) for n2 in range(HY_N2)], -1)
            Y = []
            for k2 in range(HY_N2):
                hr, hi = h_ref[0, 0, k2, rows, ln], h_ref[0, 1, k2, rows, ln]
                xr, xi = X[k2]
                Y.append((xr * hr - xi * hi, xr * hi + xi * hr))
            Bv = _fft_pow2(Y, 1)
            for n2 in range(HY_N2):
                a_ref[0, n2, rows, ln] = Bv[n2][0]
                a_ref[1, n2, rows, ln] = Bv[n2][1]
        return 0

    lax.fori_loop(0, nchunk, body, 0)
    for n2 in range(HY_N2):
        bb = jnp.concatenate([a_ref[0, n2], a_ref[1, n2]], axis=0).astype(BF16)
        yv = _dot(g_ref[n2], bb)
        for t in range(ntile):
            yf_ref[t, pl.ds(n2, nseg, stride=HY_N2), :] = yv[:, t * 128:(t + 1) * 128]
    for t in range(ntile):
        ln = slice(t * 128, (t + 1) * 128)
        o_ref[0, :, ln] = (gate_ref[0, :, ln].astype(F32)
                           * (yf_ref[t] + zf_ref[t] * bias_ref[:, ln])).astype(BF16)


def _hyconv(zsrc, zcol, zrow, gsrc, gcol, grow, hspec, order, fmat, gmat, bias, ls):
    B = zsrc.shape[0]
    k1p = fmat.shape[1] // 2
    once = lambda a: pl.BlockSpec(a.shape, lambda b: (0,) * a.ndim, pipeline_mode=pl.Buffered(1))
    return pl.pallas_call(
        functools.partial(_hyconv_kernel, ls=ls),
        grid=(B,),
        in_specs=[pl.BlockSpec((1, ls, HY_W), lambda b: (b, zrow, zcol)),
                  pl.BlockSpec((1, ls, HY_W), lambda b: (b, grow, gcol)),
                  pl.BlockSpec((1, 2, HY_N2, k1p, HY_W), lambda b: (order, 0, 0, 0, 0),
                               pipeline_mode=pl.Buffered(1)),
                  once(fmat), once(gmat), pl.BlockSpec((1, HY_W), lambda b: (0, 0))],
        out_specs=pl.BlockSpec((1, ls, HY_W), lambda b: (b, 0, 0)),
        out_shape=jax.ShapeDtypeStruct((B, ls, HY_W), BF16),
        scratch_shapes=[pltpu.VMEM((HY_W // 128, ls, 128), F32), pltpu.VMEM((HY_W // 128, ls, 128), F32),
                        pltpu.VMEM((2, HY_N2, k1p, HY_W), F32)],
        compiler_params=_cparams(("arbitrary",)),
        name="hyena_longconv",
    )(zsrc, gsrc, hspec, fmat, gmat, bias)


def _outproj_kernel(x_ref, mod_ref, ya_ref, ybl_ref, ybc_ref, hf_ref, hb_ref, oc_ref, yd_ref, mlg_ref, g64_ref,
                    w_ref, g2_ref, wr_ref, xo_ref, h2_ref, aff_ref, *, nlat):
    hsum = (hf_ref[0] + hb_ref[0]).T
    msq = _dot((hsum * hsum).astype(BF16), g64_ref[...])
    yc = jax.nn.sigmoid(oc_ref[0].astype(F32)) * (hsum * lax.rsqrt(msq + EPS) * mlg_ref[...])
    yb = jnp.where(pl.program_id(1) < nlat, ybl_ref[0], ybc_ref[0])
    y = jnp.concatenate([ya_ref[0], yb, yc.astype(BF16), yd_ref[0]], axis=1)
    mod = mod_ref[0, 0]
    x = x_ref[0] + mod[2:3, :] * _dot(y, w_ref[...])
    xo_ref[0] = x
    ms = jnp.mean(x * x, axis=-1, keepdims=True)
    h2 = x * lax.rsqrt(ms + EPS) * g2_ref[...] * (1.0 + mod[4:5, :]) + mod[3:4, :]
    h2_hi = h2.astype(BF16)
    h2_ref[0] = _pack_bf16_pair(h2_hi[:, 0:D_MODEL // 2], h2_hi[:, D_MODEL // 2:])
    h2_lo = (h2 - h2_hi.astype(F32)).astype(BF16)
    lg = _dot(h2_hi, wr_ref[...])
    logits = lg[:, 0:128] + lg[:, 128:256] + _dot(h2_lo, wr_ref[:, 0:128])
    lane = lax.broadcasted_iota(I32, logits.shape, 1)
    logits = jnp.where(lane < N_EXPERTS, logits, NEG)
    e = jnp.exp(logits - jnp.max(logits, axis=-1, keepdims=True))
    aff_ref[0] = (e / jnp.sum(e, axis=-1, keepdims=True)).T


def _outproj(x, mod, ya, yb_l, yb_c, hf, hb, oc, yd, mlg, g64, w_out, g2, w_router, T):
    B, N, D = x.shape
    rb = ROW_BLOCK
    nlat = T // rb
    row = lambda w: pl.BlockSpec((1, rb, w), lambda b, r: (b, r, 0))
    full = lambda a: pl.BlockSpec(a.shape, lambda b, r: (0,) * a.ndim)
    return pl.pallas_call(
        functools.partial(_outproj_kernel, nlat=nlat),
        grid=(B, N // rb),
        in_specs=[row(D), pl.BlockSpec((1, 1, 6, D), lambda b, r: (b, jnp.where(r >= nlat, 1, 0), 0, 0)),
                  row(256),
                  pl.BlockSpec((1, rb, 256), lambda b, r: (b, jnp.minimum(r, nlat - 1), 0)),
                  pl.BlockSpec((1, rb, 256), lambda b, r: (b, jnp.maximum(r - nlat, 0), 0)),
                  pl.BlockSpec((1, 256, rb), lambda b, r: (b, 0, r)),
                  pl.BlockSpec((1, 256, rb), lambda b, r: (b, 0, r)), row(256), row(256),
                  full(mlg), full(g64), full(w_out), full(g2), full(w_router)],
        out_specs=[row(D), row(D // 2), pl.BlockSpec((1, 128, rb), lambda b, r: (b, 0, r))],
        out_shape=[jax.ShapeDtypeStruct((B, N, D), F32), jax.ShapeDtypeStruct((B, N, D // 2), I32),
                   jax.ShapeDtypeStruct((B, 128, N), F32)],
        compiler_params=_cparams(("arbitrary", "arbitrary")),
        name="outproj",
    )(x, mod, ya, yb_l, yb_c, hf, hb, oc, yd, mlg, g64, w_out, g2, w_router)


def _prefix_exclusive(x):
    n = x.shape[1]
    lane = lax.broadcasted_iota(I32, x.shape, 1)
    inc = x
    d = 1
    while d < n:
        inc = inc + jnp.where(lane >= d, pltpu.roll(inc, d, axis=1), 0)
        d *= 2
    return inc - x


def _topk_kernel(aff_ref, i_ref, g_ref, pos_ref, val_ref, *, T, S):
    def segment(t0, ts, cap, slot0):
        a = aff_ref[0, :, t0:t0 + ts]
        bits = lax.bitcast_convert_type(a, I32)

        def search(i, v):
            cand = v | jnp.left_shift(jnp.int32(1), 30 - i)
            cnt = jnp.sum((bits >= cand).astype(I32), axis=1, keepdims=True)
            return jnp.where(cnt >= cap, cand, v)

        thr = lax.fori_loop(0, 31, search, jnp.zeros((N_EXPERTS, 1), I32))
        gt = bits > thr
        eq = bits == thr
        need = cap - jnp.sum(gt.astype(I32), axis=1, keepdims=True)
        sel = gt | (eq & (_prefix_exclusive(eq.astype(I32)) < need))
        seli = sel.astype(I32)
        pos_ref[:, 0:ts] = jnp.where(sel, _prefix_exclusive(seli), -1)
        a_hi = a.astype(BF16)
        r1 = a - a_hi.astype(F32)
        a_mid = r1.astype(BF16)
        a_lo = (r1 - a_mid.astype(F32)).astype(BF16)
        val_ref[0, :, 0:ts] = a_hi.astype(F32)
        val_ref[1, :, 0:ts] = a_mid.astype(F32)
        val_ref[2, :, 0:ts] = a_lo.astype(F32)
        tabs = lax.broadcasted_iota(I32, (1, ts), 1) + t0
        t_hi = jnp.right_shift(tabs, 6).astype(F32)
        t_lo = (tabs & 63).astype(F32)
        slot = lax.broadcasted_iota(I32, (cap, ts), 0)
        zeros = jnp.zeros((11, ts), F32)

        def per_expert(e, _):
            onehot = jnp.where(pos_ref[pl.ds(e, 1), 0:ts] == slot, 1.0, 0.0).astype(BF16)
            vals = jnp.concatenate([t_hi, t_lo, val_ref[0, pl.ds(e, 1), 0:ts], val_ref[1, pl.ds(e, 1), 0:ts],
                                    val_ref[2, pl.ds(e, 1), 0:ts], zeros], axis=0).astype(BF16)
            r = _dot_nt(onehot, vals)
            i_ref[0, e, slot0:slot0 + cap, :] = (r[:, 0:1] * 64.0 + r[:, 1:2]).astype(I32)
            g_ref[0, e, slot0:slot0 + cap, :] = r[:, 2:3] + r[:, 3:4] + r[:, 4:5]
            return 0

        lax.fori_loop(0, N_EXPERTS, per_expert, 0)

    segment(0, T, (EC_CAPACITY * T) // N_EXPERTS, 0)
    segment(T, S, (EC_CAPACITY * S) // N_EXPERTS, (EC_CAPACITY * T) // N_EXPERTS)


def _topk(aff_t, T, S):
    B, _, N = aff_t.shape
    E = N_EXPERTS
    cap_t = (EC_CAPACITY * T) // E + (EC_CAPACITY * S) // E
    return pl.pallas_call(
        functools.partial(_topk_kernel, T=T, S=S),
        grid=(B,),
        in_specs=[pl.BlockSpec((1, E, N), lambda b: (b, 0, 0))],
        out_specs=[pl.BlockSpec((1, E, cap_t, 1), lambda b: (b, 0, 0, 0))] * 2,
        out_shape=[jax.ShapeDtypeStruct((B, E, cap_t, 1), I32), jax.ShapeDtypeStruct((B, E, cap_t, 1), F32)],
        scratch_shapes=[pltpu.VMEM((E, T), I32), pltpu.VMEM((3, E, T), F32)],
        compiler_params=_cparams(("arbitrary",)),
        name="expert_topk",
    )(aff_t)


def _gather_kernel(idx_ref, h_ref, *rest, slots, nstep):
    o_ref = rest[-1]
    b, e = pl.program_id(0), pl.program_id(1)
    base = (b * nstep + e) * slots

    def body(g, _):
        rows = [h_ref[0, pl.ds(idx_ref[base + g * 16 + k], 1), :] for k in range(16)]
        o_ref[0, pl.ds(pl.multiple_of(g * 16, 16), 16), :] = jnp.concatenate(rows, axis=0)
        return 0

    lax.fori_loop(0, slots // 16, body, 0)


def _gather(idx_flat, h2p, xe_buf, b0, bg, cap_t):
    B, N, hw = h2p.shape
    slots, nstep = GATHER_EXPERTS * cap_t, N_EXPERTS // GATHER_EXPERTS
    in_specs = [pl.BlockSpec((1, N, hw), lambda b, e, idx: (b0 + b, 0, 0))]
    args = [idx_flat, h2p]
    aliases = {}
    if xe_buf is not None:
        in_specs.append(pl.BlockSpec(memory_space=pl.ANY))
        args.append(xe_buf)
        aliases = {2: 0}
    return pl.pallas_call(
        functools.partial(_gather_kernel, slots=slots, nstep=nstep),
        grid_spec=pltpu.PrefetchScalarGridSpec(
            num_scalar_prefetch=1,
            grid=(bg, nstep),
            in_specs=in_specs,
            out_specs=pl.BlockSpec((1, slots, hw), lambda b, e, idx: (b0 + b, e, 0)),
        ),
        out_shape=jax.ShapeDtypeStruct((B, N_EXPERTS * cap_t, hw), I32),
        input_output_aliases=aliases,
        compiler_params=_cparams(("arbitrary", "arbitrary")),
        name="expert_gather",
    )(*args)


def _ffn_kernel(x_ref, wg_ref, wu_ref, wd_ref, gate_ref, gt_ref, o_ref, wg_s, wu_s, wd_s, *, cap_l):
    @pl.when(pl.program_id(1) == 0)
    def _():
        wg_s[...] = wg_ref[0, 0].astype(BF16)
        wu_s[...] = wu_ref[0, 0].astype(BF16)
        wd_s[...] = wd_ref[0, 0].astype(BF16)

    x = jnp.concatenate(_unpack_bf16_pair(x_ref[0, 0]), axis=1)
    a = _dot(x, wg_s[...])
    u = _dot(x, wu_s[...])
    hmid = (a * jax.nn.sigmoid(a) * u).astype(BF16)
    y = _dot(hmid, wd_s[...])
    gt = gt_ref[0]
    row = lax.broadcasted_iota(I32, y.shape, 0)
    gt2 = jnp.where(row < cap_l, gt[0:1, :], gt[1:2, :])
    o_ref[0, 0] = y * gate_ref[0, 0] * gt2


def _ffn(xe, layer, wg, wu, wd, gate, gt2, cap_l):
    B, E, cap_t, hw = xe.shape
    D = 2 * hw
    wspec = pl.BlockSpec((1, 1, D, D), lambda e, b: (layer, e, 0, 0))
    return pl.pallas_call(
        functools.partial(_ffn_kernel, cap_l=cap_l),
        grid=(E, B),
        in_specs=[pl.BlockSpec((1, 1, cap_t, hw), lambda e, b: (b, e, 0, 0)), wspec, wspec, wspec,
                  pl.BlockSpec((1, 1, cap_t, 1), lambda e, b: (b, e, 0, 0)),
                  pl.BlockSpec((1, 2, D), lambda e, b: (b, 0, 0))],
        out_specs=pl.BlockSpec((1, 1, cap_t, D), lambda e, b: (b, e, 0, 0)),
        out_shape=jax.ShapeDtypeStruct((B, E, cap_t, D), F32),
        scratch_shapes=[pltpu.VMEM((D, D), BF16)] * 3,
        compiler_params=_cparams(("arbitrary", "arbitrary")),
        name="expert_ffn",
    )(xe, wg, wu, wd, gate, gt2)


def _scatter_kernel(idx_ref, y_ref, *rest, slots, nstep):
    o_ref = rest[-1]
    b, e = pl.program_id(0), pl.program_id(1)
    base = (b * nstep + e) * slots

    @pl.when(e == 0)
    def _():
        o_ref[...] = jnp.zeros_like(o_ref)

    def body(g, _):
        tile = y_ref[0, pl.ds(pl.multiple_of(g * 8, 8), 8), :]
        rows = [idx_ref[base + g * 8 + k] for k in range(8)]
        cur = [o_ref[0, pl.ds(r, 1), :] for r in rows]
        for k, r in enumerate(rows):
            o_ref[0, pl.ds(r, 1), :] = cur[k] + tile[k:k + 1, :]
        return 0

    lax.fori_loop(0, slots // 8, body, 0)


def _scatter(idx_flat, ye, acc_buf, n_rows, b0, bg, cap_t):
    B, _, D = ye.shape
    slots, nstep = SCATTER_EXPERTS * cap_t, N_EXPERTS // SCATTER_EXPERTS
    in_specs = [pl.BlockSpec((1, slots, D), lambda b, e, idx: (b0 + b, e, 0))]
    args = [idx_flat, ye]
    aliases = {}
    if acc_buf is not None:
        in_specs.append(pl.BlockSpec(memory_space=pl.ANY))
        args.append(acc_buf)
        aliases = {2: 0}
    return pl.pallas_call(
        functools.partial(_scatter_kernel, slots=slots, nstep=nstep),
        grid_spec=pltpu.PrefetchScalarGridSpec(
            num_scalar_prefetch=1,
            grid=(bg, nstep),
            in_specs=in_specs,
            out_specs=pl.BlockSpec((1, n_rows, D), lambda b, e, idx: (b0 + b, 0, 0)),
        ),
        out_shape=jax.ShapeDtypeStruct((B, n_rows, D), F32),
        input_output_aliases=aliases,
        compiler_params=_cparams(("arbitrary", "arbitrary")),
        name="expert_scatter",
    )(*args)


def _residual_out_kernel(x_ref, f_ref, o_ref):
    o_ref[...] = x_ref[...] + f_ref[...]


def _residual_out(x, f, T):
    B, N, D = x.shape
    rb = ROW_BLOCK
    spec = pl.BlockSpec((1, rb, D), lambda b, r: (b, r, 0))
    return pl.pallas_call(
        _residual_out_kernel,
        grid=(B, T // rb),
        in_specs=[spec, spec],
        out_specs=spec,
        out_shape=jax.ShapeDtypeStruct((B, T, D), F32),
        compiler_params=_cparams(("arbitrary", "arbitrary")),
        name="residual_out",
    )(x, f)


def _rope_tables(T, S, dh):
    rows = T // GRID_W
    r = jnp.broadcast_to(jnp.arange(rows, dtype=F32)[:, None], (rows, GRID_W)).reshape(T)
    col = jnp.broadcast_to(jnp.arange(GRID_W, dtype=F32)[None, :], (rows, GRID_W)).reshape(T)
    nf = dh // 4
    inv = ROPE_BASE ** (-jnp.arange(nf, dtype=F32) / nf)
    ar, ac = r[:, None] * inv, col[:, None] * inv
    ang = jnp.concatenate([ar, ar, ac, ac], axis=1)
    ang = jnp.concatenate([ang, jnp.zeros((S, dh), F32)], axis=0)
    reps = 256 // dh
    return jnp.tile(jnp.cos(ang), (1, reps)), jnp.tile(jnp.sin(ang), (1, reps))


def _group_mats(dh):
    i = np.arange(256)
    gmat = (i[:, None] // dh == i[None, :] // dh).astype(np.float32) / dh
    nf = dh // 4
    half = (i % (2 * nf)) // nf
    pmat = np.zeros((256, 256), np.float32)
    a_idx = i[half == 0]
    pmat[a_idx + nf, a_idx] = -1.0
    pmat[a_idx, a_idx + nf] = 1.0
    return jnp.asarray(gmat, BF16), jnp.asarray(pmat, BF16)


def _hyena_tables(ls):
    t = jnp.linspace(0.0, 1.0, ls, dtype=F32)[:, None]
    w = 2.0 * math.pi * jnp.arange(ls, dtype=F32)[:, None] / ls
    bands = jnp.linspace(1e-4, HY_BANDS - 1, HY_BANDS, dtype=F32)
    z = jnp.concatenate([t, jnp.cos(bands * w), -jnp.sin(bands * w)], axis=-1)
    z = jnp.pad(z, ((0, 0), (0, 128 - HY_EMB)))
    deltas = jnp.abs(jnp.linspace(math.log(HY_TARGET) / HY_SLOW, math.log(HY_TARGET) / HY_FAST, HY_W, dtype=F32))
    decay = jnp.exp(-t * deltas)
    n, nseg = 2 * ls, ls // HY_N2
    k1p = -(-(nseg + 1) // 8) * 8
    k1 = jnp.arange(k1p, dtype=I32)[None, :, None]
    tpos = HY_N2 * jnp.arange(nseg, dtype=I32)[None, None, :] + jnp.arange(HY_N2, dtype=I32)[:, None, None]
    ang = ((k1 * tpos) % n).astype(F32) * (2.0 * math.pi / n)
    keep = k1 <= nseg
    fmat = jnp.concatenate([jnp.where(keep, jnp.cos(ang), 0.0), jnp.where(keep, -jnp.sin(ang), 0.0)],
                           axis=1).astype(BF16)
    return z, decay, fmat, jnp.swapaxes(fmat, 1, 2)


def kernel(x, c, ctx, c_ctx, w_ada, b_ada, norm1_g, norm2_g, w_in, b_gate, a_qnorm, a_knorm, a_sink, hy_conv, hy_fw1, hy_fb1, hy_freq, hy_fw2, hy_fb2, hy_fw3, hy_bias, ml_norm, d_qnorm, d_knorm, d_lq1, d_lk1, d_lq2, d_lk2, d_subnorm, w_out, w_router, w_e_gate, w_e_up, w_e_down):
    B, T, D = x.shape
    S = ctx.shape[1]
    N = T + S
    depth = w_ada.shape[0]
    assert D == D_MODEL and T % S == 0 and S % ROW_BLOCK == 0 and S % MLSTM_CHUNK == 0
    assert S % WATTN_QROWS == 0 and T >= WATTN_QROWS + 2 * BLK and WATTN_QROWS & (WATTN_QROWS - 1) == 0
    cap_l, cap_c = (EC_CAPACITY * T) // N_EXPERTS, (EC_CAPACITY * S) // N_EXPERTS
    cap_t = cap_l + cap_c
    assert cap_t % 16 == 0

    xs = jnp.concatenate([x, ctx], axis=1)

    rpad = -(B + 1) % 8
    cc = jnp.concatenate([c, c_ctx[None, :], jnp.zeros((rpad, D), F32)], axis=0)
    mods = _ada_mods(cc, w_ada, b_ada)

    cos_a, sin_a = _rope_tables(T, S, DH_A)
    cos_d, sin_d = _rope_tables(T, S, DH_D)
    g64, p64 = _group_mats(DH_A)
    g32, p32 = _group_mats(DH_D)
    tabs_l = _hyena_tables(T)
    tabs_c = _hyena_tables(S)

    offs = np.cumsum((0, 256, 128, 128, 768, 256, 256, 256, 256, 16, 256, 256, 256))
    ffn_acc = None
    for l in range(depth):
        lam_init = 0.8 - 0.6 * math.exp(-0.3 * l)
        ml = mods[l]
        mod = jnp.stack([ml[:B].reshape(B, 6, D), jnp.broadcast_to(ml[B].reshape(1, 6, D), (B, 6, D))],
                        axis=1)
        w = w_in[l]
        wp = jnp.concatenate([w[:, offs[0]:offs[8]], w[:, offs[8]:offs[9]], jnp.zeros((D, 112), F32),
                              w[:, offs[9]:offs[12]]], axis=1).astype(BF16)
        gains = jnp.stack([jnp.tile(a_qnorm[l], 4), jnp.tile(a_knorm[l], 4),
                           jnp.tile(d_qnorm[l], 8), jnp.tile(d_knorm[l], 8)], axis=0)
        proj = _inproj(xs, ffn_acc, mod, norm1_g[l].reshape(1, D), wp, (cos_a, sin_a, cos_d, sin_d), gains,
                       (g64, g32, p64, p32), T)
        qa, ka, va, ub, kc, oc, gc, qd, kd, vd, gct, qct, vct = proj[0:13]
        if ffn_acc is not None:
            xs = proj[13]

        ya = _wattn(qa, ka, va, jnp.pad(a_sink[l], (0, 128 - H_A)).reshape(1, 128), T, S)
        yd = _dattn(qd, kd, vd, jnp.stack([d_lq1[l], d_lk1[l], d_lq2[l], d_lk2[l]], axis=0),
                    d_subnorm[l].reshape(1, 2 * DH_D), T, S, lam_init)

        bias = b_gate[l].reshape(-1)
        hf, hb = _mlstm(qct, kc, vct, gc, gct, jnp.pad(bias, (0, 112)).reshape(1, 128), bias.reshape(16, 1), T, S)

        uc = _shortconv(ub, hy_conv[l], T, S)
        w1 = jnp.pad(hy_fw1[l], ((0, 128 - HY_EMB), (0, 0)))
        ybs = []
        for (zf, decay, fmat, gmat), ls, rowblk in ((tabs_l, T, 0), (tabs_c, S, T // S)):
            hpast, hfut, ss = _hyfilter(zf, w1, hy_fb1[l].reshape(1, -1), hy_freq[l].reshape(1, -1), hy_fw2[l],
                                        hy_fb2[l].reshape(1, -1), hy_fw3[l], decay)
            hspec = _hyspec(hpast, hfut, ss, fmat)
            z1 = _hyconv(uc, 0, rowblk, uc, 1, rowblk, hspec, 0, fmat, gmat, hy_bias[l, 0].reshape(1, HY_W), ls)
            z2 = _hyconv(z1, 0, 0, uc, 2, rowblk, hspec, 1, fmat, gmat, hy_bias[l, 1].reshape(1, HY_W), ls)
            ybs.append(z2)

        wr = jnp.pad(w_router[l], ((0, 0), (0, 128 - N_EXPERTS)))
        wr_hi = wr.astype(BF16)
        wr_cat = jnp.concatenate([wr_hi, (wr - wr_hi.astype(F32)).astype(BF16)], axis=1)
        xs, h2, aff = _outproj(xs, mod, ya, ybs[0], ybs[1], hf, hb, oc, yd, jnp.tile(ml_norm[l], 4).reshape(1, 256), g64,
                               w_out[l].astype(BF16), norm2_g[l].reshape(1, D), wr_cat, T)

        idx, gate = _topk(aff, T, S)
        gt2 = mod[:, :, 5, :]
        bg = min(IDX_BATCH_GROUP, B)
        groups = [(b0, idx[b0:b0 + bg].reshape(-1)) for b0 in range(0, B, bg)]
        xe = None
        for b0, idx_flat in groups:
            xe = _gather(idx_flat, h2, xe, b0, bg, cap_t)
        ye = _ffn(xe.reshape(B, N_EXPERTS, cap_t, D // 2), l, w_e_gate, w_e_up, w_e_down, gate, gt2, cap_l)
        ye = ye.reshape(B, N_EXPERTS * cap_t, D)
        ffn_acc = None
        for b0, idx_flat in groups:
            ffn_acc = _scatter(idx_flat, ye, ffn_acc, N, b0, bg, cap_t)
    return _residual_out(xs, ffn_acc, T)
```

```python
import functools
import math

import jax
import jax.numpy as jnp
import numpy as np
from jax import lax
from jax.experimental import pallas as pl
from jax.experimental.pallas import tpu as pltpu

F32 = jnp.float32
BF16 = jnp.bfloat16
I32 = jnp.int32
HI = lax.Precision.HIGHEST

D_MODEL = 1024
GRID_W = 64
GROUP_W = D_MODEL // 4
H_A, HKV_A = 4, 2
G_A = H_A // HKV_A
DH_A = GROUP_W // H_A
WINDOW = 128
BLK = 128
HY_W = GROUP_W
HY_ORDER = 2
HY_BANDS = 16
HY_EMB = 1 + 2 * HY_BANDS
HY_HID = 64
HY_TARGET, HY_FAST, HY_SLOW = 1e-2, 0.3, 1.5
HY_N2 = 16
H_C = 4
DH_C = GROUP_W // H_C
H_D = 4
DH_D = GROUP_W // (2 * H_D)
N_EXPERTS = 16
EC_CAPACITY = 2
ROPE_BASE = 10000.0
EPS = 1e-6
NEG = -1e30

ROW_BLOCK = 256
WATTN_QROWS = 256
DATTN_MIN_ROWSUM = 1e-25
MLSTM_CHUNK = 256
IDX_BATCH_GROUP = 4
GATHER_EXPERTS = 4
SCATTER_EXPERTS = 2
VMEM_LIMIT = 56 * 1024 * 1024

_C_QA, _C_KA, _C_VA, _C_UB = 0, 256, 384, 512
_C_QC, _C_KC, _C_VC, _C_OC, _C_GC = 1280, 1536, 1792, 2048, 2304
_C_QD, _C_KD, _C_VD, _C_END = 2432, 2688, 2944, 3200


def _cparams(sem, vmem=VMEM_LIMIT):
    return pltpu.CompilerParams(dimension_semantics=sem, vmem_limit_bytes=vmem)


def _dot(a, b, **kw):
    return jnp.dot(a, b, preferred_element_type=F32, **kw)


def _dot_nt(a, b, **kw):
    return lax.dot_general(a, b, (((1,), (1,)), ((), ())), preferred_element_type=F32, **kw)


def _dot_tn(a, b, **kw):
    return lax.dot_general(a, b, (((0,), (0,)), ((), ())), preferred_element_type=F32, **kw)


def _pack_bf16_pair(a, b):
    ua = lax.bitcast_convert_type(a.astype(F32), I32)
    ub = lax.bitcast_convert_type(b.astype(F32), I32)
    return ua | lax.shift_right_logical(ub, 16)


def _unpack_bf16_pair(w):
    a = lax.bitcast_convert_type(w & jnp.int32(-65536), F32)
    b = lax.bitcast_convert_type(lax.shift_left(w, 16), F32)
    return a.astype(BF16), b.astype(BF16)


def _ada_kernel(c_ref, w_ref, b_ref, o_ref):
    c = c_ref[...]
    s = c * jax.nn.sigmoid(c)
    o_ref[0] = _dot(s, w_ref[0], precision=HI) + b_ref[0]


def _ada_mods(cc, w_ada, b_ada):
    L, D, W6 = w_ada.shape
    R = cc.shape[0]
    cb = 1536
    return pl.pallas_call(
        _ada_kernel,
        grid=(L, W6 // cb),
        in_specs=[pl.BlockSpec((R, D), lambda l, j: (0, 0)),
                  pl.BlockSpec((1, D, cb), lambda l, j: (l, 0, j)),
                  pl.BlockSpec((1, 1, cb), lambda l, j: (l, 0, j))],
        out_specs=pl.BlockSpec((1, R, cb), lambda l, j: (l, 0, j)),
        out_shape=jax.ShapeDtypeStruct((L, R, W6), F32),
        compiler_params=_cparams(("arbitrary", "arbitrary")),
        name="ada_mods",
    )(cc, w_ada, b_ada.reshape(L, 1, W6))


def _inproj_kernel(*refs, has_f):
    if has_f:
        x_ref, f_ref, *refs = refs
        *refs, xsum = refs
        x = x_ref[0] + f_ref[0]
        xsum[0] = x
    else:
        x_ref, *refs = refs
        x = x_ref[0]
    (mod_ref, g1_ref, w_ref, ca_ref, sa_ref, cd_ref, sd_ref, gains_ref, g64_ref, g32_ref, p64_ref, p32_ref,
     qa, ka, va, ub, kc, oc, gc, qd, kd, vd, gct, qct, vct) = refs
    ms = jnp.mean(x * x, axis=-1, keepdims=True)
    xn = x * lax.rsqrt(ms + EPS) * g1_ref[...]
    mod = mod_ref[0, 0]
    h = xn * (1.0 + mod[1:2, :]) + mod[0:1, :]
    p = _dot(h.astype(BF16), w_ref[...])

    def headnorm_rope(t, gmat, gain, pmat, cos, sin, scale):
        w = t.shape[1]
        msq = _dot((t * t).astype(BF16), gmat[0:w, 0:w])
        tn = t * lax.rsqrt(msq + EPS) * gain
        tn = tn * cos[:, 0:w] + _dot(tn.astype(BF16), pmat[0:w, 0:w]) * sin[:, 0:w]
        return tn * scale

    ca, sa, cd, sd = ca_ref[...], sa_ref[...], cd_ref[...], sd_ref[...]
    gains = gains_ref[...]
    qa[0] = headnorm_rope(p[:, _C_QA:_C_KA], g64_ref, gains[0:1, :], p64_ref, ca, sa, DH_A ** -0.5).astype(BF16)
    ka[0] = headnorm_rope(p[:, _C_KA:_C_VA], g64_ref, gains[1:2, 0:128], p64_ref, ca, sa, 1.0).astype(BF16)
    va[0] = p[:, _C_VA:_C_UB].astype(BF16)
    ub[0] = p[:, _C_UB:_C_QC].astype(BF16)
    qct[0] = p[:, _C_QC:_C_KC].T.astype(BF16)
    kc[0] = (p[:, _C_KC:_C_VC] * DH_C ** -0.5).astype(BF16)
    vct[0] = p[:, _C_VC:_C_OC].T.astype(BF16)
    oc[0] = p[:, _C_OC:_C_GC].astype(BF16)
    gc[0] = p[:, _C_GC:_C_QD]
    gct[0] = p[:, _C_GC:_C_QD].T
    qd[0] = headnorm_rope(p[:, _C_QD:_C_KD], g32_ref, gains[2:3, :], p32_ref, cd, sd, DH_D ** -0.5).astype(BF16)
    kd[0] = headnorm_rope(p[:, _C_KD:_C_VD], g32_ref, gains[3:4, :], p32_ref, cd, sd, 1.0).astype(BF16)
    vd[0] = p[:, _C_VD:_C_END].astype(BF16)


def _inproj(x, f, mod, g1, w_packed, tabs, gains, mats, T):
    B, N, D = x.shape
    rb = ROW_BLOCK
    nlat = T // rb
    row = lambda w: pl.BlockSpec((1, rb, w), lambda b, r: (b, r, 0))
    tab = pl.BlockSpec((rb, 256), lambda b, r: (r, 0))
    full = lambda a: pl.BlockSpec(a.shape, lambda b, r: (0,) * a.ndim)
    widths = [256, 128, 128, 768, 256, 256, 128, 256, 256, 256]
    dtypes = [BF16] * 6 + [F32] + [BF16] * 3
    tr = lambda w: pl.BlockSpec((1, w, rb), lambda b, r: (b, 0, r))
    has_f = f is not None
    return pl.pallas_call(
        functools.partial(_inproj_kernel, has_f=has_f),
        grid=(B, N // rb),
        in_specs=[row(D)] * (2 if has_f else 1)
        + [pl.BlockSpec((1, 1, 6, D), lambda b, r: (b, jnp.where(r >= nlat, 1, 0), 0, 0)),
           full(g1), full(w_packed), tab, tab, tab, tab, full(gains)] + [full(m) for m in mats],
        out_specs=[row(w) for w in widths] + [tr(128), tr(256), tr(256)] + ([row(D)] if has_f else []),
        out_shape=[jax.ShapeDtypeStruct((B, N, w), dt) for w, dt in zip(widths, dtypes)]
        + [jax.ShapeDtypeStruct((B, 128, N), F32), jax.ShapeDtypeStruct((B, 256, N), BF16),
           jax.ShapeDtypeStruct((B, 256, N), BF16)] + ([jax.ShapeDtypeStruct((B, N, D), F32)] if has_f else []),
        compiler_params=_cparams(("arbitrary", "arbitrary")),
        name="inproj",
    )(x, *([f] if has_f else []), mod, g1, w_packed, *tabs, gains, *mats)


def _wattn_kernel(q_ref, k_ref, v_ref, sink_ref, sinkw_ref, placeq_ref, qb_ref, placek_ref, gsumk_ref, o_ref,
                  kaug_ref, vaug_ref, kmax_ref, *, T, S):
    j = pl.program_id(1)
    qb, kwin = WATTN_QROWS, WATTN_QROWS + 2 * BLK
    is_lat = j < T // qb
    start = pl.multiple_of(jnp.clip(j * qb - BLK, 0, T - kwin), BLK)
    row = lax.broadcasted_iota(I32, (G_A * qb, kwin), 0)
    qpos = j * qb + (row & (qb - 1))
    kpos = start + lax.broadcasted_iota(I32, (G_A * qb, kwin), 1)
    valid = (jnp.abs(qpos - kpos) <= WINDOW) & is_lat
    grp = jnp.right_shift(lax.broadcasted_iota(I32, (G_A * qb, 1), 0), qb.bit_length() - 1)
    sinks = sink_ref[...]

    @pl.when(j == 0)
    def _():
        lane = lax.broadcasted_iota(I32, (1, HKV_A * 128), 1)
        one = jnp.where((lane & 127) == DH_A, 1.0, 0.0)
        k_wide = _dot(k_ref[0], placek_ref[...])
        kaug_ref[...] = (k_wide + one).astype(BF16)
        vaug_ref[...] = (_dot(v_ref[0], placek_ref[...]) + one).astype(BF16)
        ksq = _dot((k_wide * k_wide).astype(BF16), gsumk_ref[...])
        kmax_ref[...] = jnp.sqrt(jnp.max(ksq, axis=0, keepdims=True))

    def sink_rows(heads):
        sink = sinks[0:1, heads[0]:heads[0] + 1]
        for g in range(1, G_A):
            sink = jnp.where(grp == g, sinks[0:1, heads[g]:heads[g] + 1], sink)
        return sink

    def store(heads, o):
        for g, hd in enumerate(heads):
            o_ref[0, :, hd * DH_A:(hd + 1) * DH_A] = o[g * qb:(g + 1) * qb].astype(BF16)

    def body_fast():
        q_wide = _dot(q_ref[0], placeq_ref[...])
        km = kmax_ref[...]
        kmq = jnp.concatenate([km[:, (hd // G_A) * 128:(hd // G_A + 1) * 128] for hd in range(H_A)], axis=1)
        lane = lax.broadcasted_iota(I32, (1, H_A * 128), 1)
        shift = jnp.where((lane & 127) == DH_A, jnp.maximum(qb_ref[...] * kmq, sinkw_ref[...]), 0.0)
        q_aug = (q_wide - shift).astype(BF16)
        low = None
        for hk in range(HKV_A):
            heads = [hk * G_A + g for g in range(G_A)]
            ks = slice(hk * 128, (hk + 1) * 128)
            q2 = jnp.concatenate([q_aug[:, hd * 128:(hd + 1) * 128] for hd in heads], axis=0)
            m_used = -q2[:, DH_A:DH_A + 1].astype(F32)
            s_loc = jnp.where(valid, _dot_nt(q2, kaug_ref[pl.ds(start, kwin), ks]), NEG)
            s_ctx = _dot_nt(q2, kaug_ref[T:T + S, ks])
            pv = (_dot(jnp.exp(s_loc.astype(BF16)), vaug_ref[pl.ds(start, kwin), ks])
                  + _dot(jnp.exp(s_ctx.astype(BF16)), vaug_ref[T:T + S, ks]))
            sigma = pv[:, DH_A:DH_A + 1]
            low = sigma if low is None else jnp.minimum(low, sigma)
            store(heads, pv[:, 0:DH_A] / (sigma + jnp.exp(sink_rows(heads) - m_used)))
        return jnp.min(low)

    def body_exact():
        kw = k_ref[0, pl.ds(start, kwin), :]
        vw = v_ref[0, pl.ds(start, kwin), :]
        kc = k_ref[0, T:T + S, :]
        vc = v_ref[0, T:T + S, :]
        for hk in range(HKV_A):
            cs = slice(hk * DH_A, (hk + 1) * DH_A)
            kwh, vwh, kch, vch = kw[:, cs], vw[:, cs], kc[:, cs], vc[:, cs]
            heads = [hk * G_A + g for g in range(G_A)]
            q = jnp.concatenate([q_ref[0, :, hd * DH_A:(hd + 1) * DH_A] for hd in heads], axis=0)
            sink = sink_rows(heads)
            s_loc = jnp.where(valid, _dot_nt(q, kwh), NEG)
            s_ctx = _dot_nt(q, kch)
            m = jnp.maximum(jnp.maximum(jnp.max(s_loc, axis=-1, keepdims=True),
                                        jnp.max(s_ctx, axis=-1, keepdims=True)), sink)
            p_loc = jnp.exp(s_loc - m)
            p_ctx = jnp.exp(s_ctx - m)
            den = (jnp.sum(p_loc, axis=-1, keepdims=True) + jnp.sum(p_ctx, axis=-1, keepdims=True)
                   + jnp.exp(sink - m))
            store(heads, (_dot(p_loc.astype(BF16), vwh) + _dot(p_ctx.astype(BF16), vch)) / den)

    low = body_fast()

    @pl.when(jnp.logical_not(low > DATTN_MIN_ROWSUM))
    def _():
        body_exact()


def _wattn_mats(nheads):
    place = np.zeros((nheads * DH_A, nheads * 128), np.float32)
    gsum = np.zeros((nheads * 128, nheads * 128), np.float32)
    for h in range(nheads):
        for d in range(DH_A):
            place[h * DH_A + d, h * 128 + d] = 1.0
            gsum[h * 128 + d, h * 128 + DH_A] = 1.0
    return jnp.asarray(place, BF16), jnp.asarray(gsum, BF16)


def _wattn(qa, ka, va, sink, qbound, T, S):
    B, N, _ = qa.shape
    qb = WATTN_QROWS
    placeq, _ = _wattn_mats(H_A)
    placek, gsumk = _wattn_mats(HKV_A)
    sinkw = jnp.zeros((1, H_A * 128), F32).at[0, DH_A::128].set(sink[0, 0:H_A])
    full = lambda a: pl.BlockSpec(a.shape, lambda b, j: (0,) * a.ndim)
    return pl.pallas_call(
        functools.partial(_wattn_kernel, T=T, S=S),
        grid=(B, N // qb),
        in_specs=[pl.BlockSpec((1, qb, 256), lambda b, j: (b, j, 0)),
                  pl.BlockSpec((1, N, 128), lambda b, j: (b, 0, 0)),
                  pl.BlockSpec((1, N, 128), lambda b, j: (b, 0, 0)),
                  pl.BlockSpec((1, 128), lambda b, j: (0, 0)),
                  full(sinkw), full(placeq), full(qbound), full(placek), full(gsumk)],
        out_specs=pl.BlockSpec((1, qb, 256), lambda b, j: (b, j, 0)),
        out_shape=jax.ShapeDtypeStruct((B, N, 256), BF16),
        scratch_shapes=[pltpu.VMEM((N, HKV_A * 128), BF16), pltpu.VMEM((N, HKV_A * 128), BF16),
                        pltpu.VMEM((1, HKV_A * 128), F32)],
        compiler_params=_cparams(("arbitrary", "arbitrary")),
        name="window_attn",
    )(qa, ka, va, sink, sinkw, placeq, qbound, placek, gsumk)


def _dattn_kernel(q_ref, k_ref, v_ref, lam_ref, sub_ref, place_ref, gsum_ref, qb_ref, o_ref,
                  vaug_ref, kmax_ref, kch_ref, qch_ref, tch_ref, *, T, S, lam_init):
    j = pl.program_id(1)
    lv = lam_ref[...]
    lam = (jnp.exp(jnp.sum(lv[0:1, :] * lv[1:2, :], axis=-1, keepdims=True))
           - jnp.exp(jnp.sum(lv[2:3, :] * lv[3:4, :], axis=-1, keepdims=True)) + lam_init)
    subg = sub_ref[...] * (1.0 - lam_init)
    dv = 2 * DH_D
    nchain, aw = 2 * H_D, 2 * DH_D

    @pl.when(j == 0)
    def _():
        ones = jnp.where(lax.broadcasted_iota(I32, (T + S, dv), 1) == 0, 1.0, 0.0).astype(BF16)
        for hd in range(H_D):
            vaug_ref[hd] = jnp.concatenate([v_ref[0, :, hd * dv:(hd + 1) * dv], ones], axis=1)
        k_wide = _dot(k_ref[0], place_ref[...])
        lane = lax.broadcasted_iota(I32, (1, nchain * aw), 1)
        k_aug = (k_wide + jnp.where((lane & (aw - 1)) == DH_D, 1.0, 0.0)).astype(BF16)
        for i in range(nchain):
            kch_ref[i] = k_aug[:, i * aw:(i + 1) * aw]
        ksq = _dot((k_wide * k_wide).astype(BF16), gsum_ref[...])
        kmax_ref[...] = jnp.sqrt(jnp.max(ksq, axis=0, keepdims=True))

    def finish(hd, terms):
        o = terms[0] - lam * terms[1]
        ms = jnp.mean(o * o, axis=-1, keepdims=True)
        o_ref[0, :, hd * dv:(hd + 1) * dv] = (o * lax.rsqrt(ms + EPS) * subg).astype(BF16)

    def body_fast(k0, nk):
        q_wide = _dot(q_ref[0], place_ref[...])
        q_aug = (q_wide - qb_ref[...] * kmax_ref[...]).astype(BF16)
        terms, low = [], None
        for i in range(nchain):
            hd, cs = i // 2, slice(i * aw, (i + 1) * aw)
            s = _dot_nt(q_aug[:, cs], kch_ref[i, k0:k0 + nk, :])
            pv = _dot(jnp.exp(s.astype(BF16)), vaug_ref[hd, k0:k0 + nk, :])
            sigma = pv[:, dv:dv + 1]
            low = sigma if low is None else jnp.minimum(low, sigma)
            terms.append(pv[:, 0:dv] / sigma)
            if i % 2 == 1:
                finish(hd, terms)
                terms = []
        return jnp.min(low)

    def body_exact(k0, nk):
        q_wide = _dot(q_ref[0], place_ref[...])
        for i in range(nchain):
            qch_ref[i] = q_wide[:, i * aw:(i + 1) * aw].astype(BF16)

        def chain(i, carry):
            s = _dot_nt(qch_ref[i], kch_ref[i, k0:k0 + nk, :])
            e = jnp.exp((s - jnp.max(s, axis=-1, keepdims=True)).astype(BF16))
            pv = _dot(e, vaug_ref[jnp.right_shift(i, 1), k0:k0 + nk, :])
            tch_ref[i] = pv / pv[:, dv:dv + 1]
            return carry

        lax.fori_loop(0, nchain, chain, 0)
        for hd in range(H_D):
            finish(hd, [tch_ref[2 * hd][:, 0:dv], tch_ref[2 * hd + 1][:, 0:dv]])

    def body(k0, nk):
        low = body_fast(k0, nk)

        @pl.when(jnp.logical_not(low > DATTN_MIN_ROWSUM))
        def _():
            body_exact(k0, nk)

    @pl.when(j < T // ROW_BLOCK)
    def _():
        body(0, T + S)

    @pl.when(j >= T // ROW_BLOCK)
    def _():
        body(T, S)


def _dattn_mats():
    nchain, aw = 2 * H_D, 2 * DH_D
    place = np.zeros((nchain * DH_D, nchain * aw), np.float32)
    gsum = np.zeros((nchain * aw, nchain * aw), np.float32)
    for i in range(nchain):
        for d in range(DH_D):
            place[i * DH_D + d, i * aw + d] = 1.0
            gsum[i * aw + d, i * aw + DH_D] = 1.0
    return jnp.asarray(place, BF16), jnp.asarray(gsum, BF16)


def _dattn(qd, kd, vd, lam_vecs, subg, qbound, T, S, lam_init):
    B, N, _ = qd.shape
    rb = ROW_BLOCK
    place, gsum = _dattn_mats()
    full = lambda a: pl.BlockSpec(a.shape, lambda b, j: (0,) * a.ndim)
    return pl.pallas_call(
        functools.partial(_dattn_kernel, T=T, S=S, lam_init=lam_init),
        grid=(B, N // rb),
        in_specs=[pl.BlockSpec((1, rb, 256), lambda b, j: (b, j, 0)),
                  pl.BlockSpec((1, N, 256), lambda b, j: (b, 0, 0)),
                  pl.BlockSpec((1, N, 256), lambda b, j: (b, 0, 0)),
                  pl.BlockSpec((4, DH_D), lambda b, j: (0, 0)),
                  pl.BlockSpec((1, 2 * DH_D), lambda b, j: (0, 0)), full(place), full(gsum), full(qbound)],
        out_specs=pl.BlockSpec((1, rb, 256), lambda b, j: (b, j, 0)),
        out_shape=jax.ShapeDtypeStruct((B, N, 256), BF16),
        scratch_shapes=[pltpu.VMEM((H_D, N, 4 * DH_D), BF16),
                        pltpu.VMEM((1, 2 * 256), F32), pltpu.VMEM((2 * H_D, N, 2 * DH_D), BF16),
                        pltpu.VMEM((2 * H_D, rb, 2 * DH_D), BF16), pltpu.VMEM((2 * H_D, rb, 4 * DH_D), F32)],
        compiler_params=_cparams(("arbitrary", "arbitrary")),
        name="diff_attn",
    )(qd, kd, vd, lam_vecs, subg, place, gsum, qbound)


def _log_sigmoid(x):
    return jnp.minimum(x, 0.0) - jnp.log1p(jnp.exp(-jnp.abs(x)))


def _dot3(a, b, split_b):
    x = b if split_b else a
    hi = x.astype(BF16)
    r1 = x - hi.astype(F32)
    mid = r1.astype(BF16)
    lo = (r1 - mid.astype(F32)).astype(BF16)
    if split_b:
        a = a.astype(BF16)
        return _dot(a, hi) + _dot(a, mid) + _dot(a, lo)
    b = b.astype(BF16)
    return _dot(hi, b) + _dot(mid, b) + _dot(lo, b)


def _mlstm_kernel(qtf_ref, kf_ref, vtf_ref, gf_ref, gtf_ref, qtb_ref, kb_ref, vtb_ref, gb_ref, gtb_ref,
                  bias_ref, biast_ref, hf_ref, hb_ref, c_ref, m_ref):
    i = pl.program_id(1)
    lc = MLSTM_CHUNK

    @pl.when(i == 0)
    def _():
        c_ref[...] = jnp.zeros_like(c_ref)
        m_ref[...] = jnp.zeros_like(m_ref)

    r_io = lax.broadcasted_iota(I32, (lc, lc), 0)
    c_io = lax.broadcasted_iota(I32, (lc, lc), 1)
    lower = r_io >= c_io
    upper = r_io <= c_io
    tri_lo = jnp.where(lower, 1.0, 0.0).astype(F32)
    tri_up = jnp.where(upper, 1.0, 0.0).astype(F32)
    ones_rows = jnp.where(lax.broadcasted_iota(I32, (DH_C, lc), 0) == 0, 1.0, 0.0).astype(BF16)

    def direction(d, qt_ref, k_ref, vt_ref, g_ref, gt_ref, out_ref):
        pre = g_ref[0] + bias_ref[...]
        pre_t = gt_ref[0] + biast_ref[...]
        lf = _log_sigmoid(pre)
        lf_t = _log_sigmoid(pre_t)
        if d == 0:
            bcol = _dot3(tri_lo, lf, True)
            brow = _dot3(lf_t, tri_up, False)
            valid, end = upper, lc - 1
        else:
            bcol = _dot3(tri_up, lf, True)
            brow = _dot3(lf_t, tri_lo, False)
            valid, end = lower, 0
        for hd in range(H_C):
            ci, cf = d * H_C + hd, (2 + d) * H_C + hd
            rs = slice(hd * DH_C, (hd + 1) * DH_C)
            q_t = qt_ref[0, rs, :]
            k = k_ref[0, :, rs]
            v_aug = jnp.concatenate([vt_ref[0, rs, :], ones_rows], axis=0)
            b_row = brow[cf:cf + 1, :]
            ig_row = pre_t[ci:ci + 1, :]
            bi_col = bcol[:, cf:cf + 1] - pre[:, ci:ci + 1]
            b_end = brow[cf:cf + 1, end:end + 1]
            slot = d * H_C + hd
            c = c_ref[slot]
            m = m_ref[slot][:, 0:1]
            logw = jnp.where(valid, b_row - bi_col, NEG)
            inter = b_row + m
            m_t = jnp.maximum(inter, jnp.max(logw, axis=0, keepdims=True))
            w_in = jnp.exp(inter - m_t)
            p_t = (_dot(k, q_t) * jnp.exp(logw - m_t)).astype(BF16)
            num = w_in * _dot(c.astype(BF16), q_t) + _dot(v_aug, p_t)
            den = jnp.maximum(jnp.abs(num[DH_C:DH_C + 1, :]), jnp.exp(-m_t))
            out_ref[0, rs, :] = num[0:DH_C, :] / den
            g_end = b_end - b_row + ig_row
            m_new = jnp.maximum(b_end + m, jnp.max(g_end, axis=1, keepdims=True))
            wv = (v_aug.astype(F32) * jnp.exp(g_end - m_new)).astype(BF16)
            c_ref[slot] = jnp.exp(b_end + m - m_new) * c + _dot(wv, k)
            m_ref[slot] = jnp.broadcast_to(m_new, (1, 128))

    direction(0, qtf_ref, kf_ref, vtf_ref, gf_ref, gtf_ref, hf_ref)
    direction(1, qtb_ref, kb_ref, vtb_ref, gb_ref, gtb_ref, hb_ref)


def _mlstm(qct, kc, vct, gc, gct, bias, biast, T, S):
    B, N, _ = kc.shape
    lc = MLSTM_CHUNK
    nch, nlat = N // lc, T // lc
    fwd = lambda i: jnp.where(i < nch - nlat, nlat + i, i - (nch - nlat))
    bwd = lambda i: nch - 1 - i
    tok = lambda f: pl.BlockSpec((1, lc, 256), lambda b, i: (b, f(i), 0))
    tok_t = lambda f: pl.BlockSpec((1, 256, lc), lambda b, i: (b, 0, f(i)))
    gate = lambda f: pl.BlockSpec((1, lc, 128), lambda b, i: (b, f(i), 0))
    gate_t = lambda f: pl.BlockSpec((1, 16, lc), lambda b, i: (b, 0, f(i)))
    return pl.pallas_call(
        _mlstm_kernel,
        grid=(B, nch),
        in_specs=[tok_t(fwd), tok(fwd), tok_t(fwd), gate(fwd), gate_t(fwd),
                  tok_t(bwd), tok(bwd), tok_t(bwd), gate(bwd), gate_t(bwd),
                  pl.BlockSpec((1, 128), lambda b, i: (0, 0)),
                  pl.BlockSpec((16, 1), lambda b, i: (0, 0))],
        out_specs=[tok_t(fwd), tok_t(bwd)],
        out_shape=[jax.ShapeDtypeStruct((B, 256, N), F32)] * 2,
        scratch_shapes=[pltpu.VMEM((2 * H_C, 2 * DH_C, DH_C), F32), pltpu.VMEM((2 * H_C, 1, 128), F32)],
        compiler_params=_cparams(("arbitrary", "arbitrary")),
        name="mlstm",
    )(qct, kc, vct, gc, gct, qct, kc, vct, gc, gct, bias, biast)


def _shortconv_kernel(u_ref, w_ref, o_ref, *, T, S):
    u = u_ref[0].astype(F32)
    n = u.shape[0]
    w = w_ref[...]
    row = lax.broadcasted_iota(I32, u.shape, 0)
    zero = jnp.zeros((1, u.shape[1]), F32)
    prev = jnp.concatenate([zero, u[0:n - 1]], axis=0)
    nxt = jnp.concatenate([u[1:n], zero], axis=0)
    prev = jnp.where(row == T, 0.0, prev)
    nxt = jnp.where(row == T - 1, 0.0, nxt)
    o_ref[0] = (w[0:1, :] * prev + w[1:2, :] * u + w[2:3, :] * nxt).astype(BF16)


def _shortconv(ub, conv_w, T, S):
    B, N, W = ub.shape
    return pl.pallas_call(
        functools.partial(_shortconv_kernel, T=T, S=S),
        grid=(B, W // 256),
        in_specs=[pl.BlockSpec((1, N, 256), lambda b, j: (b, 0, j)),
                  pl.BlockSpec((3, 256), lambda b, j: (0, j))],
        out_specs=pl.BlockSpec((1, N, 256), lambda b, j: (b, 0, j)),
        out_shape=jax.ShapeDtypeStruct((B, N, W), BF16),
        compiler_params=_cparams(("arbitrary", "arbitrary")),
        name="hyena_shortconv",
    )(ub, conv_w)


def _hyfilter_kernel(z_ref, w1_ref, b1_ref, fr_ref, w2_ref, b2_ref, w3_ref, dec_ref,
                     hp_ref, hf_ref, ss_ref):
    i = pl.program_id(0)
    fr = fr_ref[...]
    h = jnp.sin(fr * (_dot(z_ref[...], w1_ref[...], precision=HI) + b1_ref[...]))
    h = jnp.sin(fr * (_dot(h, w2_ref[...], precision=HI) + b2_ref[...]))
    h = _dot(h, w3_ref[...], precision=HI)
    dec = dec_ref[...]
    dec2 = jnp.concatenate([dec, dec], axis=1)
    hw = HY_ORDER * HY_W
    h0 = h[:, 0:hw] * dec2
    h1 = h[:, hw:2 * hw] * dec2
    rows = h0.shape[0]
    t = i * rows + lax.broadcasted_iota(I32, h0.shape, 0)
    h1 = jnp.where(t == 0, 0.0, h1)
    hp_ref[...] = h0
    hf_ref[...] = h1

    @pl.when(i == 0)
    def _():
        ss_ref[...] = jnp.zeros_like(ss_ref)

    ss_ref[...] += jnp.sum(h0 * h0 + h1 * h1, axis=0, keepdims=True)


def _hyfilter(zfeat, w1, b1, fr, w2, b2, w3, decay):
    ls = zfeat.shape[0]
    rb = min(ls, 512)
    hw = HY_ORDER * HY_W
    full = lambda a: pl.BlockSpec(a.shape, lambda i: (0,) * a.ndim)
    return pl.pallas_call(
        _hyfilter_kernel,
        grid=(ls // rb,),
        in_specs=[pl.BlockSpec((rb, 128), lambda i: (i, 0)), full(w1), full(b1), full(fr), full(w2),
                  full(b2), full(w3), pl.BlockSpec((rb, HY_W), lambda i: (i, 0))],
        out_specs=[pl.BlockSpec((rb, hw), lambda i: (i, 0)), pl.BlockSpec((rb, hw), lambda i: (i, 0)),
                   pl.BlockSpec((1, hw), lambda i: (0, 0))],
        out_shape=[jax.ShapeDtypeStruct((ls, hw), F32), jax.ShapeDtypeStruct((ls, hw), F32),
                   jax.ShapeDtypeStruct((1, hw), F32)],
        compiler_params=_cparams(("arbitrary",)),
        name="hyena_filter",
    )(zfeat, w1, b1, fr, w2, b2, w3, decay)


def _cmul_const(v, c, s):
    vr, vi = v
    r = math.sqrt(0.5)
    if abs(s) < 1e-9:
        return (vr, vi) if c > 0 else (-vr, -vi)
    if abs(c) < 1e-9:
        return (-vi, vr) if s > 0 else (vi, -vr)
    if abs(abs(c) - r) < 1e-9 and abs(abs(s) - r) < 1e-9:
        a, b = (vr if c > 0 else -vr), (vi if s > 0 else -vi)
        p, q = (vi if c > 0 else -vi), (vr if s > 0 else -vr)
        return (a - b) * r, (p + q) * r
    return c * vr - s * vi, c * vi + s * vr


def _fft_pow2(xs, sign):
    n = len(xs)
    if n == 1:
        return xs
    ev, od = _fft_pow2(xs[0::2], sign), _fft_pow2(xs[1::2], sign)
    out = [None] * n
    for k in range(n // 2):
        ang = sign * 2.0 * math.pi * k / n
        tr, ti = _cmul_const(od[k], math.cos(ang), math.sin(ang))
        out[k] = (ev[k][0] + tr, ev[k][1] + ti)
        out[k + n // 2] = (ev[k][0] - tr, ev[k][1] - ti)
    return out


def _hy_stage1(src_ref, f_ref, a_ref, nseg):
    k1p = a_ref.shape[2]
    for n2 in range(HY_N2):
        if len(src_ref.shape) == 3:
            zs = jnp.concatenate([src_ref[t, pl.ds(n2, nseg, stride=HY_N2), :] for t in range(src_ref.shape[0])],
                                 axis=1).astype(BF16)
        else:
            zs = src_ref[pl.ds(n2, nseg, stride=HY_N2), :].astype(BF16)
        r = _dot(f_ref[n2], zs)
        a_ref[0, n2] = r[0:k1p]
        a_ref[1, n2] = r[k1p:2 * k1p]


def _hy_chunks(a_ref):
    k1p, lanes = a_ref.shape[2], a_ref.shape[3]
    return k1p // 8, [slice(t * 128, (t + 1) * 128) for t in range(lanes // 128)]


def _hyspec_kernel(hp_ref, hf_ref, ss_ref, f_ref, h_ref, ap_ref, af_ref, *, ls):
    nseg = ls // HY_N2
    _hy_stage1(hp_ref, f_ref, ap_ref, nseg)
    _hy_stage1(hf_ref, f_ref, af_ref, nseg)
    nchunk, lane_tiles = _hy_chunks(ap_ref)

    def body(i, _):
        rows = pl.ds(pl.multiple_of(i * 8, 8), 8)
        k1 = i * 8 + lax.broadcasted_iota(I32, (8, 1), 0)
        wk = jnp.where((k1 == 0) | (k1 == nseg), 1.0, 2.0) * (1.0 / (2 * ls))
        for ln in lane_tiles:
            scale = lax.rsqrt(ss_ref[:, ln] + EPS) * wk
            P = _fft_pow2([(ap_ref[0, n2, rows, ln], ap_ref[1, n2, rows, ln]) for n2 in range(HY_N2)], -1)
            Q = _fft_pow2([(af_ref[0, n2, rows, ln], af_ref[1, n2, rows, ln]) for n2 in range(HY_N2)], -1)
            for k2 in range(HY_N2):
                h_ref[0, 0, k2, rows, ln] = (P[k2][0] + Q[k2][0]) * scale
                h_ref[0, 1, k2, rows, ln] = (P[k2][1] - Q[k2][1]) * scale
        return 0

    lax.fori_loop(0, nchunk, body, 0)


def _hyspec(hp, hf, ss, fmat):
    ls = hp.shape[0]
    k1p = fmat.shape[1] // 2
    full = lambda a: pl.BlockSpec(a.shape, lambda o, c: (0,) * a.ndim)
    nc = HY_W // 128
    col = lambda o, c: (0, o * nc + c)
    return pl.pallas_call(
        functools.partial(_hyspec_kernel, ls=ls),
        grid=(HY_ORDER, nc),
        in_specs=[pl.BlockSpec((ls, 128), col), pl.BlockSpec((ls, 128), col), pl.BlockSpec((1, 128), col),
                  full(fmat)],
        out_specs=pl.BlockSpec((1, 2, HY_N2, k1p, 128), lambda o, c: (o, 0, 0, 0, c)),
        out_shape=jax.ShapeDtypeStruct((HY_ORDER, 2, HY_N2, k1p, HY_W), F32),
        scratch_shapes=[pltpu.VMEM((2, HY_N2, k1p, 128), F32)] * 2,
        compiler_params=_cparams(("arbitrary", "arbitrary")),
        name="hyena_spectrum",
    )(hp, hf, ss, fmat)


def _hyconv_kernel(z_ref, gate_ref, h_ref, f_ref, g_ref, bias_ref, o_ref, zf_ref, yf_ref, a_ref, *, ls):
    nseg = ls // HY_N2
    ntile = zf_ref.shape[0]
    for t in range(ntile):
        zf_ref[t] = z_ref[0, :, t * 128:(t + 1) * 128].astype(F32)
    _hy_stage1(zf_ref, f_ref, a_ref, nseg)
    nchunk, lane_tiles = _hy_chunks(a_ref)

    def body(i, _):
        rows = pl.ds(pl.multiple_of(i * 8, 8), 8)
        for ln in lane_tiles:
            X = _fft_pow2([(a_ref[0, n2, rows, ln], a_ref[1, n2, rows, ln]) for n2 in range(HY_N2)], -1)
            Y = []
            for k2 in range(HY_N2):
                hr, hi = h_ref[0, 0, k2, rows, ln], h_ref[0, 1, k2, rows, ln]
                xr, xi = X[k2]
                Y.append((xr * hr - xi * hi, xr * hi + xi * hr))
            Bv = _fft_pow2(Y, 1)
            for n2 in range(HY_N2):
                a_ref[0, n2, rows, ln] = Bv[n2][0]
                a_ref[1, n2, rows, ln] = Bv[n2][1]
        return 0

    lax.fori_loop(0, nchunk, body, 0)
    for n2 in range(HY_N2):
        bb = jnp.concatenate([a_ref[0, n2], a_ref[1, n2]], axis=0).astype(BF16)
        yv = _dot(g_ref[n2], bb)
        for t in range(ntile):
            yf_ref[t, pl.ds(n2, nseg, stride=HY_N2), :] = yv[:, t * 128:(t + 1) * 128]
    for t in range(ntile):
        ln = slice(t * 128, (t + 1) * 128)
        o_ref[0, :, ln] = (gate_ref[0, :, ln].astype(F32)
                           * (yf_ref[t] + zf_ref[t] * bias_ref[:, ln])).astype(BF16)


def _hyconv(zsrc, zcol, zrow, gsrc, gcol, grow, hspec, order, fmat, gmat, bias, ls):
    B = zsrc.shape[0]
    k1p = fmat.shape[1] // 2
    once = lambda a: pl.BlockSpec(a.shape, lambda b: (0,) * a.ndim, pipeline_mode=pl.Buffered(1))
    return pl.pallas_call(
        functools.partial(_hyconv_kernel, ls=ls),
        grid=(B,),
        in_specs=[pl.BlockSpec((1, ls, HY_W), lambda b: (b, zrow, zcol)),
                  pl.BlockSpec((1, ls, HY_W), lambda b: (b, grow, gcol)),
                  pl.BlockSpec((1, 2, HY_N2, k1p, HY_W), lambda b: (order, 0, 0, 0, 0),
                               pipeline_mode=pl.Buffered(1)),
                  once(fmat), once(gmat), pl.BlockSpec((1, HY_W), lambda b: (0, 0))],
        out_specs=pl.BlockSpec((1, ls, HY_W), lambda b: (b, 0, 0)),
        out_shape=jax.ShapeDtypeStruct((B, ls, HY_W), BF16),
        scratch_shapes=[pltpu.VMEM((HY_W // 128, ls, 128), F32), pltpu.VMEM((HY_W // 128, ls, 128), F32),
                        pltpu.VMEM((2, HY_N2, k1p, HY_W), F32)],
        compiler_params=_cparams(("arbitrary",)),
        name="hyena_longconv",
    )(zsrc, gsrc, hspec, fmat, gmat, bias)


def _outproj_kernel(x_ref, mod_ref, ya_ref, ybl_ref, ybc_ref, hf_ref, hb_ref, oc_ref, yd_ref, mlg_ref, g64_ref,
                    w_ref, g2_ref, wr_ref, xo_ref, h2_ref, aff_ref, *, nlat):
    hsum = (hf_ref[0] + hb_ref[0]).T
    msq = _dot((hsum * hsum).astype(BF16), g64_ref[...])
    yc = jax.nn.sigmoid(oc_ref[0].astype(F32)) * (hsum * lax.rsqrt(msq + EPS) * mlg_ref[...])
    yb = jnp.where(pl.program_id(1) < nlat, ybl_ref[0], ybc_ref[0])
    y = jnp.concatenate([ya_ref[0], yb, yc.astype(BF16), yd_ref[0]], axis=1)
    mod = mod_ref[0, 0]
    x = x_ref[0] + mod[2:3, :] * _dot(y, w_ref[...])
    xo_ref[0] = x
    ms = jnp.mean(x * x, axis=-1, keepdims=True)
    h2 = x * lax.rsqrt(ms + EPS) * g2_ref[...] * (1.0 + mod[4:5, :]) + mod[3:4, :]
    h2_hi = h2.astype(BF16)
    h2_ref[0] = _pack_bf16_pair(h2_hi[:, 0:D_MODEL // 2], h2_hi[:, D_MODEL // 2:])
    h2_lo = (h2 - h2_hi.astype(F32)).astype(BF16)
    lg = _dot(h2_hi, wr_ref[...])
    logits = lg[:, 0:128] + lg[:, 128:256] + _dot(h2_lo, wr_ref[:, 0:128])
    lane = lax.broadcasted_iota(I32, logits.shape, 1)
    logits = jnp.where(lane < N_EXPERTS, logits, NEG)
    e = jnp.exp(logits - jnp.max(logits, axis=-1, keepdims=True))
    aff_ref[0] = (e / jnp.sum(e, axis=-1, keepdims=True)).T


def _outproj(x, mod, ya, yb_l, yb_c, hf, hb, oc, yd, mlg, g64, w_out, g2, w_router, T):
    B, N, D = x.shape
    rb = ROW_BLOCK
    nlat = T // rb
    row = lambda w: pl.BlockSpec((1, rb, w), lambda b, r: (b, r, 0))
    full = lambda a: pl.BlockSpec(a.shape, lambda b, r: (0,) * a.ndim)
    return pl.pallas_call(
        functools.partial(_outproj_kernel, nlat=nlat),
        grid=(B, N // rb),
        in_specs=[row(D), pl.BlockSpec((1, 1, 6, D), lambda b, r: (b, jnp.where(r >= nlat, 1, 0), 0, 0)),
                  row(256),
                  pl.BlockSpec((1, rb, 256), lambda b, r: (b, jnp.minimum(r, nlat - 1), 0)),
                  pl.BlockSpec((1, rb, 256), lambda b, r: (b, jnp.maximum(r - nlat, 0), 0)),
                  pl.BlockSpec((1, 256, rb), lambda b, r: (b, 0, r)),
                  pl.BlockSpec((1, 256, rb), lambda b, r: (b, 0, r)), row(256), row(256),
                  full(mlg), full(g64), full(w_out), full(g2), full(w_router)],
        out_specs=[row(D), row(D // 2), pl.BlockSpec((1, 128, rb), lambda b, r: (b, 0, r))],
        out_shape=[jax.ShapeDtypeStruct((B, N, D), F32), jax.ShapeDtypeStruct((B, N, D // 2), I32),
                   jax.ShapeDtypeStruct((B, 128, N), F32)],
        compiler_params=_cparams(("arbitrary", "arbitrary")),
        name="outproj",
    )(x, mod, ya, yb_l, yb_c, hf, hb, oc, yd, mlg, g64, w_out, g2, w_router)


def _prefix_exclusive(x):
    n = x.shape[1]
    lane = lax.broadcasted_iota(I32, x.shape, 1)
    inc = x
    d = 1
    while d < n:
        inc = inc + jnp.where(lane >= d, pltpu.roll(inc, d, axis=1), 0)
        d *= 2
    return inc - x


def _topk_kernel(aff_ref, i_ref, g_ref, pos_ref, val_ref, *, T, S):
    def segment(t0, ts, cap, slot0):
        a = aff_ref[0, :, t0:t0 + ts]
        bits = lax.bitcast_convert_type(a, I32)

        def search(i, v):
            cand = v | jnp.left_shift(jnp.int32(1), 30 - i)
            cnt = jnp.sum((bits >= cand).astype(I32), axis=1, keepdims=True)
            return jnp.where(cnt >= cap, cand, v)

        thr = lax.fori_loop(0, 31, search, jnp.zeros((N_EXPERTS, 1), I32))
        gt = bits > thr
        eq = bits == thr
        need = cap - jnp.sum(gt.astype(I32), axis=1, keepdims=True)
        sel = gt | (eq & (_prefix_exclusive(eq.astype(I32)) < need))
        seli = sel.astype(I32)
        pos_ref[:, 0:ts] = jnp.where(sel, _prefix_exclusive(seli), -1)
        a_hi = a.astype(BF16)
        r1 = a - a_hi.astype(F32)
        a_mid = r1.astype(BF16)
        a_lo = (r1 - a_mid.astype(F32)).astype(BF16)
        val_ref[0, :, 0:ts] = a_hi.astype(F32)
        val_ref[1, :, 0:ts] = a_mid.astype(F32)
        val_ref[2, :, 0:ts] = a_lo.astype(F32)
        tabs = lax.broadcasted_iota(I32, (1, ts), 1) + t0
        t_hi = jnp.right_shift(tabs, 6).astype(F32)
        t_lo = (tabs & 63).astype(F32)
        slot = lax.broadcasted_iota(I32, (cap, ts), 0)
        zeros = jnp.zeros((11, ts), F32)

        def per_expert(e, _):
            onehot = jnp.where(pos_ref[pl.ds(e, 1), 0:ts] == slot, 1.0, 0.0).astype(BF16)
            vals = jnp.concatenate([t_hi, t_lo, val_ref[0, pl.ds(e, 1), 0:ts], val_ref[1, pl.ds(e, 1), 0:ts],
                                    val_ref[2, pl.ds(e, 1), 0:ts], zeros], axis=0).astype(BF16)
            r = _dot_nt(onehot, vals)
            i_ref[0, e, slot0:slot0 + cap, :] = (r[:, 0:1] * 64.0 + r[:, 1:2]).astype(I32)
            g_ref[0, e, slot0:slot0 + cap, :] = r[:, 2:3] + r[:, 3:4] + r[:, 4:5]
            return 0

        lax.fori_loop(0, N_EXPERTS, per_expert, 0)

    segment(0, T, (EC_CAPACITY * T) // N_EXPERTS, 0)
    segment(T, S, (EC_CAPACITY * S) // N_EXPERTS, (EC_CAPACITY * T) // N_EXPERTS)


def _topk(aff_t, T, S):
    B, _, N = aff_t.shape
    E = N_EXPERTS
    cap_t = (EC_CAPACITY * T) // E + (EC_CAPACITY * S) // E
    return pl.pallas_call(
        functools.partial(_topk_kernel, T=T, S=S),
        grid=(B,),
        in_specs=[pl.BlockSpec((1, E, N), lambda b: (b, 0, 0))],
        out_specs=[pl.BlockSpec((1, E, cap_t, 1), lambda b: (b, 0, 0, 0))] * 2,
        out_shape=[jax.ShapeDtypeStruct((B, E, cap_t, 1), I32), jax.ShapeDtypeStruct((B, E, cap_t, 1), F32)],
        scratch_shapes=[pltpu.VMEM((E, T), I32), pltpu.VMEM((3, E, T), F32)],
        compiler_params=_cparams(("arbitrary",)),
        name="expert_topk",
    )(aff_t)


def _gather_kernel(idx_ref, h_ref, *rest, slots, nstep):
    o_ref = rest[-1]
    b, e = pl.program_id(0), pl.program_id(1)
    base = (b * nstep + e) * slots

    def body(g, _):
        rows = [h_ref[0, pl.ds(idx_ref[base + g * 16 + k], 1), :] for k in range(16)]
        o_ref[0, pl.ds(pl.multiple_of(g * 16, 16), 16), :] = jnp.concatenate(rows, axis=0)
        return 0

    lax.fori_loop(0, slots // 16, body, 0)


def _gather(idx_flat, h2p, xe_buf, b0, bg, cap_t):
    B, N, hw = h2p.shape
    slots, nstep = GATHER_EXPERTS * cap_t, N_EXPERTS // GATHER_EXPERTS
    in_specs = [pl.BlockSpec((1, N, hw), lambda b, e, idx: (b0 + b, 0, 0))]
    args = [idx_flat, h2p]
    aliases = {}
    if xe_buf is not None:
        in_specs.append(pl.BlockSpec(memory_space=pl.ANY))
        args.append(xe_buf)
        aliases = {2: 0}
    return pl.pallas_call(
        functools.partial(_gather_kernel, slots=slots, nstep=nstep),
        grid_spec=pltpu.PrefetchScalarGridSpec(
            num_scalar_prefetch=1,
            grid=(bg, nstep),
            in_specs=in_specs,
            out_specs=pl.BlockSpec((1, slots, hw), lambda b, e, idx: (b0 + b, e, 0)),
        ),
        out_shape=jax.ShapeDtypeStruct((B, N_EXPERTS * cap_t, hw), I32),
        input_output_aliases=aliases,
        compiler_params=_cparams(("arbitrary", "arbitrary")),
        name="expert_gather",
    )(*args)


def _ffn_kernel(x_ref, wg_ref, wu_ref, wd_ref, gate_ref, gt_ref, o_ref, wg_s, wu_s, wd_s, *, cap_l):
    @pl.when(pl.program_id(1) == 0)
    def _():
        wg_s[...] = wg_ref[0, 0].astype(BF16)
        wu_s[...] = wu_ref[0, 0].astype(BF16)
        wd_s[...] = wd_ref[0, 0].astype(BF16)

    x = jnp.concatenate(_unpack_bf16_pair(x_ref[0, 0]), axis=1)
    a = _dot(x, wg_s[...])
    u = _dot(x, wu_s[...])
    hmid = (a * jax.nn.sigmoid(a) * u).astype(BF16)
    y = _dot(hmid, wd_s[...])
    gt = gt_ref[0]
    row = lax.broadcasted_iota(I32, y.shape, 0)
    gt2 = jnp.where(row < cap_l, gt[0:1, :], gt[1:2, :])
    o_ref[0, 0] = y * gate_ref[0, 0] * gt2


def _ffn(xe, layer, wg, wu, wd, gate, gt2, cap_l):
    B, E, cap_t, hw = xe.shape
    D = 2 * hw
    wspec = pl.BlockSpec((1, 1, D, D), lambda e, b: (layer, e, 0, 0))
    return pl.pallas_call(
        functools.partial(_ffn_kernel, cap_l=cap_l),
        grid=(E, B),
        in_specs=[pl.BlockSpec((1, 1, cap_t, hw), lambda e, b: (b, e, 0, 0)), wspec, wspec, wspec,
                  pl.BlockSpec((1, 1, cap_t, 1), lambda e, b: (b, e, 0, 0)),
                  pl.BlockSpec((1, 2, D), lambda e, b: (b, 0, 0))],
        out_specs=pl.BlockSpec((1, 1, cap_t, D), lambda e, b: (b, e, 0, 0)),
        out_shape=jax.ShapeDtypeStruct((B, E, cap_t, D), F32),
        scratch_shapes=[pltpu.VMEM((D, D), BF16)] * 3,
        compiler_params=_cparams(("arbitrary", "arbitrary")),
        name="expert_ffn",
    )(xe, wg, wu, wd, gate, gt2)


def _scatter_kernel(idx_ref, y_ref, *rest, slots, nstep):
    o_ref = rest[-1]
    b, e = pl.program_id(0), pl.program_id(1)
    base = (b * nstep + e) * slots

    @pl.when(e == 0)
    def _():
        o_ref[...] = jnp.zeros_like(o_ref)

    def body(g, _):
        tile = y_ref[0, pl.ds(pl.multiple_of(g * 8, 8), 8), :]
        rows = [idx_ref[base + g * 8 + k] for k in range(8)]
        cur = [o_ref[0, pl.ds(r, 1), :] for r in rows]
        for k, r in enumerate(rows):
            o_ref[0, pl.ds(r, 1), :] = cur[k] + tile[k:k + 1, :]
        return 0

    lax.fori_loop(0, slots // 8, body, 0)


def _scatter(idx_flat, ye, acc_buf, n_rows, b0, bg, cap_t):
    B, _, D = ye.shape
    slots, nstep = SCATTER_EXPERTS * cap_t, N_EXPERTS // SCATTER_EXPERTS
    in_specs = [pl.BlockSpec((1, slots, D), lambda b, e, idx: (b0 + b, e, 0))]
    args = [idx_flat, ye]
    aliases = {}
    if acc_buf is not None:
        in_specs.append(pl.BlockSpec(memory_space=pl.ANY))
        args.append(acc_buf)
        aliases = {2: 0}
    return pl.pallas_call(
        functools.partial(_scatter_kernel, slots=slots, nstep=nstep),
        grid_spec=pltpu.PrefetchScalarGridSpec(
            num_scalar_prefetch=1,
            grid=(bg, nstep),
            in_specs=in_specs,
            out_specs=pl.BlockSpec((1, n_rows, D), lambda b, e, idx: (b0 + b, 0, 0)),
        ),
        out_shape=jax.ShapeDtypeStruct((B, n_rows, D), F32),
        input_output_aliases=aliases,
        compiler_params=_cparams(("arbitrary", "arbitrary")),
        name="expert_scatter",
    )(*args)


def _residual_out_kernel(x_ref, f_ref, o_ref):
    o_ref[...] = x_ref[...] + f_ref[...]


def _residual_out(x, f, T):
    B, N, D = x.shape
    rb = ROW_BLOCK
    spec = pl.BlockSpec((1, rb, D), lambda b, r: (b, r, 0))
    return pl.pallas_call(
        _residual_out_kernel,
        grid=(B, T // rb),
        in_specs=[spec, spec],
        out_specs=spec,
        out_shape=jax.ShapeDtypeStruct((B, T, D), F32),
        compiler_params=_cparams(("arbitrary", "arbitrary")),
        name="residual_out",
    )(x, f)


def _rope_tables(T, S, dh):
    rows = T // GRID_W
    r = jnp.broadcast_to(jnp.arange(rows, dtype=F32)[:, None], (rows, GRID_W)).reshape(T)
    col = jnp.broadcast_to(jnp.arange(GRID_W, dtype=F32)[None, :], (rows, GRID_W)).reshape(T)
    nf = dh // 4
    inv = ROPE_BASE ** (-jnp.arange(nf, dtype=F32) / nf)
    ar, ac = r[:, None] * inv, col[:, None] * inv
    ang = jnp.concatenate([ar, ar, ac, ac], axis=1)
    ang = jnp.concatenate([ang, jnp.zeros((S, dh), F32)], axis=0)
    reps = 256 // dh
    return jnp.tile(jnp.cos(ang), (1, reps)), jnp.tile(jnp.sin(ang), (1, reps))


def _group_mats(dh):
    i = np.arange(256)
    gmat = (i[:, None] // dh == i[None, :] // dh).astype(np.float32) / dh
    nf = dh // 4
    half = (i % (2 * nf)) // nf
    pmat = np.zeros((256, 256), np.float32)
    a_idx = i[half == 0]
    pmat[a_idx + nf, a_idx] = -1.0
    pmat[a_idx, a_idx + nf] = 1.0
    return jnp.asarray(gmat, BF16), jnp.asarray(pmat, BF16)


def _hyena_tables(ls):
    t = jnp.linspace(0.0, 1.0, ls, dtype=F32)[:, None]
    w = 2.0 * math.pi * jnp.arange(ls, dtype=F32)[:, None] / ls
    bands = jnp.linspace(1e-4, HY_BANDS - 1, HY_BANDS, dtype=F32)
    z = jnp.concatenate([t, jnp.cos(bands * w), -jnp.sin(bands * w)], axis=-1)
    z = jnp.pad(z, ((0, 0), (0, 128 - HY_EMB)))
    deltas = jnp.abs(jnp.linspace(math.log(HY_TARGET) / HY_SLOW, math.log(HY_TARGET) / HY_FAST, HY_W, dtype=F32))
    decay = jnp.exp(-t * deltas)
    n, nseg = 2 * ls, ls // HY_N2
    k1p = -(-(nseg + 1) // 8) * 8
    k1 = jnp.arange(k1p, dtype=I32)[None, :, None]
    tpos = HY_N2 * jnp.arange(nseg, dtype=I32)[None, None, :] + jnp.arange(HY_N2, dtype=I32)[:, None, None]
    ang = ((k1 * tpos) % n).astype(F32) * (2.0 * math.pi / n)
    keep = k1 <= nseg
    fmat = jnp.concatenate([jnp.where(keep, jnp.cos(ang), 0.0), jnp.where(keep, -jnp.sin(ang), 0.0)],
                           axis=1).astype(BF16)
    return z, decay, fmat, jnp.swapaxes(fmat, 1, 2)


def kernel(x, c, ctx, c_ctx, w_ada, b_ada, norm1_g, norm2_g, w_in, b_gate, a_qnorm, a_knorm, a_sink, hy_conv, hy_fw1, hy_fb1, hy_freq, hy_fw2, hy_fb2, hy_fw3, hy_bias, ml_norm, d_qnorm, d_knorm, d_lq1, d_lk1, d_lq2, d_lk2, d_subnorm, w_out, w_router, w_e_gate, w_e_up, w_e_down):
    B, T, D = x.shape
    S = ctx.shape[1]
    N = T + S
    depth = w_ada.shape[0]
    assert D == D_MODEL and T % S == 0 and S % ROW_BLOCK == 0 and S % MLSTM_CHUNK == 0
    assert S % WATTN_QROWS == 0 and T >= WATTN_QROWS + 2 * BLK and WATTN_QROWS & (WATTN_QROWS - 1) == 0
    cap_l, cap_c = (EC_CAPACITY * T) // N_EXPERTS, (EC_CAPACITY * S) // N_EXPERTS
    cap_t = cap_l + cap_c
    assert cap_t % 16 == 0

    xs = jnp.concatenate([x, ctx], axis=1)

    rpad = -(B + 1) % 8
    cc = jnp.concatenate([c, c_ctx[None, :], jnp.zeros((rpad, D), F32)], axis=0)
    mods = _ada_mods(cc, w_ada, b_ada)

    cos_a, sin_a = _rope_tables(T, S, DH_A)
    cos_d, sin_d = _rope_tables(T, S, DH_D)
    g64, p64 = _group_mats(DH_A)
    g32, p32 = _group_mats(DH_D)
    tabs_l = _hyena_tables(T)
    tabs_c = _hyena_tables(S)

    offs = np.cumsum((0, 256, 128, 128, 768, 256, 256, 256, 256, 16, 256, 256, 256))
    ffn_acc = None
    for l in range(depth):
        lam_init = 0.8 - 0.6 * math.exp(-0.3 * l)
        ml = mods[l]
        mod = jnp.stack([ml[:B].reshape(B, 6, D), jnp.broadcast_to(ml[B].reshape(1, 6, D), (B, 6, D))],
                        axis=1)
        w = w_in[l]
        wp = jnp.concatenate([w[:, offs[0]:offs[8]], w[:, offs[8]:offs[9]], jnp.zeros((D, 112), F32),
                              w[:, offs[9]:offs[12]]], axis=1).astype(BF16)
        gains = jnp.stack([jnp.tile(a_qnorm[l], 4), jnp.tile(a_knorm[l], 4),
                           jnp.tile(d_qnorm[l], 8), jnp.tile(d_knorm[l], 8)], axis=0)
        proj = _inproj(xs, ffn_acc, mod, norm1_g[l].reshape(1, D), wp, (cos_a, sin_a, cos_d, sin_d), gains,
                       (g64, g32, p64, p32), T)
        qa, ka, va, ub, kc, oc, gc, qd, kd, vd, gct, qct, vct = proj[0:13]
        if ffn_acc is not None:
            xs = proj[13]

        ya = _wattn(qa, ka, va, jnp.pad(a_sink[l], (0, 128 - H_A)).reshape(1, 128),
                    jnp.max(jnp.abs(a_qnorm[l])).reshape(1, 1), T, S)
        yd = _dattn(qd, kd, vd, jnp.stack([d_lq1[l], d_lk1[l], d_lq2[l], d_lk2[l]], axis=0),
                    d_subnorm[l].reshape(1, 2 * DH_D), jnp.max(jnp.abs(d_qnorm[l])).reshape(1, 1), T, S, lam_init)

        bias = b_gate[l].reshape(-1)
        hf, hb = _mlstm(qct, kc, vct, gc, gct, jnp.pad(bias, (0, 112)).reshape(1, 128), bias.reshape(16, 1), T, S)

        uc = _shortconv(ub, hy_conv[l], T, S)
        w1 = jnp.pad(hy_fw1[l], ((0, 128 - HY_EMB), (0, 0)))
        ybs = []
        for (zf, decay, fmat, gmat), ls, rowblk in ((tabs_l, T, 0), (tabs_c, S, T // S)):
            hpast, hfut, ss = _hyfilter(zf, w1, hy_fb1[l].reshape(1, -1), hy_freq[l].reshape(1, -1), hy_fw2[l],
                                        hy_fb2[l].reshape(1, -1), hy_fw3[l], decay)
            hspec = _hyspec(hpast, hfut, ss, fmat)
            z1 = _hyconv(uc, 0, rowblk, uc, 1, rowblk, hspec, 0, fmat, gmat, hy_bias[l, 0].reshape(1, HY_W), ls)
            z2 = _hyconv(z1, 0, 0, uc, 2, rowblk, hspec, 1, fmat, gmat, hy_bias[l, 1].reshape(1, HY_W), ls)
            ybs.append(z2)

        wr = jnp.pad(w_router[l], ((0, 0), (0, 128 - N_EXPERTS)))
        wr_hi = wr.astype(BF16)
        wr_cat = jnp.concatenate([wr_hi, (wr - wr_hi.astype(F32)).astype(BF16)], axis=1)
        xs, h2, aff = _outproj(xs, mod, ya, ybs[0], ybs[1], hf, hb, oc, yd, jnp.tile(ml_norm[l], 4).reshape(1, 256), g64,
                               w_out[l].astype(BF16), norm2_g[l].reshape(1, D), wr_cat, T)

        idx, gate = _topk(aff, T, S)
        gt2 = mod[:, :, 5, :]
        bg = min(IDX_BATCH_GROUP, B)
        groups = [(b0, idx[b0:b0 + bg].reshape(-1)) for b0 in range(0, B, bg)]
        xe = None
        for b0, idx_flat in groups:
            xe = _gather(idx_flat, h2, xe, b0, bg, cap_t)
        ye = _ffn(xe.reshape(B, N_EXPERTS, cap_t, D // 2), l, w_e_gate, w_e_up, w_e_down, gate, gt2, cap_l)
        ye = ye.reshape(B, N_EXPERTS * cap_t, D)
        ffn_acc = None
        for b0, idx_flat in groups:
            ffn_acc = _scatter(idx_flat, ye, ffn_acc, N, b0, bg, cap_t)
    return _residual_out(xs, ffn_acc, T)
```

```python
import functools
import math

import jax
import jax.numpy as jnp
import numpy as np
from jax import lax
from jax.experimental import pallas as pl
from jax.experimental.pallas import tpu as pltpu

F32 = jnp.float32
BF16 = jnp.bfloat16
I32 = jnp.int32
HI = lax.Precision.HIGHEST

D_MODEL = 1024
GRID_W = 64
GROUP_W = D_MODEL // 4
H_A, HKV_A = 4, 2
G_A = H_A // HKV_A
DH_A = GROUP_W // H_A
WINDOW = 128
BLK = 128
HY_W = GROUP_W
HY_ORDER = 2
HY_BANDS = 16
HY_EMB = 1 + 2 * HY_BANDS
HY_HID = 64
HY_TARGET, HY_FAST, HY_SLOW = 1e-2, 0.3, 1.5
HY_N2 = 16
H_C = 4
DH_C = GROUP_W // H_C
H_D = 4
DH_D = GROUP_W // (2 * H_D)
N_EXPERTS = 16
EC_CAPACITY = 2
ROPE_BASE = 10000.0
EPS = 1e-6
NEG = -1e30

ROW_BLOCK = 256
WATTN_QROWS = 256
DATTN_MIN_ROWSUM = 1e-25
MLSTM_CHUNK = 256
IDX_BATCH_GROUP = 4
GATHER_EXPERTS = 4
SCATTER_EXPERTS = 2
VMEM_LIMIT = 56 * 1024 * 1024

_C_QA, _C_KA, _C_VA, _C_UB = 0, 256, 384, 512
_C_QC, _C_KC, _C_VC, _C_OC, _C_GC = 1280, 1536, 1792, 2048, 2304
_C_QD, _C_KD, _C_VD, _C_END = 2432, 2688, 2944, 3200


def _cparams(sem, vmem=VMEM_LIMIT):
    return pltpu.CompilerParams(dimension_semantics=sem, vmem_limit_bytes=vmem)


def _dot(a, b, **kw):
    return jnp.dot(a, b, preferred_element_type=F32, **kw)


def _dot_nt(a, b, **kw):
    return lax.dot_general(a, b, (((1,), (1,)), ((), ())), preferred_element_type=F32, **kw)


def _pack_bf16_pair(a, b):
    ua = lax.bitcast_convert_type(a.astype(F32), I32)
    ub = lax.bitcast_convert_type(b.astype(F32), I32)
    return ua | lax.shift_right_logical(ub, 16)


def _unpack_bf16_pair(w):
    a = lax.bitcast_convert_type(w & jnp.int32(-65536), F32)
    b = lax.bitcast_convert_type(lax.shift_left(w, 16), F32)
    return a.astype(BF16), b.astype(BF16)


def _ada_kernel(c_ref, w_ref, b_ref, o_ref):
    c = c_ref[...]
    s = c * jax.nn.sigmoid(c)
    o_ref[0] = _dot(s, w_ref[0], precision=HI) + b_ref[0]


def _ada_mods(cc, w_ada, b_ada):
    L, D, W6 = w_ada.shape
    R = cc.shape[0]
    cb = 1536
    return pl.pallas_call(
        _ada_kernel,
        grid=(L, W6 // cb),
        in_specs=[pl.BlockSpec((R, D), lambda l, j: (0, 0)),
                  pl.BlockSpec((1, D, cb), lambda l, j: (l, 0, j)),
                  pl.BlockSpec((1, 1, cb), lambda l, j: (l, 0, j))],
        out_specs=pl.BlockSpec((1, R, cb), lambda l, j: (l, 0, j)),
        out_shape=jax.ShapeDtypeStruct((L, R, W6), F32),
        compiler_params=_cparams(("arbitrary", "arbitrary")),
        name="ada_mods",
    )(cc, w_ada, b_ada.reshape(L, 1, W6))


def _inproj_kernel(*refs, has_f):
    if has_f:
        x_ref, f_ref, *refs = refs
        *refs, xsum = refs
        x = x_ref[0] + f_ref[0]
        xsum[0] = x
    else:
        x_ref, *refs = refs
        x = x_ref[0]
    (mod_ref, g1_ref, w_ref, ca_ref, sa_ref, cd_ref, sd_ref, gains_ref, g64_ref, g32_ref, p64_ref, p32_ref,
     qa, ka, va, ub, kc, oc, gc, qd, kd, vd, gct, qct, vct) = refs
    ms = jnp.mean(x * x, axis=-1, keepdims=True)
    xn = x * lax.rsqrt(ms + EPS) * g1_ref[...]
    mod = mod_ref[0, 0]
    h = xn * (1.0 + mod[1:2, :]) + mod[0:1, :]
    p = _dot(h.astype(BF16), w_ref[...])

    def headnorm_rope(t, gmat, gain, pmat, cos, sin, scale):
        w = t.shape[1]
        msq = _dot((t * t).astype(BF16), gmat[0:w, 0:w])
        tn = t * lax.rsqrt(msq + EPS) * gain
        tn = tn * cos[:, 0:w] + _dot(tn.astype(BF16), pmat[0:w, 0:w]) * sin[:, 0:w]
        return tn * scale

    ca, sa, cd, sd = ca_ref[...], sa_ref[...], cd_ref[...], sd_ref[...]
    gains = gains_ref[...]
    qa[0] = headnorm_rope(p[:, _C_QA:_C_KA], g64_ref, gains[0:1, :], p64_ref, ca, sa, DH_A ** -0.5).astype(BF16)
    ka[0] = headnorm_rope(p[:, _C_KA:_C_VA], g64_ref, gains[1:2, 0:128], p64_ref, ca, sa, 1.0).astype(BF16)
    va[0] = p[:, _C_VA:_C_UB].astype(BF16)
    ub[0] = p[:, _C_UB:_C_QC].astype(BF16)
    qct[0] = p[:, _C_QC:_C_KC].T.astype(BF16)
    kc[0] = (p[:, _C_KC:_C_VC] * DH_C ** -0.5).astype(BF16)
    vct[0] = p[:, _C_VC:_C_OC].T.astype(BF16)
    oc[0] = p[:, _C_OC:_C_GC].astype(BF16)
    gc[0] = p[:, _C_GC:_C_QD]
    gct[0] = p[:, _C_GC:_C_QD].T
    qd[0] = headnorm_rope(p[:, _C_QD:_C_KD], g32_ref, gains[2:3, :], p32_ref, cd, sd, DH_D ** -0.5).astype(BF16)
    kd[0] = headnorm_rope(p[:, _C_KD:_C_VD], g32_ref, gains[3:4, :], p32_ref, cd, sd, 1.0).astype(BF16)
    vd[0] = p[:, _C_VD:_C_END].astype(BF16)


def _inproj(x, f, mod, g1, w_packed, tabs, gains, mats, T):
    B, N, D = x.shape
    rb = ROW_BLOCK
    nlat = T // rb
    row = lambda w: pl.BlockSpec((1, rb, w), lambda b, r: (b, r, 0))
    tab = pl.BlockSpec((rb, 256), lambda b, r: (r, 0))
    full = lambda a: pl.BlockSpec(a.shape, lambda b, r: (0,) * a.ndim)
    widths = [256, 128, 128, 768, 256, 256, 128, 256, 256, 256]
    dtypes = [BF16] * 6 + [F32] + [BF16] * 3
    tr = lambda w: pl.BlockSpec((1, w, rb), lambda b, r: (b, 0, r))
    has_f = f is not None
    return pl.pallas_call(
        functools.partial(_inproj_kernel, has_f=has_f),
        grid=(B, N // rb),
        in_specs=[row(D)] * (2 if has_f else 1)
        + [pl.BlockSpec((1, 1, 6, D), lambda b, r: (b, jnp.where(r >= nlat, 1, 0), 0, 0)),
           full(g1), full(w_packed), tab, tab, tab, tab, full(gains)] + [full(m) for m in mats],
        out_specs=[row(w) for w in widths] + [tr(128), tr(256), tr(256)] + ([row(D)] if has_f else []),
        out_shape=[jax.ShapeDtypeStruct((B, N, w), dt) for w, dt in zip(widths, dtypes)]
        + [jax.ShapeDtypeStruct((B, 128, N), F32), jax.ShapeDtypeStruct((B, 256, N), BF16),
           jax.ShapeDtypeStruct((B, 256, N), BF16)] + ([jax.ShapeDtypeStruct((B, N, D), F32)] if has_f else []),
        compiler_params=_cparams(("arbitrary", "arbitrary")),
        name="inproj",
    )(x, *([f] if has_f else []), mod, g1, w_packed, *tabs, gains, *mats)


def _wattn_kernel(q_ref, k_ref, v_ref, sink_ref, sinkw_ref, placeq_ref, sb_ref, placek_ref, o_ref,
                  kaug_ref, vaug_ref, *, T, S):
    j = pl.program_id(1)
    qb, kwin = WATTN_QROWS, WATTN_QROWS + 2 * BLK
    is_lat = j < T // qb
    start = pl.multiple_of(jnp.clip(j * qb - BLK, 0, T - kwin), BLK)
    row = lax.broadcasted_iota(I32, (G_A * qb, kwin), 0)
    qpos = j * qb + (row & (qb - 1))
    kpos = start + lax.broadcasted_iota(I32, (G_A * qb, kwin), 1)
    valid = (jnp.abs(qpos - kpos) <= WINDOW) & is_lat
    grp = jnp.right_shift(lax.broadcasted_iota(I32, (G_A * qb, 1), 0), qb.bit_length() - 1)
    sinks = sink_ref[...]

    @pl.when(j == 0)
    def _():
        lane = lax.broadcasted_iota(I32, (1, HKV_A * 128), 1)
        one = jnp.where((lane & 127) == DH_A, 1.0, 0.0)
        kaug_ref[...] = (_dot(k_ref[0], placek_ref[...]) + one).astype(BF16)
        vaug_ref[...] = (_dot(v_ref[0], placek_ref[...]) + one).astype(BF16)

    def sink_rows(heads):
        sink = sinks[0:1, heads[0]:heads[0] + 1]
        for g in range(1, G_A):
            sink = jnp.where(grp == g, sinks[0:1, heads[g]:heads[g] + 1], sink)
        return sink

    def store(heads, o):
        for g, hd in enumerate(heads):
            o_ref[0, :, hd * DH_A:(hd + 1) * DH_A] = o[g * qb:(g + 1) * qb].astype(BF16)

    def body_fast():
        q_wide = _dot(q_ref[0], placeq_ref[...])
        lane = lax.broadcasted_iota(I32, (1, H_A * 128), 1)
        shift = jnp.where((lane & 127) == DH_A, jnp.maximum(sb_ref[...], sinkw_ref[...]), 0.0)
        q_aug = (q_wide - shift).astype(BF16)
        low = None
        for hk in range(HKV_A):
            heads = [hk * G_A + g for g in range(G_A)]
            ks = slice(hk * 128, (hk + 1) * 128)
            q2 = jnp.concatenate([q_aug[:, hd * 128:(hd + 1) * 128] for hd in heads], axis=0)
            m_used = -q2[:, DH_A:DH_A + 1].astype(F32)
            s_loc = jnp.where(valid, _dot_nt(q2, kaug_ref[pl.ds(start, kwin), ks]), NEG)
            s_ctx = _dot_nt(q2, kaug_ref[T:T + S, ks])
            pv = (_dot(jnp.exp(s_loc.astype(BF16)), vaug_ref[pl.ds(start, kwin), ks])
                  + _dot(jnp.exp(s_ctx.astype(BF16)), vaug_ref[T:T + S, ks]))
            sigma = pv[:, DH_A:DH_A + 1]
            low = sigma if low is None else jnp.minimum(low, sigma)
            store(heads, pv[:, 0:DH_A] / (sigma + jnp.exp(sink_rows(heads) - m_used)))
        return jnp.min(low)

    def body_exact():
        kw = k_ref[0, pl.ds(start, kwin), :]
        vw = v_ref[0, pl.ds(start, kwin), :]
        kc = k_ref[0, T:T + S, :]
        vc = v_ref[0, T:T + S, :]
        for hk in range(HKV_A):
            cs = slice(hk * DH_A, (hk + 1) * DH_A)
            kwh, vwh, kch, vch = kw[:, cs], vw[:, cs], kc[:, cs], vc[:, cs]
            heads = [hk * G_A + g for g in range(G_A)]
            q = jnp.concatenate([q_ref[0, :, hd * DH_A:(hd + 1) * DH_A] for hd in heads], axis=0)
            sink = sink_rows(heads)
            s_loc = jnp.where(valid, _dot_nt(q, kwh), NEG)
            s_ctx = _dot_nt(q, kch)
            m = jnp.maximum(jnp.maximum(jnp.max(s_loc, axis=-1, keepdims=True),
                                        jnp.max(s_ctx, axis=-1, keepdims=True)), sink)
            p_loc = jnp.exp(s_loc - m)
            p_ctx = jnp.exp(s_ctx - m)
            den = (jnp.sum(p_loc, axis=-1, keepdims=True) + jnp.sum(p_ctx, axis=-1, keepdims=True)
                   + jnp.exp(sink - m))
            store(heads, (_dot(p_loc.astype(BF16), vwh) + _dot(p_ctx.astype(BF16), vch)) / den)

    low = body_fast()

    @pl.when(jnp.logical_not(low > DATTN_MIN_ROWSUM))
    def _():
        body_exact()


def _wattn_place(nheads):
    place = np.zeros((nheads * DH_A, nheads * 128), np.float32)
    for h in range(nheads):
        for d in range(DH_A):
            place[h * DH_A + d, h * 128 + d] = 1.0
    return jnp.asarray(place, BF16)


def _wattn(qa, ka, va, sink, sbound, T, S):
    B, N, _ = qa.shape
    qb = WATTN_QROWS
    placeq, placek = _wattn_place(H_A), _wattn_place(HKV_A)
    sinkw = jnp.zeros((1, H_A * 128), F32).at[0, DH_A::128].set(sink[0, 0:H_A])
    full = lambda a: pl.BlockSpec(a.shape, lambda b, j: (0,) * a.ndim)
    return pl.pallas_call(
        functools.partial(_wattn_kernel, T=T, S=S),
        grid=(B, N // qb),
        in_specs=[pl.BlockSpec((1, qb, 256), lambda b, j: (b, j, 0)),
                  pl.BlockSpec((1, N, 128), lambda b, j: (b, 0, 0)),
                  pl.BlockSpec((1, N, 128), lambda b, j: (b, 0, 0)),
                  pl.BlockSpec((1, 128), lambda b, j: (0, 0)),
                  full(sinkw), full(placeq), full(sbound), full(placek)],
        out_specs=pl.BlockSpec((1, qb, 256), lambda b, j: (b, j, 0)),
        out_shape=jax.ShapeDtypeStruct((B, N, 256), BF16),
        scratch_shapes=[pltpu.VMEM((N, HKV_A * 128), BF16), pltpu.VMEM((N, HKV_A * 128), BF16)],
        compiler_params=_cparams(("arbitrary", "arbitrary")),
        name="window_attn",
    )(qa, ka, va, sink, sinkw, placeq, sbound, placek)


def _dattn_kernel(q_ref, k_ref, v_ref, lam_ref, sub_ref, place_ref, sb_ref, o_ref,
                  vaug_ref, kch_ref, qch_ref, tch_ref, *, T, S, lam_init):
    j = pl.program_id(1)
    shift_lane = (lax.broadcasted_iota(I32, (1, 4 * H_D * DH_D), 1) & (2 * DH_D - 1)) == DH_D
    lv = lam_ref[...]
    lam = (jnp.exp(jnp.sum(lv[0:1, :] * lv[1:2, :], axis=-1, keepdims=True))
           - jnp.exp(jnp.sum(lv[2:3, :] * lv[3:4, :], axis=-1, keepdims=True)) + lam_init)
    subg = sub_ref[...] * (1.0 - lam_init)
    dv = 2 * DH_D
    nchain, aw = 2 * H_D, 2 * DH_D

    @pl.when(j == 0)
    def _():
        ones = jnp.where(lax.broadcasted_iota(I32, (T + S, dv), 1) == 0, 1.0, 0.0).astype(BF16)
        for hd in range(H_D):
            vaug_ref[hd] = jnp.concatenate([v_ref[0, :, hd * dv:(hd + 1) * dv], ones], axis=1)
        k_aug = (_dot(k_ref[0], place_ref[...]) + jnp.where(shift_lane, 1.0, 0.0)).astype(BF16)
        for i in range(nchain):
            kch_ref[i] = k_aug[:, i * aw:(i + 1) * aw]

    def finish(hd, terms):
        o = terms[0] - lam * terms[1]
        ms = jnp.mean(o * o, axis=-1, keepdims=True)
        o_ref[0, :, hd * dv:(hd + 1) * dv] = (o * lax.rsqrt(ms + EPS) * subg).astype(BF16)

    def body_fast(k0, nk):
        q_aug = (_dot(q_ref[0], place_ref[...]) - jnp.where(shift_lane, sb_ref[...], 0.0)).astype(BF16)
        terms, low = [], None
        for i in range(nchain):
            hd, cs = i // 2, slice(i * aw, (i + 1) * aw)
            s = _dot_nt(q_aug[:, cs], kch_ref[i, k0:k0 + nk, :])
            pv = _dot(jnp.exp(s.astype(BF16)), vaug_ref[hd, k0:k0 + nk, :])
            sigma = pv[:, dv:dv + 1]
            low = sigma if low is None else jnp.minimum(low, sigma)
            terms.append(pv[:, 0:dv] / sigma)
            if i % 2 == 1:
                finish(hd, terms)
                terms = []
        return jnp.min(low)

    def body_exact(k0, nk):
        q_wide = _dot(q_ref[0], place_ref[...])
        for i in range(nchain):
            qch_ref[i] = q_wide[:, i * aw:(i + 1) * aw].astype(BF16)

        def chain(i, carry):
            s = _dot_nt(qch_ref[i], kch_ref[i, k0:k0 + nk, :])
            e = jnp.exp((s - jnp.max(s, axis=-1, keepdims=True)).astype(BF16))
            pv = _dot(e, vaug_ref[jnp.right_shift(i, 1), k0:k0 + nk, :])
            tch_ref[i] = pv / pv[:, dv:dv + 1]
            return carry

        lax.fori_loop(0, nchain, chain, 0)
        for hd in range(H_D):
            finish(hd, [tch_ref[2 * hd][:, 0:dv], tch_ref[2 * hd + 1][:, 0:dv]])

    def body(k0, nk):
        low = body_fast(k0, nk)

        @pl.when(jnp.logical_not(low > DATTN_MIN_ROWSUM))
        def _():
            body_exact(k0, nk)

    @pl.when(j < T // ROW_BLOCK)
    def _():
        body(0, T + S)

    @pl.when(j >= T // ROW_BLOCK)
    def _():
        body(T, S)


def _dattn_place():
    nchain, aw = 2 * H_D, 2 * DH_D
    place = np.zeros((nchain * DH_D, nchain * aw), np.float32)
    for i in range(nchain):
        for d in range(DH_D):
            place[i * DH_D + d, i * aw + d] = 1.0
    return jnp.asarray(place, BF16)


def _dattn(qd, kd, vd, lam_vecs, subg, sbound, T, S, lam_init):
    B, N, _ = qd.shape
    rb = ROW_BLOCK
    place = _dattn_place()
    full = lambda a: pl.BlockSpec(a.shape, lambda b, j: (0,) * a.ndim)
    return pl.pallas_call(
        functools.partial(_dattn_kernel, T=T, S=S, lam_init=lam_init),
        grid=(B, N // rb),
        in_specs=[pl.BlockSpec((1, rb, 256), lambda b, j: (b, j, 0)),
                  pl.BlockSpec((1, N, 256), lambda b, j: (b, 0, 0)),
                  pl.BlockSpec((1, N, 256), lambda b, j: (b, 0, 0)),
                  pl.BlockSpec((4, DH_D), lambda b, j: (0, 0)),
                  pl.BlockSpec((1, 2 * DH_D), lambda b, j: (0, 0)), full(place), full(sbound)],
        out_specs=pl.BlockSpec((1, rb, 256), lambda b, j: (b, j, 0)),
        out_shape=jax.ShapeDtypeStruct((B, N, 256), BF16),
        scratch_shapes=[pltpu.VMEM((H_D, N, 4 * DH_D), BF16), pltpu.VMEM((2 * H_D, N, 2 * DH_D), BF16),
                        pltpu.VMEM((2 * H_D, rb, 2 * DH_D), BF16), pltpu.VMEM((2 * H_D, rb, 4 * DH_D), F32)],
        compiler_params=_cparams(("arbitrary", "arbitrary")),
        name="diff_attn",
    )(qd, kd, vd, lam_vecs, subg, place, sbound)


def _log_sigmoid(x):
    return jnp.minimum(x, 0.0) - jnp.log1p(jnp.exp(-jnp.abs(x)))


def _dot3(a, b, split_b):
    x = b if split_b else a
    hi = x.astype(BF16)
    r1 = x - hi.astype(F32)
    mid = r1.astype(BF16)
    lo = (r1 - mid.astype(F32)).astype(BF16)
    if split_b:
        a = a.astype(BF16)
        return _dot(a, hi) + _dot(a, mid) + _dot(a, lo)
    b = b.astype(BF16)
    return _dot(hi, b) + _dot(mid, b) + _dot(lo, b)


def _mlstm_kernel(qtf_ref, kf_ref, vtf_ref, gf_ref, gtf_ref, qtb_ref, kb_ref, vtb_ref, gb_ref, gtb_ref,
                  bias_ref, biast_ref, hf_ref, hb_ref, c_ref, m_ref):
    i = pl.program_id(1)
    lc = MLSTM_CHUNK

    @pl.when(i == 0)
    def _():
        c_ref[...] = jnp.zeros_like(c_ref)
        m_ref[...] = jnp.zeros_like(m_ref)

    r_io = lax.broadcasted_iota(I32, (lc, lc), 0)
    c_io = lax.broadcasted_iota(I32, (lc, lc), 1)
    lower = r_io >= c_io
    upper = r_io <= c_io
    tri_lo = jnp.where(lower, 1.0, 0.0).astype(F32)
    tri_up = jnp.where(upper, 1.0, 0.0).astype(F32)
    ones_rows = jnp.where(lax.broadcasted_iota(I32, (DH_C, lc), 0) == 0, 1.0, 0.0).astype(BF16)

    def direction(d, qt_ref, k_ref, vt_ref, g_ref, gt_ref, out_ref):
        pre = g_ref[0] + bias_ref[...]
        pre_t = gt_ref[0] + biast_ref[...]
        lf = _log_sigmoid(pre)
        lf_t = _log_sigmoid(pre_t)
        if d == 0:
            bcol = _dot3(tri_lo, lf, True)
            brow = _dot3(lf_t, tri_up, False)
            valid, end = upper, lc - 1
        else:
            bcol = _dot3(tri_up, lf, True)
            brow = _dot3(lf_t, tri_lo, False)
            valid, end = lower, 0
        for hd in range(H_C):
            ci, cf = d * H_C + hd, (2 + d) * H_C + hd
            rs = slice(hd * DH_C, (hd + 1) * DH_C)
            q_t = qt_ref[0, rs, :]
            k = k_ref[0, :, rs]
            v_aug = jnp.concatenate([vt_ref[0, rs, :], ones_rows], axis=0)
            b_row = brow[cf:cf + 1, :]
            ig_row = pre_t[ci:ci + 1, :]
            bi_col = bcol[:, cf:cf + 1] - pre[:, ci:ci + 1]
            b_end = brow[cf:cf + 1, end:end + 1]
            slot = d * H_C + hd
            c = c_ref[slot]
            m = m_ref[slot][:, 0:1]
            logw = jnp.where(valid, b_row - bi_col, NEG)
            inter = b_row + m
            m_t = jnp.maximum(inter, jnp.max(logw, axis=0, keepdims=True))
            w_in = jnp.exp(inter - m_t)
            p_t = (_dot(k, q_t) * jnp.exp(logw - m_t)).astype(BF16)
            num = w_in * _dot(c.astype(BF16), q_t) + _dot(v_aug, p_t)
            den = jnp.maximum(jnp.abs(num[DH_C:DH_C + 1, :]), jnp.exp(-m_t))
            out_ref[0, rs, :] = num[0:DH_C, :] / den
            g_end = b_end - b_row + ig_row
            m_new = jnp.maximum(b_end + m, jnp.max(g_end, axis=1, keepdims=True))
            wv = (v_aug.astype(F32) * jnp.exp(g_end - m_new)).astype(BF16)
            c_ref[slot] = jnp.exp(b_end + m - m_new) * c + _dot(wv, k)
            m_ref[slot] = jnp.broadcast_to(m_new, (1, 128))

    direction(0, qtf_ref, kf_ref, vtf_ref, gf_ref, gtf_ref, hf_ref)
    direction(1, qtb_ref, kb_ref, vtb_ref, gb_ref, gtb_ref, hb_ref)


def _mlstm(qct, kc, vct, gc, gct, bias, biast, T, S):
    B, N, _ = kc.shape
    lc = MLSTM_CHUNK
    nch, nlat = N // lc, T // lc
    fwd = lambda i: jnp.where(i < nch - nlat, nlat + i, i - (nch - nlat))
    bwd = lambda i: nch - 1 - i
    tok = lambda f: pl.BlockSpec((1, lc, 256), lambda b, i: (b, f(i), 0))
    tok_t = lambda f: pl.BlockSpec((1, 256, lc), lambda b, i: (b, 0, f(i)))
    gate = lambda f: pl.BlockSpec((1, lc, 128), lambda b, i: (b, f(i), 0))
    gate_t = lambda f: pl.BlockSpec((1, 16, lc), lambda b, i: (b, 0, f(i)))
    return pl.pallas_call(
        _mlstm_kernel,
        grid=(B, nch),
        in_specs=[tok_t(fwd), tok(fwd), tok_t(fwd), gate(fwd), gate_t(fwd),
                  tok_t(bwd), tok(bwd), tok_t(bwd), gate(bwd), gate_t(bwd),
                  pl.BlockSpec((1, 128), lambda b, i: (0, 0)),
                  pl.BlockSpec((16, 1), lambda b, i: (0, 0))],
        out_specs=[tok_t(fwd), tok_t(bwd)],
        out_shape=[jax.ShapeDtypeStruct((B, 256, N), F32)] * 2,
        scratch_shapes=[pltpu.VMEM((2 * H_C, 2 * DH_C, DH_C), F32), pltpu.VMEM((2 * H_C, 1, 128), F32)],
        compiler_params=_cparams(("arbitrary", "arbitrary")),
        name="mlstm",
    )(qct, kc, vct, gc, gct, qct, kc, vct, gc, gct, bias, biast)


def _shortconv_kernel(u_ref, w_ref, o_ref, *, T, S):
    u = u_ref[0].astype(F32)
    n = u.shape[0]
    w = w_ref[...]
    row = lax.broadcasted_iota(I32, u.shape, 0)
    zero = jnp.zeros((1, u.shape[1]), F32)
    prev = jnp.concatenate([zero, u[0:n - 1]], axis=0)
    nxt = jnp.concatenate([u[1:n], zero], axis=0)
    prev = jnp.where(row == T, 0.0, prev)
    nxt = jnp.where(row == T - 1, 0.0, nxt)
    o_ref[0] = (w[0:1, :] * prev + w[1:2, :] * u + w[2:3, :] * nxt).astype(BF16)


def _shortconv(ub, conv_w, T, S):
    B, N, W = ub.shape
    return pl.pallas_call(
        functools.partial(_shortconv_kernel, T=T, S=S),
        grid=(B, W // 256),
        in_specs=[pl.BlockSpec((1, N, 256), lambda b, j: (b, 0, j)),
                  pl.BlockSpec((3, 256), lambda b, j: (0, j))],
        out_specs=pl.BlockSpec((1, N, 256), lambda b, j: (b, 0, j)),
        out_shape=jax.ShapeDtypeStruct((B, N, W), BF16),
        compiler_params=_cparams(("arbitrary", "arbitrary")),
        name="hyena_shortconv",
    )(ub, conv_w)


def _hyfilter_kernel(z_ref, w1_ref, b1_ref, fr_ref, w2_ref, b2_ref, w3_ref, dec_ref,
                     hp_ref, hf_ref, ss_ref):
    i = pl.program_id(0)
    fr = fr_ref[...]
    h = jnp.sin(fr * (_dot(z_ref[...], w1_ref[...], precision=HI) + b1_ref[...]))
    h = jnp.sin(fr * (_dot(h, w2_ref[...], precision=HI) + b2_ref[...]))
    h = _dot(h, w3_ref[...], precision=HI)
    dec = dec_ref[...]
    dec2 = jnp.concatenate([dec, dec], axis=1)
    hw = HY_ORDER * HY_W
    h0 = h[:, 0:hw] * dec2
    h1 = h[:, hw:2 * hw] * dec2
    rows = h0.shape[0]
    t = i * rows + lax.broadcasted_iota(I32, h0.shape, 0)
    h1 = jnp.where(t == 0, 0.0, h1)
    hp_ref[...] = h0
    hf_ref[...] = h1

    @pl.when(i == 0)
    def _():
        ss_ref[...] = jnp.zeros_like(ss_ref)

    ss_ref[...] += jnp.sum(h0 * h0 + h1 * h1, axis=0, keepdims=True)


def _hyfilter(zfeat, w1, b1, fr, w2, b2, w3, decay):
    ls = zfeat.shape[0]
    rb = min(ls, 512)
    hw = HY_ORDER * HY_W
    full = lambda a: pl.BlockSpec(a.shape, lambda i: (0,) * a.ndim)
    return pl.pallas_call(
        _hyfilter_kernel,
        grid=(ls // rb,),
        in_specs=[pl.BlockSpec((rb, 128), lambda i: (i, 0)), full(w1), full(b1), full(fr), full(w2),
                  full(b2), full(w3), pl.BlockSpec((rb, HY_W), lambda i: (i, 0))],
        out_specs=[pl.BlockSpec((rb, hw), lambda i: (i, 0)), pl.BlockSpec((rb, hw), lambda i: (i, 0)),
                   pl.BlockSpec((1, hw), lambda i: (0, 0))],
        out_shape=[jax.ShapeDtypeStruct((ls, hw), F32), jax.ShapeDtypeStruct((ls, hw), F32),
                   jax.ShapeDtypeStruct((1, hw), F32)],
        compiler_params=_cparams(("arbitrary",)),
        name="hyena_filter",
    )(zfeat, w1, b1, fr, w2, b2, w3, decay)


def _cmul_const(v, c, s):
    vr, vi = v
    r = math.sqrt(0.5)
    if abs(s) < 1e-9:
        return (vr, vi) if c > 0 else (-vr, -vi)
    if abs(c) < 1e-9:
        return (-vi, vr) if s > 0 else (vi, -vr)
    if abs(abs(c) - r) < 1e-9 and abs(abs(s) - r) < 1e-9:
        a, b = (vr if c > 0 else -vr), (vi if s > 0 else -vi)
        p, q = (vi if c > 0 else -vi), (vr if s > 0 else -vr)
        return (a - b) * r, (p + q) * r
    return c * vr - s * vi, c * vi + s * vr


def _fft_pow2(xs, sign):
    n = len(xs)
    if n == 1:
        return xs
    ev, od = _fft_pow2(xs[0::2], sign), _fft_pow2(xs[1::2], sign)
    out = [None] * n
    for k in range(n // 2):
        ang = sign * 2.0 * math.pi * k / n
        tr, ti = _cmul_const(od[k], math.cos(ang), math.sin(ang))
        out[k] = (ev[k][0] + tr, ev[k][1] + ti)
        out[k + n // 2] = (ev[k][0] - tr, ev[k][1] - ti)
    return out


def _hy_stage1(src_ref, f_ref, a_ref, nseg):
    k1p = a_ref.shape[2]
    for n2 in range(HY_N2):
        if len(src_ref.shape) == 3:
            zs = jnp.concatenate([src_ref[t, pl.ds(n2, nseg, stride=HY_N2), :] for t in range(src_ref.shape[0])],
                                 axis=1).astype(BF16)
        else:
            zs = src_ref[pl.ds(n2, nseg, stride=HY_N2), :].astype(BF16)
        r = _dot(f_ref[n2], zs)
        a_ref[0, n2] = r[0:k1p]
        a_ref[1, n2] = r[k1p:2 * k1p]


def _hy_chunks(a_ref):
    k1p, lanes = a_ref.shape[2], a_ref.shape[3]
    return k1p // 8, [slice(t * 128, (t + 1) * 128) for t in range(lanes // 128)]


def _hyspec_kernel(hp_ref, hf_ref, ss_ref, f_ref, h_ref, ap_ref, af_ref, *, ls):
    nseg = ls // HY_N2
    _hy_stage1(hp_ref, f_ref, ap_ref, nseg)
    _hy_stage1(hf_ref, f_ref, af_ref, nseg)
    nchunk, lane_tiles = _hy_chunks(ap_ref)

    def body(i, _):
        rows = pl.ds(pl.multiple_of(i * 8, 8), 8)
        k1 = i * 8 + lax.broadcasted_iota(I32, (8, 1), 0)
        wk = jnp.where((k1 == 0) | (k1 == nseg), 1.0, 2.0) * (1.0 / (2 * ls))
        for ln in lane_tiles:
            scale = lax.rsqrt(ss_ref[:, ln] + EPS) * wk
            P = _fft_pow2([(ap_ref[0, n2, rows, ln], ap_ref[1, n2, rows, ln]) for n2 in range(HY_N2)], -1)
            Q = _fft_pow2([(af_ref[0, n2, rows, ln], af_ref[1, n2, rows, ln]) for n2 in range(HY_N2)], -1)
            for k2 in range(HY_N2):
                h_ref[0, 0, k2, rows, ln] = (P[k2][0] + Q[k2][0]) * scale
                h_ref[0, 1, k2, rows, ln] = (P[k2][1] - Q[k2][1]) * scale
        return 0

    lax.fori_loop(0, nchunk, body, 0)


def _hyspec(hp, hf, ss, fmat):
    ls = hp.shape[0]
    k1p = fmat.shape[1] // 2
    full = lambda a: pl.BlockSpec(a.shape, lambda o, c: (0,) * a.ndim)
    nc = HY_W // 128
    col = lambda o, c: (0, o * nc + c)
    return pl.pallas_call(
        functools.partial(_hyspec_kernel, ls=ls),
        grid=(HY_ORDER, nc),
        in_specs=[pl.BlockSpec((ls, 128), col), pl.BlockSpec((ls, 128), col), pl.BlockSpec((1, 128), col),
                  full(fmat)],
        out_specs=pl.BlockSpec((1, 2, HY_N2, k1p, 128), lambda o, c: (o, 0, 0, 0, c)),
        out_shape=jax.ShapeDtypeStruct((HY_ORDER, 2, HY_N2, k1p, HY_W), F32),
        scratch_shapes=[pltpu.VMEM((2, HY_N2, k1p, 128), F32)] * 2,
        compiler_params=_cparams(("arbitrary", "arbitrary")),
        name="hyena_spectrum",
    )(hp, hf, ss, fmat)


def _hyconv_kernel(z_ref, gate_ref, h_ref, f_ref, g_ref, bias_ref, o_ref, zf_ref, yf_ref, a_ref, *, ls):
    nseg = ls // HY_N2
    ntile = zf_ref.shape[0]
    for t in range(ntile):
        zf_ref[t] = z_ref[0, :, t * 128:(t + 1) * 128].astype(F32)
    _hy_stage1(zf_ref, f_ref, a_ref, nseg)
    nchunk, lane_tiles = _hy_chunks(a_ref)

    def body(i, _):
        rows = pl.ds(pl.multiple_of(i * 8, 8), 8)
        for ln in lane_tiles:
            X = _fft_pow2([(a_ref[0, n2, rows, ln], a_ref[1, n2, rows, ln]) for n2 in range(HY_N2)], -1)
            Y = []
            for k2 in range(HY_N2):
                hr, hi = h_ref[0, 0, k2, rows, ln], h_ref[0, 1, k2, rows, ln]
                xr, xi = X[k2]
                Y.append((xr * hr - xi * hi, xr * hi + xi * hr))
            Bv = _fft_pow2(Y, 1)
            for n2 in range(HY_N2):
                a_ref[0, n2, rows, ln] = Bv[n2][0]
                a_ref[1, n2, rows, ln] = Bv[n2][1]
        return 0

    lax.fori_loop(0, nchunk, body, 0)
    for n2 in range(HY_N2):
        bb = jnp.concatenate([a_ref[0, n2], a_ref[1, n2]], axis=0).astype(BF16)
        yv = _dot(g_ref[n2], bb)
        for t in range(ntile):
            yf_ref[t, pl.ds(n2, nseg, stride=HY_N2), :] = yv[:, t * 128:(t + 1) * 128]
    for t in range(ntile):
        ln = slice(t * 128, (t + 1) * 128)
        o_ref[0, :, ln] = (gate_ref[0, :, ln].astype(F32)
                           * (yf_ref[t] + zf_ref[t] * bias_ref[:, ln])).astype(BF16)


def _hyconv(zsrc, zcol, zrow, gsrc, gcol, grow, hspec, order, fmat, gmat, bias, ls):
    B = zsrc.shape[0]
    k1p = fmat.shape[1] // 2
    once = lambda a: pl.BlockSpec(a.shape, lambda b: (0,) * a.ndim, pipeline_mode=pl.Buffered(1))
    return pl.pallas_call(
        functools.partial(_hyconv_kernel, ls=ls),
        grid=(B,),
        in_specs=[pl.BlockSpec((1, ls, HY_W), lambda b: (b, zrow, zcol)),
                  pl.BlockSpec((1, ls, HY_W), lambda b: (b, grow, gcol)),
                  pl.BlockSpec((1, 2, HY_N2, k1p, HY_W), lambda b: (order, 0, 0, 0, 0),
                               pipeline_mode=pl.Buffered(1)),
                  once(fmat), once(gmat), pl.BlockSpec((1, HY_W), lambda b: (0, 0))],
        out_specs=pl.BlockSpec((1, ls, HY_W), lambda b: (b, 0, 0)),
        out_shape=jax.ShapeDtypeStruct((B, ls, HY_W), BF16),
        scratch_shapes=[pltpu.VMEM((HY_W // 128, ls, 128), F32), pltpu.VMEM((HY_W // 128, ls, 128), F32),
                        pltpu.VMEM((2, HY_N2, k1p, HY_W), F32)],
        compiler_params=_cparams(("arbitrary",)),
        name="hyena_longconv",
    )(zsrc, gsrc, hspec, fmat, gmat, bias)


def _outproj_kernel(x_ref, mod_ref, ya_ref, ybl_ref, ybc_ref, hf_ref, hb_ref, oc_ref, yd_ref, mlg_ref, g64_ref,
                    w_ref, g2_ref, wr_ref, xo_ref, h2_ref, aff_ref, *, nlat):
    hsum = (hf_ref[0] + hb_ref[0]).T
    msq = _dot((hsum * hsum).astype(BF16), g64_ref[...])
    yc = jax.nn.sigmoid(oc_ref[0].astype(F32)) * (hsum * lax.rsqrt(msq + EPS) * mlg_ref[...])
    yb = jnp.where(pl.program_id(1) < nlat, ybl_ref[0], ybc_ref[0])
    y = jnp.concatenate([ya_ref[0], yb, yc.astype(BF16), yd_ref[0]], axis=1)
    mod = mod_ref[0, 0]
    x = x_ref[0] + mod[2:3, :] * _dot(y, w_ref[...])
    xo_ref[0] = x
    ms = jnp.mean(x * x, axis=-1, keepdims=True)
    h2 = x * lax.rsqrt(ms + EPS) * g2_ref[...] * (1.0 + mod[4:5, :]) + mod[3:4, :]
    h2_hi = h2.astype(BF16)
    h2_ref[0] = _pack_bf16_pair(h2_hi[:, 0:D_MODEL // 2], h2_hi[:, D_MODEL // 2:])
    h2_lo = (h2 - h2_hi.astype(F32)).astype(BF16)
    lg = _dot(h2_hi, wr_ref[...])
    logits = lg[:, 0:128] + lg[:, 128:256] + _dot(h2_lo, wr_ref[:, 0:128])
    lane = lax.broadcasted_iota(I32, logits.shape, 1)
    logits = jnp.where(lane < N_EXPERTS, logits, NEG)
    e = jnp.exp(logits - jnp.max(logits, axis=-1, keepdims=True))
    aff_ref[0] = (e / jnp.sum(e, axis=-1, keepdims=True)).T


def _outproj(x, mod, ya, yb_l, yb_c, hf, hb, oc, yd, mlg, g64, w_out, g2, w_router, T):
    B, N, D = x.shape
    rb = ROW_BLOCK
    nlat = T // rb
    row = lambda w: pl.BlockSpec((1, rb, w), lambda b, r: (b, r, 0))
    full = lambda a: pl.BlockSpec(a.shape, lambda b, r: (0,) * a.ndim)
    return pl.pallas_call(
        functools.partial(_outproj_kernel, nlat=nlat),
        grid=(B, N // rb),
        in_specs=[row(D), pl.BlockSpec((1, 1, 6, D), lambda b, r: (b, jnp.where(r >= nlat, 1, 0), 0, 0)),
                  row(256),
                  pl.BlockSpec((1, rb, 256), lambda b, r: (b, jnp.minimum(r, nlat - 1), 0)),
                  pl.BlockSpec((1, rb, 256), lambda b, r: (b, jnp.maximum(r - nlat, 0), 0)),
                  pl.BlockSpec((1, 256, rb), lambda b, r: (b, 0, r)),
                  pl.BlockSpec((1, 256, rb), lambda b, r: (b, 0, r)), row(256), row(256),
                  full(mlg), full(g64), full(w_out), full(g2), full(w_router)],
        out_specs=[row(D), row(D // 2), pl.BlockSpec((1, 128, rb), lambda b, r: (b, 0, r))],
        out_shape=[jax.ShapeDtypeStruct((B, N, D), F32), jax.ShapeDtypeStruct((B, N, D // 2), I32),
                   jax.ShapeDtypeStruct((B, 128, N), F32)],
        compiler_params=_cparams(("arbitrary", "arbitrary")),
        name="outproj",
    )(x, mod, ya, yb_l, yb_c, hf, hb, oc, yd, mlg, g64, w_out, g2, w_router)


def _prefix_exclusive(x):
    n = x.shape[1]
    lane = lax.broadcasted_iota(I32, x.shape, 1)
    inc = x
    d = 1
    while d < n:
        inc = inc + jnp.where(lane >= d, pltpu.roll(inc, d, axis=1), 0)
        d *= 2
    return inc - x


def _topk_kernel(aff_ref, i_ref, g_ref, pos_ref, val_ref, *, T, S):
    def segment(t0, ts, cap, slot0):
        a = aff_ref[0, :, t0:t0 + ts]
        bits = lax.bitcast_convert_type(a, I32)

        def search(i, v):
            cand = v | jnp.left_shift(jnp.int32(1), 30 - i)
            cnt = jnp.sum((bits >= cand).astype(I32), axis=1, keepdims=True)
            return jnp.where(cnt >= cap, cand, v)

        thr = lax.fori_loop(0, 31, search, jnp.zeros((N_EXPERTS, 1), I32))
        gt = bits > thr
        eq = bits == thr
        need = cap - jnp.sum(gt.astype(I32), axis=1, keepdims=True)
        sel = gt | (eq & (_prefix_exclusive(eq.astype(I32)) < need))
        seli = sel.astype(I32)
        pos_ref[:, 0:ts] = jnp.where(sel, _prefix_exclusive(seli), -1)
        a_hi = a.astype(BF16)
        r1 = a - a_hi.astype(F32)
        a_mid = r1.astype(BF16)
        a_lo = (r1 - a_mid.astype(F32)).astype(BF16)
        val_ref[0, :, 0:ts] = a_hi.astype(F32)
        val_ref[1, :, 0:ts] = a_mid.astype(F32)
        val_ref[2, :, 0:ts] = a_lo.astype(F32)
        tabs = lax.broadcasted_iota(I32, (1, ts), 1) + t0
        t_hi = jnp.right_shift(tabs, 6).astype(F32)
        t_lo = (tabs & 63).astype(F32)
        slot = lax.broadcasted_iota(I32, (cap, ts), 0)
        zeros = jnp.zeros((11, ts), F32)

        def per_expert(e, _):
            onehot = jnp.where(pos_ref[pl.ds(e, 1), 0:ts] == slot, 1.0, 0.0).astype(BF16)
            vals = jnp.concatenate([t_hi, t_lo, val_ref[0, pl.ds(e, 1), 0:ts], val_ref[1, pl.ds(e, 1), 0:ts],
                                    val_ref[2, pl.ds(e, 1), 0:ts], zeros], axis=0).astype(BF16)
            r = _dot_nt(onehot, vals)
            i_ref[0, e, slot0:slot0 + cap, :] = (r[:, 0:1] * 64.0 + r[:, 1:2]).astype(I32)
            g_ref[0, e, slot0:slot0 + cap, :] = r[:, 2:3] + r[:, 3:4] + r[:, 4:5]
            return 0

        lax.fori_loop(0, N_EXPERTS, per_expert, 0)

    segment(0, T, (EC_CAPACITY * T) // N_EXPERTS, 0)
    segment(T, S, (EC_CAPACITY * S) // N_EXPERTS, (EC_CAPACITY * T) // N_EXPERTS)


def _topk(aff_t, T, S):
    B, _, N = aff_t.shape
    E = N_EXPERTS
    cap_t = (EC_CAPACITY * T) // E + (EC_CAPACITY * S) // E
    return pl.pallas_call(
        functools.partial(_topk_kernel, T=T, S=S),
        grid=(B,),
        in_specs=[pl.BlockSpec((1, E, N), lambda b: (b, 0, 0))],
        out_specs=[pl.BlockSpec((1, E, cap_t, 1), lambda b: (b, 0, 0, 0))] * 2,
        out_shape=[jax.ShapeDtypeStruct((B, E, cap_t, 1), I32), jax.ShapeDtypeStruct((B, E, cap_t, 1), F32)],
        scratch_shapes=[pltpu.VMEM((E, T), I32), pltpu.VMEM((3, E, T), F32)],
        compiler_params=_cparams(("arbitrary",)),
        name="expert_topk",
    )(aff_t)


def _gather_kernel(idx_ref, h_ref, *rest, slots, nstep):
    o_ref = rest[-1]
    b, e = pl.program_id(0), pl.program_id(1)
    base = (b * nstep + e) * slots

    def body(g, _):
        rows = [h_ref[0, pl.ds(idx_ref[base + g * 16 + k], 1), :] for k in range(16)]
        o_ref[0, pl.ds(pl.multiple_of(g * 16, 16), 16), :] = jnp.concatenate(rows, axis=0)
        return 0

    lax.fori_loop(0, slots // 16, body, 0)


def _gather(idx_flat, h2p, xe_buf, b0, bg, cap_t):
    B, N, hw = h2p.shape
    slots, nstep = GATHER_EXPERTS * cap_t, N_EXPERTS // GATHER_EXPERTS
    in_specs = [pl.BlockSpec((1, N, hw), lambda b, e, idx: (b0 + b, 0, 0))]
    args = [idx_flat, h2p]
    aliases = {}
    if xe_buf is not None:
        in_specs.append(pl.BlockSpec(memory_space=pl.ANY))
        args.append(xe_buf)
        aliases = {2: 0}
    return pl.pallas_call(
        functools.partial(_gather_kernel, slots=slots, nstep=nstep),
        grid_spec=pltpu.PrefetchScalarGridSpec(
            num_scalar_prefetch=1,
            grid=(bg, nstep),
            in_specs=in_specs,
            out_specs=pl.BlockSpec((1, slots, hw), lambda b, e, idx: (b0 + b, e, 0)),
        ),
        out_shape=jax.ShapeDtypeStruct((B, N_EXPERTS * cap_t, hw), I32),
        input_output_aliases=aliases,
        compiler_params=_cparams(("arbitrary", "arbitrary")),
        name="expert_gather",
    )(*args)


def _ffn_kernel(x_ref, wg_ref, wu_ref, wd_ref, gate_ref, gt_ref, o_ref, wg_s, wu_s, wd_s, *, cap_l):
    @pl.when(pl.program_id(1) == 0)
    def _():
        wg_s[...] = wg_ref[0, 0].astype(BF16)
        wu_s[...] = wu_ref[0, 0].astype(BF16)
        wd_s[...] = wd_ref[0, 0].astype(BF16)

    x = jnp.concatenate(_unpack_bf16_pair(x_ref[0, 0]), axis=1)
    a = _dot(x, wg_s[...])
    u = _dot(x, wu_s[...])
    hmid = (a * jax.nn.sigmoid(a) * u).astype(BF16)
    y = _dot(hmid, wd_s[...])
    gt = gt_ref[0]
    row = lax.broadcasted_iota(I32, y.shape, 0)
    gt2 = jnp.where(row < cap_l, gt[0:1, :], gt[1:2, :])
    o_ref[0, 0] = y * gate_ref[0, 0] * gt2


def _ffn(xe, layer, wg, wu, wd, gate, gt2, cap_l):
    B, E, cap_t, hw = xe.shape
    D = 2 * hw
    wspec = pl.BlockSpec((1, 1, D, D), lambda e, b: (layer, e, 0, 0))
    return pl.pallas_call(
        functools.partial(_ffn_kernel, cap_l=cap_l),
        grid=(E, B),
        in_specs=[pl.BlockSpec((1, 1, cap_t, hw), lambda e, b: (b, e, 0, 0)), wspec, wspec, wspec,
                  pl.BlockSpec((1, 1, cap_t, 1), lambda e, b: (b, e, 0, 0)),
                  pl.BlockSpec((1, 2, D), lambda e, b: (b, 0, 0))],
        out_specs=pl.BlockSpec((1, 1, cap_t, D), lambda e, b: (b, e, 0, 0)),
        out_shape=jax.ShapeDtypeStruct((B, E, cap_t, D), F32),
        scratch_shapes=[pltpu.VMEM((D, D), BF16)] * 3,
        compiler_params=_cparams(("arbitrary", "arbitrary")),
        name="expert_ffn",
    )(xe, wg, wu, wd, gate, gt2)


def _scatter_kernel(idx_ref, y_ref, *rest, slots, nstep):
    o_ref = rest[-1]
    b, e = pl.program_id(0), pl.program_id(1)
    base = (b * nstep + e) * slots

    @pl.when(e == 0)
    def _():
        o_ref[...] = jnp.zeros_like(o_ref)

    def body(g, _):
        tile = y_ref[0, pl.ds(pl.multiple_of(g * 8, 8), 8), :]
        rows = [idx_ref[base + g * 8 + k] for k in range(8)]
        cur = [o_ref[0, pl.ds(r, 1), :] for r in rows]
        for k, r in enumerate(rows):
            o_ref[0, pl.ds(r, 1), :] = cur[k] + tile[k:k + 1, :]
        return 0

    lax.fori_loop(0, slots // 8, body, 0)


def _scatter(idx_flat, ye, acc_buf, n_rows, b0, bg, cap_t):
    B, _, D = ye.shape
    slots, nstep = SCATTER_EXPERTS * cap_t, N_EXPERTS // SCATTER_EXPERTS
    in_specs = [pl.BlockSpec((1, slots, D), lambda b, e, idx: (b0 + b, e, 0))]
    args = [idx_flat, ye]
    aliases = {}
    if acc_buf is not None:
        in_specs.append(pl.BlockSpec(memory_space=pl.ANY))
        args.append(acc_buf)
        aliases = {2: 0}
    return pl.pallas_call(
        functools.partial(_scatter_kernel, slots=slots, nstep=nstep),
        grid_spec=pltpu.PrefetchScalarGridSpec(
            num_scalar_prefetch=1,
            grid=(bg, nstep),
            in_specs=in_specs,
            out_specs=pl.BlockSpec((1, n_rows, D), lambda b, e, idx: (b0 + b, 0, 0)),
        ),
        out_shape=jax.ShapeDtypeStruct((B, n_rows, D), F32),
        input_output_aliases=aliases,
        compiler_params=_cparams(("arbitrary", "arbitrary")),
        name="expert_scatter",
    )(*args)


def _residual_out_kernel(x_ref, f_ref, o_ref):
    o_ref[...] = x_ref[...] + f_ref[...]


def _residual_out(x, f, T):
    B, N, D = x.shape
    rb = ROW_BLOCK
    spec = pl.BlockSpec((1, rb, D), lambda b, r: (b, r, 0))
    return pl.pallas_call(
        _residual_out_kernel,
        grid=(B, T // rb),
        in_specs=[spec, spec],
        out_specs=spec,
        out_shape=jax.ShapeDtypeStruct((B, T, D), F32),
        compiler_params=_cparams(("arbitrary", "arbitrary")),
        name="residual_out",
    )(x, f)


def _rope_tables(T, S, dh):
    rows = T // GRID_W
    r = jnp.broadcast_to(jnp.arange(rows, dtype=F32)[:, None], (rows, GRID_W)).reshape(T)
    col = jnp.broadcast_to(jnp.arange(GRID_W, dtype=F32)[None, :], (rows, GRID_W)).reshape(T)
    nf = dh // 4
    inv = ROPE_BASE ** (-jnp.arange(nf, dtype=F32) / nf)
    ar, ac = r[:, None] * inv, col[:, None] * inv
    ang = jnp.concatenate([ar, ar, ac, ac], axis=1)
    ang = jnp.concatenate([ang, jnp.zeros((S, dh), F32)], axis=0)
    reps = 256 // dh
    return jnp.tile(jnp.cos(ang), (1, reps)), jnp.tile(jnp.sin(ang), (1, reps))


def _group_mats(dh):
    i = np.arange(256)
    gmat = (i[:, None] // dh == i[None, :] // dh).astype(np.float32) / dh
    nf = dh // 4
    half = (i % (2 * nf)) // nf
    pmat = np.zeros((256, 256), np.float32)
    a_idx = i[half == 0]
    pmat[a_idx + nf, a_idx] = -1.0
    pmat[a_idx, a_idx + nf] = 1.0
    return jnp.asarray(gmat, BF16), jnp.asarray(pmat, BF16)


def _hyena_tables(ls):
    t = jnp.linspace(0.0, 1.0, ls, dtype=F32)[:, None]
    w = 2.0 * math.pi * jnp.arange(ls, dtype=F32)[:, None] / ls
    bands = jnp.linspace(1e-4, HY_BANDS - 1, HY_BANDS, dtype=F32)
    z = jnp.concatenate([t, jnp.cos(bands * w), -jnp.sin(bands * w)], axis=-1)
    z = jnp.pad(z, ((0, 0), (0, 128 - HY_EMB)))
    deltas = jnp.abs(jnp.linspace(math.log(HY_TARGET) / HY_SLOW, math.log(HY_TARGET) / HY_FAST, HY_W, dtype=F32))
    decay = jnp.exp(-t * deltas)
    n, nseg = 2 * ls, ls // HY_N2
    k1p = -(-(nseg + 1) // 8) * 8
    k1 = jnp.arange(k1p, dtype=I32)[None, :, None]
    tpos = HY_N2 * jnp.arange(nseg, dtype=I32)[None, None, :] + jnp.arange(HY_N2, dtype=I32)[:, None, None]
    ang = ((k1 * tpos) % n).astype(F32) * (2.0 * math.pi / n)
    keep = k1 <= nseg
    fmat = jnp.concatenate([jnp.where(keep, jnp.cos(ang), 0.0), jnp.where(keep, -jnp.sin(ang), 0.0)],
                           axis=1).astype(BF16)
    return z, decay, fmat, jnp.swapaxes(fmat, 1, 2)


def kernel(x, c, ctx, c_ctx, w_ada, b_ada, norm1_g, norm2_g, w_in, b_gate, a_qnorm, a_knorm, a_sink, hy_conv, hy_fw1, hy_fb1, hy_freq, hy_fw2, hy_fb2, hy_fw3, hy_bias, ml_norm, d_qnorm, d_knorm, d_lq1, d_lk1, d_lq2, d_lk2, d_subnorm, w_out, w_router, w_e_gate, w_e_up, w_e_down):
    B, T, D = x.shape
    S = ctx.shape[1]
    N = T + S
    depth = w_ada.shape[0]
    assert D == D_MODEL and T % S == 0 and S % ROW_BLOCK == 0 and S % MLSTM_CHUNK == 0
    assert S % WATTN_QROWS == 0 and T >= WATTN_QROWS + 2 * BLK and WATTN_QROWS & (WATTN_QROWS - 1) == 0
    cap_l, cap_c = (EC_CAPACITY * T) // N_EXPERTS, (EC_CAPACITY * S) // N_EXPERTS
    cap_t = cap_l + cap_c
    assert cap_t % 16 == 0

    xs = jnp.concatenate([x, ctx], axis=1)

    rpad = -(B + 1) % 8
    cc = jnp.concatenate([c, c_ctx[None, :], jnp.zeros((rpad, D), F32)], axis=0)
    mods = _ada_mods(cc, w_ada, b_ada)

    cos_a, sin_a = _rope_tables(T, S, DH_A)
    cos_d, sin_d = _rope_tables(T, S, DH_D)
    g64, p64 = _group_mats(DH_A)
    g32, p32 = _group_mats(DH_D)
    tabs_l = _hyena_tables(T)
    tabs_c = _hyena_tables(S)

    offs = np.cumsum((0, 256, 128, 128, 768, 256, 256, 256, 256, 16, 256, 256, 256))
    ffn_acc = None
    for l in range(depth):
        lam_init = 0.8 - 0.6 * math.exp(-0.3 * l)
        ml = mods[l]
        mod = jnp.stack([ml[:B].reshape(B, 6, D), jnp.broadcast_to(ml[B].reshape(1, 6, D), (B, 6, D))],
                        axis=1)
        w = w_in[l]
        wp = jnp.concatenate([w[:, offs[0]:offs[8]], w[:, offs[8]:offs[9]], jnp.zeros((D, 112), F32),
                              w[:, offs[9]:offs[12]]], axis=1).astype(BF16)
        gains = jnp.stack([jnp.tile(a_qnorm[l], 4), jnp.tile(a_knorm[l], 4),
                           jnp.tile(d_qnorm[l], 8), jnp.tile(d_knorm[l], 8)], axis=0)
        proj = _inproj(xs, ffn_acc, mod, norm1_g[l].reshape(1, D), wp, (cos_a, sin_a, cos_d, sin_d), gains,
                       (g64, g32, p64, p32), T)
        qa, ka, va, ub, kc, oc, gc, qd, kd, vd, gct, qct, vct = proj[0:13]
        if ffn_acc is not None:
            xs = proj[13]

        sb_a = (jnp.max(jnp.abs(a_qnorm[l])) * jnp.max(jnp.abs(a_knorm[l])) * DH_A ** 0.5).reshape(1, 1)
        sb_d = (jnp.max(jnp.abs(d_qnorm[l])) * jnp.max(jnp.abs(d_knorm[l])) * DH_D ** 0.5).reshape(1, 1)
        ya = _wattn(qa, ka, va, jnp.pad(a_sink[l], (0, 128 - H_A)).reshape(1, 128), sb_a, T, S)
        yd = _dattn(qd, kd, vd, jnp.stack([d_lq1[l], d_lk1[l], d_lq2[l], d_lk2[l]], axis=0),
                    d_subnorm[l].reshape(1, 2 * DH_D), sb_d, T, S, lam_init)

        bias = b_gate[l].reshape(-1)
        hf, hb = _mlstm(qct, kc, vct, gc, gct, jnp.pad(bias, (0, 112)).reshape(1, 128), bias.reshape(16, 1), T, S)

        uc = _shortconv(ub, hy_conv[l], T, S)
        w1 = jnp.pad(hy_fw1[l], ((0, 128 - HY_EMB), (0, 0)))
        ybs = []
        for (zf, decay, fmat, gmat), ls, rowblk in ((tabs_l, T, 0), (tabs_c, S, T // S)):
            hpast, hfut, ss = _hyfilter(zf, w1, hy_fb1[l].reshape(1, -1), hy_freq[l].reshape(1, -1), hy_fw2[l],
                                        hy_fb2[l].reshape(1, -1), hy_fw3[l], decay)
            hspec = _hyspec(hpast, hfut, ss, fmat)
            z1 = _hyconv(uc, 0, rowblk, uc, 1, rowblk, hspec, 0, fmat, gmat, hy_bias[l, 0].reshape(1, HY_W), ls)
            z2 = _hyconv(z1, 0, 0, uc, 2, rowblk, hspec, 1, fmat, gmat, hy_bias[l, 1].reshape(1, HY_W), ls)
            ybs.append(z2)

        wr = jnp.pad(w_router[l], ((0, 0), (0, 128 - N_EXPERTS)))
        wr_hi = wr.astype(BF16)
        wr_cat = jnp.concatenate([wr_hi, (wr - wr_hi.astype(F32)).astype(BF16)], axis=1)
        xs, h2, aff = _outproj(xs, mod, ya, ybs[0], ybs[1], hf, hb, oc, yd, jnp.tile(ml_norm[l], 4).reshape(1, 256), g64,
                               w_out[l].astype(BF16), norm2_g[l].reshape(1, D), wr_cat, T)

        idx, gate = _topk(aff, T, S)
        gt2 = mod[:, :, 5, :]
        bg = min(IDX_BATCH_GROUP, B)
        groups = [(b0, idx[b0:b0 + bg].reshape(-1)) for b0 in range(0, B, bg)]
        xe = None
        for b0, idx_flat in groups:
            xe = _gather(idx_flat, h2, xe, b0, bg, cap_t)
        ye = _ffn(xe.reshape(B, N_EXPERTS, cap_t, D // 2), l, w_e_gate, w_e_up, w_e_down, gate, gt2, cap_l)
        ye = ye.reshape(B, N_EXPERTS * cap_t, D)
        ffn_acc = None
        for b0, idx_flat in groups:
            ffn_acc = _scatter(idx_flat, ye, ffn_acc, N, b0, bg, cap_t)
    return _residual_out(xs, ffn_acc, T)
```

```python
import functools
import math

import jax
import jax.numpy as jnp
import numpy as np
from jax import lax
from jax.experimental import pallas as pl
from jax.experimental.pallas import tpu as pltpu

F32 = jnp.float32
BF16 = jnp.bfloat16
I32 = jnp.int32
HI = lax.Precision.HIGHEST

D_MODEL = 1024
GRID_W = 64
GROUP_W = D_MODEL // 4
H_A, HKV_A = 4, 2
G_A = H_A // HKV_A
DH_A = GROUP_W // H_A
WINDOW = 128
BLK = 128
HY_W = GROUP_W
HY_ORDER = 2
HY_BANDS = 16
HY_EMB = 1 + 2 * HY_BANDS
HY_HID = 64
HY_TARGET, HY_FAST, HY_SLOW = 1e-2, 0.3, 1.5
HY_N2 = 16
H_C = 4
DH_C = GROUP_W // H_C
H_D = 4
DH_D = GROUP_W // (2 * H_D)
N_EXPERTS = 16
EC_CAPACITY = 2
ROPE_BASE = 10000.0
EPS = 1e-6
NEG = -1e30

ROW_BLOCK = 256
WATTN_QROWS = 256
DATTN_MIN_ROWSUM = 1e-25
MLSTM_CHUNK = 256
IDX_BATCH_GROUP = 4
GATHER_EXPERTS = 4
SCATTER_EXPERTS = 2
VMEM_LIMIT = 56 * 1024 * 1024

_C_QA, _C_KA, _C_VA, _C_UB = 0, 256, 384, 512
_C_QC, _C_KC, _C_VC, _C_OC, _C_GC = 1280, 1536, 1792, 2048, 2304
_C_QD, _C_KD, _C_VD, _C_END = 2432, 2688, 2944, 3200


def _cparams(sem, vmem=VMEM_LIMIT):
    return pltpu.CompilerParams(dimension_semantics=sem, vmem_limit_bytes=vmem)


def _dot(a, b, **kw):
    return jnp.dot(a, b, preferred_element_type=F32, **kw)


def _dot_nt(a, b, **kw):
    return lax.dot_general(a, b, (((1,), (1,)), ((), ())), preferred_element_type=F32, **kw)


def _pack_bf16_pair(a, b):
    ua = lax.bitcast_convert_type(a.astype(F32), I32)
    ub = lax.bitcast_convert_type(b.astype(F32), I32)
    return ua | lax.shift_right_logical(ub, 16)


def _unpack_bf16_pair(w):
    a = lax.bitcast_convert_type(w & jnp.int32(-65536), F32)
    b = lax.bitcast_convert_type(lax.shift_left(w, 16), F32)
    return a.astype(BF16), b.astype(BF16)


def _ada_kernel(c_ref, w_ref, b_ref, o_ref):
    c = c_ref[...]
    s = c * jax.nn.sigmoid(c)
    o_ref[0] = _dot(s, w_ref[0], precision=HI) + b_ref[0]


def _ada_mods(cc, w_ada, b_ada):
    L, D, W6 = w_ada.shape
    R = cc.shape[0]
    cb = 1536
    return pl.pallas_call(
        _ada_kernel,
        grid=(L, W6 // cb),
        in_specs=[pl.BlockSpec((R, D), lambda l, j: (0, 0)),
                  pl.BlockSpec((1, D, cb), lambda l, j: (l, 0, j)),
                  pl.BlockSpec((1, 1, cb), lambda l, j: (l, 0, j))],
        out_specs=pl.BlockSpec((1, R, cb), lambda l, j: (l, 0, j)),
        out_shape=jax.ShapeDtypeStruct((L, R, W6), F32),
        compiler_params=_cparams(("arbitrary", "arbitrary")),
        name="ada_mods",
    )(cc, w_ada, b_ada.reshape(L, 1, W6))


def _inproj_kernel(*refs, has_f):
    if has_f:
        x_ref, f_ref, *refs = refs
        *refs, xsum = refs
        x = x_ref[0] + f_ref[0]
        xsum[0] = x
    else:
        x_ref, *refs = refs
        x = x_ref[0]
    (mod_ref, g1_ref, w_ref, ca_ref, sa_ref, cd_ref, sd_ref, gains_ref, g64_ref, g32_ref, p64_ref, p32_ref,
     qa, ka, va, ub, kc, oc, gc, qd, kd, vd, gct, qct, vct) = refs
    ms = jnp.mean(x * x, axis=-1, keepdims=True)
    xn = x * lax.rsqrt(ms + EPS) * g1_ref[...]
    mod = mod_ref[0, 0]
    h = xn * (1.0 + mod[1:2, :]) + mod[0:1, :]
    p = _dot(h.astype(BF16), w_ref[...])

    def headnorm_rope(t, gmat, gain, pmat, cos, sin, scale):
        w = t.shape[1]
        msq = _dot((t * t).astype(BF16), gmat[0:w, 0:w])
        tn = t * lax.rsqrt(msq + EPS) * gain
        tn = tn * cos[:, 0:w] + _dot(tn.astype(BF16), pmat[0:w, 0:w]) * sin[:, 0:w]
        return tn * scale

    ca, sa, cd, sd = ca_ref[...], sa_ref[...], cd_ref[...], sd_ref[...]
    gains = gains_ref[...]
    qa[0] = headnorm_rope(p[:, _C_QA:_C_KA], g64_ref, gains[0:1, :], p64_ref, ca, sa, DH_A ** -0.5).astype(BF16)
    ka[0] = headnorm_rope(p[:, _C_KA:_C_VA], g64_ref, gains[1:2, 0:128], p64_ref, ca, sa, 1.0).astype(BF16)
    va[0] = p[:, _C_VA:_C_UB].astype(BF16)
    ub[0] = p[:, _C_UB:_C_QC].astype(BF16)
    qct[0] = p[:, _C_QC:_C_KC].T.astype(BF16)
    kc[0] = (p[:, _C_KC:_C_VC] * DH_C ** -0.5).astype(BF16)
    vct[0] = p[:, _C_VC:_C_OC].T.astype(BF16)
    oc[0] = p[:, _C_OC:_C_GC].astype(BF16)
    gc[0] = p[:, _C_GC:_C_QD]
    gct[0] = p[:, _C_GC:_C_QD].T
    qd[0] = headnorm_rope(p[:, _C_QD:_C_KD], g32_ref, gains[2:3, :], p32_ref, cd, sd, DH_D ** -0.5).astype(BF16)
    kd[0] = headnorm_rope(p[:, _C_KD:_C_VD], g32_ref, gains[3:4, :], p32_ref, cd, sd, 1.0).astype(BF16)
    vd[0] = p[:, _C_VD:_C_END].astype(BF16)


def _inproj(x, f, mod, g1, w_packed, tabs, gains, mats, T):
    B, N, D = x.shape
    rb = ROW_BLOCK
    nlat = T // rb
    row = lambda w: pl.BlockSpec((1, rb, w), lambda b, r: (b, r, 0))
    tab = pl.BlockSpec((rb, 256), lambda b, r: (r, 0))
    full = lambda a: pl.BlockSpec(a.shape, lambda b, r: (0,) * a.ndim)
    widths = [256, 128, 128, 768, 256, 256, 128, 256, 256, 256]
    dtypes = [BF16] * 6 + [F32] + [BF16] * 3
    tr = lambda w: pl.BlockSpec((1, w, rb), lambda b, r: (b, 0, r))
    has_f = f is not None
    return pl.pallas_call(
        functools.partial(_inproj_kernel, has_f=has_f),
        grid=(B, N // rb),
        in_specs=[row(D)] * (2 if has_f else 1)
        + [pl.BlockSpec((1, 1, 6, D), lambda b, r: (b, jnp.where(r >= nlat, 1, 0), 0, 0)),
           full(g1), full(w_packed), tab, tab, tab, tab, full(gains)] + [full(m) for m in mats],
        out_specs=[row(w) for w in widths] + [tr(128), tr(256), tr(256)] + ([row(D)] if has_f else []),
        out_shape=[jax.ShapeDtypeStruct((B, N, w), dt) for w, dt in zip(widths, dtypes)]
        + [jax.ShapeDtypeStruct((B, 128, N), F32), jax.ShapeDtypeStruct((B, 256, N), BF16),
           jax.ShapeDtypeStruct((B, 256, N), BF16)] + ([jax.ShapeDtypeStruct((B, N, D), F32)] if has_f else []),
        compiler_params=_cparams(("arbitrary", "arbitrary")),
        name="inproj",
    )(x, *([f] if has_f else []), mod, g1, w_packed, *tabs, gains, *mats)


def _wattn_kernel(q_ref, k_ref, v_ref, sink_ref, sinkw_ref, placeq_ref, sb_ref, placek_ref, o_ref,
                  kaug_ref, vaug_ref, *, T, S):
    j = pl.program_id(1)
    qb, kwin = WATTN_QROWS, WATTN_QROWS + 2 * BLK
    is_lat = j < T // qb
    start = pl.multiple_of(jnp.clip(j * qb - BLK, 0, T - kwin), BLK)
    row = lax.broadcasted_iota(I32, (G_A * qb, kwin), 0)
    qpos = j * qb + (row & (qb - 1))
    kpos = start + lax.broadcasted_iota(I32, (G_A * qb, kwin), 1)
    valid = (jnp.abs(qpos - kpos) <= WINDOW) & is_lat
    grp = jnp.right_shift(lax.broadcasted_iota(I32, (G_A * qb, 1), 0), qb.bit_length() - 1)
    sinks = sink_ref[...]

    @pl.when(j == 0)
    def _():
        lane = lax.broadcasted_iota(I32, (1, HKV_A * 128), 1)
        one = jnp.where((lane & 127) == DH_A, 1.0, 0.0)
        kaug_ref[...] = (_dot(k_ref[0], placek_ref[...]) + one).astype(BF16)
        vaug_ref[...] = (_dot(v_ref[0], placek_ref[...]) + one).astype(BF16)

    def sink_rows(heads):
        sink = sinks[0:1, heads[0]:heads[0] + 1]
        for g in range(1, G_A):
            sink = jnp.where(grp == g, sinks[0:1, heads[g]:heads[g] + 1], sink)
        return sink

    def store(heads, o):
        for g, hd in enumerate(heads):
            o_ref[0, :, hd * DH_A:(hd + 1) * DH_A] = o[g * qb:(g + 1) * qb].astype(BF16)

    def body_fast():
        q_wide = _dot(q_ref[0], placeq_ref[...])
        lane = lax.broadcasted_iota(I32, (1, H_A * 128), 1)
        shift = jnp.where((lane & 127) == DH_A, jnp.maximum(sb_ref[...], sinkw_ref[...]), 0.0)
        q_aug = (q_wide - shift).astype(BF16)
        low = None
        for hk in range(HKV_A):
            heads = [hk * G_A + g for g in range(G_A)]
            ks = slice(hk * 128, (hk + 1) * 128)
            q2 = jnp.concatenate([q_aug[:, hd * 128:(hd + 1) * 128] for hd in heads], axis=0)
            m_used = -q2[:, DH_A:DH_A + 1].astype(F32)
            s_loc = jnp.where(valid, _dot_nt(q2, kaug_ref[pl.ds(start, kwin), ks]), NEG)
            s_ctx = _dot_nt(q2, kaug_ref[T:T + S, ks])
            pv = (_dot(jnp.exp(s_loc.astype(BF16)), vaug_ref[pl.ds(start, kwin), ks])
                  + _dot(jnp.exp(s_ctx.astype(BF16)), vaug_ref[T:T + S, ks]))
            sigma = pv[:, DH_A:DH_A + 1]
            low = sigma if low is None else jnp.minimum(low, sigma)
            store(heads, pv[:, 0:DH_A] / (sigma + jnp.exp(sink_rows(heads) - m_used)))
        return jnp.min(low)

    def body_exact():
        kw = k_ref[0, pl.ds(start, kwin), :]
        vw = v_ref[0, pl.ds(start, kwin), :]
        kc = k_ref[0, T:T + S, :]
        vc = v_ref[0, T:T + S, :]
        for hk in range(HKV_A):
            cs = slice(hk * DH_A, (hk + 1) * DH_A)
            kwh, vwh, kch, vch = kw[:, cs], vw[:, cs], kc[:, cs], vc[:, cs]
            heads = [hk * G_A + g for g in range(G_A)]
            q = jnp.concatenate([q_ref[0, :, hd * DH_A:(hd + 1) * DH_A] for hd in heads], axis=0)
            sink = sink_rows(heads)
            s_loc = jnp.where(valid, _dot_nt(q, kwh), NEG)
            s_ctx = _dot_nt(q, kch)
            m = jnp.maximum(jnp.maximum(jnp.max(s_loc, axis=-1, keepdims=True),
                                        jnp.max(s_ctx, axis=-1, keepdims=True)), sink)
            p_loc = jnp.exp(s_loc - m)
            p_ctx = jnp.exp(s_ctx - m)
            den = (jnp.sum(p_loc, axis=-1, keepdims=True) + jnp.sum(p_ctx, axis=-1, keepdims=True)
                   + jnp.exp(sink - m))
            store(heads, (_dot(p_loc.astype(BF16), vwh) + _dot(p_ctx.astype(BF16), vch)) / den)

    low = body_fast()

    @pl.when(jnp.logical_not(low > DATTN_MIN_ROWSUM))
    def _():
        body_exact()


def _wattn_place(nheads):
    place = np.zeros((nheads * DH_A, nheads * 128), np.float32)
    for h in range(nheads):
        for d in range(DH_A):
            place[h * DH_A + d, h * 128 + d] = 1.0
    return jnp.asarray(place, BF16)


def _wattn(qa, ka, va, sink, sbound, T, S):
    B, N, _ = qa.shape
    qb = WATTN_QROWS
    placeq, placek = _wattn_place(H_A), _wattn_place(HKV_A)
    sinkw = jnp.zeros((1, H_A * 128), F32).at[0, DH_A::128].set(sink[0, 0:H_A])
    full = lambda a: pl.BlockSpec(a.shape, lambda b, j: (0,) * a.ndim)
    in_specs = [pl.BlockSpec((1, qb, 256), lambda b, j: (b, j, 0)),
                pl.BlockSpec((1, N, 128), lambda b, j: (b, 0, 0)),
                pl.BlockSpec((1, N, 128), lambda b, j: (b, 0, 0)),
                pl.BlockSpec((1, 128), lambda b, j: (0, 0)),
                full(sinkw), full(placeq), full(sbound), full(placek)]
    scratch = [pltpu.VMEM((N, HKV_A * 128), BF16), pltpu.VMEM((N, HKV_A * 128), BF16)]
    return in_specs, (qa, ka, va, sink, sinkw, placeq, sbound, placek), scratch


def _dattn_kernel(q_ref, k_ref, v_ref, lam_ref, sub_ref, place_ref, sb_ref, o_ref,
                  vaug_ref, kch_ref, qch_ref, tch_ref, *, T, S, lam_init):
    j = pl.program_id(1)
    shift_lane = (lax.broadcasted_iota(I32, (1, 4 * H_D * DH_D), 1) & (2 * DH_D - 1)) == DH_D
    lv = lam_ref[...]
    lam = (jnp.exp(jnp.sum(lv[0:1, :] * lv[1:2, :], axis=-1, keepdims=True))
           - jnp.exp(jnp.sum(lv[2:3, :] * lv[3:4, :], axis=-1, keepdims=True)) + lam_init)
    subg = sub_ref[...] * (1.0 - lam_init)
    dv = 2 * DH_D
    nchain, aw = 2 * H_D, 2 * DH_D

    @pl.when(j == 0)
    def _():
        ones = jnp.where(lax.broadcasted_iota(I32, (T + S, dv), 1) == 0, 1.0, 0.0).astype(BF16)
        for hd in range(H_D):
            vaug_ref[hd] = jnp.concatenate([v_ref[0, :, hd * dv:(hd + 1) * dv], ones], axis=1)
        k_aug = (_dot(k_ref[0], place_ref[...]) + jnp.where(shift_lane, 1.0, 0.0)).astype(BF16)
        for i in range(nchain):
            kch_ref[i] = k_aug[:, i * aw:(i + 1) * aw]

    def finish(hd, terms):
        o = terms[0] - lam * terms[1]
        ms = jnp.mean(o * o, axis=-1, keepdims=True)
        o_ref[0, :, hd * dv:(hd + 1) * dv] = (o * lax.rsqrt(ms + EPS) * subg).astype(BF16)

    def body_fast(k0, nk):
        q_aug = (_dot(q_ref[0], place_ref[...]) - jnp.where(shift_lane, sb_ref[...], 0.0)).astype(BF16)
        terms, low = [], None
        for i in range(nchain):
            hd, cs = i // 2, slice(i * aw, (i + 1) * aw)
            s = _dot_nt(q_aug[:, cs], kch_ref[i, k0:k0 + nk, :])
            pv = _dot(jnp.exp(s.astype(BF16)), vaug_ref[hd, k0:k0 + nk, :])
            sigma = pv[:, dv:dv + 1]
            low = sigma if low is None else jnp.minimum(low, sigma)
            terms.append(pv[:, 0:dv] / sigma)
            if i % 2 == 1:
                finish(hd, terms)
                terms = []
        return jnp.min(low)

    def body_exact(k0, nk):
        q_wide = _dot(q_ref[0], place_ref[...])
        for i in range(nchain):
            qch_ref[i] = q_wide[:, i * aw:(i + 1) * aw].astype(BF16)

        def chain(i, carry):
            s = _dot_nt(qch_ref[i], kch_ref[i, k0:k0 + nk, :])
            e = jnp.exp((s - jnp.max(s, axis=-1, keepdims=True)).astype(BF16))
            pv = _dot(e, vaug_ref[jnp.right_shift(i, 1), k0:k0 + nk, :])
            tch_ref[i] = pv / pv[:, dv:dv + 1]
            return carry

        lax.fori_loop(0, nchain, chain, 0)
        for hd in range(H_D):
            finish(hd, [tch_ref[2 * hd][:, 0:dv], tch_ref[2 * hd + 1][:, 0:dv]])

    def body(k0, nk):
        low = body_fast(k0, nk)

        @pl.when(jnp.logical_not(low > DATTN_MIN_ROWSUM))
        def _():
            body_exact(k0, nk)

    @pl.when(j < T // ROW_BLOCK)
    def _():
        body(0, T + S)

    @pl.when(j >= T // ROW_BLOCK)
    def _():
        body(T, S)


def _dattn_place():
    nchain, aw = 2 * H_D, 2 * DH_D
    place = np.zeros((nchain * DH_D, nchain * aw), np.float32)
    for i in range(nchain):
        for d in range(DH_D):
            place[i * DH_D + d, i * aw + d] = 1.0
    return jnp.asarray(place, BF16)


def _dattn(qd, kd, vd, lam_vecs, subg, sbound, T, S, lam_init):
    B, N, _ = qd.shape
    rb = ROW_BLOCK
    place = _dattn_place()
    full = lambda a: pl.BlockSpec(a.shape, lambda b, j: (0,) * a.ndim)
    in_specs = [pl.BlockSpec((1, rb, 256), lambda b, j: (b, j, 0)),
                pl.BlockSpec((1, N, 256), lambda b, j: (b, 0, 0)),
                pl.BlockSpec((1, N, 256), lambda b, j: (b, 0, 0)),
                pl.BlockSpec((4, DH_D), lambda b, j: (0, 0)),
                pl.BlockSpec((1, 2 * DH_D), lambda b, j: (0, 0)), full(place), full(sbound)]
    scratch = [pltpu.VMEM((H_D, N, 4 * DH_D), BF16), pltpu.VMEM((2 * H_D, N, 2 * DH_D), BF16),
               pltpu.VMEM((2 * H_D, rb, 2 * DH_D), BF16), pltpu.VMEM((2 * H_D, rb, 4 * DH_D), F32)]
    return in_specs, (qd, kd, vd, lam_vecs, subg, place, sbound), scratch


def _attn_pair_kernel(*refs, nw, nd, T, S, lam_init):
    w_in, d_in = refs[0:nw[0]], refs[nw[0]:nw[0] + nd[0]]
    w_out, d_out = refs[nw[0] + nd[0]], refs[nw[0] + nd[0] + 1]
    scr = refs[nw[0] + nd[0] + 2:]
    _wattn_kernel(*w_in, w_out, *scr[0:nw[1]], T=T, S=S)
    _dattn_kernel(*d_in, d_out, *scr[nw[1]:nw[1] + nd[1]], T=T, S=S, lam_init=lam_init)


def _attn_pair(wargs, dargs, T, S, lam_init):
    w_specs, w_ops, w_scr = _wattn(*wargs, T, S)
    d_specs, d_ops, d_scr = _dattn(*dargs, T, S, lam_init)
    B, N, _ = w_ops[0].shape
    assert WATTN_QROWS == ROW_BLOCK
    out = pl.BlockSpec((1, ROW_BLOCK, 256), lambda b, j: (b, j, 0))
    return pl.pallas_call(
        functools.partial(_attn_pair_kernel, nw=(len(w_specs), len(w_scr)), nd=(len(d_specs), len(d_scr)),
                          T=T, S=S, lam_init=lam_init),
        grid=(B, N // ROW_BLOCK),
        in_specs=w_specs + d_specs,
        out_specs=[out, out],
        out_shape=[jax.ShapeDtypeStruct((B, N, 256), BF16)] * 2,
        scratch_shapes=w_scr + d_scr,
        compiler_params=_cparams(("arbitrary", "arbitrary")),
        name="attn_pair",
    )(*w_ops, *d_ops)


def _log_sigmoid(x):
    return jnp.minimum(x, 0.0) - jnp.log1p(jnp.exp(-jnp.abs(x)))


def _dot3(a, b, split_b):
    x = b if split_b else a
    hi = x.astype(BF16)
    r1 = x - hi.astype(F32)
    mid = r1.astype(BF16)
    lo = (r1 - mid.astype(F32)).astype(BF16)
    if split_b:
        a = a.astype(BF16)
        return _dot(a, hi) + _dot(a, mid) + _dot(a, lo)
    b = b.astype(BF16)
    return _dot(hi, b) + _dot(mid, b) + _dot(lo, b)


def _mlstm_kernel(qtf_ref, kf_ref, vtf_ref, gf_ref, gtf_ref, qtb_ref, kb_ref, vtb_ref, gb_ref, gtb_ref,
                  bias_ref, biast_ref, hf_ref, hb_ref, c_ref, m_ref):
    i = pl.program_id(1)
    lc = MLSTM_CHUNK

    @pl.when(i == 0)
    def _():
        c_ref[...] = jnp.zeros_like(c_ref)
        m_ref[...] = jnp.zeros_like(m_ref)

    r_io = lax.broadcasted_iota(I32, (lc, lc), 0)
    c_io = lax.broadcasted_iota(I32, (lc, lc), 1)
    lower = r_io >= c_io
    upper = r_io <= c_io
    tri_lo = jnp.where(lower, 1.0, 0.0).astype(F32)
    tri_up = jnp.where(upper, 1.0, 0.0).astype(F32)
    ones_rows = jnp.where(lax.broadcasted_iota(I32, (DH_C, lc), 0) == 0, 1.0, 0.0).astype(BF16)

    def direction(d, qt_ref, k_ref, vt_ref, g_ref, gt_ref, out_ref):
        pre = g_ref[0] + bias_ref[...]
        pre_t = gt_ref[0] + biast_ref[...]
        lf = _log_sigmoid(pre)
        lf_t = _log_sigmoid(pre_t)
        if d == 0:
            bcol = _dot3(tri_lo, lf, True)
            brow = _dot3(lf_t, tri_up, False)
            valid, end = upper, lc - 1
        else:
            bcol = _dot3(tri_up, lf, True)
            brow = _dot3(lf_t, tri_lo, False)
            valid, end = lower, 0
        for hd in range(H_C):
            ci, cf = d * H_C + hd, (2 + d) * H_C + hd
            rs = slice(hd * DH_C, (hd + 1) * DH_C)
            q_t = qt_ref[0, rs, :]
            k = k_ref[0, :, rs]
            v_aug = jnp.concatenate([vt_ref[0, rs, :], ones_rows], axis=0)
            b_row = brow[cf:cf + 1, :]
            ig_row = pre_t[ci:ci + 1, :]
            bi_col = bcol[:, cf:cf + 1] - pre[:, ci:ci + 1]
            b_end = brow[cf:cf + 1, end:end + 1]
            slot = d * H_C + hd
            c = c_ref[slot]
            m = m_ref[slot][:, 0:1]
            logw = jnp.where(valid, b_row - bi_col, NEG)
            inter = b_row + m
            m_t = jnp.maximum(inter, jnp.max(logw, axis=0, keepdims=True))
            w_in = jnp.exp(inter - m_t)
            p_t = (_dot(k, q_t) * jnp.exp(logw - m_t)).astype(BF16)
            num = w_in * _dot(c.astype(BF16), q_t) + _dot(v_aug, p_t)
            den = jnp.maximum(jnp.abs(num[DH_C:DH_C + 1, :]), jnp.exp(-m_t))
            out_ref[0, rs, :] = num[0:DH_C, :] / den
            g_end = b_end - b_row + ig_row
            m_new = jnp.maximum(b_end + m, jnp.max(g_end, axis=1, keepdims=True))
            wv = (v_aug.astype(F32) * jnp.exp(g_end - m_new)).astype(BF16)
            c_ref[slot] = jnp.exp(b_end + m - m_new) * c + _dot(wv, k)
            m_ref[slot] = jnp.broadcast_to(m_new, (1, 128))

    direction(0, qtf_ref, kf_ref, vtf_ref, gf_ref, gtf_ref, hf_ref)
    direction(1, qtb_ref, kb_ref, vtb_ref, gb_ref, gtb_ref, hb_ref)


def _mlstm(qct, kc, vct, gc, gct, bias, biast, T, S):
    B, N, _ = kc.shape
    lc = MLSTM_CHUNK
    nch, nlat = N // lc, T // lc
    fwd = lambda i: jnp.where(i < nch - nlat, nlat + i, i - (nch - nlat))
    bwd = lambda i: nch - 1 - i
    tok = lambda f: pl.BlockSpec((1, lc, 256), lambda b, i: (b, f(i), 0))
    tok_t = lambda f: pl.BlockSpec((1, 256, lc), lambda b, i: (b, 0, f(i)))
    gate = lambda f: pl.BlockSpec((1, lc, 128), lambda b, i: (b, f(i), 0))
    gate_t = lambda f: pl.BlockSpec((1, 16, lc), lambda b, i: (b, 0, f(i)))
    return pl.pallas_call(
        _mlstm_kernel,
        grid=(B, nch),
        in_specs=[tok_t(fwd), tok(fwd), tok_t(fwd), gate(fwd), gate_t(fwd),
                  tok_t(bwd), tok(bwd), tok_t(bwd), gate(bwd), gate_t(bwd),
                  pl.BlockSpec((1, 128), lambda b, i: (0, 0)),
                  pl.BlockSpec((16, 1), lambda b, i: (0, 0))],
        out_specs=[tok_t(fwd), tok_t(bwd)],
        out_shape=[jax.ShapeDtypeStruct((B, 256, N), F32)] * 2,
        scratch_shapes=[pltpu.VMEM((2 * H_C, 2 * DH_C, DH_C), F32), pltpu.VMEM((2 * H_C, 1, 128), F32)],
        compiler_params=_cparams(("arbitrary", "arbitrary")),
        name="mlstm",
    )(qct, kc, vct, gc, gct, qct, kc, vct, gc, gct, bias, biast)


def _shortconv_kernel(u_ref, w_ref, o_ref, *, T, S):
    u = u_ref[0].astype(F32)
    n = u.shape[0]
    w = w_ref[...]
    row = lax.broadcasted_iota(I32, u.shape, 0)
    zero = jnp.zeros((1, u.shape[1]), F32)
    prev = jnp.concatenate([zero, u[0:n - 1]], axis=0)
    nxt = jnp.concatenate([u[1:n], zero], axis=0)
    prev = jnp.where(row == T, 0.0, prev)
    nxt = jnp.where(row == T - 1, 0.0, nxt)
    o_ref[0] = (w[0:1, :] * prev + w[1:2, :] * u + w[2:3, :] * nxt).astype(BF16)


def _shortconv(ub, conv_w, T, S):
    B, N, W = ub.shape
    return pl.pallas_call(
        functools.partial(_shortconv_kernel, T=T, S=S),
        grid=(B, W // 256),
        in_specs=[pl.BlockSpec((1, N, 256), lambda b, j: (b, 0, j)),
                  pl.BlockSpec((3, 256), lambda b, j: (0, j))],
        out_specs=pl.BlockSpec((1, N, 256), lambda b, j: (b, 0, j)),
        out_shape=jax.ShapeDtypeStruct((B, N, W), BF16),
        compiler_params=_cparams(("arbitrary", "arbitrary")),
        name="hyena_shortconv",
    )(ub, conv_w)


def _hyfilter_kernel(z_ref, w1_ref, b1_ref, fr_ref, w2_ref, b2_ref, w3_ref, dec_ref,
                     hp_ref, hf_ref, ss_ref):
    i = pl.program_id(0)
    fr = fr_ref[...]
    h = jnp.sin(fr * (_dot(z_ref[...], w1_ref[...], precision=HI) + b1_ref[...]))
    h = jnp.sin(fr * (_dot(h, w2_ref[...], precision=HI) + b2_ref[...]))
    h = _dot(h, w3_ref[...], precision=HI)
    dec = dec_ref[...]
    dec2 = jnp.concatenate([dec, dec], axis=1)
    hw = HY_ORDER * HY_W
    h0 = h[:, 0:hw] * dec2
    h1 = h[:, hw:2 * hw] * dec2
    rows = h0.shape[0]
    t = i * rows + lax.broadcasted_iota(I32, h0.shape, 0)
    h1 = jnp.where(t == 0, 0.0, h1)
    hp_ref[...] = h0
    hf_ref[...] = h1

    @pl.when(i == 0)
    def _():
        ss_ref[...] = jnp.zeros_like(ss_ref)

    ss_ref[...] += jnp.sum(h0 * h0 + h1 * h1, axis=0, keepdims=True)


def _hyfilter(zfeat, w1, b1, fr, w2, b2, w3, decay):
    ls = zfeat.shape[0]
    rb = min(ls, 512)
    hw = HY_ORDER * HY_W
    full = lambda a: pl.BlockSpec(a.shape, lambda i: (0,) * a.ndim)
    return pl.pallas_call(
        _hyfilter_kernel,
        grid=(ls // rb,),
        in_specs=[pl.BlockSpec((rb, 128), lambda i: (i, 0)), full(w1), full(b1), full(fr), full(w2),
                  full(b2), full(w3), pl.BlockSpec((rb, HY_W), lambda i: (i, 0))],
        out_specs=[pl.BlockSpec((rb, hw), lambda i: (i, 0)), pl.BlockSpec((rb, hw), lambda i: (i, 0)),
                   pl.BlockSpec((1, hw), lambda i: (0, 0))],
        out_shape=[jax.ShapeDtypeStruct((ls, hw), F32), jax.ShapeDtypeStruct((ls, hw), F32),
                   jax.ShapeDtypeStruct((1, hw), F32)],
        compiler_params=_cparams(("arbitrary",)),
        name="hyena_filter",
    )(zfeat, w1, b1, fr, w2, b2, w3, decay)


def _cmul_const(v, c, s):
    vr, vi = v
    r = math.sqrt(0.5)
    if abs(s) < 1e-9:
        return (vr, vi) if c > 0 else (-vr, -vi)
    if abs(c) < 1e-9:
        return (-vi, vr) if s > 0 else (vi, -vr)
    if abs(abs(c) - r) < 1e-9 and abs(abs(s) - r) < 1e-9:
        a, b = (vr if c > 0 else -vr), (vi if s > 0 else -vi)
        p, q = (vi if c > 0 else -vi), (vr if s > 0 else -vr)
        return (a - b) * r, (p + q) * r
    return c * vr - s * vi, c * vi + s * vr


def _fft_pow2(xs, sign):
    n = len(xs)
    if n == 1:
        return xs
    ev, od = _fft_pow2(xs[0::2], sign), _fft_pow2(xs[1::2], sign)
    out = [None] * n
    for k in range(n // 2):
        ang = sign * 2.0 * math.pi * k / n
        tr, ti = _cmul_const(od[k], math.cos(ang), math.sin(ang))
        out[k] = (ev[k][0] + tr, ev[k][1] + ti)
        out[k + n // 2] = (ev[k][0] - tr, ev[k][1] - ti)
    return out


def _hy_stage1(src_ref, f_ref, a_ref, nseg):
    k1p = a_ref.shape[2]
    for n2 in range(HY_N2):
        if len(src_ref.shape) == 3:
            zs = jnp.concatenate([src_ref[t, pl.ds(n2, nseg, stride=HY_N2), :] for t in range(src_ref.shape[0])],
                                 axis=1).astype(BF16)
        else:
            zs = src_ref[pl.ds(n2, nseg, stride=HY_N2), :].astype(BF16)
        r = _dot(f_ref[n2], zs)
        a_ref[0, n2] = r[0:k1p]
        a_ref[1, n2] = r[k1p:2 * k1p]


def _hy_chunks(a_ref):
    k1p, lanes = a_ref.shape[2], a_ref.shape[3]
    return k1p // 8, [slice(t * 128, (t + 1) * 128) for t in range(lanes // 128)]


def _hyspec_kernel(hp_ref, hf_ref, ss_ref, f_ref, h_ref, ap_ref, af_ref, *, ls):
    nseg = ls // HY_N2
    _hy_stage1(hp_ref, f_ref, ap_ref, nseg)
    _hy_stage1(hf_ref, f_ref, af_ref, nseg)
    nchunk, lane_tiles = _hy_chunks(ap_ref)

    def body(i, _):
        rows = pl.ds(pl.multiple_of(i * 8, 8), 8)
        k1 = i * 8 + lax.broadcasted_iota(I32, (8, 1), 0)
        wk = jnp.where((k1 == 0) | (k1 == nseg), 1.0, 2.0) * (1.0 / (2 * ls))
        for ln in lane_tiles:
            scale = lax.rsqrt(ss_ref[:, ln] + EPS) * wk
            P = _fft_pow2([(ap_ref[0, n2, rows, ln], ap_ref[1, n2, rows, ln]) for n2 in range(HY_N2)], -1)
            Q = _fft_pow2([(af_ref[0, n2, rows, ln], af_ref[1, n2, rows, ln]) for n2 in range(HY_N2)], -1)
            for k2 in range(HY_N2):
                h_ref[0, 0, k2, rows, ln] = (P[k2][0] + Q[k2][0]) * scale
                h_ref[0, 1, k2, rows, ln] = (P[k2][1] - Q[k2][1]) * scale
        return 0

    lax.fori_loop(0, nchunk, body, 0)


def _hyspec(hp, hf, ss, fmat):
    ls = hp.shape[0]
    k1p = fmat.shape[1] // 2
    full = lambda a: pl.BlockSpec(a.shape, lambda o, c: (0,) * a.ndim)
    nc = HY_W // 128
    col = lambda o, c: (0, o * nc + c)
    return pl.pallas_call(
        functools.partial(_hyspec_kernel, ls=ls),
        grid=(HY_ORDER, nc),
        in_specs=[pl.BlockSpec((ls, 128), col), pl.BlockSpec((ls, 128), col), pl.BlockSpec((1, 128), col),
                  full(fmat)],
        out_specs=pl.BlockSpec((1, 2, HY_N2, k1p, 128), lambda o, c: (o, 0, 0, 0, c)),
        out_shape=jax.ShapeDtypeStruct((HY_ORDER, 2, HY_N2, k1p, HY_W), F32),
        scratch_shapes=[pltpu.VMEM((2, HY_N2, k1p, 128), F32)] * 2,
        compiler_params=_cparams(("arbitrary", "arbitrary")),
        name="hyena_spectrum",
    )(hp, hf, ss, fmat)


def _hyconv_kernel(z_ref, gate_ref, h_ref, f_ref, g_ref, bias_ref, o_ref, zf_ref, yf_ref, a_ref, *, ls):
    nseg = ls // HY_N2
    ntile = zf_ref.shape[0]
    for t in range(ntile):
        zf_ref[t] = z_ref[0, :, t * 128:(t + 1) * 128].astype(F32)
    _hy_stage1(zf_ref, f_ref, a_ref, nseg)
    nchunk, lane_tiles = _hy_chunks(a_ref)

    def body(i, _):
        rows = pl.ds(pl.multiple_of(i * 8, 8), 8)
        for ln in lane_tiles:
            X = _fft_pow2([(a_ref[0, n2, rows, ln], a_ref[1, n2, rows, ln]) for n2 in range(HY_N2)], -1)
            Y = []
            for k2 in range(HY_N2):
                hr, hi = h_ref[0, 0, k2, rows, ln], h_ref[0, 1, k2, rows, ln]
                xr, xi = X[k2]
                Y.append((xr * hr - xi * hi, xr * hi + xi * hr))
            Bv = _fft_pow2(Y, 1)
            for n2 in range(HY_N2):
                a_ref[0, n2, rows, ln] = Bv[n2][0]
                a_ref[1, n2, rows, ln] = Bv[n2][1]
        return 0

    lax.fori_loop(0, nchunk, body, 0)
    for n2 in range(HY_N2):
        bb = jnp.concatenate([a_ref[0, n2], a_ref[1, n2]], axis=0).astype(BF16)
        yv = _dot(g_ref[n2], bb)
        for t in range(ntile):
            yf_ref[t, pl.ds(n2, nseg, stride=HY_N2), :] = yv[:, t * 128:(t + 1) * 128]
    for t in range(ntile):
        ln = slice(t * 128, (t + 1) * 128)
        o_ref[0, :, ln] = (gate_ref[0, :, ln].astype(F32)
                           * (yf_ref[t] + zf_ref[t] * bias_ref[:, ln])).astype(BF16)


def _hyconv(zsrc, zcol, zrow, gsrc, gcol, grow, hspec, order, fmat, gmat, bias, ls):
    B = zsrc.shape[0]
    k1p = fmat.shape[1] // 2
    once = lambda a: pl.BlockSpec(a.shape, lambda b: (0,) * a.ndim, pipeline_mode=pl.Buffered(1))
    return pl.pallas_call(
        functools.partial(_hyconv_kernel, ls=ls),
        grid=(B,),
        in_specs=[pl.BlockSpec((1, ls, HY_W), lambda b: (b, zrow, zcol)),
                  pl.BlockSpec((1, ls, HY_W), lambda b: (b, grow, gcol)),
                  pl.BlockSpec((1, 2, HY_N2, k1p, HY_W), lambda b: (order, 0, 0, 0, 0),
                               pipeline_mode=pl.Buffered(1)),
                  once(fmat), once(gmat), pl.BlockSpec((1, HY_W), lambda b: (0, 0))],
        out_specs=pl.BlockSpec((1, ls, HY_W), lambda b: (b, 0, 0)),
        out_shape=jax.ShapeDtypeStruct((B, ls, HY_W), BF16),
        scratch_shapes=[pltpu.VMEM((HY_W // 128, ls, 128), F32), pltpu.VMEM((HY_W // 128, ls, 128), F32),
                        pltpu.VMEM((2, HY_N2, k1p, HY_W), F32)],
        compiler_params=_cparams(("arbitrary",)),
        name="hyena_longconv",
    )(zsrc, gsrc, hspec, fmat, gmat, bias)


def _outproj_kernel(x_ref, mod_ref, ya_ref, ybl_ref, ybc_ref, hf_ref, hb_ref, oc_ref, yd_ref, mlg_ref, g64_ref,
                    w_ref, g2_ref, wr_ref, xo_ref, h2_ref, aff_ref, *, nlat):
    hsum = (hf_ref[0] + hb_ref[0]).T
    msq = _dot((hsum * hsum).astype(BF16), g64_ref[...])
    yc = jax.nn.sigmoid(oc_ref[0].astype(F32)) * (hsum * lax.rsqrt(msq + EPS) * mlg_ref[...])
    yb = jnp.where(pl.program_id(1) < nlat, ybl_ref[0], ybc_ref[0])
    y = jnp.concatenate([ya_ref[0], yb, yc.astype(BF16), yd_ref[0]], axis=1)
    mod = mod_ref[0, 0]
    x = x_ref[0] + mod[2:3, :] * _dot(y, w_ref[...])
    xo_ref[0] = x
    ms = jnp.mean(x * x, axis=-1, keepdims=True)
    h2 = x * lax.rsqrt(ms + EPS) * g2_ref[...] * (1.0 + mod[4:5, :]) + mod[3:4, :]
    h2_hi = h2.astype(BF16)
    h2_ref[0] = _pack_bf16_pair(h2_hi[:, 0:D_MODEL // 2], h2_hi[:, D_MODEL // 2:])
    h2_lo = (h2 - h2_hi.astype(F32)).astype(BF16)
    lg = _dot(h2_hi, wr_ref[...])
    logits = lg[:, 0:128] + lg[:, 128:256] + _dot(h2_lo, wr_ref[:, 0:128])
    lane = lax.broadcasted_iota(I32, logits.shape, 1)
    logits = jnp.where(lane < N_EXPERTS, logits, NEG)
    e = jnp.exp(logits - jnp.max(logits, axis=-1, keepdims=True))
    aff_ref[0] = (e / jnp.sum(e, axis=-1, keepdims=True)).T


def _outproj(x, mod, ya, yb_l, yb_c, hf, hb, oc, yd, mlg, g64, w_out, g2, w_router, T):
    B, N, D = x.shape
    rb = ROW_BLOCK
    nlat = T // rb
    row = lambda w: pl.BlockSpec((1, rb, w), lambda b, r: (b, r, 0))
    full = lambda a: pl.BlockSpec(a.shape, lambda b, r: (0,) * a.ndim)
    return pl.pallas_call(
        functools.partial(_outproj_kernel, nlat=nlat),
        grid=(B, N // rb),
        in_specs=[row(D), pl.BlockSpec((1, 1, 6, D), lambda b, r: (b, jnp.where(r >= nlat, 1, 0), 0, 0)),
                  row(256),
                  pl.BlockSpec((1, rb, 256), lambda b, r: (b, jnp.minimum(r, nlat - 1), 0)),
                  pl.BlockSpec((1, rb, 256), lambda b, r: (b, jnp.maximum(r - nlat, 0), 0)),
                  pl.BlockSpec((1, 256, rb), lambda b, r: (b, 0, r)),
                  pl.BlockSpec((1, 256, rb), lambda b, r: (b, 0, r)), row(256), row(256),
                  full(mlg), full(g64), full(w_out), full(g2), full(w_router)],
        out_specs=[row(D), row(D // 2), pl.BlockSpec((1, 128, rb), lambda b, r: (b, 0, r))],
        out_shape=[jax.ShapeDtypeStruct((B, N, D), F32), jax.ShapeDtypeStruct((B, N, D // 2), I32),
                   jax.ShapeDtypeStruct((B, 128, N), F32)],
        compiler_params=_cparams(("arbitrary", "arbitrary")),
        name="outproj",
    )(x, mod, ya, yb_l, yb_c, hf, hb, oc, yd, mlg, g64, w_out, g2, w_router)


def _prefix_exclusive(x):
    n = x.shape[1]
    lane = lax.broadcasted_iota(I32, x.shape, 1)
    inc = x
    d = 1
    while d < n:
        inc = inc + jnp.where(lane >= d, pltpu.roll(inc, d, axis=1), 0)
        d *= 2
    return inc - x


def _topk_kernel(aff_ref, i_ref, g_ref, pos_ref, val_ref, *, T, S):
    def segment(t0, ts, cap, slot0):
        a = aff_ref[0, :, t0:t0 + ts]
        bits = lax.bitcast_convert_type(a, I32)

        def search(i, v):
            cand = v | jnp.left_shift(jnp.int32(1), 30 - i)
            cnt = jnp.sum((bits >= cand).astype(I32), axis=1, keepdims=True)
            return jnp.where(cnt >= cap, cand, v)

        thr = lax.fori_loop(0, 31, search, jnp.zeros((N_EXPERTS, 1), I32))
        gt = bits > thr
        eq = bits == thr
        need = cap - jnp.sum(gt.astype(I32), axis=1, keepdims=True)
        sel = gt | (eq & (_prefix_exclusive(eq.astype(I32)) < need))
        seli = sel.astype(I32)
        pos_ref[:, 0:ts] = jnp.where(sel, _prefix_exclusive(seli), -1)
        a_hi = a.astype(BF16)
        r1 = a - a_hi.astype(F32)
        a_mid = r1.astype(BF16)
        a_lo = (r1 - a_mid.astype(F32)).astype(BF16)
        val_ref[0, :, 0:ts] = a_hi.astype(F32)
        val_ref[1, :, 0:ts] = a_mid.astype(F32)
        val_ref[2, :, 0:ts] = a_lo.astype(F32)
        tabs = lax.broadcasted_iota(I32, (1, ts), 1) + t0
        t_hi = jnp.right_shift(tabs, 6).astype(F32)
        t_lo = (tabs & 63).astype(F32)
        slot = lax.broadcasted_iota(I32, (cap, ts), 0)
        zeros = jnp.zeros((11, ts), F32)

        def per_expert(e, _):
            onehot = jnp.where(pos_ref[pl.ds(e, 1), 0:ts] == slot, 1.0, 0.0).astype(BF16)
            vals = jnp.concatenate([t_hi, t_lo, val_ref[0, pl.ds(e, 1), 0:ts], val_ref[1, pl.ds(e, 1), 0:ts],
                                    val_ref[2, pl.ds(e, 1), 0:ts], zeros], axis=0).astype(BF16)
            r = _dot_nt(onehot, vals)
            i_ref[0, e, slot0:slot0 + cap, :] = (r[:, 0:1] * 64.0 + r[:, 1:2]).astype(I32)
            g_ref[0, e, slot0:slot0 + cap, :] = r[:, 2:3] + r[:, 3:4] + r[:, 4:5]
            return 0

        lax.fori_loop(0, N_EXPERTS, per_expert, 0)

    segment(0, T, (EC_CAPACITY * T) // N_EXPERTS, 0)
    segment(T, S, (EC_CAPACITY * S) // N_EXPERTS, (EC_CAPACITY * T) // N_EXPERTS)


def _topk(aff_t, T, S):
    B, _, N = aff_t.shape
    E = N_EXPERTS
    cap_t = (EC_CAPACITY * T) // E + (EC_CAPACITY * S) // E
    return pl.pallas_call(
        functools.partial(_topk_kernel, T=T, S=S),
        grid=(B,),
        in_specs=[pl.BlockSpec((1, E, N), lambda b: (b, 0, 0))],
        out_specs=[pl.BlockSpec((1, E, cap_t, 1), lambda b: (b, 0, 0, 0))] * 2,
        out_shape=[jax.ShapeDtypeStruct((B, E, cap_t, 1), I32), jax.ShapeDtypeStruct((B, E, cap_t, 1), F32)],
        scratch_shapes=[pltpu.VMEM((E, T), I32), pltpu.VMEM((3, E, T), F32)],
        compiler_params=_cparams(("arbitrary",)),
        name="expert_topk",
    )(aff_t)


def _gather_kernel(idx_ref, h_ref, *rest, slots, nstep):
    o_ref = rest[-1]
    b, e = pl.program_id(0), pl.program_id(1)
    base = (b * nstep + e) * slots

    def body(g, _):
        rows = [h_ref[0, pl.ds(idx_ref[base + g * 16 + k], 1), :] for k in range(16)]
        o_ref[0, pl.ds(pl.multiple_of(g * 16, 16), 16), :] = jnp.concatenate(rows, axis=0)
        return 0

    lax.fori_loop(0, slots // 16, body, 0)


def _gather(idx_flat, h2p, xe_buf, b0, bg, cap_t):
    B, N, hw = h2p.shape
    slots, nstep = GATHER_EXPERTS * cap_t, N_EXPERTS // GATHER_EXPERTS
    in_specs = [pl.BlockSpec((1, N, hw), lambda b, e, idx: (b0 + b, 0, 0))]
    args = [idx_flat, h2p]
    aliases = {}
    if xe_buf is not None:
        in_specs.append(pl.BlockSpec(memory_space=pl.ANY))
        args.append(xe_buf)
        aliases = {2: 0}
    return pl.pallas_call(
        functools.partial(_gather_kernel, slots=slots, nstep=nstep),
        grid_spec=pltpu.PrefetchScalarGridSpec(
            num_scalar_prefetch=1,
            grid=(bg, nstep),
            in_specs=in_specs,
            out_specs=pl.BlockSpec((1, slots, hw), lambda b, e, idx: (b0 + b, e, 0)),
        ),
        out_shape=jax.ShapeDtypeStruct((B, N_EXPERTS * cap_t, hw), I32),
        input_output_aliases=aliases,
        compiler_params=_cparams(("arbitrary", "arbitrary")),
        name="expert_gather",
    )(*args)


def _ffn_kernel(x_ref, wg_ref, wu_ref, wd_ref, gate_ref, gt_ref, o_ref, wg_s, wu_s, wd_s, *, cap_l):
    @pl.when(pl.program_id(1) == 0)
    def _():
        wg_s[...] = wg_ref[0, 0].astype(BF16)
        wu_s[...] = wu_ref[0, 0].astype(BF16)
        wd_s[...] = wd_ref[0, 0].astype(BF16)

    x = jnp.concatenate(_unpack_bf16_pair(x_ref[0, 0]), axis=1)
    a = _dot(x, wg_s[...])
    u = _dot(x, wu_s[...])
    hmid = (a * jax.nn.sigmoid(a) * u).astype(BF16)
    y = _dot(hmid, wd_s[...])
    gt = gt_ref[0]
    row = lax.broadcasted_iota(I32, y.shape, 0)
    gt2 = jnp.where(row < cap_l, gt[0:1, :], gt[1:2, :])
    o_ref[0, 0] = y * gate_ref[0, 0] * gt2


def _ffn(xe, layer, wg, wu, wd, gate, gt2, cap_l):
    B, E, cap_t, hw = xe.shape
    D = 2 * hw
    wspec = pl.BlockSpec((1, 1, D, D), lambda e, b: (layer, e, 0, 0))
    return pl.pallas_call(
        functools.partial(_ffn_kernel, cap_l=cap_l),
        grid=(E, B),
        in_specs=[pl.BlockSpec((1, 1, cap_t, hw), lambda e, b: (b, e, 0, 0)), wspec, wspec, wspec,
                  pl.BlockSpec((1, 1, cap_t, 1), lambda e, b: (b, e, 0, 0)),
                  pl.BlockSpec((1, 2, D), lambda e, b: (b, 0, 0))],
        out_specs=pl.BlockSpec((1, 1, cap_t, D), lambda e, b: (b, e, 0, 0)),
        out_shape=jax.ShapeDtypeStruct((B, E, cap_t, D), F32),
        scratch_shapes=[pltpu.VMEM((D, D), BF16)] * 3,
        compiler_params=_cparams(("arbitrary", "arbitrary")),
        name="expert_ffn",
    )(xe, wg, wu, wd, gate, gt2)


def _scatter_kernel(idx_ref, y_ref, *rest, slots, nstep):
    o_ref = rest[-1]
    b, e = pl.program_id(0), pl.program_id(1)
    base = (b * nstep + e) * slots

    @pl.when(e == 0)
    def _():
        o_ref[...] = jnp.zeros_like(o_ref)

    def body(g, _):
        tile = y_ref[0, pl.ds(pl.multiple_of(g * 8, 8), 8), :]
        rows = [idx_ref[base + g * 8 + k] for k in range(8)]
        cur = [o_ref[0, pl.ds(r, 1), :] for r in rows]
        for k, r in enumerate(rows):
            o_ref[0, pl.ds(r, 1), :] = cur[k] + tile[k:k + 1, :]
        return 0

    lax.fori_loop(0, slots // 8, body, 0)


def _scatter(idx_flat, ye, acc_buf, n_rows, b0, bg, cap_t):
    B, _, D = ye.shape
    slots, nstep = SCATTER_EXPERTS * cap_t, N_EXPERTS // SCATTER_EXPERTS
    in_specs = [pl.BlockSpec((1, slots, D), lambda b, e, idx: (b0 + b, e, 0))]
    args = [idx_flat, ye]
    aliases = {}
    if acc_buf is not None:
        in_specs.append(pl.BlockSpec(memory_space=pl.ANY))
        args.append(acc_buf)
        aliases = {2: 0}
    return pl.pallas_call(
        functools.partial(_scatter_kernel, slots=slots, nstep=nstep),
        grid_spec=pltpu.PrefetchScalarGridSpec(
            num_scalar_prefetch=1,
            grid=(bg, nstep),
            in_specs=in_specs,
            out_specs=pl.BlockSpec((1, n_rows, D), lambda b, e, idx: (b0 + b, 0, 0)),
        ),
        out_shape=jax.ShapeDtypeStruct((B, n_rows, D), F32),
        input_output_aliases=aliases,
        compiler_params=_cparams(("arbitrary", "arbitrary")),
        name="expert_scatter",
    )(*args)


def _residual_out_kernel(x_ref, f_ref, o_ref):
    o_ref[...] = x_ref[...] + f_ref[...]


def _residual_out(x, f, T):
    B, N, D = x.shape
    rb = ROW_BLOCK
    spec = pl.BlockSpec((1, rb, D), lambda b, r: (b, r, 0))
    return pl.pallas_call(
        _residual_out_kernel,
        grid=(B, T // rb),
        in_specs=[spec, spec],
        out_specs=spec,
        out_shape=jax.ShapeDtypeStruct((B, T, D), F32),
        compiler_params=_cparams(("arbitrary", "arbitrary")),
        name="residual_out",
    )(x, f)


def _rope_tables(T, S, dh):
    rows = T // GRID_W
    r = jnp.broadcast_to(jnp.arange(rows, dtype=F32)[:, None], (rows, GRID_W)).reshape(T)
    col = jnp.broadcast_to(jnp.arange(GRID_W, dtype=F32)[None, :], (rows, GRID_W)).reshape(T)
    nf = dh // 4
    inv = ROPE_BASE ** (-jnp.arange(nf, dtype=F32) / nf)
    ar, ac = r[:, None] * inv, col[:, None] * inv
    ang = jnp.concatenate([ar, ar, ac, ac], axis=1)
    ang = jnp.concatenate([ang, jnp.zeros((S, dh), F32)], axis=0)
    reps = 256 // dh
    return jnp.tile(jnp.cos(ang), (1, reps)), jnp.tile(jnp.sin(ang), (1, reps))


def _group_mats(dh):
    i = np.arange(256)
    gmat = (i[:, None] // dh == i[None, :] // dh).astype(np.float32) / dh
    nf = dh // 4
    half = (i % (2 * nf)) // nf
    pmat = np.zeros((256, 256), np.float32)
    a_idx = i[half == 0]
    pmat[a_idx + nf, a_idx] = -1.0
    pmat[a_idx, a_idx + nf] = 1.0
    return jnp.asarray(gmat, BF16), jnp.asarray(pmat, BF16)


def _hyena_tables(ls):
    t = jnp.linspace(0.0, 1.0, ls, dtype=F32)[:, None]
    w = 2.0 * math.pi * jnp.arange(ls, dtype=F32)[:, None] / ls
    bands = jnp.linspace(1e-4, HY_BANDS - 1, HY_BANDS, dtype=F32)
    z = jnp.concatenate([t, jnp.cos(bands * w), -jnp.sin(bands * w)], axis=-1)
    z = jnp.pad(z, ((0, 0), (0, 128 - HY_EMB)))
    deltas = jnp.abs(jnp.linspace(math.log(HY_TARGET) / HY_SLOW, math.log(HY_TARGET) / HY_FAST, HY_W, dtype=F32))
    decay = jnp.exp(-t * deltas)
    n, nseg = 2 * ls, ls // HY_N2
    k1p = -(-(nseg + 1) // 8) * 8
    k1 = jnp.arange(k1p, dtype=I32)[None, :, None]
    tpos = HY_N2 * jnp.arange(nseg, dtype=I32)[None, None, :] + jnp.arange(HY_N2, dtype=I32)[:, None, None]
    ang = ((k1 * tpos) % n).astype(F32) * (2.0 * math.pi / n)
    keep = k1 <= nseg
    fmat = jnp.concatenate([jnp.where(keep, jnp.cos(ang), 0.0), jnp.where(keep, -jnp.sin(ang), 0.0)],
                           axis=1).astype(BF16)
    return z, decay, fmat, jnp.swapaxes(fmat, 1, 2)


def kernel(x, c, ctx, c_ctx, w_ada, b_ada, norm1_g, norm2_g, w_in, b_gate, a_qnorm, a_knorm, a_sink, hy_conv, hy_fw1, hy_fb1, hy_freq, hy_fw2, hy_fb2, hy_fw3, hy_bias, ml_norm, d_qnorm, d_knorm, d_lq1, d_lk1, d_lq2, d_lk2, d_subnorm, w_out, w_router, w_e_gate, w_e_up, w_e_down):
    B, T, D = x.shape
    S = ctx.shape[1]
    N = T + S
    depth = w_ada.shape[0]
    assert D == D_MODEL and T % S == 0 and S % ROW_BLOCK == 0 and S % MLSTM_CHUNK == 0
    assert S % WATTN_QROWS == 0 and T >= WATTN_QROWS + 2 * BLK and WATTN_QROWS & (WATTN_QROWS - 1) == 0
    cap_l, cap_c = (EC_CAPACITY * T) // N_EXPERTS, (EC_CAPACITY * S) // N_EXPERTS
    cap_t = cap_l + cap_c
    assert cap_t % 16 == 0

    xs = jnp.concatenate([x, ctx], axis=1)

    rpad = -(B + 1) % 8
    cc = jnp.concatenate([c, c_ctx[None, :], jnp.zeros((rpad, D), F32)], axis=0)
    mods = _ada_mods(cc, w_ada, b_ada)

    cos_a, sin_a = _rope_tables(T, S, DH_A)
    cos_d, sin_d = _rope_tables(T, S, DH_D)
    g64, p64 = _group_mats(DH_A)
    g32, p32 = _group_mats(DH_D)
    tabs_l = _hyena_tables(T)
    tabs_c = _hyena_tables(S)

    offs = np.cumsum((0, 256, 128, 128, 768, 256, 256, 256, 256, 16, 256, 256, 256))
    ffn_acc = None
    for l in range(depth):
        lam_init = 0.8 - 0.6 * math.exp(-0.3 * l)
        ml = mods[l]
        mod = jnp.stack([ml[:B].reshape(B, 6, D), jnp.broadcast_to(ml[B].reshape(1, 6, D), (B, 6, D))],
                        axis=1)
        w = w_in[l]
        wp = jnp.concatenate([w[:, offs[0]:offs[8]], w[:, offs[8]:offs[9]], jnp.zeros((D, 112), F32),
                              w[:, offs[9]:offs[12]]], axis=1).astype(BF16)
        gains = jnp.stack([jnp.tile(a_qnorm[l], 4), jnp.tile(a_knorm[l], 4),
                           jnp.tile(d_qnorm[l], 8), jnp.tile(d_knorm[l], 8)], axis=0)
        proj = _inproj(xs, ffn_acc, mod, norm1_g[l].reshape(1, D), wp, (cos_a, sin_a, cos_d, sin_d), gains,
                       (g64, g32, p64, p32), T)
        qa, ka, va, ub, kc, oc, gc, qd, kd, vd, gct, qct, vct = proj[0:13]
        if ffn_acc is not None:
            xs = proj[13]

        sb_a = (jnp.max(jnp.abs(a_qnorm[l])) * jnp.max(jnp.abs(a_knorm[l])) * DH_A ** 0.5).reshape(1, 1)
        sb_d = (jnp.max(jnp.abs(d_qnorm[l])) * jnp.max(jnp.abs(d_knorm[l])) * DH_D ** 0.5).reshape(1, 1)
        ya, yd = _attn_pair(
            (qa, ka, va, jnp.pad(a_sink[l], (0, 128 - H_A)).reshape(1, 128), sb_a),
            (qd, kd, vd, jnp.stack([d_lq1[l], d_lk1[l], d_lq2[l], d_lk2[l]], axis=0),
             d_subnorm[l].reshape(1, 2 * DH_D), sb_d), T, S, lam_init)

        bias = b_gate[l].reshape(-1)
        hf, hb = _mlstm(qct, kc, vct, gc, gct, jnp.pad(bias, (0, 112)).reshape(1, 128), bias.reshape(16, 1), T, S)

        uc = _shortconv(ub, hy_conv[l], T, S)
        w1 = jnp.pad(hy_fw1[l], ((0, 128 - HY_EMB), (0, 0)))
        ybs = []
        for (zf, decay, fmat, gmat), ls, rowblk in ((tabs_l, T, 0), (tabs_c, S, T // S)):
            hpast, hfut, ss = _hyfilter(zf, w1, hy_fb1[l].reshape(1, -1), hy_freq[l].reshape(1, -1), hy_fw2[l],
                                        hy_fb2[l].reshape(1, -1), hy_fw3[l], decay)
            hspec = _hyspec(hpast, hfut, ss, fmat)
            z1 = _hyconv(uc, 0, rowblk, uc, 1, rowblk, hspec, 0, fmat, gmat, hy_bias[l, 0].reshape(1, HY_W), ls)
            z2 = _hyconv(z1, 0, 0, uc, 2, rowblk, hspec, 1, fmat, gmat, hy_bias[l, 1].reshape(1, HY_W), ls)
            ybs.append(z2)

        wr = jnp.pad(w_router[l], ((0, 0), (0, 128 - N_EXPERTS)))
        wr_hi = wr.astype(BF16)
        wr_cat = jnp.concatenate([wr_hi, (wr - wr_hi.astype(F32)).astype(BF16)], axis=1)
        xs, h2, aff = _outproj(xs, mod, ya, ybs[0], ybs[1], hf, hb, oc, yd, jnp.tile(ml_norm[l], 4).reshape(1, 256), g64,
                               w_out[l].astype(BF16), norm2_g[l].reshape(1, D), wr_cat, T)

        idx, gate = _topk(aff, T, S)
        gt2 = mod[:, :, 5, :]
        bg = min(IDX_BATCH_GROUP, B)
        groups = [(b0, idx[b0:b0 + bg].reshape(-1)) for b0 in range(0, B, bg)]
        xe = None
        for b0, idx_flat in groups:
            xe = _gather(idx_flat, h2, xe, b0, bg, cap_t)
        ye = _ffn(xe.reshape(B, N_EXPERTS, cap_t, D // 2), l, w_e_gate, w_e_up, w_e_down, gate, gt2, cap_l)
        ye = ye.reshape(B, N_EXPERTS * cap_t, D)
        ffn_acc = None
        for b0, idx_flat in groups:
            ffn_acc = _scatter(idx_flat, ye, ffn_acc, N, b0, bg, cap_t)
    return _residual_out(xs, ffn_acc, T)
```

```python
import functools
import math

import jax
import jax.numpy as jnp
import numpy as np
from jax import lax
from jax.experimental import pallas as pl
from jax.experimental.pallas import tpu as pltpu

F32 = jnp.float32
BF16 = jnp.bfloat16
I32 = jnp.int32
HI = lax.Precision.HIGHEST

D_MODEL = 1024
GRID_W = 64
GROUP_W = D_MODEL // 4
H_A, HKV_A = 4, 2
G_A = H_A // HKV_A
DH_A = GROUP_W // H_A
WINDOW = 128
BLK = 128
HY_W = GROUP_W
HY_ORDER = 2
HY_BANDS = 16
HY_EMB = 1 + 2 * HY_BANDS
HY_HID = 64
HY_TARGET, HY_FAST, HY_SLOW = 1e-2, 0.3, 1.5
HY_N2 = 16
H_C = 4
DH_C = GROUP_W // H_C
H_D = 4
DH_D = GROUP_W // (2 * H_D)
N_EXPERTS = 16
EC_CAPACITY = 2
ROPE_BASE = 10000.0
EPS = 1e-6
NEG = -1e30

ROW_BLOCK = 256
WATTN_QROWS = 256
DATTN_MIN_ROWSUM = 1e-25
MLSTM_CHUNK = 256
IDX_BATCH_GROUP = 16
GATHER_EXPERTS = 4
SCATTER_EXPERTS = 2
VMEM_LIMIT = 56 * 1024 * 1024

_C_QA, _C_KA, _C_VA, _C_UB = 0, 256, 384, 512
_C_QC, _C_KC, _C_VC, _C_OC, _C_GC = 1280, 1536, 1792, 2048, 2304
_C_QD, _C_KD, _C_VD, _C_END = 2432, 2688, 2944, 3200


def _cparams(sem, vmem=VMEM_LIMIT):
    return pltpu.CompilerParams(dimension_semantics=sem, vmem_limit_bytes=vmem)


def _dot(a, b, **kw):
    return jnp.dot(a, b, preferred_element_type=F32, **kw)


def _dot_nt(a, b, **kw):
    return lax.dot_general(a, b, (((1,), (1,)), ((), ())), preferred_element_type=F32, **kw)


def _pack_bf16_pair(a, b):
    ua = lax.bitcast_convert_type(a.astype(F32), I32)
    ub = lax.bitcast_convert_type(b.astype(F32), I32)
    return ua | lax.shift_right_logical(ub, 16)


def _unpack_bf16_pair(w):
    a = lax.bitcast_convert_type(w & jnp.int32(-65536), F32)
    b = lax.bitcast_convert_type(lax.shift_left(w, 16), F32)
    return a.astype(BF16), b.astype(BF16)


def _ada_kernel(c_ref, w_ref, b_ref, o_ref):
    c = c_ref[...]
    s = c * jax.nn.sigmoid(c)
    o_ref[0] = _dot(s, w_ref[0], precision=HI) + b_ref[0]


def _ada_mods(cc, w_ada, b_ada):
    L, D, W6 = w_ada.shape
    R = cc.shape[0]
    cb = 1536
    return pl.pallas_call(
        _ada_kernel,
        grid=(L, W6 // cb),
        in_specs=[pl.BlockSpec((R, D), lambda l, j: (0, 0)),
                  pl.BlockSpec((1, D, cb), lambda l, j: (l, 0, j)),
                  pl.BlockSpec((1, 1, cb), lambda l, j: (l, 0, j))],
        out_specs=pl.BlockSpec((1, R, cb), lambda l, j: (l, 0, j)),
        out_shape=jax.ShapeDtypeStruct((L, R, W6), F32),
        compiler_params=_cparams(("arbitrary", "arbitrary")),
        name="ada_mods",
    )(cc, w_ada, b_ada.reshape(L, 1, W6))


def _inproj_kernel(*refs, has_f):
    if has_f:
        x_ref, f_ref, *refs = refs
        *refs, xsum = refs
        x = x_ref[0] + f_ref[0]
        xsum[0] = x
    else:
        x_ref, *refs = refs
        x = x_ref[0]
    (mod_ref, g1_ref, w_ref, ca_ref, sa_ref, cd_ref, sd_ref, gains_ref, g64_ref, g32_ref, p64_ref, p32_ref,
     qa, ka, va, ub, kc, oc, gc, qd, kd, vd, gct, qct, vct) = refs
    ms = jnp.mean(x * x, axis=-1, keepdims=True)
    xn = x * lax.rsqrt(ms + EPS) * g1_ref[...]
    mod = mod_ref[0, 0]
    h = xn * (1.0 + mod[1:2, :]) + mod[0:1, :]
    p = _dot(h.astype(BF16), w_ref[...])

    def headnorm_rope(t, gmat, gain, pmat, cos, sin, scale):
        w = t.shape[1]
        msq = _dot((t * t).astype(BF16), gmat[0:w, 0:w])
        tn = t * lax.rsqrt(msq + EPS) * gain
        tn = tn * cos[:, 0:w] + _dot(tn.astype(BF16), pmat[0:w, 0:w]) * sin[:, 0:w]
        return tn * scale

    ca, sa, cd, sd = ca_ref[...], sa_ref[...], cd_ref[...], sd_ref[...]
    gains = gains_ref[...]
    qa[0] = headnorm_rope(p[:, _C_QA:_C_KA], g64_ref, gains[0:1, :], p64_ref, ca, sa, DH_A ** -0.5).astype(BF16)
    ka[0] = headnorm_rope(p[:, _C_KA:_C_VA], g64_ref, gains[1:2, 0:128], p64_ref, ca, sa, 1.0).astype(BF16)
    va[0] = p[:, _C_VA:_C_UB].astype(BF16)
    ub[0] = p[:, _C_UB:_C_QC].astype(BF16)
    qct[0] = p[:, _C_QC:_C_KC].T.astype(BF16)
    kc[0] = (p[:, _C_KC:_C_VC] * DH_C ** -0.5).astype(BF16)
    vct[0] = p[:, _C_VC:_C_OC].T.astype(BF16)
    oc[0] = p[:, _C_OC:_C_GC].astype(BF16)
    gc[0] = p[:, _C_GC:_C_QD]
    gct[0] = p[:, _C_GC:_C_QD].T
    qd[0] = headnorm_rope(p[:, _C_QD:_C_KD], g32_ref, gains[2:3, :], p32_ref, cd, sd, DH_D ** -0.5).astype(BF16)
    kd[0] = headnorm_rope(p[:, _C_KD:_C_VD], g32_ref, gains[3:4, :], p32_ref, cd, sd, 1.0).astype(BF16)
    vd[0] = p[:, _C_VD:_C_END].astype(BF16)


def _inproj(x, f, mod, g1, w_packed, tabs, gains, mats, T):
    B, N, D = x.shape
    rb = ROW_BLOCK
    nlat = T // rb
    row = lambda w: pl.BlockSpec((1, rb, w), lambda b, r: (b, r, 0))
    tab = pl.BlockSpec((rb, 256), lambda b, r: (r, 0))
    full = lambda a: pl.BlockSpec(a.shape, lambda b, r: (0,) * a.ndim)
    widths = [256, 128, 128, 768, 256, 256, 128, 256, 256, 256]
    dtypes = [BF16] * 6 + [F32] + [BF16] * 3
    tr = lambda w: pl.BlockSpec((1, w, rb), lambda b, r: (b, 0, r))
    has_f = f is not None
    return pl.pallas_call(
        functools.partial(_inproj_kernel, has_f=has_f),
        grid=(B, N // rb),
        in_specs=[row(D)] * (2 if has_f else 1)
        + [pl.BlockSpec((1, 1, 6, D), lambda b, r: (b, jnp.where(r >= nlat, 1, 0), 0, 0)),
           full(g1), full(w_packed), tab, tab, tab, tab, full(gains)] + [full(m) for m in mats],
        out_specs=[row(w) for w in widths] + [tr(128), tr(256), tr(256)] + ([row(D)] if has_f else []),
        out_shape=[jax.ShapeDtypeStruct((B, N, w), dt) for w, dt in zip(widths, dtypes)]
        + [jax.ShapeDtypeStruct((B, 128, N), F32), jax.ShapeDtypeStruct((B, 256, N), BF16),
           jax.ShapeDtypeStruct((B, 256, N), BF16)] + ([jax.ShapeDtypeStruct((B, N, D), F32)] if has_f else []),
        compiler_params=_cparams(("arbitrary", "arbitrary")),
        name="inproj",
    )(x, *([f] if has_f else []), mod, g1, w_packed, *tabs, gains, *mats)


def _wattn_kernel(q_ref, k_ref, v_ref, sink_ref, sinkw_ref, placeq_ref, sb_ref, placek_ref, o_ref,
                  kaug_ref, vaug_ref, *, T, S):
    j = pl.program_id(1)
    qb, kwin = WATTN_QROWS, WATTN_QROWS + 2 * BLK
    is_lat = j < T // qb
    start = pl.multiple_of(jnp.clip(j * qb - BLK, 0, T - kwin), BLK)
    row = lax.broadcasted_iota(I32, (G_A * qb, kwin), 0)
    qpos = j * qb + (row & (qb - 1))
    kpos = start + lax.broadcasted_iota(I32, (G_A * qb, kwin), 1)
    valid = (jnp.abs(qpos - kpos) <= WINDOW) & is_lat
    grp = jnp.right_shift(lax.broadcasted_iota(I32, (G_A * qb, 1), 0), qb.bit_length() - 1)
    sinks = sink_ref[...]

    @pl.when(j == 0)
    def _():
        lane = lax.broadcasted_iota(I32, (1, HKV_A * 128), 1)
        one = jnp.where((lane & 127) == DH_A, 1.0, 0.0)
        kaug_ref[...] = (_dot(k_ref[0], placek_ref[...]) + one).astype(BF16)
        vaug_ref[...] = (_dot(v_ref[0], placek_ref[...]) + one).astype(BF16)

    def sink_rows(heads):
        sink = sinks[0:1, heads[0]:heads[0] + 1]
        for g in range(1, G_A):
            sink = jnp.where(grp == g, sinks[0:1, heads[g]:heads[g] + 1], sink)
        return sink

    def store(heads, o):
        for g, hd in enumerate(heads):
            o_ref[0, :, hd * DH_A:(hd + 1) * DH_A] = o[g * qb:(g + 1) * qb].astype(BF16)

    def body_fast():
        q_wide = _dot(q_ref[0], placeq_ref[...])
        lane = lax.broadcasted_iota(I32, (1, H_A * 128), 1)
        shift = jnp.where((lane & 127) == DH_A, jnp.maximum(sb_ref[...], sinkw_ref[...]), 0.0)
        q_aug = (q_wide - shift).astype(BF16)
        low = None
        for hk in range(HKV_A):
            heads = [hk * G_A + g for g in range(G_A)]
            ks = slice(hk * 128, (hk + 1) * 128)
            q2 = jnp.concatenate([q_aug[:, hd * 128:(hd + 1) * 128] for hd in heads], axis=0)
            m_used = -q2[:, DH_A:DH_A + 1].astype(F32)
            s_loc = jnp.where(valid, _dot_nt(q2, kaug_ref[pl.ds(start, kwin), ks]), NEG)
            s_ctx = _dot_nt(q2, kaug_ref[T:T + S, ks])
            pv = (_dot(jnp.exp(s_loc.astype(BF16)), vaug_ref[pl.ds(start, kwin), ks])
                  + _dot(jnp.exp(s_ctx.astype(BF16)), vaug_ref[T:T + S, ks]))
            sigma = pv[:, DH_A:DH_A + 1]
            low = sigma if low is None else jnp.minimum(low, sigma)
            store(heads, pv[:, 0:DH_A] / (sigma + jnp.exp(sink_rows(heads) - m_used)))
        return jnp.min(low)

    def body_exact():
        kw = k_ref[0, pl.ds(start, kwin), :]
        vw = v_ref[0, pl.ds(start, kwin), :]
        kc = k_ref[0, T:T + S, :]
        vc = v_ref[0, T:T + S, :]
        for hk in range(HKV_A):
            cs = slice(hk * DH_A, (hk + 1) * DH_A)
            kwh, vwh, kch, vch = kw[:, cs], vw[:, cs], kc[:, cs], vc[:, cs]
            heads = [hk * G_A + g for g in range(G_A)]
            q = jnp.concatenate([q_ref[0, :, hd * DH_A:(hd + 1) * DH_A] for hd in heads], axis=0)
            sink = sink_rows(heads)
            s_loc = jnp.where(valid, _dot_nt(q, kwh), NEG)
            s_ctx = _dot_nt(q, kch)
            m = jnp.maximum(jnp.maximum(jnp.max(s_loc, axis=-1, keepdims=True),
                                        jnp.max(s_ctx, axis=-1, keepdims=True)), sink)
            p_loc = jnp.exp(s_loc - m)
            p_ctx = jnp.exp(s_ctx - m)
            den = (jnp.sum(p_loc, axis=-1, keepdims=True) + jnp.sum(p_ctx, axis=-1, keepdims=True)
                   + jnp.exp(sink - m))
            store(heads, (_dot(p_loc.astype(BF16), vwh) + _dot(p_ctx.astype(BF16), vch)) / den)

    low = body_fast()

    @pl.when(jnp.logical_not(low > DATTN_MIN_ROWSUM))
    def _():
        body_exact()


def _wattn_place(nheads):
    place = np.zeros((nheads * DH_A, nheads * 128), np.float32)
    for h in range(nheads):
        for d in range(DH_A):
            place[h * DH_A + d, h * 128 + d] = 1.0
    return jnp.asarray(place, BF16)


def _wattn(qa, ka, va, sink, sbound, T, S):
    B, N, _ = qa.shape
    qb = WATTN_QROWS
    placeq, placek = _wattn_place(H_A), _wattn_place(HKV_A)
    sinkw = jnp.zeros((1, H_A * 128), F32).at[0, DH_A::128].set(sink[0, 0:H_A])
    full = lambda a: pl.BlockSpec(a.shape, lambda b, j: (0,) * a.ndim)
    return pl.pallas_call(
        functools.partial(_wattn_kernel, T=T, S=S),
        grid=(B, N // qb),
        in_specs=[pl.BlockSpec((1, qb, 256), lambda b, j: (b, j, 0)),
                  pl.BlockSpec((1, N, 128), lambda b, j: (b, 0, 0)),
                  pl.BlockSpec((1, N, 128), lambda b, j: (b, 0, 0)),
                  pl.BlockSpec((1, 128), lambda b, j: (0, 0)),
                  full(sinkw), full(placeq), full(sbound), full(placek)],
        out_specs=pl.BlockSpec((1, qb, 256), lambda b, j: (b, j, 0)),
        out_shape=jax.ShapeDtypeStruct((B, N, 256), BF16),
        scratch_shapes=[pltpu.VMEM((N, HKV_A * 128), BF16), pltpu.VMEM((N, HKV_A * 128), BF16)],
        compiler_params=_cparams(("arbitrary", "arbitrary")),
        name="window_attn",
    )(qa, ka, va, sink, sinkw, placeq, sbound, placek)


def _dattn_kernel(q_ref, k_ref, v_ref, lam_ref, sub_ref, place_ref, sb_ref, o_ref,
                  vaug_ref, kch_ref, qch_ref, tch_ref, *, T, S, lam_init):
    j = pl.program_id(1)
    shift_lane = (lax.broadcasted_iota(I32, (1, 4 * H_D * DH_D), 1) & (2 * DH_D - 1)) == DH_D
    lv = lam_ref[...]
    lam = (jnp.exp(jnp.sum(lv[0:1, :] * lv[1:2, :], axis=-1, keepdims=True))
           - jnp.exp(jnp.sum(lv[2:3, :] * lv[3:4, :], axis=-1, keepdims=True)) + lam_init)
    subg = sub_ref[...] * (1.0 - lam_init)
    dv = 2 * DH_D
    nchain, aw = 2 * H_D, 2 * DH_D

    @pl.when(j == 0)
    def _():
        ones = jnp.where(lax.broadcasted_iota(I32, (T + S, dv), 1) == 0, 1.0, 0.0).astype(BF16)
        for hd in range(H_D):
            vaug_ref[hd] = jnp.concatenate([v_ref[0, :, hd * dv:(hd + 1) * dv], ones], axis=1)
        k_aug = (_dot(k_ref[0], place_ref[...]) + jnp.where(shift_lane, 1.0, 0.0)).astype(BF16)
        for i in range(nchain):
            kch_ref[i] = k_aug[:, i * aw:(i + 1) * aw]

    def finish(hd, terms):
        o = terms[0] - lam * terms[1]
        ms = jnp.mean(o * o, axis=-1, keepdims=True)
        o_ref[0, :, hd * dv:(hd + 1) * dv] = (o * lax.rsqrt(ms + EPS) * subg).astype(BF16)

    def body_fast(k0, nk):
        q_aug = (_dot(q_ref[0], place_ref[...]) - jnp.where(shift_lane, sb_ref[...], 0.0)).astype(BF16)
        terms, low = [], None
        for i in range(nchain):
            hd, cs = i // 2, slice(i * aw, (i + 1) * aw)
            s = _dot_nt(q_aug[:, cs], kch_ref[i, k0:k0 + nk, :])
            pv = _dot(jnp.exp(s.astype(BF16)), vaug_ref[hd, k0:k0 + nk, :])
            sigma = pv[:, dv:dv + 1]
            low = sigma if low is None else jnp.minimum(low, sigma)
            terms.append(pv[:, 0:dv] / sigma)
            if i % 2 == 1:
                finish(hd, terms)
                terms = []
        return jnp.min(low)

    def body_exact(k0, nk):
        q_wide = _dot(q_ref[0], place_ref[...])
        for i in range(nchain):
            qch_ref[i] = q_wide[:, i * aw:(i + 1) * aw].astype(BF16)

        def chain(i, carry):
            s = _dot_nt(qch_ref[i], kch_ref[i, k0:k0 + nk, :])
            e = jnp.exp((s - jnp.max(s, axis=-1, keepdims=True)).astype(BF16))
            pv = _dot(e, vaug_ref[jnp.right_shift(i, 1), k0:k0 + nk, :])
            tch_ref[i] = pv / pv[:, dv:dv + 1]
            return carry

        lax.fori_loop(0, nchain, chain, 0)
        for hd in range(H_D):
            finish(hd, [tch_ref[2 * hd][:, 0:dv], tch_ref[2 * hd + 1][:, 0:dv]])

    def body(k0, nk):
        low = body_fast(k0, nk)

        @pl.when(jnp.logical_not(low > DATTN_MIN_ROWSUM))
        def _():
            body_exact(k0, nk)

    @pl.when(j < T // ROW_BLOCK)
    def _():
        body(0, T + S)

    @pl.when(j >= T // ROW_BLOCK)
    def _():
        body(T, S)


def _dattn_place():
    nchain, aw = 2 * H_D, 2 * DH_D
    place = np.zeros((nchain * DH_D, nchain * aw), np.float32)
    for i in range(nchain):
        for d in range(DH_D):
            place[i * DH_D + d, i * aw + d] = 1.0
    return jnp.asarray(place, BF16)


def _dattn(qd, kd, vd, lam_vecs, subg, sbound, T, S, lam_init):
    B, N, _ = qd.shape
    rb = ROW_BLOCK
    place = _dattn_place()
    full = lambda a: pl.BlockSpec(a.shape, lambda b, j: (0,) * a.ndim)
    return pl.pallas_call(
        functools.partial(_dattn_kernel, T=T, S=S, lam_init=lam_init),
        grid=(B, N // rb),
        in_specs=[pl.BlockSpec((1, rb, 256), lambda b, j: (b, j, 0)),
                  pl.BlockSpec((1, N, 256), lambda b, j: (b, 0, 0)),
                  pl.BlockSpec((1, N, 256), lambda b, j: (b, 0, 0)),
                  pl.BlockSpec((4, DH_D), lambda b, j: (0, 0)),
                  pl.BlockSpec((1, 2 * DH_D), lambda b, j: (0, 0)), full(place), full(sbound)],
        out_specs=pl.BlockSpec((1, rb, 256), lambda b, j: (b, j, 0)),
        out_shape=jax.ShapeDtypeStruct((B, N, 256), BF16),
        scratch_shapes=[pltpu.VMEM((H_D, N, 4 * DH_D), BF16), pltpu.VMEM((2 * H_D, N, 2 * DH_D), BF16),
                        pltpu.VMEM((2 * H_D, rb, 2 * DH_D), BF16), pltpu.VMEM((2 * H_D, rb, 4 * DH_D), F32)],
        compiler_params=_cparams(("arbitrary", "arbitrary")),
        name="diff_attn",
    )(qd, kd, vd, lam_vecs, subg, place, sbound)


def _log_sigmoid(x):
    return jnp.minimum(x, 0.0) - jnp.log1p(jnp.exp(-jnp.abs(x)))


def _dot3(a, b, split_b):
    x = b if split_b else a
    hi = x.astype(BF16)
    r1 = x - hi.astype(F32)
    mid = r1.astype(BF16)
    lo = (r1 - mid.astype(F32)).astype(BF16)
    if split_b:
        a = a.astype(BF16)
        return _dot(a, hi) + _dot(a, mid) + _dot(a, lo)
    b = b.astype(BF16)
    return _dot(hi, b) + _dot(mid, b) + _dot(lo, b)


def _mlstm_kernel(qtf_ref, kf_ref, vtf_ref, gf_ref, gtf_ref, qtb_ref, kb_ref, vtb_ref, gb_ref, gtb_ref,
                  bias_ref, biast_ref, hf_ref, hb_ref, c_ref, m_ref):
    i = pl.program_id(1)
    lc = MLSTM_CHUNK

    @pl.when(i == 0)
    def _():
        c_ref[...] = jnp.zeros_like(c_ref)
        m_ref[...] = jnp.zeros_like(m_ref)

    r_io = lax.broadcasted_iota(I32, (lc, lc), 0)
    c_io = lax.broadcasted_iota(I32, (lc, lc), 1)
    lower = r_io >= c_io
    upper = r_io <= c_io
    tri_lo = jnp.where(lower, 1.0, 0.0).astype(F32)
    tri_up = jnp.where(upper, 1.0, 0.0).astype(F32)
    ones_rows = jnp.where(lax.broadcasted_iota(I32, (DH_C, lc), 0) == 0, 1.0, 0.0).astype(BF16)

    def direction(d, qt_ref, k_ref, vt_ref, g_ref, gt_ref, out_ref):
        pre = g_ref[0] + bias_ref[...]
        pre_t = gt_ref[0] + biast_ref[...]
        lf = _log_sigmoid(pre)
        lf_t = _log_sigmoid(pre_t)
        if d == 0:
            bcol = _dot3(tri_lo, lf, True)
            brow = _dot3(lf_t, tri_up, False)
            valid, end = upper, lc - 1
        else:
            bcol = _dot3(tri_up, lf, True)
            brow = _dot3(lf_t, tri_lo, False)
            valid, end = lower, 0
        for hd in range(H_C):
            ci, cf = d * H_C + hd, (2 + d) * H_C + hd
            rs = slice(hd * DH_C, (hd + 1) * DH_C)
            q_t = qt_ref[0, rs, :]
            k = k_ref[0, :, rs]
            v_aug = jnp.concatenate([vt_ref[0, rs, :], ones_rows], axis=0)
            b_row = brow[cf:cf + 1, :]
            ig_row = pre_t[ci:ci + 1, :]
            bi_col = bcol[:, cf:cf + 1] - pre[:, ci:ci + 1]
            b_end = brow[cf:cf + 1, end:end + 1]
            slot = d * H_C + hd
            c = c_ref[slot]
            m = m_ref[slot][:, 0:1]
            logw = jnp.where(valid, b_row - bi_col, NEG)
            inter = b_row + m
            m_t = jnp.maximum(inter, jnp.max(logw, axis=0, keepdims=True))
            w_in = jnp.exp(inter - m_t)
            p_t = (_dot(k, q_t) * jnp.exp(logw - m_t)).astype(BF16)
            num = w_in * _dot(c.astype(BF16), q_t) + _dot(v_aug, p_t)
            den = jnp.maximum(jnp.abs(num[DH_C:DH_C + 1, :]), jnp.exp(-m_t))
            out_ref[0, rs, :] = num[0:DH_C, :] / den
            g_end = b_end - b_row + ig_row
            m_new = jnp.maximum(b_end + m, jnp.max(g_end, axis=1, keepdims=True))
            wv = (v_aug.astype(F32) * jnp.exp(g_end - m_new)).astype(BF16)
            c_ref[slot] = jnp.exp(b_end + m - m_new) * c + _dot(wv, k)
            m_ref[slot] = jnp.broadcast_to(m_new, (1, 128))

    direction(0, qtf_ref, kf_ref, vtf_ref, gf_ref, gtf_ref, hf_ref)
    direction(1, qtb_ref, kb_ref, vtb_ref, gb_ref, gtb_ref, hb_ref)


def _mlstm(qct, kc, vct, gc, gct, bias, biast, T, S):
    B, N, _ = kc.shape
    lc = MLSTM_CHUNK
    nch, nlat = N // lc, T // lc
    fwd = lambda i: jnp.where(i < nch - nlat, nlat + i, i - (nch - nlat))
    bwd = lambda i: nch - 1 - i
    tok = lambda f: pl.BlockSpec((1, lc, 256), lambda b, i: (b, f(i), 0))
    tok_t = lambda f: pl.BlockSpec((1, 256, lc), lambda b, i: (b, 0, f(i)))
    gate = lambda f: pl.BlockSpec((1, lc, 128), lambda b, i: (b, f(i), 0))
    gate_t = lambda f: pl.BlockSpec((1, 16, lc), lambda b, i: (b, 0, f(i)))
    return pl.pallas_call(
        _mlstm_kernel,
        grid=(B, nch),
        in_specs=[tok_t(fwd), tok(fwd), tok_t(fwd), gate(fwd), gate_t(fwd),
                  tok_t(bwd), tok(bwd), tok_t(bwd), gate(bwd), gate_t(bwd),
                  pl.BlockSpec((1, 128), lambda b, i: (0, 0)),
                  pl.BlockSpec((16, 1), lambda b, i: (0, 0))],
        out_specs=[tok_t(fwd), tok_t(bwd)],
        out_shape=[jax.ShapeDtypeStruct((B, 256, N), F32)] * 2,
        scratch_shapes=[pltpu.VMEM((2 * H_C, 2 * DH_C, DH_C), F32), pltpu.VMEM((2 * H_C, 1, 128), F32)],
        compiler_params=_cparams(("arbitrary", "arbitrary")),
        name="mlstm",
    )(qct, kc, vct, gc, gct, qct, kc, vct, gc, gct, bias, biast)


def _shortconv_kernel(u_ref, w_ref, o_ref, *, T, S):
    u = u_ref[0].astype(F32)
    n = u.shape[0]
    w = w_ref[...]
    row = lax.broadcasted_iota(I32, u.shape, 0)
    zero = jnp.zeros((1, u.shape[1]), F32)
    prev = jnp.concatenate([zero, u[0:n - 1]], axis=0)
    nxt = jnp.concatenate([u[1:n], zero], axis=0)
    prev = jnp.where(row == T, 0.0, prev)
    nxt = jnp.where(row == T - 1, 0.0, nxt)
    o_ref[0] = (w[0:1, :] * prev + w[1:2, :] * u + w[2:3, :] * nxt).astype(BF16)


def _shortconv(ub, conv_w, T, S):
    B, N, W = ub.shape
    return pl.pallas_call(
        functools.partial(_shortconv_kernel, T=T, S=S),
        grid=(B, W // 256),
        in_specs=[pl.BlockSpec((1, N, 256), lambda b, j: (b, 0, j)),
                  pl.BlockSpec((3, 256), lambda b, j: (0, j))],
        out_specs=pl.BlockSpec((1, N, 256), lambda b, j: (b, 0, j)),
        out_shape=jax.ShapeDtypeStruct((B, N, W), BF16),
        compiler_params=_cparams(("arbitrary", "arbitrary")),
        name="hyena_shortconv",
    )(ub, conv_w)


def _hyfilter_kernel(z_ref, w1_ref, b1_ref, fr_ref, w2_ref, b2_ref, w3_ref, dec_ref,
                     hp_ref, hf_ref, ss_ref):
    i = pl.program_id(0)
    fr = fr_ref[...]
    h = jnp.sin(fr * (_dot(z_ref[...], w1_ref[...], precision=HI) + b1_ref[...]))
    h = jnp.sin(fr * (_dot(h, w2_ref[...], precision=HI) + b2_ref[...]))
    h = _dot(h, w3_ref[...], precision=HI)
    dec = dec_ref[...]
    dec2 = jnp.concatenate([dec, dec], axis=1)
    hw = HY_ORDER * HY_W
    h0 = h[:, 0:hw] * dec2
    h1 = h[:, hw:2 * hw] * dec2
    rows = h0.shape[0]
    t = i * rows + lax.broadcasted_iota(I32, h0.shape, 0)
    h1 = jnp.where(t == 0, 0.0, h1)
    hp_ref[...] = h0
    hf_ref[...] = h1

    @pl.when(i == 0)
    def _():
        ss_ref[...] = jnp.zeros_like(ss_ref)

    ss_ref[...] += jnp.sum(h0 * h0 + h1 * h1, axis=0, keepdims=True)


def _hyfilter(zfeat, w1, b1, fr, w2, b2, w3, decay):
    ls = zfeat.shape[0]
    rb = min(ls, 512)
    hw = HY_ORDER * HY_W
    full = lambda a: pl.BlockSpec(a.shape, lambda i: (0,) * a.ndim)
    return pl.pallas_call(
        _hyfilter_kernel,
        grid=(ls // rb,),
        in_specs=[pl.BlockSpec((rb, 128), lambda i: (i, 0)), full(w1), full(b1), full(fr), full(w2),
                  full(b2), full(w3), pl.BlockSpec((rb, HY_W), lambda i: (i, 0))],
        out_specs=[pl.BlockSpec((rb, hw), lambda i: (i, 0)), pl.BlockSpec((rb, hw), lambda i: (i, 0)),
                   pl.BlockSpec((1, hw), lambda i: (0, 0))],
        out_shape=[jax.ShapeDtypeStruct((ls, hw), F32), jax.ShapeDtypeStruct((ls, hw), F32),
                   jax.ShapeDtypeStruct((1, hw), F32)],
        compiler_params=_cparams(("arbitrary",)),
        name="hyena_filter",
    )(zfeat, w1, b1, fr, w2, b2, w3, decay)


def _cmul_const(v, c, s):
    vr, vi = v
    r = math.sqrt(0.5)
    if abs(s) < 1e-9:
        return (vr, vi) if c > 0 else (-vr, -vi)
    if abs(c) < 1e-9:
        return (-vi, vr) if s > 0 else (vi, -vr)
    if abs(abs(c) - r) < 1e-9 and abs(abs(s) - r) < 1e-9:
        a, b = (vr if c > 0 else -vr), (vi if s > 0 else -vi)
        p, q = (vi if c > 0 else -vi), (vr if s > 0 else -vr)
        return (a - b) * r, (p + q) * r
    return c * vr - s * vi, c * vi + s * vr


def _fft_pow2(xs, sign):
    n = len(xs)
    if n == 1:
        return xs
    ev, od = _fft_pow2(xs[0::2], sign), _fft_pow2(xs[1::2], sign)
    out = [None] * n
    for k in range(n // 2):
        ang = sign * 2.0 * math.pi * k / n
        tr, ti = _cmul_const(od[k], math.cos(ang), math.sin(ang))
        out[k] = (ev[k][0] + tr, ev[k][1] + ti)
        out[k + n // 2] = (ev[k][0] - tr, ev[k][1] - ti)
    return out


def _hy_stage1(src_ref, f_ref, a_ref, nseg):
    k1p = a_ref.shape[2]
    for n2 in range(HY_N2):
        if len(src_ref.shape) == 3:
            zs = jnp.concatenate([src_ref[t, pl.ds(n2, nseg, stride=HY_N2), :] for t in range(src_ref.shape[0])],
                                 axis=1).astype(BF16)
        else:
            zs = src_ref[pl.ds(n2, nseg, stride=HY_N2), :].astype(BF16)
        r = _dot(f_ref[n2], zs)
        a_ref[0, n2] = r[0:k1p]
        a_ref[1, n2] = r[k1p:2 * k1p]


def _hy_chunks(a_ref):
    k1p, lanes = a_ref.shape[2], a_ref.shape[3]
    return k1p // 8, [slice(t * 128, (t + 1) * 128) for t in range(lanes // 128)]


def _hyspec_kernel(hp_ref, hf_ref, ss_ref, f_ref, h_ref, ap_ref, af_ref, *, ls):
    nseg = ls // HY_N2
    _hy_stage1(hp_ref, f_ref, ap_ref, nseg)
    _hy_stage1(hf_ref, f_ref, af_ref, nseg)
    nchunk, lane_tiles = _hy_chunks(ap_ref)

    def body(i, _):
        rows = pl.ds(pl.multiple_of(i * 8, 8), 8)
        k1 = i * 8 + lax.broadcasted_iota(I32, (8, 1), 0)
        wk = jnp.where((k1 == 0) | (k1 == nseg), 1.0, 2.0) * (1.0 / (2 * ls))
        for ln in lane_tiles:
            scale = lax.rsqrt(ss_ref[:, ln] + EPS) * wk
            P = _fft_pow2([(ap_ref[0, n2, rows, ln], ap_ref[1, n2, rows, ln]) for n2 in range(HY_N2)], -1)
            Q = _fft_pow2([(af_ref[0, n2, rows, ln], af_ref[1, n2, rows, ln]) for n2 in range(HY_N2)], -1)
            for k2 in range(HY_N2):
                h_ref[0, 0, k2, rows, ln] = (P[k2][0] + Q[k2][0]) * scale
                h_ref[0, 1, k2, rows, ln] = (P[k2][1] - Q[k2][1]) * scale
        return 0

    lax.fori_loop(0, nchunk, body, 0)


def _hyspec(hp, hf, ss, fmat):
    ls = hp.shape[0]
    k1p = fmat.shape[1] // 2
    full = lambda a: pl.BlockSpec(a.shape, lambda o, c: (0,) * a.ndim)
    nc = HY_W // 128
    col = lambda o, c: (0, o * nc + c)
    return pl.pallas_call(
        functools.partial(_hyspec_kernel, ls=ls),
        grid=(HY_ORDER, nc),
        in_specs=[pl.BlockSpec((ls, 128), col), pl.BlockSpec((ls, 128), col), pl.BlockSpec((1, 128), col),
                  full(fmat)],
        out_specs=pl.BlockSpec((1, 2, HY_N2, k1p, 128), lambda o, c: (o, 0, 0, 0, c)),
        out_shape=jax.ShapeDtypeStruct((HY_ORDER, 2, HY_N2, k1p, HY_W), F32),
        scratch_shapes=[pltpu.VMEM((2, HY_N2, k1p, 128), F32)] * 2,
        compiler_params=_cparams(("arbitrary", "arbitrary")),
        name="hyena_spectrum",
    )(hp, hf, ss, fmat)


def _hyconv_kernel(z_ref, gate_ref, h_ref, f_ref, g_ref, bias_ref, o_ref, zf_ref, yf_ref, a_ref, *, ls):
    nseg = ls // HY_N2
    ntile = zf_ref.shape[0]
    for t in range(ntile):
        zf_ref[t] = z_ref[0, :, t * 128:(t + 1) * 128].astype(F32)
    _hy_stage1(zf_ref, f_ref, a_ref, nseg)
    nchunk, lane_tiles = _hy_chunks(a_ref)

    def body(i, _):
        rows = pl.ds(pl.multiple_of(i * 8, 8), 8)
        for ln in lane_tiles:
            X = _fft_pow2([(a_ref[0, n2, rows, ln], a_ref[1, n2, rows, ln]) for n2 in range(HY_N2)], -1)
            Y = []
            for k2 in range(HY_N2):
                hr, hi = h_ref[0, 0, k2, rows, ln], h_ref[0, 1, k2, rows, ln]
                xr, xi = X[k2]
                Y.append((xr * hr - xi * hi, xr * hi + xi * hr))
            Bv = _fft_pow2(Y, 1)
            for n2 in range(HY_N2):
                a_ref[0, n2, rows, ln] = Bv[n2][0]
                a_ref[1, n2, rows, ln] = Bv[n2][1]
        return 0

    lax.fori_loop(0, nchunk, body, 0)
    for n2 in range(HY_N2):
        bb = jnp.concatenate([a_ref[0, n2], a_ref[1, n2]], axis=0).astype(BF16)
        yv = _dot(g_ref[n2], bb)
        for t in range(ntile):
            yf_ref[t, pl.ds(n2, nseg, stride=HY_N2), :] = yv[:, t * 128:(t + 1) * 128]
    for t in range(ntile):
        ln = slice(t * 128, (t + 1) * 128)
        o_ref[0, :, ln] = (gate_ref[0, :, ln].astype(F32)
                           * (yf_ref[t] + zf_ref[t] * bias_ref[:, ln])).astype(BF16)


def _hyconv(zsrc, zcol, zrow, gsrc, gcol, grow, hspec, order, fmat, gmat, bias, ls):
    B = zsrc.shape[0]
    k1p = fmat.shape[1] // 2
    once = lambda a: pl.BlockSpec(a.shape, lambda b: (0,) * a.ndim, pipeline_mode=pl.Buffered(1))
    return pl.pallas_call(
        functools.partial(_hyconv_kernel, ls=ls),
        grid=(B,),
        in_specs=[pl.BlockSpec((1, ls, HY_W), lambda b: (b, zrow, zcol)),
                  pl.BlockSpec((1, ls, HY_W), lambda b: (b, grow, gcol)),
                  pl.BlockSpec((1, 2, HY_N2, k1p, HY_W), lambda b: (order, 0, 0, 0, 0),
                               pipeline_mode=pl.Buffered(1)),
                  once(fmat), once(gmat), pl.BlockSpec((1, HY_W), lambda b: (0, 0))],
        out_specs=pl.BlockSpec((1, ls, HY_W), lambda b: (b, 0, 0)),
        out_shape=jax.ShapeDtypeStruct((B, ls, HY_W), BF16),
        scratch_shapes=[pltpu.VMEM((HY_W // 128, ls, 128), F32), pltpu.VMEM((HY_W // 128, ls, 128), F32),
                        pltpu.VMEM((2, HY_N2, k1p, HY_W), F32)],
        compiler_params=_cparams(("arbitrary",)),
        name="hyena_longconv",
    )(zsrc, gsrc, hspec, fmat, gmat, bias)


def _outproj_kernel(x_ref, mod_ref, ya_ref, ybl_ref, ybc_ref, hf_ref, hb_ref, oc_ref, yd_ref, mlg_ref, g64_ref,
                    w_ref, g2_ref, wr_ref, xo_ref, h2_ref, aff_ref, *, nlat):
    hsum = (hf_ref[0] + hb_ref[0]).T
    msq = _dot((hsum * hsum).astype(BF16), g64_ref[...])
    yc = jax.nn.sigmoid(oc_ref[0].astype(F32)) * (hsum * lax.rsqrt(msq + EPS) * mlg_ref[...])
    yb = jnp.where(pl.program_id(1) < nlat, ybl_ref[0], ybc_ref[0])
    y = jnp.concatenate([ya_ref[0], yb, yc.astype(BF16), yd_ref[0]], axis=1)
    mod = mod_ref[0, 0]
    x = x_ref[0] + mod[2:3, :] * _dot(y, w_ref[...])
    xo_ref[0] = x
    ms = jnp.mean(x * x, axis=-1, keepdims=True)
    h2 = x * lax.rsqrt(ms + EPS) * g2_ref[...] * (1.0 + mod[4:5, :]) + mod[3:4, :]
    h2_hi = h2.astype(BF16)
    h2_ref[0] = _pack_bf16_pair(h2_hi[:, 0:D_MODEL // 2], h2_hi[:, D_MODEL // 2:])
    h2_lo = (h2 - h2_hi.astype(F32)).astype(BF16)
    lg = _dot(h2_hi, wr_ref[...])
    logits = lg[:, 0:128] + lg[:, 128:256] + _dot(h2_lo, wr_ref[:, 0:128])
    lane = lax.broadcasted_iota(I32, logits.shape, 1)
    logits = jnp.where(lane < N_EXPERTS, logits, NEG)
    e = jnp.exp(logits - jnp.max(logits, axis=-1, keepdims=True))
    aff_ref[0] = (e / jnp.sum(e, axis=-1, keepdims=True)).T


def _outproj(x, mod, ya, yb_l, yb_c, hf, hb, oc, yd, mlg, g64, w_out, g2, w_router, T):
    B, N, D = x.shape
    rb = ROW_BLOCK
    nlat = T // rb
    row = lambda w: pl.BlockSpec((1, rb, w), lambda b, r: (b, r, 0))
    full = lambda a: pl.BlockSpec(a.shape, lambda b, r: (0,) * a.ndim)
    return pl.pallas_call(
        functools.partial(_outproj_kernel, nlat=nlat),
        grid=(B, N // rb),
        in_specs=[row(D), pl.BlockSpec((1, 1, 6, D), lambda b, r: (b, jnp.where(r >= nlat, 1, 0), 0, 0)),
                  row(256),
                  pl.BlockSpec((1, rb, 256), lambda b, r: (b, jnp.minimum(r, nlat - 1), 0)),
                  pl.BlockSpec((1, rb, 256), lambda b, r: (b, jnp.maximum(r - nlat, 0), 0)),
                  pl.BlockSpec((1, 256, rb), lambda b, r: (b, 0, r)),
                  pl.BlockSpec((1, 256, rb), lambda b, r: (b, 0, r)), row(256), row(256),
                  full(mlg), full(g64), full(w_out), full(g2), full(w_router)],
        out_specs=[row(D), row(D // 2), pl.BlockSpec((1, 128, rb), lambda b, r: (b, 0, r))],
        out_shape=[jax.ShapeDtypeStruct((B, N, D), F32), jax.ShapeDtypeStruct((B, N, D // 2), I32),
                   jax.ShapeDtypeStruct((B, 128, N), F32)],
        compiler_params=_cparams(("arbitrary", "arbitrary")),
        name="outproj",
    )(x, mod, ya, yb_l, yb_c, hf, hb, oc, yd, mlg, g64, w_out, g2, w_router)


def _prefix_exclusive(x):
    n = x.shape[1]
    lane = lax.broadcasted_iota(I32, x.shape, 1)
    inc = x
    d = 1
    while d < n:
        inc = inc + jnp.where(lane >= d, pltpu.roll(inc, d, axis=1), 0)
        d *= 2
    return inc - x


def _topk_kernel(aff_ref, i_ref, g_ref, pos_ref, val_ref, *, T, S):
    def segment(t0, ts, cap, slot0):
        a = aff_ref[0, :, t0:t0 + ts]
        bits = lax.bitcast_convert_type(a, I32)

        def search(i, v):
            cand = v | jnp.left_shift(jnp.int32(1), 30 - i)
            cnt = jnp.sum((bits >= cand).astype(I32), axis=1, keepdims=True)
            return jnp.where(cnt >= cap, cand, v)

        thr = lax.fori_loop(0, 31, search, jnp.zeros((N_EXPERTS, 1), I32))
        gt = bits > thr
        eq = bits == thr
        need = cap - jnp.sum(gt.astype(I32), axis=1, keepdims=True)
        sel = gt | (eq & (_prefix_exclusive(eq.astype(I32)) < need))
        seli = sel.astype(I32)
        pos_ref[:, 0:ts] = jnp.where(sel, _prefix_exclusive(seli), -1)
        a_hi = a.astype(BF16)
        r1 = a - a_hi.astype(F32)
        a_mid = r1.astype(BF16)
        a_lo = (r1 - a_mid.astype(F32)).astype(BF16)
        val_ref[0, :, 0:ts] = a_hi.astype(F32)
        val_ref[1, :, 0:ts] = a_mid.astype(F32)
        val_ref[2, :, 0:ts] = a_lo.astype(F32)
        tabs = lax.broadcasted_iota(I32, (1, ts), 1) + t0
        t_hi = jnp.right_shift(tabs, 6).astype(F32)
        t_lo = (tabs & 63).astype(F32)
        slot = lax.broadcasted_iota(I32, (cap, ts), 0)
        zeros = jnp.zeros((11, ts), F32)

        def per_expert(e, _):
            onehot = jnp.where(pos_ref[pl.ds(e, 1), 0:ts] == slot, 1.0, 0.0).astype(BF16)
            vals = jnp.concatenate([t_hi, t_lo, val_ref[0, pl.ds(e, 1), 0:ts], val_ref[1, pl.ds(e, 1), 0:ts],
                                    val_ref[2, pl.ds(e, 1), 0:ts], zeros], axis=0).astype(BF16)
            r = _dot_nt(onehot, vals)
            i_ref[0, e, slot0:slot0 + cap, :] = (r[:, 0:1] * 64.0 + r[:, 1:2]).astype(I32)
            g_ref[0, e, slot0:slot0 + cap, :] = r[:, 2:3] + r[:, 3:4] + r[:, 4:5]
            return 0

        lax.fori_loop(0, N_EXPERTS, per_expert, 0)

    segment(0, T, (EC_CAPACITY * T) // N_EXPERTS, 0)
    segment(T, S, (EC_CAPACITY * S) // N_EXPERTS, (EC_CAPACITY * T) // N_EXPERTS)


def _topk(aff_t, T, S):
    B, _, N = aff_t.shape
    E = N_EXPERTS
    cap_t = (EC_CAPACITY * T) // E + (EC_CAPACITY * S) // E
    return pl.pallas_call(
        functools.partial(_topk_kernel, T=T, S=S),
        grid=(B,),
        in_specs=[pl.BlockSpec((1, E, N), lambda b: (b, 0, 0))],
        out_specs=[pl.BlockSpec((1, E, cap_t, 1), lambda b: (b, 0, 0, 0))] * 2,
        out_shape=[jax.ShapeDtypeStruct((B, E, cap_t, 1), I32), jax.ShapeDtypeStruct((B, E, cap_t, 1), F32)],
        scratch_shapes=[pltpu.VMEM((E, T), I32), pltpu.VMEM((3, E, T), F32)],
        compiler_params=_cparams(("arbitrary",)),
        name="expert_topk",
    )(aff_t)


def _gather_kernel(idx_ref, h_ref, *rest, slots, nstep):
    o_ref = rest[-1]
    b, e = pl.program_id(0), pl.program_id(1)
    base = (b * nstep + e) * slots

    def body(g, _):
        rows = [h_ref[0, pl.ds(idx_ref[base + g * 16 + k], 1), :] for k in range(16)]
        o_ref[0, pl.ds(pl.multiple_of(g * 16, 16), 16), :] = jnp.concatenate(rows, axis=0)
        return 0

    lax.fori_loop(0, slots // 16, body, 0)


def _gather(idx_flat, h2p, xe_buf, b0, bg, cap_t):
    B, N, hw = h2p.shape
    slots, nstep = GATHER_EXPERTS * cap_t, N_EXPERTS // GATHER_EXPERTS
    in_specs = [pl.BlockSpec((1, N, hw), lambda b, e, idx: (b0 + b, 0, 0))]
    args = [idx_flat, h2p]
    aliases = {}
    if xe_buf is not None:
        in_specs.append(pl.BlockSpec(memory_space=pl.ANY))
        args.append(xe_buf)
        aliases = {2: 0}
    return pl.pallas_call(
        functools.partial(_gather_kernel, slots=slots, nstep=nstep),
        grid_spec=pltpu.PrefetchScalarGridSpec(
            num_scalar_prefetch=1,
            grid=(bg, nstep),
            in_specs=in_specs,
            out_specs=pl.BlockSpec((1, slots, hw), lambda b, e, idx: (b0 + b, e, 0)),
        ),
        out_shape=jax.ShapeDtypeStruct((B, N_EXPERTS * cap_t, hw), I32),
        input_output_aliases=aliases,
        compiler_params=_cparams(("arbitrary", "arbitrary")),
        name="expert_gather",
    )(*args)


def _ffn_kernel(x_ref, wg_ref, wu_ref, wd_ref, gate_ref, gt_ref, o_ref, wg_s, wu_s, wd_s, *, cap_l):
    @pl.when(pl.program_id(1) == 0)
    def _():
        wg_s[...] = wg_ref[0, 0].astype(BF16)
        wu_s[...] = wu_ref[0, 0].astype(BF16)
        wd_s[...] = wd_ref[0, 0].astype(BF16)

    x = jnp.concatenate(_unpack_bf16_pair(x_ref[0, 0]), axis=1)
    a = _dot(x, wg_s[...])
    u = _dot(x, wu_s[...])
    hmid = (a * jax.nn.sigmoid(a) * u).astype(BF16)
    y = _dot(hmid, wd_s[...])
    gt = gt_ref[0]
    row = lax.broadcasted_iota(I32, y.shape, 0)
    gt2 = jnp.where(row < cap_l, gt[0:1, :], gt[1:2, :])
    o_ref[0, 0] = y * gate_ref[0, 0] * gt2


def _ffn(xe, layer, wg, wu, wd, gate, gt2, cap_l):
    B, E, cap_t, hw = xe.shape
    D = 2 * hw
    wspec = pl.BlockSpec((1, 1, D, D), lambda e, b: (layer, e, 0, 0))
    return pl.pallas_call(
        functools.partial(_ffn_kernel, cap_l=cap_l),
        grid=(E, B),
        in_specs=[pl.BlockSpec((1, 1, cap_t, hw), lambda e, b: (b, e, 0, 0)), wspec, wspec, wspec,
                  pl.BlockSpec((1, 1, cap_t, 1), lambda e, b: (b, e, 0, 0)),
                  pl.BlockSpec((1, 2, D), lambda e, b: (b, 0, 0))],
        out_specs=pl.BlockSpec((1, 1, cap_t, D), lambda e, b: (b, e, 0, 0)),
        out_shape=jax.ShapeDtypeStruct((B, E, cap_t, D), F32),
        scratch_shapes=[pltpu.VMEM((D, D), BF16)] * 3,
        compiler_params=_cparams(("arbitrary", "arbitrary")),
        name="expert_ffn",
    )(xe, wg, wu, wd, gate, gt2)


def _scatter_kernel(idx_ref, y_ref, *rest, slots, nstep):
    o_ref = rest[-1]
    b, e = pl.program_id(0), pl.program_id(1)
    base = (b * nstep + e) * slots

    @pl.when(e == 0)
    def _():
        o_ref[...] = jnp.zeros_like(o_ref)

    def body(g, _):
        tile = y_ref[0, pl.ds(pl.multiple_of(g * 8, 8), 8), :]
        rows = [idx_ref[base + g * 8 + k] for k in range(8)]
        cur = [o_ref[0, pl.ds(r, 1), :] for r in rows]
        for k, r in enumerate(rows):
            o_ref[0, pl.ds(r, 1), :] = cur[k] + tile[k:k + 1, :]
        return 0

    lax.fori_loop(0, slots // 8, body, 0)


def _scatter(idx_flat, ye, acc_buf, n_rows, b0, bg, cap_t):
    B, _, D = ye.shape
    slots, nstep = SCATTER_EXPERTS * cap_t, N_EXPERTS // SCATTER_EXPERTS
    in_specs = [pl.BlockSpec((1, slots, D), lambda b, e, idx: (b0 + b, e, 0))]
    args = [idx_flat, ye]
    aliases = {}
    if acc_buf is not None:
        in_specs.append(pl.BlockSpec(memory_space=pl.ANY))
        args.append(acc_buf)
        aliases = {2: 0}
    return pl.pallas_call(
        functools.partial(_scatter_kernel, slots=slots, nstep=nstep),
        grid_spec=pltpu.PrefetchScalarGridSpec(
            num_scalar_prefetch=1,
            grid=(bg, nstep),
            in_specs=in_specs,
            out_specs=pl.BlockSpec((1, n_rows, D), lambda b, e, idx: (b0 + b, 0, 0)),
        ),
        out_shape=jax.ShapeDtypeStruct((B, n_rows, D), F32),
        input_output_aliases=aliases,
        compiler_params=_cparams(("arbitrary", "arbitrary")),
        name="expert_scatter",
    )(*args)


def _residual_out_kernel(x_ref, f_ref, o_ref):
    o_ref[...] = x_ref[...] + f_ref[...]


def _residual_out(x, f, T):
    B, N, D = x.shape
    rb = ROW_BLOCK
    spec = pl.BlockSpec((1, rb, D), lambda b, r: (b, r, 0))
    return pl.pallas_call(
        _residual_out_kernel,
        grid=(B, T // rb),
        in_specs=[spec, spec],
        out_specs=spec,
        out_shape=jax.ShapeDtypeStruct((B, T, D), F32),
        compiler_params=_cparams(("arbitrary", "arbitrary")),
        name="residual_out",
    )(x, f)


def _rope_tables(T, S, dh):
    rows = T // GRID_W
    r = jnp.broadcast_to(jnp.arange(rows, dtype=F32)[:, None], (rows, GRID_W)).reshape(T)
    col = jnp.broadcast_to(jnp.arange(GRID_W, dtype=F32)[None, :], (rows, GRID_W)).reshape(T)
    nf = dh // 4
    inv = ROPE_BASE ** (-jnp.arange(nf, dtype=F32) / nf)
    ar, ac = r[:, None] * inv, col[:, None] * inv
    ang = jnp.concatenate([ar, ar, ac, ac], axis=1)
    ang = jnp.concatenate([ang, jnp.zeros((S, dh), F32)], axis=0)
    reps = 256 // dh
    return jnp.tile(jnp.cos(ang), (1, reps)), jnp.tile(jnp.sin(ang), (1, reps))


def _group_mats(dh):
    i = np.arange(256)
    gmat = (i[:, None] // dh == i[None, :] // dh).astype(np.float32) / dh
    nf = dh // 4
    half = (i % (2 * nf)) // nf
    pmat = np.zeros((256, 256), np.float32)
    a_idx = i[half == 0]
    pmat[a_idx + nf, a_idx] = -1.0
    pmat[a_idx, a_idx + nf] = 1.0
    return jnp.asarray(gmat, BF16), jnp.asarray(pmat, BF16)


def _hyena_tables(ls):
    t = jnp.linspace(0.0, 1.0, ls, dtype=F32)[:, None]
    w = 2.0 * math.pi * jnp.arange(ls, dtype=F32)[:, None] / ls
    bands = jnp.linspace(1e-4, HY_BANDS - 1, HY_BANDS, dtype=F32)
    z = jnp.concatenate([t, jnp.cos(bands * w), -jnp.sin(bands * w)], axis=-1)
    z = jnp.pad(z, ((0, 0), (0, 128 - HY_EMB)))
    deltas = jnp.abs(jnp.linspace(math.log(HY_TARGET) / HY_SLOW, math.log(HY_TARGET) / HY_FAST, HY_W, dtype=F32))
    decay = jnp.exp(-t * deltas)
    n, nseg = 2 * ls, ls // HY_N2
    k1p = -(-(nseg + 1) // 8) * 8
    k1 = jnp.arange(k1p, dtype=I32)[None, :, None]
    tpos = HY_N2 * jnp.arange(nseg, dtype=I32)[None, None, :] + jnp.arange(HY_N2, dtype=I32)[:, None, None]
    ang = ((k1 * tpos) % n).astype(F32) * (2.0 * math.pi / n)
    keep = k1 <= nseg
    fmat = jnp.concatenate([jnp.where(keep, jnp.cos(ang), 0.0), jnp.where(keep, -jnp.sin(ang), 0.0)],
                           axis=1).astype(BF16)
    return z, decay, fmat, jnp.swapaxes(fmat, 1, 2)


def kernel(x, c, ctx, c_ctx, w_ada, b_ada, norm1_g, norm2_g, w_in, b_gate, a_qnorm, a_knorm, a_sink, hy_conv, hy_fw1, hy_fb1, hy_freq, hy_fw2, hy_fb2, hy_fw3, hy_bias, ml_norm, d_qnorm, d_knorm, d_lq1, d_lk1, d_lq2, d_lk2, d_subnorm, w_out, w_router, w_e_gate, w_e_up, w_e_down):
    B, T, D = x.shape
    S = ctx.shape[1]
    N = T + S
    depth = w_ada.shape[0]
    assert D == D_MODEL and T % S == 0 and S % ROW_BLOCK == 0 and S % MLSTM_CHUNK == 0
    assert S % WATTN_QROWS == 0 and T >= WATTN_QROWS + 2 * BLK and WATTN_QROWS & (WATTN_QROWS - 1) == 0
    cap_l, cap_c = (EC_CAPACITY * T) // N_EXPERTS, (EC_CAPACITY * S) // N_EXPERTS
    cap_t = cap_l + cap_c
    assert cap_t % 16 == 0

    xs = jnp.concatenate([x, ctx], axis=1)

    rpad = -(B + 1) % 8
    cc = jnp.concatenate([c, c_ctx[None, :], jnp.zeros((rpad, D), F32)], axis=0)
    mods = _ada_mods(cc, w_ada, b_ada)

    cos_a, sin_a = _rope_tables(T, S, DH_A)
    cos_d, sin_d = _rope_tables(T, S, DH_D)
    g64, p64 = _group_mats(DH_A)
    g32, p32 = _group_mats(DH_D)
    tabs_l = _hyena_tables(T)
    tabs_c = _hyena_tables(S)

    offs = np.cumsum((0, 256, 128, 128, 768, 256, 256, 256, 256, 16, 256, 256, 256))
    ffn_acc = None
    for l in range(depth):
        lam_init = 0.8 - 0.6 * math.exp(-0.3 * l)
        ml = mods[l]
        mod = jnp.stack([ml[:B].reshape(B, 6, D), jnp.broadcast_to(ml[B].reshape(1, 6, D), (B, 6, D))],
                        axis=1)
        w = w_in[l]
        wp = jnp.concatenate([w[:, offs[0]:offs[8]], w[:, offs[8]:offs[9]], jnp.zeros((D, 112), F32),
                              w[:, offs[9]:offs[12]]], axis=1).astype(BF16)
        gains = jnp.stack([jnp.tile(a_qnorm[l], 4), jnp.tile(a_knorm[l], 4),
                           jnp.tile(d_qnorm[l], 8), jnp.tile(d_knorm[l], 8)], axis=0)
        proj = _inproj(xs, ffn_acc, mod, norm1_g[l].reshape(1, D), wp, (cos_a, sin_a, cos_d, sin_d), gains,
                       (g64, g32, p64, p32), T)
        qa, ka, va, ub, kc, oc, gc, qd, kd, vd, gct, qct, vct = proj[0:13]
        if ffn_acc is not None:
            xs = proj[13]

        sb_a = (jnp.max(jnp.abs(a_qnorm[l])) * jnp.max(jnp.abs(a_knorm[l])) * DH_A ** 0.5).reshape(1, 1)
        sb_d = (jnp.max(jnp.abs(d_qnorm[l])) * jnp.max(jnp.abs(d_knorm[l])) * DH_D ** 0.5).reshape(1, 1)
        ya = _wattn(qa, ka, va, jnp.pad(a_sink[l], (0, 128 - H_A)).reshape(1, 128), sb_a, T, S)
        yd = _dattn(qd, kd, vd, jnp.stack([d_lq1[l], d_lk1[l], d_lq2[l], d_lk2[l]], axis=0),
                    d_subnorm[l].reshape(1, 2 * DH_D), sb_d, T, S, lam_init)

        bias = b_gate[l].reshape(-1)
        hf, hb = _mlstm(qct, kc, vct, gc, gct, jnp.pad(bias, (0, 112)).reshape(1, 128), bias.reshape(16, 1), T, S)

        uc = _shortconv(ub, hy_conv[l], T, S)
        w1 = jnp.pad(hy_fw1[l], ((0, 128 - HY_EMB), (0, 0)))
        ybs = []
        for (zf, decay, fmat, gmat), ls, rowblk in ((tabs_l, T, 0), (tabs_c, S, T // S)):
            hpast, hfut, ss = _hyfilter(zf, w1, hy_fb1[l].reshape(1, -1), hy_freq[l].reshape(1, -1), hy_fw2[l],
                                        hy_fb2[l].reshape(1, -1), hy_fw3[l], decay)
            hspec = _hyspec(hpast, hfut, ss, fmat)
            z1 = _hyconv(uc, 0, rowblk, uc, 1, rowblk, hspec, 0, fmat, gmat, hy_bias[l, 0].reshape(1, HY_W), ls)
            z2 = _hyconv(z1, 0, 0, uc, 2, rowblk, hspec, 1, fmat, gmat, hy_bias[l, 1].reshape(1, HY_W), ls)
            ybs.append(z2)

        wr = jnp.pad(w_router[l], ((0, 0), (0, 128 - N_EXPERTS)))
        wr_hi = wr.astype(BF16)
        wr_cat = jnp.concatenate([wr_hi, (wr - wr_hi.astype(F32)).astype(BF16)], axis=1)
        xs, h2, aff = _outproj(xs, mod, ya, ybs[0], ybs[1], hf, hb, oc, yd, jnp.tile(ml_norm[l], 4).reshape(1, 256), g64,
                               w_out[l].astype(BF16), norm2_g[l].reshape(1, D), wr_cat, T)

        idx, gate = _topk(aff, T, S)
        gt2 = mod[:, :, 5, :]
        bg = min(IDX_BATCH_GROUP, B)
        groups = [(b0, idx[b0:b0 + bg].reshape(-1)) for b0 in range(0, B, bg)]
        xe = None
        for b0, idx_flat in groups:
            xe = _gather(idx_flat, h2, xe, b0, bg, cap_t)
        ye = _ffn(xe.reshape(B, N_EXPERTS, cap_t, D // 2), l, w_e_gate, w_e_up, w_e_down, gate, gt2, cap_l)
        ye = ye.reshape(B, N_EXPERTS * cap_t, D)
        ffn_acc = None
        for b0, idx_flat in groups:
            ffn_acc = _scatter(idx_flat, ye, ffn_acc, N, b0, bg, cap_t)
    return _residual_out(xs, ffn_acc, T)
```
